```python
import math
import jax
import jax.numpy as jnp
from jax import lax
import numpy as np

D_MODEL = 2048
BATCH = 8
SEQ = 8192
DEPTH = 1

CTX_LEN = 256
GRID_W = 64
SSM_EXPAND = 2
D_INNER = SSM_EXPAND * D_MODEL
SSM_HEAD_DIM = 64
SSM_HEADS = D_INNER // SSM_HEAD_DIM
SSM_GROUPS = 8
HEADS_PER_GROUP = SSM_HEADS // SSM_GROUPS
SSM_STATE = 128
SSM_CONV = 5
CHUNK = 128
CONV_DIM = D_INNER + 2 * SSM_GROUPS * SSM_STATE
D_CF = D_MODEL
CF_KERNEL = 31
D_FF = -(-8 * D_MODEL // (3 * 256)) * 256
EPS = 1e-6
OFF_Z = D_INNER
OFF_XBC = OFF_Z + CONV_DIM
OFF_DT = OFF_XBC + 2 * SSM_HEADS
OFF_GLU = OFF_DT + 2 * D_CF
D_IN_TOTAL = OFF_GLU + 2 * D_MODEL

kernel_name = "hybrid_ssd_conformer_dit_block"


def rmsnorm(x, w):
    xf = x.astype(jnp.float32)
    y = xf * lax.rsqrt(jnp.mean(xf * xf, axis=-1, keepdims=True) + EPS)
    return (y * w.astype(jnp.float32)).astype(x.dtype)


def layernorm(x, g, b):
    xf = x.astype(jnp.float32)
    mu = jnp.mean(xf, axis=-1, keepdims=True)
    xc = xf - mu
    var = jnp.mean(xc * xc, axis=-1, keepdims=True)
    y = xc * lax.rsqrt(var + EPS) * g.astype(jnp.float32) + b.astype(jnp.float32)
    return y.astype(x.dtype)


def ada_params(cvec, w_mod, b_mod):
    m = jax.nn.silu(cvec) @ w_mod + b_mod
    return jnp.split(m, 6, axis=-1)


def modulate(h, shift, scale):
    return h * (1 + scale) + shift


def dwconv_seq(u, w, bias):
    k = w.shape[0]
    out = lax.conv_general_dilated(
        u, w[:, None, :].astype(u.dtype), window_strides=(1,), padding=[(k // 2, k // 2)],
        dimension_numbers=("NWC", "WIO", "NWC"), feature_group_count=u.shape[-1])
    return out + bias.astype(u.dtype)


def dwconv_grid_columns(u, w, bias, rows):
    b, L, C = u.shape
    k = w.shape[0]
    ug = u.reshape(b, rows, GRID_W, C)
    out = lax.conv_general_dilated(
        ug, w[:, None, None, :].astype(u.dtype), window_strides=(1, 1),
        padding=[(k // 2, k // 2), (0, 0)],
        dimension_numbers=("NHWC", "HWIO", "NHWC"), feature_group_count=C)
    return out.reshape(b, L, C) + bias.astype(u.dtype)


def ssd_chunked(xs, bm, cm, dt, a, h0):
    b, L = xs.shape[:2]
    nc = L // CHUNK

    def to_chunks(t):
        return jnp.moveaxis(t.reshape(b, nc, CHUNK, *t.shape[2:]), 1, 0)

    xdt = (xs.astype(jnp.float32) * dt[..., None]).reshape(
        b, L, SSM_GROUPS, HEADS_PER_GROUP, SSM_HEAD_DIM)
    da = (dt * a).reshape(b, L, SSM_GROUPS, HEADS_PER_GROUP)
    lower = jnp.tril(jnp.ones((CHUNK, CHUNK), dtype=bool))[None, :, :, None, None]

    def step(h, inp):
        xdt_c, b_c, c_c, da_c = inp
        cs = jnp.cumsum(da_c, axis=1)
        seg = cs[:, :, None] - cs[:, None, :]
        decay = jnp.exp(jnp.where(lower, seg, -jnp.inf))
        cb = jnp.einsum("blgn,bsgn->blsg", c_c, b_c)
        y_diag = jnp.einsum("blsgj,bsgjp->blgjp", cb[..., None] * decay, xdt_c)
        y_off = jnp.einsum("blgn,bgjpn->blgjp", c_c, h) * jnp.exp(cs)[..., None]
        to_end = jnp.exp(cs[:, -1:] - cs)
        h_new = h * jnp.exp(cs[:, -1])[..., None, None] + jnp.einsum(
            "bsgn,bsgjp->bgjpn", b_c, xdt_c * to_end[..., None])
        return h_new, y_diag + y_off

    h_last, ys = lax.scan(step, h0, (to_chunks(xdt), to_chunks(bm.astype(jnp.float32)),
                                     to_chunks(cm.astype(jnp.float32)), to_chunks(da)))
    y = jnp.moveaxis(ys, 0, 1).reshape(b, L, SSM_HEADS, SSM_HEAD_DIM)
    return y, h_last


def flip_seq(t):
    return jnp.flip(t, axis=1)


def mixer_front(h, w_in, conv_w, conv_b, dt_bias, a_log, h0_f, h0_b):
    b, L, _ = h.shape
    proj = h @ w_in
    z, xbc, dt_raw, glu, gates = jnp.split(proj, [OFF_Z, OFF_XBC, OFF_DT, OFF_GLU], axis=-1)
    xbc = jax.nn.silu(dwconv_seq(xbc, conv_w, conv_b))
    xs, bm, cm = jnp.split(xbc, [D_INNER, D_INNER + SSM_GROUPS * SSM_STATE], axis=-1)
    xs = xs.reshape(b, L, SSM_HEADS, SSM_HEAD_DIM)
    bm = bm.reshape(b, L, SSM_GROUPS, SSM_STATE)
    cm = cm.reshape(b, L, SSM_GROUPS, SSM_STATE)
    dt = jax.nn.softplus(dt_raw.astype(jnp.float32).reshape(b, L, 2, SSM_HEADS)
                         + dt_bias.astype(jnp.float32))
    a = -jnp.exp(a_log.astype(jnp.float32))
    y_f, h_f = ssd_chunked(xs, bm, cm, dt[:, :, 0], a[0], h0_f)
    y_b, h_b = ssd_chunked(flip_seq(xs), flip_seq(bm), flip_seq(cm), flip_seq(dt[:, :, 1]), a[1], h0_b)
    y = y_f + flip_seq(y_b)
    return (z, xs, y, glu, gates), h_f, h_b


def mixer_back(parts, d_skip, ssm_norm, cf_conv, cf_ln_g, cf_ln_b, w_proj_a, w_proj_b, w_out):
    z, xs, y, glu, gates = parts
    b, L = z.shape[:2]
    y = (y + d_skip.astype(jnp.float32)[:, None] * xs.astype(jnp.float32)).reshape(
        b, L, D_INNER).astype(z.dtype)
    y_a = rmsnorm(y * jax.nn.silu(z), ssm_norm) @ w_proj_a
    u, v = jnp.split(glu, 2, axis=-1)
    cf = jax.nn.silu(layernorm(cf_conv(u * jax.nn.sigmoid(v)), cf_ln_g, cf_ln_b))
    y_b = cf @ w_proj_b
    g_a, g_b = jnp.split(gates, 2, axis=-1)
    return (jax.nn.sigmoid(g_a) * y_a + jax.nn.sigmoid(g_b) * y_b) @ w_out


def swiglu(h, w_gate, w_up, w_down):
    return (jax.nn.silu(h @ w_gate) * (h @ w_up)) @ w_down


def _fwd_setup_inputs(seed: int = 0) -> dict:
    key = jax.random.key(seed)
    ks = jax.random.split(key, 26)

    def nrm(k, shape, std):
        return jax.random.normal(k, shape, jnp.float32) * std

    x = nrm(ks[0], (BATCH, SEQ, D_MODEL), 1.0)
    c = nrm(ks[1], (BATCH, D_MODEL), 1.0)
    ctx = nrm(ks[2], (BATCH, CTX_LEN, D_MODEL), 1.0)
    c_ctx = nrm(ks[3], (D_MODEL,), 1.0)
    w_mod = nrm(ks[4], (DEPTH, D_MODEL, 6 * D_MODEL), 0.5 * D_MODEL ** -0.5)
    b_mod = nrm(ks[5], (DEPTH, 6 * D_MODEL), 0.02)
    norm_mix = 1.0 + nrm(ks[6], (DEPTH, D_MODEL), 0.02)
    w_in = nrm(ks[7], (DEPTH, D_MODEL, D_IN_TOTAL), D_MODEL ** -0.5)
    ssm_conv_w = nrm(ks[8], (DEPTH, SSM_CONV, CONV_DIM), SSM_CONV ** -0.5)
    ssm_conv_b = nrm(ks[9], (DEPTH, CONV_DIM), 0.02)
    dt0 = jnp.exp(jax.random.uniform(ks[10], (DEPTH, 2, SSM_HEADS), jnp.float32,
                                     minval=math.log(1e-3), maxval=math.log(1e-1)))
    dt_bias = dt0 + jnp.log(-jnp.expm1(-dt0))
    a_log = jnp.log(jax.random.uniform(ks[11], (DEPTH, 2, SSM_HEADS), jnp.float32,
                                       minval=1.0, maxval=16.0))
    d_skip = 1.0 + nrm(ks[12], (DEPTH, SSM_HEADS), 0.1)
    ssm_norm = 1.0 + nrm(ks[13], (DEPTH, D_INNER), 0.02)
    cf_conv_w = nrm(ks[14], (DEPTH, CF_KERNEL, D_CF), CF_KERNEL ** -0.5)
    cf_conv_b = nrm(ks[15], (DEPTH, D_CF), 0.02)
    cf_ln_g = 1.0 + nrm(ks[16], (DEPTH, D_CF), 0.02)
    cf_ln_b = nrm(ks[17], (DEPTH, D_CF), 0.02)
    w_proj_a = nrm(ks[18], (DEPTH, D_INNER, D_MODEL), D_INNER ** -0.5)
    w_proj_b = nrm(ks[19], (DEPTH, D_CF, D_MODEL), D_CF ** -0.5)
    w_out = nrm(ks[20], (DEPTH, D_MODEL, D_MODEL), D_MODEL ** -0.5)
    norm_ffn = 1.0 + nrm(ks[21], (DEPTH, D_MODEL), 0.02)
    w_ffn_gate = nrm(ks[22], (DEPTH, D_MODEL, D_FF), D_MODEL ** -0.5)
    w_ffn_up = nrm(ks[23], (DEPTH, D_MODEL, D_FF), D_MODEL ** -0.5)
    w_ffn_down = nrm(ks[24], (DEPTH, D_FF, D_MODEL), D_FF ** -0.5)
    norm_final = 1.0 + nrm(ks[25], (D_MODEL,), 0.02)
    return {"x": x, "c": c, "ctx": ctx, "c_ctx": c_ctx, "w_mod": w_mod, "b_mod": b_mod,
            "norm_mix": norm_mix, "w_in": w_in, "ssm_conv_w": ssm_conv_w, "ssm_conv_b": ssm_conv_b,
            "dt_bias": dt_bias, "a_log": a_log, "d_skip": d_skip, "ssm_norm": ssm_norm,
            "cf_conv_w": cf_conv_w, "cf_conv_b": cf_conv_b, "cf_ln_g": cf_ln_g, "cf_ln_b": cf_ln_b,
            "w_proj_a": w_proj_a, "w_proj_b": w_proj_b, "w_out": w_out, "norm_ffn": norm_ffn,
            "w_ffn_gate": w_ffn_gate, "w_ffn_up": w_ffn_up, "w_ffn_down": w_ffn_down,
            "norm_final": norm_final}


def _fwd_reference(x, c, ctx, c_ctx, w_mod, b_mod, norm_mix, w_in, ssm_conv_w, ssm_conv_b, dt_bias,
              a_log, d_skip, ssm_norm, cf_conv_w, cf_conv_b, cf_ln_g, cf_ln_b, w_proj_a, w_proj_b,
              w_out, norm_ffn, w_ffn_gate, w_ffn_up, w_ffn_down, norm_final):
    b = x.shape[0]
    rows = x.shape[1] // GRID_W
    zero_state = jnp.zeros((b, SSM_GROUPS, HEADS_PER_GROUP, SSM_HEAD_DIM, SSM_STATE), jnp.float32)
    for l in range(DEPTH):
        sh1, sc1, g1, sh2, sc2, g2 = [t[:, None, :] for t in ada_params(c, w_mod[l], b_mod[l])]
        csh1, csc1, cg1, csh2, csc2, cg2 = ada_params(c_ctx, w_mod[l], b_mod[l])

        hc = modulate(rmsnorm(ctx, norm_mix[l]), csh1, csc1)
        ctx_parts, h_f, h_b = mixer_front(hc, w_in[l], ssm_conv_w[l], ssm_conv_b[l], dt_bias[l],
                                          a_log[l], zero_state, zero_state)

        hx = modulate(rmsnorm(x, norm_mix[l]), sh1, sc1)
        x_parts, _, _ = mixer_front(hx, w_in[l], ssm_conv_w[l], ssm_conv_b[l], dt_bias[l],
                                    a_log[l], h_f, h_b)
        x = x + g1 * mixer_back(
            x_parts, d_skip[l], ssm_norm[l],
            lambda t: dwconv_grid_columns(t, cf_conv_w[l], cf_conv_b[l], rows),
            cf_ln_g[l], cf_ln_b[l], w_proj_a[l], w_proj_b[l], w_out[l])
        hx = modulate(rmsnorm(x, norm_ffn[l]), sh2, sc2)
        x = x + g2 * swiglu(hx, w_ffn_gate[l], w_ffn_up[l], w_ffn_down[l])

        if l + 1 < DEPTH:
            ctx = ctx + cg1 * mixer_back(
                ctx_parts, d_skip[l], ssm_norm[l],
                lambda t: dwconv_seq(t, cf_conv_w[l], cf_conv_b[l]),
                cf_ln_g[l], cf_ln_b[l], w_proj_a[l], w_proj_b[l], w_out[l])
            hc = modulate(rmsnorm(ctx, norm_ffn[l]), csh2, csc2)
            ctx = ctx + cg2 * swiglu(hc, w_ffn_gate[l], w_ffn_up[l], w_ffn_down[l])
    return rmsnorm(x, norm_final)


import jax as _jax
import jax.numpy as _jnp

TWIN_FORMAT = 'train_step'
FWD_PARAMS = ['x', 'c', 'ctx', 'c_ctx', 'w_mod', 'b_mod', 'norm_mix', 'w_in', 'ssm_conv_w', 'ssm_conv_b', 'dt_bias', 'a_log', 'd_skip', 'ssm_norm', 'cf_conv_w', 'cf_conv_b', 'cf_ln_g', 'cf_ln_b', 'w_proj_a', 'w_proj_b', 'w_out', 'norm_ffn', 'w_ffn_gate', 'w_ffn_up', 'w_ffn_down', 'norm_final']
TWIN_WEIGHTS = ['c_ctx', 'w_mod', 'b_mod', 'norm_mix', 'w_in', 'ssm_conv_w', 'ssm_conv_b', 'dt_bias', 'a_log', 'd_skip', 'ssm_norm', 'cf_conv_w', 'cf_conv_b', 'cf_ln_g', 'cf_ln_b', 'w_proj_a', 'w_proj_b', 'w_out', 'norm_ffn', 'w_ffn_gate', 'w_ffn_up', 'w_ffn_down', 'norm_final']
TWIN_DIFF_INPUT = 'x'
TWIN_INPUTS = ['x', 'c', 'ctx', 'c_ctx', 'w_mod', 'b_mod', 'norm_mix', 'w_in', 'ssm_conv_w', 'ssm_conv_b', 'dt_bias', 'a_log', 'd_skip', 'ssm_norm', 'cf_conv_w', 'cf_conv_b', 'cf_ln_g', 'cf_ln_b', 'w_proj_a', 'w_proj_b', 'w_out', 'norm_ffn', 'w_ffn_gate', 'w_ffn_up', 'w_ffn_down', 'norm_final', 'loss_target', 'm_c_ctx', 'm_w_mod', 'm_b_mod', 'm_norm_mix', 'm_w_in', 'm_ssm_conv_w', 'm_ssm_conv_b', 'm_dt_bias', 'm_a_log', 'm_d_skip', 'm_ssm_norm', 'm_cf_conv_w', 'm_cf_conv_b', 'm_cf_ln_g', 'm_cf_ln_b', 'm_w_proj_a', 'm_w_proj_b', 'm_w_out', 'm_norm_ffn', 'm_w_ffn_gate', 'm_w_ffn_up', 'm_w_ffn_down', 'm_norm_final', 'v_c_ctx', 'v_w_mod', 'v_b_mod', 'v_norm_mix', 'v_w_in', 'v_ssm_conv_w', 'v_ssm_conv_b', 'v_dt_bias', 'v_a_log', 'v_d_skip', 'v_ssm_norm', 'v_cf_conv_w', 'v_cf_conv_b', 'v_cf_ln_g', 'v_cf_ln_b', 'v_w_proj_a', 'v_w_proj_b', 'v_w_out', 'v_norm_ffn', 'v_w_ffn_gate', 'v_w_ffn_up', 'v_w_ffn_down', 'v_norm_final']
TWIN_OUTPUTS = ['loss', 'grad_x', 'grad_c_ctx', 'grad_w_mod', 'grad_b_mod', 'grad_norm_mix', 'grad_w_in', 'grad_ssm_conv_w', 'grad_ssm_conv_b', 'grad_dt_bias', 'grad_a_log', 'grad_d_skip', 'grad_ssm_norm', 'grad_cf_conv_w', 'grad_cf_conv_b', 'grad_cf_ln_g', 'grad_cf_ln_b', 'grad_w_proj_a', 'grad_w_proj_b', 'grad_w_out', 'grad_norm_ffn', 'grad_w_ffn_gate', 'grad_w_ffn_up', 'grad_w_ffn_down', 'grad_norm_final', 'delta_c_ctx', 'delta_w_mod', 'delta_b_mod', 'delta_norm_mix', 'delta_w_in', 'delta_ssm_conv_w', 'delta_ssm_conv_b', 'delta_dt_bias', 'delta_a_log', 'delta_d_skip', 'delta_ssm_norm', 'delta_cf_conv_w', 'delta_cf_conv_b', 'delta_cf_ln_g', 'delta_cf_ln_b', 'delta_w_proj_a', 'delta_w_proj_b', 'delta_w_out', 'delta_norm_ffn', 'delta_w_ffn_gate', 'delta_w_ffn_up', 'delta_w_ffn_down', 'delta_norm_final', 'new_m_c_ctx', 'new_m_w_mod', 'new_m_b_mod', 'new_m_norm_mix', 'new_m_w_in', 'new_m_ssm_conv_w', 'new_m_ssm_conv_b', 'new_m_dt_bias', 'new_m_a_log', 'new_m_d_skip', 'new_m_ssm_norm', 'new_m_cf_conv_w', 'new_m_cf_conv_b', 'new_m_cf_ln_g', 'new_m_cf_ln_b', 'new_m_w_proj_a', 'new_m_w_proj_b', 'new_m_w_out', 'new_m_norm_ffn', 'new_m_w_ffn_gate', 'new_m_w_ffn_up', 'new_m_w_ffn_down', 'new_m_norm_final', 'new_v_c_ctx', 'new_v_w_mod', 'new_v_b_mod', 'new_v_norm_mix', 'new_v_w_in', 'new_v_ssm_conv_w', 'new_v_ssm_conv_b', 'new_v_dt_bias', 'new_v_a_log', 'new_v_d_skip', 'new_v_ssm_norm', 'new_v_cf_conv_w', 'new_v_cf_conv_b', 'new_v_cf_ln_g', 'new_v_cf_ln_b', 'new_v_w_proj_a', 'new_v_w_proj_b', 'new_v_w_out', 'new_v_norm_ffn', 'new_v_w_ffn_gate', 'new_v_w_ffn_up', 'new_v_w_ffn_down', 'new_v_norm_final']
TWIN_LEAF_KINDS = {'loss': 'loss', 'grad_x': 'grad_x', 'grad_c_ctx': 'grad_w', 'grad_w_mod': 'grad_w', 'grad_b_mod': 'grad_w', 'grad_norm_mix': 'grad_w', 'grad_w_in': 'grad_w', 'grad_ssm_conv_w': 'grad_w', 'grad_ssm_conv_b': 'grad_w', 'grad_dt_bias': 'grad_w', 'grad_a_log': 'grad_w', 'grad_d_skip': 'grad_w', 'grad_ssm_norm': 'grad_w', 'grad_cf_conv_w': 'grad_w', 'grad_cf_conv_b': 'grad_w', 'grad_cf_ln_g': 'grad_w', 'grad_cf_ln_b': 'grad_w', 'grad_w_proj_a': 'grad_w', 'grad_w_proj_b': 'grad_w', 'grad_w_out': 'grad_w', 'grad_norm_ffn': 'grad_w', 'grad_w_ffn_gate': 'grad_w', 'grad_w_ffn_up': 'grad_w', 'grad_w_ffn_down': 'grad_w', 'grad_norm_final': 'grad_w', 'delta_c_ctx': 'delta_w', 'delta_w_mod': 'delta_w', 'delta_b_mod': 'delta_w', 'delta_norm_mix': 'delta_w', 'delta_w_in': 'delta_w', 'delta_ssm_conv_w': 'delta_w', 'delta_ssm_conv_b': 'delta_w', 'delta_dt_bias': 'delta_w', 'delta_a_log': 'delta_w', 'delta_d_skip': 'delta_w', 'delta_ssm_norm': 'delta_w', 'delta_cf_conv_w': 'delta_w', 'delta_cf_conv_b': 'delta_w', 'delta_cf_ln_g': 'delta_w', 'delta_cf_ln_b': 'delta_w', 'delta_w_proj_a': 'delta_w', 'delta_w_proj_b': 'delta_w', 'delta_w_out': 'delta_w', 'delta_norm_ffn': 'delta_w', 'delta_w_ffn_gate': 'delta_w', 'delta_w_ffn_up': 'delta_w', 'delta_w_ffn_down': 'delta_w', 'delta_norm_final': 'delta_w', 'new_m_c_ctx': 'new_m', 'new_m_w_mod': 'new_m', 'new_m_b_mod': 'new_m', 'new_m_norm_mix': 'new_m', 'new_m_w_in': 'new_m', 'new_m_ssm_conv_w': 'new_m', 'new_m_ssm_conv_b': 'new_m', 'new_m_dt_bias': 'new_m', 'new_m_a_log': 'new_m', 'new_m_d_skip': 'new_m', 'new_m_ssm_norm': 'new_m', 'new_m_cf_conv_w': 'new_m', 'new_m_cf_conv_b': 'new_m', 'new_m_cf_ln_g': 'new_m', 'new_m_cf_ln_b': 'new_m', 'new_m_w_proj_a': 'new_m', 'new_m_w_proj_b': 'new_m', 'new_m_w_out': 'new_m', 'new_m_norm_ffn': 'new_m', 'new_m_w_ffn_gate': 'new_m', 'new_m_w_ffn_up': 'new_m', 'new_m_w_ffn_down': 'new_m', 'new_m_norm_final': 'new_m', 'new_v_c_ctx': 'new_v', 'new_v_w_mod': 'new_v', 'new_v_b_mod': 'new_v', 'new_v_norm_mix': 'new_v', 'new_v_w_in': 'new_v', 'new_v_ssm_conv_w': 'new_v', 'new_v_ssm_conv_b': 'new_v', 'new_v_dt_bias': 'new_v', 'new_v_a_log': 'new_v', 'new_v_d_skip': 'new_v', 'new_v_ssm_norm': 'new_v', 'new_v_cf_conv_w': 'new_v', 'new_v_cf_conv_b': 'new_v', 'new_v_cf_ln_g': 'new_v', 'new_v_cf_ln_b': 'new_v', 'new_v_w_proj_a': 'new_v', 'new_v_w_proj_b': 'new_v', 'new_v_w_out': 'new_v', 'new_v_norm_ffn': 'new_v', 'new_v_w_ffn_gate': 'new_v', 'new_v_w_ffn_up': 'new_v', 'new_v_w_ffn_down': 'new_v', 'new_v_norm_final': 'new_v'}


def _forward(args):
    return _fwd_reference(*[args[k] for k in FWD_PARAMS])


def _output_shape():
    def fwd():
        inp = _fwd_setup_inputs(0)
        return _fwd_reference(*[inp[k] for k in FWD_PARAMS])
    out = _jax.eval_shape(fwd)
    return out.shape, out.dtype

N_MICROBATCH = 1
ADAM_LR = 0.001
ADAM_B1 = 0.9
ADAM_B2 = 0.999
ADAM_EPS = 1e-08
ADAM_WD = 0.01
ADAM_STEP = 10
PER_EXAMPLE_BATCH_AXIS = {'x': 0, 'c': 0, 'ctx': 0, 'loss_target': 0}
SHARED_INPUTS = []
_WEIGHT_DTYPES = {'c_ctx': _jnp.float32, 'w_mod': _jnp.float32, 'b_mod': _jnp.float32, 'norm_mix': _jnp.float32, 'w_in': _jnp.float32, 'ssm_conv_w': _jnp.float32, 'ssm_conv_b': _jnp.float32, 'dt_bias': _jnp.float32, 'a_log': _jnp.float32, 'd_skip': _jnp.float32, 'ssm_norm': _jnp.float32, 'cf_conv_w': _jnp.float32, 'cf_conv_b': _jnp.float32, 'cf_ln_g': _jnp.float32, 'cf_ln_b': _jnp.float32, 'w_proj_a': _jnp.float32, 'w_proj_b': _jnp.float32, 'w_out': _jnp.float32, 'norm_ffn': _jnp.float32, 'w_ffn_gate': _jnp.float32, 'w_ffn_up': _jnp.float32, 'w_ffn_down': _jnp.float32, 'norm_final': _jnp.float32}
MOMENT_SCALE = {'c_ctx': 1.704006e-03, 'w_mod': 3.529983e-02, 'b_mod': 6.094988e-02, 'norm_mix': 3.388857e-02, 'w_in': 1.158766e-02, 'ssm_conv_w': 1.232462e-02, 'ssm_conv_b': 1.768197e-02, 'dt_bias': 2.906444e-02, 'a_log': 3.951011e-02, 'd_skip': 4.439544e-02, 'ssm_norm': 1.495162e-02, 'cf_conv_w': 1.290225e-02, 'cf_conv_b': 5.707164e-02, 'cf_ln_g': 1.494938e-02, 'cf_ln_b': 1.360404e-02, 'w_proj_a': 2.036554e-02, 'w_proj_b': 1.252745e-02, 'w_out': 2.386257e-02, 'norm_ffn': 3.597264e-02, 'w_ffn_gate': 1.603749e-02, 'w_ffn_up': 1.554384e-02, 'w_ffn_down': 2.573539e-02, 'norm_final': 3.195931e+01}


def _to_microbatches(a, axis):
    t = _jnp.moveaxis(a, axis, 0)
    t = t.reshape((N_MICROBATCH, t.shape[0] // N_MICROBATCH) + t.shape[1:])
    return _jnp.moveaxis(t, 1, axis + 1)


def setup_inputs(seed: int = 0) -> dict:
    inp = _fwd_setup_inputs(seed)
    key = _jax.random.fold_in(_jax.random.key(seed), 7919)
    shape, _ = _output_shape()
    out = dict(inp)
    out["loss_target"] = _jax.random.normal(_jax.random.fold_in(key, 0), shape, _jnp.float32)
    for i, name in enumerate(TWIN_WEIGHTS):
        w = inp[name].astype(_jnp.float32)
        if MOMENT_SCALE is None:
            s = _jnp.sqrt(_jnp.mean(_jnp.square(w)) + 1e-30)
        else:
            s = MOMENT_SCALE[name]
        km, kv = _jax.random.split(_jax.random.fold_in(key, i + 1))
        out[name] = w
        out["m_" + name] = s * _jax.random.normal(km, w.shape, _jnp.float32)
        out["v_" + name] = (s * s) * _jax.random.uniform(kv, w.shape, _jnp.float32, 0.5, 1.5)
    if N_MICROBATCH > 1:
        for name, axis in PER_EXAMPLE_BATCH_AXIS.items():
            out[name] = _to_microbatches(out[name], axis)
    return {'x': out['x'], 'c': out['c'], 'ctx': out['ctx'], 'c_ctx': out['c_ctx'], 'w_mod': out['w_mod'], 'b_mod': out['b_mod'], 'norm_mix': out['norm_mix'], 'w_in': out['w_in'], 'ssm_conv_w': out['ssm_conv_w'], 'ssm_conv_b': out['ssm_conv_b'], 'dt_bias': out['dt_bias'], 'a_log': out['a_log'], 'd_skip': out['d_skip'], 'ssm_norm': out['ssm_norm'], 'cf_conv_w': out['cf_conv_w'], 'cf_conv_b': out['cf_conv_b'], 'cf_ln_g': out['cf_ln_g'], 'cf_ln_b': out['cf_ln_b'], 'w_proj_a': out['w_proj_a'], 'w_proj_b': out['w_proj_b'], 'w_out': out['w_out'], 'norm_ffn': out['norm_ffn'], 'w_ffn_gate': out['w_ffn_gate'], 'w_ffn_up': out['w_ffn_up'], 'w_ffn_down': out['w_ffn_down'], 'norm_final': out['norm_final'], 'loss_target': out['loss_target'], 'm_c_ctx': out['m_c_ctx'], 'm_w_mod': out['m_w_mod'], 'm_b_mod': out['m_b_mod'], 'm_norm_mix': out['m_norm_mix'], 'm_w_in': out['m_w_in'], 'm_ssm_conv_w': out['m_ssm_conv_w'], 'm_ssm_conv_b': out['m_ssm_conv_b'], 'm_dt_bias': out['m_dt_bias'], 'm_a_log': out['m_a_log'], 'm_d_skip': out['m_d_skip'], 'm_ssm_norm': out['m_ssm_norm'], 'm_cf_conv_w': out['m_cf_conv_w'], 'm_cf_conv_b': out['m_cf_conv_b'], 'm_cf_ln_g': out['m_cf_ln_g'], 'm_cf_ln_b': out['m_cf_ln_b'], 'm_w_proj_a': out['m_w_proj_a'], 'm_w_proj_b': out['m_w_proj_b'], 'm_w_out': out['m_w_out'], 'm_norm_ffn': out['m_norm_ffn'], 'm_w_ffn_gate': out['m_w_ffn_gate'], 'm_w_ffn_up': out['m_w_ffn_up'], 'm_w_ffn_down': out['m_w_ffn_down'], 'm_norm_final': out['m_norm_final'], 'v_c_ctx': out['v_c_ctx'], 'v_w_mod': out['v_w_mod'], 'v_b_mod': out['v_b_mod'], 'v_norm_mix': out['v_norm_mix'], 'v_w_in': out['v_w_in'], 'v_ssm_conv_w': out['v_ssm_conv_w'], 'v_ssm_conv_b': out['v_ssm_conv_b'], 'v_dt_bias': out['v_dt_bias'], 'v_a_log': out['v_a_log'], 'v_d_skip': out['v_d_skip'], 'v_ssm_norm': out['v_ssm_norm'], 'v_cf_conv_w': out['v_cf_conv_w'], 'v_cf_conv_b': out['v_cf_conv_b'], 'v_cf_ln_g': out['v_cf_ln_g'], 'v_cf_ln_b': out['v_cf_ln_b'], 'v_w_proj_a': out['v_w_proj_a'], 'v_w_proj_b': out['v_w_proj_b'], 'v_w_out': out['v_w_out'], 'v_norm_ffn': out['v_norm_ffn'], 'v_w_ffn_gate': out['v_w_ffn_gate'], 'v_w_ffn_up': out['v_w_ffn_up'], 'v_w_ffn_down': out['v_w_ffn_down'], 'v_norm_final': out['v_norm_final']}


def _loss(weights, diff, rest, loss_target):
    with _jax.named_scope("forward"):
        args = {**rest, TWIN_DIFF_INPUT: diff, **{k: w.astype(_WEIGHT_DTYPES[k]) for k, w in weights.items()}}
        y = _forward(args)
    with _jax.named_scope("loss_head"):
        err = _jnp.square(y.astype(_jnp.float32) - loss_target)
        return 0.5 * _jnp.sum(_jnp.mean(err, axis=-1)) if err.ndim else 0.5 * err


def _adamw(w, g, m, v):
    m = ADAM_B1 * m + (1.0 - ADAM_B1) * g
    v = ADAM_B2 * v + (1.0 - ADAM_B2) * _jnp.square(g)
    m_hat = m / (1.0 - ADAM_B1 ** ADAM_STEP)
    v_hat = v / (1.0 - ADAM_B2 ** ADAM_STEP)
    delta = -ADAM_LR * (m_hat / (_jnp.sqrt(v_hat) + ADAM_EPS) + ADAM_WD * w)
    return delta, m, v


def reference(x, c, ctx, c_ctx, w_mod, b_mod, norm_mix, w_in, ssm_conv_w, ssm_conv_b, dt_bias, a_log, d_skip, ssm_norm, cf_conv_w, cf_conv_b, cf_ln_g, cf_ln_b, w_proj_a, w_proj_b, w_out, norm_ffn, w_ffn_gate, w_ffn_up, w_ffn_down, norm_final, loss_target, m_c_ctx, m_w_mod, m_b_mod, m_norm_mix, m_w_in, m_ssm_conv_w, m_ssm_conv_b, m_dt_bias, m_a_log, m_d_skip, m_ssm_norm, m_cf_conv_w, m_cf_conv_b, m_cf_ln_g, m_cf_ln_b, m_w_proj_a, m_w_proj_b, m_w_out, m_norm_ffn, m_w_ffn_gate, m_w_ffn_up, m_w_ffn_down, m_norm_final, v_c_ctx, v_w_mod, v_b_mod, v_norm_mix, v_w_in, v_ssm_conv_w, v_ssm_conv_b, v_dt_bias, v_a_log, v_d_skip, v_ssm_norm, v_cf_conv_w, v_cf_conv_b, v_cf_ln_g, v_cf_ln_b, v_w_proj_a, v_w_proj_b, v_w_out, v_norm_ffn, v_w_ffn_gate, v_w_ffn_up, v_w_ffn_down, v_norm_final):
    given = dict(x=x, c=c, ctx=ctx, c_ctx=c_ctx, w_mod=w_mod, b_mod=b_mod, norm_mix=norm_mix, w_in=w_in, ssm_conv_w=ssm_conv_w, ssm_conv_b=ssm_conv_b, dt_bias=dt_bias, a_log=a_log, d_skip=d_skip, ssm_norm=ssm_norm, cf_conv_w=cf_conv_w, cf_conv_b=cf_conv_b, cf_ln_g=cf_ln_g, cf_ln_b=cf_ln_b, w_proj_a=w_proj_a, w_proj_b=w_proj_b, w_out=w_out, norm_ffn=norm_ffn, w_ffn_gate=w_ffn_gate, w_ffn_up=w_ffn_up, w_ffn_down=w_ffn_down, norm_final=norm_final, loss_target=loss_target, m_c_ctx=m_c_ctx, m_w_mod=m_w_mod, m_b_mod=m_b_mod, m_norm_mix=m_norm_mix, m_w_in=m_w_in, m_ssm_conv_w=m_ssm_conv_w, m_ssm_conv_b=m_ssm_conv_b, m_dt_bias=m_dt_bias, m_a_log=m_a_log, m_d_skip=m_d_skip, m_ssm_norm=m_ssm_norm, m_cf_conv_w=m_cf_conv_w, m_cf_conv_b=m_cf_conv_b, m_cf_ln_g=m_cf_ln_g, m_cf_ln_b=m_cf_ln_b, m_w_proj_a=m_w_proj_a, m_w_proj_b=m_w_proj_b, m_w_out=m_w_out, m_norm_ffn=m_norm_ffn, m_w_ffn_gate=m_w_ffn_gate, m_w_ffn_up=m_w_ffn_up, m_w_ffn_down=m_w_ffn_down, m_norm_final=m_norm_final, v_c_ctx=v_c_ctx, v_w_mod=v_w_mod, v_b_mod=v_b_mod, v_norm_mix=v_norm_mix, v_w_in=v_w_in, v_ssm_conv_w=v_ssm_conv_w, v_ssm_conv_b=v_ssm_conv_b, v_dt_bias=v_dt_bias, v_a_log=v_a_log, v_d_skip=v_d_skip, v_ssm_norm=v_ssm_norm, v_cf_conv_w=v_cf_conv_w, v_cf_conv_b=v_cf_conv_b, v_cf_ln_g=v_cf_ln_g, v_cf_ln_b=v_cf_ln_b, v_w_proj_a=v_w_proj_a, v_w_proj_b=v_w_proj_b, v_w_out=v_w_out, v_norm_ffn=v_norm_ffn, v_w_ffn_gate=v_w_ffn_gate, v_w_ffn_up=v_w_ffn_up, v_w_ffn_down=v_w_ffn_down, v_norm_final=v_norm_final)
    weights = {n: given[n] for n in TWIN_WEIGHTS}
    shared = {n: given[n] for n in SHARED_INPUTS}
    per_example = {n: given[n] for n in ['x', 'c', 'ctx']}
    grad_fn = _jax.value_and_grad(_loss, argnums=(0, 1))

    def one_microbatch(ex, loss_target):
        ex = dict(ex)
        diff = ex.pop(TWIN_DIFF_INPUT)
        return grad_fn(weights, diff, {**shared, **ex}, loss_target)

    if N_MICROBATCH == 1:
        loss, (grad_w, grad_x) = one_microbatch(per_example, given["loss_target"])
    else:
        def body(carry, xs):
            loss_sum, grad_sum = carry
            l_k, (gw_k, gx_k) = one_microbatch(xs[0], xs[1])
            with _jax.named_scope("update"):
                return (loss_sum + l_k, _jax.tree.map(_jnp.add, grad_sum, gw_k)), gx_k

        init = (_jnp.zeros((), _jnp.float32), _jax.tree.map(_jnp.zeros_like, weights))
        (loss, grad_w), grad_x = _jax.lax.scan(body, init, (per_example, given["loss_target"]))
    with _jax.named_scope("update"):
        delta_w, new_m, new_v = {}, {}, {}
        for n in TWIN_WEIGHTS:
            delta_w[n], new_m[n], new_v[n] = _adamw(weights[n], grad_w[n], given["m_" + n], given["v_" + n])
    return (loss, grad_x, *[grad_w[n] for n in TWIN_WEIGHTS], *[delta_w[n] for n in TWIN_WEIGHTS],
            *[new_m[n] for n in TWIN_WEIGHTS], *[new_v[n] for n in TWIN_WEIGHTS])
```

```python
import functools
import math

import jax
import jax.numpy as jnp
from jax import lax
from jax.experimental import pallas as pl
from jax.experimental.pallas import tpu as pltpu

F32 = jnp.float32
ACT = jnp.bfloat16
HIGHEST = lax.Precision.HIGHEST
MESH = pl.DeviceIdType.MESH
AXES = ("x", "y", "c")
N_DEV = 8

GRID_W = 64
CHUNK = 128
HEAD_DIM = 64
GROUPS = 8
STATE = 128
EPS = 1e-6
ADAM_LR = 0.001
ADAM_B1 = 0.9
ADAM_B2 = 0.999
ADAM_EPS = 1e-08
ADAM_WD = 0.01
ADAM_STEP = 10

V7X_VMEM_LIMIT = 56 * 1024 * 1024
NEG = -1e30

NN = (((1,), (0,)), ((), ()))
NT = (((1,), (1,)), ((), ()))
TN = (((0,), (0,)), ((), ()))


def _tile(n, target, quantum):
    best = None
    t = quantum
    while t <= min(n, target):
        if n % t == 0:
            best = t
        t += quantum
    return n if best is None else best


def _params(sem=None):
    kw = dict(vmem_limit_bytes=V7X_VMEM_LIMIT)
    if sem is not None:
        kw["dimension_semantics"] = sem
    return pltpu.CompilerParams(**kw)


def _silu(v):
    return v * jax.nn.sigmoid(v)


def _dsilu(v):
    s = jax.nn.sigmoid(v)
    return s * (1.0 + v * (1.0 - s))


def _exchange(x, *, name, gather):
    shape = x.shape[-2:]

    def body(x_ref, o_ref, send_sems, recv_sems, loc_sem):
        ix, iy, ic = lax.axis_index("x"), lax.axis_index("y"), lax.axis_index("c")
        me = 4 * ix + 2 * iy + ic

        def src(d):
            return x_ref if gather else x_ref.at[d]

        def remote(k, slot, peer_xyz, src_ref):
            return pltpu.make_async_remote_copy(
                src_ref=src_ref, dst_ref=o_ref.at[slot], send_sem=send_sems.at[k], recv_sem=recv_sems.at[k],
                device_id=peer_xyz, device_id_type=MESH)

        local = pltpu.make_async_copy(src(me), o_ref.at[me], loc_sem)
        local.start()
        sends, peers = [], []
        for k in range(1, N_DEV):
            px = 1 - ix if k & 4 else ix
            py = 1 - iy if k & 2 else iy
            pc = 1 - ic if k & 1 else ic
            peer = 4 * px + 2 * py + pc
            cp = remote(k - 1, me, (px, py, pc), src(peer))
            cp.start()
            sends.append(cp)
            peers.append((peer, (px, py, pc)))
        for k in range(1, N_DEV):
            peer, xyz = peers[k - 1]
            remote(k - 1, peer, xyz, src(peer)).wait_recv()
        for cp in sends:
            cp.wait_send()
        local.wait()

    return pl.pallas_call(
        body, name=name,
        out_shape=jax.ShapeDtypeStruct((N_DEV,) + shape, x.dtype),
        in_specs=[pl.BlockSpec(memory_space=pl.ANY)],
        out_specs=pl.BlockSpec(memory_space=pl.ANY),
        scratch_shapes=[pltpu.SemaphoreType.DMA((N_DEV - 1,)), pltpu.SemaphoreType.DMA((N_DEV - 1,)),
                        pltpu.SemaphoreType.DMA],
    )(x)


def _sum_slots(x, *, name):
    n, R, C = x.shape
    tr = _tile(R, 256, 8)

    def body(x_ref, o_ref):
        acc = x_ref[0].astype(F32)
        for d in range(1, n):
            acc = acc + x_ref[d].astype(F32)
        o_ref[...] = acc

    return pl.pallas_call(
        body, name=name, grid=(R // tr,),
        out_shape=jax.ShapeDtypeStruct((R, C), F32),
        in_specs=[pl.BlockSpec((n, tr, C), lambda i: (0, i, 0))],
        out_specs=pl.BlockSpec((tr, C), lambda i: (i, 0)),
        compiler_params=_params(("parallel",)),
    )(x)


def _colsum(x, *, name):
    R, C = x.shape

    def body(x_ref, o_ref):
        o_ref[...] = jnp.sum(x_ref[...], axis=0, keepdims=True)

    return pl.pallas_call(
        body, name=name, out_shape=jax.ShapeDtypeStruct((1, C), F32),
        in_specs=[pl.BlockSpec((R, C), lambda: (0, 0))], out_specs=pl.BlockSpec((1, C), lambda: (0, 0)),
        compiler_params=_params(),
    )(x)


def _adamw(parts, w, m, v, *, name):
    n, R, C = parts.shape
    tr = _tile(R, 128, 8)
    c1 = 1.0 - ADAM_B1 ** ADAM_STEP
    c2 = 1.0 - ADAM_B2 ** ADAM_STEP

    def body(p_ref, w_ref, m_ref, v_ref, g_ref, d_ref, nm_ref, nv_ref):
        g = p_ref[0].astype(F32)
        for d in range(1, n):
            g = g + p_ref[d].astype(F32)
        mn = ADAM_B1 * m_ref[...] + (1.0 - ADAM_B1) * g
        vn = ADAM_B2 * v_ref[...] + (1.0 - ADAM_B2) * (g * g)
        g_ref[...] = g
        nm_ref[...] = mn
        nv_ref[...] = vn
        d_ref[...] = -ADAM_LR * ((mn / c1) / (jnp.sqrt(vn / c2) + ADAM_EPS) + ADAM_WD * w_ref[...])

    spec = pl.BlockSpec((tr, C), lambda i: (i, 0))
    shp = jax.ShapeDtypeStruct((R, C), F32)
    return pl.pallas_call(
        body, name=name, grid=(R // tr,), out_shape=(shp, shp, shp, shp),
        in_specs=[pl.BlockSpec((n, tr, C), lambda i: (0, i, 0)), spec, spec, spec],
        out_specs=(spec, spec, spec, spec),
        compiler_params=_params(("parallel",)),
    )(parts, w, m, v)


def _mm(a, b, mode, *, name, out_dtype, add=None, tm=1024, tn=1024, tk=512):
    if mode == "nn":
        (M, K), (K2, N) = a.shape, b.shape
    elif mode == "nt":
        (M, K), (N, K2) = a.shape, b.shape
    else:
        (K, M), (K2, N) = a.shape, b.shape
    assert K == K2, (name, a.shape, b.shape)
    tm = _tile(M, tm, 128 if mode == "tn" else 8)
    tn = _tile(N, tn, 128)
    tk = _tile(K, tk, 128)
    nk = K // tk
    dims = {"nn": NN, "nt": NT, "tn": TN}[mode]

    a_spec = {"nn": pl.BlockSpec((tm, tk), lambda i, j, k: (i, k)),
              "nt": pl.BlockSpec((tm, tk), lambda i, j, k: (i, k)),
              "tn": pl.BlockSpec((tk, tm), lambda i, j, k: (k, i))}[mode]
    b_spec = {"nn": pl.BlockSpec((tk, tn), lambda i, j, k: (k, j)),
              "nt": pl.BlockSpec((tn, tk), lambda i, j, k: (j, k)),
              "tn": pl.BlockSpec((tk, tn), lambda i, j, k: (k, j))}[mode]
    o_spec = pl.BlockSpec((tm, tn), lambda i, j, k: (i, j))

    def body(a_ref, b_ref, *rest):
        if add is None:
            o_ref, acc = rest
        else:
            add_ref, o_ref, acc = rest
        k = pl.program_id(2)

        @pl.when(k == 0)
        def _():
            acc[...] = jnp.zeros_like(acc)

        acc[...] += lax.dot_general(a_ref[...].astype(ACT), b_ref[...].astype(ACT), dims,
                                    preferred_element_type=F32)

        @pl.when(k == nk - 1)
        def _():
            r = acc[...]
            if add is not None:
                r = r + add_ref[...].astype(F32)
            o_ref[...] = r.astype(out_dtype)

    operands = [a, b] + ([] if add is None else [add])
    in_specs = [a_spec, b_spec] + ([] if add is None else [o_spec])
    return pl.pallas_call(
        body, name=name, grid=(M // tm, N // tn, nk),
        out_shape=jax.ShapeDtypeStruct((M, N), out_dtype),
        in_specs=in_specs, out_specs=o_spec,
        scratch_shapes=[pltpu.VMEM((tm, tn), F32)],
        compiler_params=_params(("parallel", "parallel", "arbitrary")),
    )(*operands)


def _row(tr, cols, blk=0):
    return pl.BlockSpec((tr, cols), lambda i: (i, blk))


def _vec(cols):
    return pl.BlockSpec((1, cols), lambda i: (0, 0))


def _rms(xf):
    return lax.rsqrt(jnp.mean(xf * xf, axis=-1, keepdims=True) + EPS)


def _rms_bwd(dxhat, xhat, r):
    return r * (dxhat - xhat * jnp.mean(dxhat * xhat, axis=-1, keepdims=True))


def _acc_rows(ref, val, first):
    s = jnp.sum(val, axis=0, keepdims=True)

    @pl.when(first)
    def _():
        ref[...] = s

    @pl.when(jnp.logical_not(first))
    def _():
        ref[...] += s


def _norm_mod_fwd(x, nw, shift, scale, *, name):
    T, D = x.shape
    tr = _tile(T, 256, 8)

    def body(x_ref, nw_ref, sh_ref, sc_ref, o_ref):
        xf = x_ref[...]
        n = xf * _rms(xf) * nw_ref[...]
        o_ref[...] = (n * (1.0 + sc_ref[...]) + sh_ref[...]).astype(ACT)

    return pl.pallas_call(
        body, name=name, grid=(T // tr,), out_shape=jax.ShapeDtypeStruct((T, D), ACT),
        in_specs=[_row(tr, D), _vec(D), _vec(D), _vec(D)], out_specs=_row(tr, D),
        compiler_params=_params(("parallel",)),
    )(x, nw, shift, scale)


def _resid_norm_mod_fwd(x, o, g, nw, shift, scale, *, name):
    T, D = x.shape
    tr = _tile(T, 256, 8)

    def body(x_ref, o_ref, g_ref, nw_ref, sh_ref, sc_ref, x1_ref, h_ref):
        x1 = x_ref[...] + g_ref[...] * o_ref[...].astype(F32)
        x1_ref[...] = x1
        n = x1 * _rms(x1) * nw_ref[...]
        h_ref[...] = (n * (1.0 + sc_ref[...]) + sh_ref[...]).astype(ACT)

    return pl.pallas_call(
        body, name=name, grid=(T // tr,),
        out_shape=(jax.ShapeDtypeStruct((T, D), F32), jax.ShapeDtypeStruct((T, D), ACT)),
        in_specs=[_row(tr, D), _row(tr, D), _vec(D), _vec(D), _vec(D), _vec(D)],
        out_specs=(_row(tr, D), _row(tr, D)),
        compiler_params=_params(("parallel",)),
    )(x, o, g, nw, shift, scale)


def _final_fwd_bwd(x1, dn, g2, nw, target, *, name):
    T, D = x1.shape
    tr = _tile(T, 256, 8)

    def body(x1_ref, dn_ref, g_ref, nw_ref, t_ref, loss_ref, dx_ref, ddn_ref, dnw_ref, dg_ref):
        first = pl.program_id(0) == 0
        dn_f = dn_ref[...].astype(F32)
        x2 = x1_ref[...] + g_ref[...] * dn_f
        r = _rms(x2)
        xhat = x2 * r
        err = xhat * nw_ref[...] - t_ref[...]
        part = 0.5 * jnp.sum(jnp.mean(err * err, axis=-1, keepdims=True), axis=0, keepdims=True)

        @pl.when(first)
        def _():
            loss_ref[...] = part

        @pl.when(jnp.logical_not(first))
        def _():
            loss_ref[...] += part

        dy = err * (1.0 / D)
        _acc_rows(dnw_ref, dy * xhat, first)
        dx2 = _rms_bwd(dy * nw_ref[...], xhat, r)
        dx_ref[...] = dx2
        ddn_ref[...] = (g_ref[...] * dx2).astype(ACT)
        _acc_rows(dg_ref, dx2 * dn_f, first)

    vec = jax.ShapeDtypeStruct((1, D), F32)
    return pl.pallas_call(
        body, name=name, grid=(T // tr,),
        out_shape=(jax.ShapeDtypeStruct((1, 1), F32), jax.ShapeDtypeStruct((T, D), F32),
                   jax.ShapeDtypeStruct((T, D), ACT), vec, vec),
        in_specs=[_row(tr, D), _row(tr, D), _vec(D), _vec(D), _row(tr, D)],
        out_specs=(pl.BlockSpec((1, 1), lambda i: (0, 0)), _row(tr, D), _row(tr, D), _vec(D), _vec(D)),
        compiler_params=_params(("arbitrary",)),
    )(x1, dn, g2, nw, target)


def _norm_mod_bwd(xin, nw, scale, dh, *, name, dres=None, o=None, g=None):
    T, D = xin.shape
    tr = _tile(T, 256, 8)
    has_res, has_o = dres is not None, o is not None

    def body(*refs):
        refs = list(refs)
        x_ref, nw_ref, sc_ref, dh_ref = refs[:4]
        pos = 4
        dres_ref = o_ref = g_ref = None
        if has_res:
            dres_ref = refs[pos]
            pos += 1
        if has_o:
            o_ref, g_ref = refs[pos], refs[pos + 1]
            pos += 2
        dx_ref, dsh_ref, dsc_ref, dnw_ref = refs[pos:pos + 4]
        pos += 4
        first = pl.program_id(0) == 0
        xf = x_ref[...]
        r = _rms(xf)
        xhat = xf * r
        n = xhat * nw_ref[...]
        dhf = dh_ref[...].astype(F32)
        _acc_rows(dsh_ref, dhf, first)
        _acc_rows(dsc_ref, dhf * n, first)
        dn = dhf * (1.0 + sc_ref[...])
        _acc_rows(dnw_ref, dn * xhat, first)
        dx = _rms_bwd(dn * nw_ref[...], xhat, r)
        if has_res:
            dx = dx + dres_ref[...]
        dx_ref[...] = dx
        if has_o:
            do_ref, dg_ref = refs[pos], refs[pos + 1]
            do_ref[...] = (g_ref[...] * dx).astype(ACT)
            _acc_rows(dg_ref, dx * o_ref[...].astype(F32), first)

    vec = jax.ShapeDtypeStruct((1, D), F32)
    operands = [xin, nw, scale, dh]
    in_specs = [_row(tr, D), _vec(D), _vec(D), _row(tr, D)]
    if has_res:
        operands.append(dres)
        in_specs.append(_row(tr, D))
    if has_o:
        operands += [o, g]
        in_specs += [_row(tr, D), _vec(D)]
    out_shape = [jax.ShapeDtypeStruct((T, D), F32), vec, vec, vec]
    out_specs = [_row(tr, D), _vec(D), _vec(D), _vec(D)]
    if has_o:
        out_shape += [jax.ShapeDtypeStruct((T, D), ACT), vec]
        out_specs += [_row(tr, D), _vec(D)]
    return pl.pallas_call(
        body, name=name, grid=(T // tr,), out_shape=tuple(out_shape),
        in_specs=in_specs, out_specs=tuple(out_specs),
        compiler_params=_params(("arbitrary",)),
    )(*operands)


def _swiglu_fwd(gate, up, *, name):
    T, F = gate.shape
    tr = _tile(T, 256, 8)

    def body(g_ref, u_ref, o_ref):
        o_ref[...] = (_silu(g_ref[...].astype(F32)) * u_ref[...].astype(F32)).astype(ACT)

    return pl.pallas_call(
        body, name=name, grid=(T // tr,), out_shape=jax.ShapeDtypeStruct((T, F), ACT),
        in_specs=[_row(tr, F), _row(tr, F)], out_specs=_row(tr, F),
        compiler_params=_params(("parallel",)),
    )(gate, up)


def _swiglu_bwd(gate, up, dact, *, name):
    T, F = gate.shape
    tr = _tile(T, 256, 8)

    def body(g_ref, u_ref, d_ref, dg_ref, du_ref):
        gf, uf, df = g_ref[...].astype(F32), u_ref[...].astype(F32), d_ref[...].astype(F32)
        dg_ref[...] = (df * uf * _dsilu(gf)).astype(ACT)
        du_ref[...] = (df * _silu(gf)).astype(ACT)

    shp = jax.ShapeDtypeStruct((T, F), ACT)
    return pl.pallas_call(
        body, name=name, grid=(T // tr,), out_shape=(shp, shp),
        in_specs=[_row(tr, F)] * 3, out_specs=(_row(tr, F), _row(tr, F)),
        compiler_params=_params(("parallel",)),
    )(gate, up, dact)


def _merge_fwd(ya, yb, gates, *, name):
    T, D = ya.shape
    tr = _tile(T, 256, 8)

    def body(a_ref, b_ref, g_ref, o_ref):
        ga = g_ref[:, :D].astype(F32)
        gb = g_ref[:, D:].astype(F32)
        o_ref[...] = (jax.nn.sigmoid(ga) * a_ref[...].astype(F32)
                      + jax.nn.sigmoid(gb) * b_ref[...].astype(F32)).astype(ACT)

    return pl.pallas_call(
        body, name=name, grid=(T // tr,), out_shape=jax.ShapeDtypeStruct((T, D), ACT),
        in_specs=[_row(tr, D), _row(tr, D), _row(tr, 2 * D)], out_specs=_row(tr, D),
        compiler_params=_params(("parallel",)),
    )(ya, yb, gates)


def _merge_bwd(dmer, ya, yb, gates, *, name):
    T, D = ya.shape
    tr = _tile(T, 256, 8)

    def body(d_ref, a_ref, b_ref, g_ref, da_ref, db_ref, dg_ref):
        d = d_ref[...].astype(F32)
        sa = jax.nn.sigmoid(g_ref[:, :D].astype(F32))
        sb = jax.nn.sigmoid(g_ref[:, D:].astype(F32))
        da_ref[...] = (d * sa).astype(ACT)
        db_ref[...] = (d * sb).astype(ACT)
        dg_ref[:, :D] = (d * a_ref[...].astype(F32) * sa * (1.0 - sa)).astype(ACT)
        dg_ref[:, D:] = (d * b_ref[...].astype(F32) * sb * (1.0 - sb)).astype(ACT)

    shp = jax.ShapeDtypeStruct((T, D), ACT)
    return pl.pallas_call(
        body, name=name, grid=(T // tr,), out_shape=(shp, shp, jax.ShapeDtypeStruct((T, 2 * D), ACT)),
        in_specs=[_row(tr, D), _row(tr, D), _row(tr, D), _row(tr, 2 * D)],
        out_specs=(_row(tr, D), _row(tr, D), _row(tr, 2 * D)),
        compiler_params=_params(("parallel",)),
    )(dmer, ya, yb, gates)


def _gate_norm_fwd(yf, yb, xbc, z, dskip, nw, *, name):
    T, DI = z.shape
    tr = _tile(T, 128, 8)

    def body(yf_ref, yb_ref, xs_ref, z_ref, ds_ref, nw_ref, o_ref):
        y = yf_ref[...].astype(F32) + yb_ref[...].astype(F32) + ds_ref[...] * xs_ref[...].astype(F32)
        gz = y * _silu(z_ref[...].astype(F32))
        o_ref[...] = (gz * _rms(gz) * nw_ref[...]).astype(ACT)

    return pl.pallas_call(
        body, name=name, grid=(T // tr,), out_shape=jax.ShapeDtypeStruct((T, DI), ACT),
        in_specs=[_row(tr, DI), _row(tr, DI), _row(tr, DI), _row(tr, DI), _vec(DI), _vec(DI)],
        out_specs=_row(tr, DI),
        compiler_params=_params(("parallel",)),
    )(yf, yb, xbc, z, dskip, nw)


def _gate_norm_bwd(dout, yf, yb, xbc, z, dskip, nw, *, name):
    T, DI = z.shape
    tr = _tile(T, 128, 8)

    def body(do_ref, yf_ref, yb_ref, xs_ref, z_ref, ds_ref, nw_ref, dy_ref, dz_ref, dxs_ref, dnw_ref, dds_ref):
        first = pl.program_id(0) == 0
        xs = xs_ref[...].astype(F32)
        zf = z_ref[...].astype(F32)
        y = yf_ref[...].astype(F32) + yb_ref[...].astype(F32) + ds_ref[...] * xs
        sz = _silu(zf)
        gz = y * sz
        r = _rms(gz)
        ghat = gz * r
        do = do_ref[...].astype(F32)
        _acc_rows(dnw_ref, do * ghat, first)
        dgz = _rms_bwd(do * nw_ref[...], ghat, r)
        dy = dgz * sz
        dy_ref[...] = dy.astype(ACT)
        dz_ref[...] = (dgz * y * _dsilu(zf)).astype(ACT)
        dxs_ref[...] = (dy * ds_ref[...]).astype(ACT)
        _acc_rows(dds_ref, dy * xs, first)

    shp = jax.ShapeDtypeStruct((T, DI), ACT)
    vec = jax.ShapeDtypeStruct((1, DI), F32)
    return pl.pallas_call(
        body, name=name, grid=(T // tr,), out_shape=(shp, shp, shp, vec, vec),
        in_specs=[_row(tr, DI)] * 5 + [_vec(DI), _vec(DI)],
        out_specs=(_row(tr, DI), _row(tr, DI), _row(tr, DI), _vec(DI), _vec(DI)),
        compiler_params=_params(("arbitrary",)),
    )(dout, yf, yb, xbc, z, dskip, nw)


def _ln_silu_fwd(x, g, b, *, name):
    T, D = x.shape
    tr = _tile(T, 256, 8)

    def body(x_ref, g_ref, b_ref, o_ref):
        xf = x_ref[...].astype(F32)
        xc = xf - jnp.mean(xf, axis=-1, keepdims=True)
        rstd = lax.rsqrt(jnp.mean(xc * xc, axis=-1, keepdims=True) + EPS)
        o_ref[...] = _silu(xc * rstd * g_ref[...] + b_ref[...]).astype(ACT)

    return pl.pallas_call(
        body, name=name, grid=(T // tr,), out_shape=jax.ShapeDtypeStruct((T, D), ACT),
        in_specs=[_row(tr, D), _vec(D), _vec(D)], out_specs=_row(tr, D),
        compiler_params=_params(("parallel",)),
    )(x, g, b)


def _ln_silu_bwd(x, g, b, dcf, *, name):
    T, D = x.shape
    tr = _tile(T, 256, 8)

    def body(x_ref, g_ref, b_ref, d_ref, dx_ref, dg_ref, db_ref):
        first = pl.program_id(0) == 0
        xf = x_ref[...].astype(F32)
        xc = xf - jnp.mean(xf, axis=-1, keepdims=True)
        rstd = lax.rsqrt(jnp.mean(xc * xc, axis=-1, keepdims=True) + EPS)
        xhat = xc * rstd
        dyln = d_ref[...].astype(F32) * _dsilu(xhat * g_ref[...] + b_ref[...])
        _acc_rows(dg_ref, dyln * xhat, first)
        _acc_rows(db_ref, dyln, first)
        dxh = dyln * g_ref[...]
        dx = rstd * (dxh - jnp.mean(dxh, axis=-1, keepdims=True)
                     - xhat * jnp.mean(dxh * xhat, axis=-1, keepdims=True))
        dx_ref[...] = dx.astype(ACT)

    vec = jax.ShapeDtypeStruct((1, D), F32)
    return pl.pallas_call(
        body, name=name, grid=(T // tr,), out_shape=(jax.ShapeDtypeStruct((T, D), ACT), vec, vec),
        in_specs=[_row(tr, D), _vec(D), _vec(D), _row(tr, D)],
        out_specs=(_row(tr, D), _vec(D), _vec(D)),
        compiler_params=_params(("arbitrary",)),
    )(x, g, b, dcf)


CONV_CW = 128
CONV_RT = 256
SEQ_PAD = 8


def _window(ext, off, n):
    if off % 8 == 0:
        return ext[off:off + n]
    return pltpu.roll(ext, ext.shape[0] - off, 0)[:n]


def _sum8(v):
    R, C = v.shape
    return jnp.sum(v.reshape(R // 8, 8, C), axis=0)


def _conv5_silu_fwd(x, w, b, *, name):
    T, C = x.shape
    K = 5
    cw, rt = CONV_CW, _tile(T, CONV_RT, 8)
    half = K // 2

    def body(x_ref, w_ref, b_ref, o_ref, pad):
        zeros = jnp.zeros((SEQ_PAD, cw), F32)
        pad[0:SEQ_PAD, :] = zeros
        pad[T + SEQ_PAD:T + 2 * SEQ_PAD, :] = zeros

        def fill(i, c):
            base = pl.multiple_of(i * rt, rt)
            pad[pl.ds(base + SEQ_PAD, rt), :] = x_ref[pl.ds(base, rt), :].astype(F32)
            return c

        lax.fori_loop(0, T // rt, fill, 0)
        wv = w_ref[...]
        bias = b_ref[...]

        def step(i, c):
            base = pl.multiple_of(i * rt, rt)
            ext = pad[pl.ds(base, rt + 2 * SEQ_PAD), :]
            acc = jnp.zeros((rt, cw), F32) + bias
            for k in range(K):
                acc = acc + wv[k:k + 1, :] * _window(ext, SEQ_PAD + k - half, rt)
            o_ref[pl.ds(base, rt), :] = _silu(acc).astype(ACT)
            return c

        lax.fori_loop(0, T // rt, step, 0)

    return pl.pallas_call(
        body, name=name, grid=(C // cw,), out_shape=jax.ShapeDtypeStruct((T, C), ACT),
        in_specs=[pl.BlockSpec((T, cw), lambda j: (0, j)), pl.BlockSpec((8, cw), lambda j: (0, j)),
                  pl.BlockSpec((1, cw), lambda j: (0, j))],
        out_specs=pl.BlockSpec((T, cw), lambda j: (0, j)),
        scratch_shapes=[pltpu.VMEM((T + 2 * SEQ_PAD, cw), F32)],
        compiler_params=_params(("parallel",)),
    )(x, w, b)


def _conv5_silu_bwd(x, w, b, dout, *, name):
    T, C = x.shape
    K = 5
    cw, rt = CONV_CW, _tile(T, CONV_RT, 8)
    half = K // 2

    def body(x_ref, w_ref, b_ref, d_ref, dx_ref, dw_ref, db_ref, pad, dpad, wacc):
        zeros = jnp.zeros((SEQ_PAD, cw), F32)
        for p in (pad, dpad):
            p[0:SEQ_PAD, :] = zeros
            p[T + SEQ_PAD:T + 2 * SEQ_PAD, :] = zeros
        wacc[...] = jnp.zeros_like(wacc)

        def fill(i, c):
            base = pl.multiple_of(i * rt, rt)
            pad[pl.ds(base + SEQ_PAD, rt), :] = x_ref[pl.ds(base, rt), :].astype(F32)
            return c

        lax.fori_loop(0, T // rt, fill, 0)
        wv = w_ref[...]
        bias = b_ref[...]

        def step1(i, c):
            base = pl.multiple_of(i * rt, rt)
            ext = pad[pl.ds(base, rt + 2 * SEQ_PAD), :]
            wins = [_window(ext, SEQ_PAD + k - half, rt) for k in range(K)]
            pre = jnp.zeros((rt, cw), F32) + bias
            for k in range(K):
                pre = pre + wv[k:k + 1, :] * wins[k]
            dpre = d_ref[pl.ds(base, rt), :].astype(F32) * _dsilu(pre)
            dpad[pl.ds(base + SEQ_PAD, rt), :] = dpre
            for k in range(K):
                wacc[k] += _sum8(dpre * wins[k])
            wacc[K] += _sum8(dpre)
            return c

        lax.fori_loop(0, T // rt, step1, 0)

        def step2(i, c):
            base = pl.multiple_of(i * rt, rt)
            ext = dpad[pl.ds(base, rt + 2 * SEQ_PAD), :]
            acc = jnp.zeros((rt, cw), F32)
            for k in range(K):
                acc = acc + wv[k:k + 1, :] * _window(ext, SEQ_PAD - (k - half), rt)
            dx_ref[pl.ds(base, rt), :] = acc.astype(ACT)
            return c

        lax.fori_loop(0, T // rt, step2, 0)
        rows = [jnp.sum(wacc[k], axis=0, keepdims=True) for k in range(K)]
        rows += [jnp.zeros((1, cw), F32)] * (8 - K)
        dw_ref[...] = jnp.concatenate(rows, axis=0)
        db_ref[...] = jnp.sum(wacc[K], axis=0, keepdims=True)

    return pl.pallas_call(
        body, name=name, grid=(C // cw,),
        out_shape=(jax.ShapeDtypeStruct((T, C), ACT), jax.ShapeDtypeStruct((8, C), F32),
                   jax.ShapeDtypeStruct((1, C), F32)),
        in_specs=[pl.BlockSpec((T, cw), lambda j: (0, j)), pl.BlockSpec((8, cw), lambda j: (0, j)),
                  pl.BlockSpec((1, cw), lambda j: (0, j)), pl.BlockSpec((T, cw), lambda j: (0, j))],
        out_specs=(pl.BlockSpec((T, cw), lambda j: (0, j)), pl.BlockSpec((8, cw), lambda j: (0, j)),
                   pl.BlockSpec((1, cw), lambda j: (0, j))),
        scratch_shapes=[pltpu.VMEM((T + 2 * SEQ_PAD, cw), F32), pltpu.VMEM((T + 2 * SEQ_PAD, cw), F32),
                        pltpu.VMEM((K + 1, 8, cw), F32)],
        compiler_params=_params(("parallel",)),
    )(x, w, b, dout)


def _glu_conv_fwd(u, v, w, b, *, name):
    T, C = u.shape
    K = 31
    KP = w.shape[0]
    cw, rt = CONV_CW, _tile(T, CONV_RT, GRID_W)
    half = K // 2
    P = half * GRID_W

    def body(u_ref, v_ref, w_ref, b_ref, o_ref, pad):
        zeros = jnp.zeros((P, cw), F32)
        pad[0:P, :] = zeros
        pad[T + P:T + 2 * P, :] = zeros

        def fill(i, c):
            base = pl.multiple_of(i * rt, rt)
            uf = u_ref[pl.ds(base, rt), :].astype(F32)
            vf = v_ref[pl.ds(base, rt), :].astype(F32)
            pad[pl.ds(base + P, rt), :] = uf * jax.nn.sigmoid(vf)
            return c

        lax.fori_loop(0, T // rt, fill, 0)
        wv = w_ref[...]
        bias = b_ref[...]

        def step(i, c):
            base = pl.multiple_of(i * rt, rt)
            acc = jnp.zeros((rt, cw), F32) + bias
            for k in range(K):
                acc = acc + wv[k:k + 1, :] * pad[pl.ds(base + k * GRID_W, rt), :]
            o_ref[pl.ds(base, rt), :] = acc.astype(ACT)
            return c

        lax.fori_loop(0, T // rt, step, 0)

    col = pl.BlockSpec((T, cw), lambda j: (0, j))
    return pl.pallas_call(
        body, name=name, grid=(C // cw,), out_shape=jax.ShapeDtypeStruct((T, C), ACT),
        in_specs=[col, col, pl.BlockSpec((KP, cw), lambda j: (0, j)), pl.BlockSpec((1, cw), lambda j: (0, j))],
        out_specs=col,
        scratch_shapes=[pltpu.VMEM((T + 2 * P, cw), F32)],
        compiler_params=_params(("parallel",)),
    )(u, v, w, b)


def _glu_conv_bwd(u, v, w, dout, *, name):
    T, C = u.shape
    K = 31
    KP = w.shape[0]
    cw, rt = CONV_CW, _tile(T, CONV_RT, GRID_W)
    half = K // 2
    P = half * GRID_W

    def body(u_ref, v_ref, w_ref, d_ref, du_ref, dv_ref, dw_ref, db_ref, pad, dpad, wacc):
        zeros = jnp.zeros((P, cw), F32)
        for p in (pad, dpad):
            p[0:P, :] = zeros
            p[T + P:T + 2 * P, :] = zeros
        wacc[...] = jnp.zeros_like(wacc)

        def fill(i, c):
            base = pl.multiple_of(i * rt, rt)
            uf = u_ref[pl.ds(base, rt), :].astype(F32)
            vf = v_ref[pl.ds(base, rt), :].astype(F32)
            pad[pl.ds(base + P, rt), :] = uf * jax.nn.sigmoid(vf)
            dpad[pl.ds(base + P, rt), :] = d_ref[pl.ds(base, rt), :].astype(F32)
            return c

        lax.fori_loop(0, T // rt, fill, 0)
        wv = w_ref[...]

        def step(i, c):
            base = pl.multiple_of(i * rt, rt)
            d = dpad[pl.ds(base + P, rt), :]
            dg = jnp.zeros((rt, cw), F32)
            for k in range(K):
                wacc[k] += _sum8(d * pad[pl.ds(base + k * GRID_W, rt), :])
                dg = dg + wv[k:k + 1, :] * dpad[pl.ds(base + (K - 1 - k) * GRID_W, rt), :]
            wacc[K] += _sum8(d)
            uf = u_ref[pl.ds(base, rt), :].astype(F32)
            sv = jax.nn.sigmoid(v_ref[pl.ds(base, rt), :].astype(F32))
            du_ref[pl.ds(base, rt), :] = (dg * sv).astype(ACT)
            dv_ref[pl.ds(base, rt), :] = (dg * uf * sv * (1.0 - sv)).astype(ACT)
            return c

        lax.fori_loop(0, T // rt, step, 0)
        rows = [jnp.sum(wacc[k], axis=0, keepdims=True) for k in range(K)]
        rows += [jnp.zeros((1, cw), F32)] * (KP - K)
        dw_ref[...] = jnp.concatenate(rows, axis=0)
        db_ref[...] = jnp.sum(wacc[K], axis=0, keepdims=True)

    col = pl.BlockSpec((T, cw), lambda j: (0, j))
    shp = jax.ShapeDtypeStruct((T, C), ACT)
    return pl.pallas_call(
        body, name=name, grid=(C // cw,),
        out_shape=(shp, shp, jax.ShapeDtypeStruct((KP, C), F32), jax.ShapeDtypeStruct((1, C), F32)),
        in_specs=[col, col, pl.BlockSpec((KP, cw), lambda j: (0, j)), col],
        out_specs=(col, col, pl.BlockSpec((KP, cw), lambda j: (0, j)), pl.BlockSpec((1, cw), lambda j: (0, j))),
        scratch_shapes=[pltpu.VMEM((T + 2 * P, cw), F32), pltpu.VMEM((T + 2 * P, cw), F32),
                        pltpu.VMEM((K + 1, 8, cw), F32)],
        compiler_params=_params(("parallel",)),
    )(u, v, w, dout)


def _dt_fwd(rawT, bias, *, name):
    H2, T = rawT.shape
    tc = _tile(T, 2048, 128)

    def body(r_ref, b_ref, o_ref):
        v = r_ref[...] + b_ref[...]
        o_ref[...] = jnp.maximum(v, 0.0) + jnp.log(1.0 + jnp.exp(-jnp.abs(v)))

    return pl.pallas_call(
        body, name=name, grid=(T // tc,), out_shape=jax.ShapeDtypeStruct((H2, T), F32),
        in_specs=[pl.BlockSpec((H2, tc), lambda i: (0, i)), pl.BlockSpec((H2, 1), lambda i: (0, 0))],
        out_specs=pl.BlockSpec((H2, tc), lambda i: (0, i)),
        compiler_params=_params(("parallel",)),
    )(rawT, bias)


def _dt_bwd(rawT, bias, ddtT, *, name):
    H2, T = rawT.shape
    tc = _tile(T, 2048, 128)

    def body(r_ref, b_ref, d_ref, o_ref, db_ref):
        first = pl.program_id(0) == 0
        dr = d_ref[...] * jax.nn.sigmoid(r_ref[...] + b_ref[...])
        o_ref[...] = dr
        s = jnp.sum(dr, axis=1, keepdims=True)

        @pl.when(first)
        def _():
            db_ref[...] = s

        @pl.when(jnp.logical_not(first))
        def _():
            db_ref[...] += s

    return pl.pallas_call(
        body, name=name, grid=(T // tc,),
        out_shape=(jax.ShapeDtypeStruct((H2, T), F32), jax.ShapeDtypeStruct((H2, 1), F32)),
        in_specs=[pl.BlockSpec((H2, tc), lambda i: (0, i)), pl.BlockSpec((H2, 1), lambda i: (0, 0)),
                  pl.BlockSpec((H2, tc), lambda i: (0, i))],
        out_specs=(pl.BlockSpec((H2, tc), lambda i: (0, i)), pl.BlockSpec((H2, 1), lambda i: (0, 0))),
        compiler_params=_params(("arbitrary",)),
    )(rawT, bias, ddtT)


def _ssd_common(dtT, a, reverse):
    J, Q = dtT.shape
    li = lax.broadcasted_iota(jnp.int32, (Q, Q), 0)
    si = lax.broadcasted_iota(jnp.int32, (Q, Q), 1)
    mask = (si >= li) if reverse else (si <= li)
    Mf = mask.astype(F32)
    daT = dtT * a
    csT = lax.dot_general(daT, Mf, NT, precision=HIGHEST, preferred_element_type=F32)
    last = 0 if reverse else Q - 1
    totT = csT[:, last:last + 1]
    return mask, Mf, csT, totT, last


def _to_cols(rows):
    R, Q = rows.shape
    if R < 128:
        rows = jnp.concatenate([rows, jnp.zeros((128 - R, Q), F32)], axis=0)
    return rows.T


def _expand(cols, base, lane_head, J):
    out = jnp.zeros(lane_head.shape, F32)
    for j in range(J):
        out = jnp.where(lane_head == j, cols[:, base + j:base + j + 1], out)
    return out


def _ssd_fwd(xbc, dtT, a, h0, *, reverse, name, di):
    T = xbc.shape[0]
    G, JP, N = h0.shape
    J, P, Q = JP // HEAD_DIM, HEAD_DIM, CHUNK
    nc = T // Q
    QW = 256
    HQ = QW // P

    def ci(k):
        return nc - 1 - k if reverse else k

    def body(x_ref, b_ref, c_ref, dt_ref, a_ref, h0_ref, y_ref, hs_ref, hl_ref, h_scr):
        k = pl.program_id(1)

        @pl.when(k == 0)
        def _():
            h_scr[...] = h0_ref[0]

        h = h_scr[...]
        hs_ref[0, 0] = h
        X = x_ref[...].astype(F32)
        Bm, Cm = b_ref[...], c_ref[...]
        dtT_v = dt_ref[...]
        mask, _, csT, totT, _ = _ssd_common(dtT_v, a_ref[...], reverse)
        cols = _to_cols(jnp.concatenate([dtT_v, csT, jnp.exp(csT), dtT_v * jnp.exp(totT - csT)], axis=0))
        lane_head = lax.broadcasted_iota(jnp.int32, (Q, JP), 1) // P
        xdt = (X * _expand(cols, 0, lane_head, J)).astype(ACT)
        CB = lax.dot_general(Cm, Bm, NT, preferred_element_type=F32)
        yo = lax.dot_general(Cm, h.astype(ACT), NT, preferred_element_type=F32) * _expand(cols, 2 * J, lane_head, J)
        lh = lane_head[:, :QW]
        for q in range(JP // QW):
            xq = xdt[:, q * QW:(q + 1) * QW]
            acc = yo[:, q * QW:(q + 1) * QW]
            for jj in range(HQ):
                j = q * HQ + jj
                seg = cols[:, J + j:J + j + 1] - csT[j:j + 1, :]
                Mj = (CB * jnp.exp(jnp.where(mask, seg, NEG))).astype(ACT)
                acc = acc + jnp.dot(Mj, jnp.where(lh == jj, xq, jnp.zeros_like(xq)), preferred_element_type=F32)
            y_ref[:, q * QW:(q + 1) * QW] = acc.astype(ACT)
        xw = (X * _expand(cols, 3 * J, lane_head, J)).astype(ACT)
        upd = lax.dot_general(xw, Bm, TN, preferred_element_type=F32)
        for j in range(J):
            rows = slice(j * P, (j + 1) * P)
            h_scr[rows, :] = h[rows, :] * jnp.exp(totT[j:j + 1, :]) + upd[rows, :]

        @pl.when(k == nc - 1)
        def _():
            hl_ref[0] = h_scr[...]

    nb = N // 128
    return pl.pallas_call(
        body, name=name, grid=(G, nc),
        out_shape=(jax.ShapeDtypeStruct((T, di), ACT), jax.ShapeDtypeStruct((nc, G, JP, N), F32),
                   jax.ShapeDtypeStruct((G, JP, N), F32)),
        in_specs=[pl.BlockSpec((Q, JP), lambda g, k: (ci(k), g)),
                  pl.BlockSpec((Q, N), lambda g, k: (ci(k), di // N + g)),
                  pl.BlockSpec((Q, N), lambda g, k: (ci(k), di // N + G + g)),
                  pl.BlockSpec((J, Q), lambda g, k: (g, ci(k))),
                  pl.BlockSpec((J, 1), lambda g, k: (g, 0)),
                  pl.BlockSpec((1, JP, N), lambda g, k: (g, 0, 0))],
        out_specs=(pl.BlockSpec((Q, JP), lambda g, k: (ci(k), g)),
                   pl.BlockSpec((1, 1, JP, N), lambda g, k: (ci(k), g, 0, 0)),
                   pl.BlockSpec((1, JP, N), lambda g, k: (g, 0, 0))),
        scratch_shapes=[pltpu.VMEM((JP, N), F32)],
        compiler_params=_params(("arbitrary", "arbitrary")),
    )(xbc, xbc, xbc, dtT, a, h0)


def _ssd_bwd(xbc, dtT, a, dy, hs, dh_last, add, *, reverse, name, di):
    T = xbc.shape[0]
    G, JP, N = dh_last.shape
    J, P, Q = JP // HEAD_DIM, HEAD_DIM, CHUNK
    nc = T // Q
    QW = 256
    HQ = QW // P
    has_add = add is not None

    def ci(k):
        return k if reverse else nc - 1 - k

    def body(x_ref, b_ref, c_ref, dt_ref, a_ref, dy_ref, hs_ref, dhl_ref, *rest):
        if has_add:
            adx_ref, adb_ref, adc_ref = rest[:3]
            rest = rest[3:]
        dx_ref, db_ref, dc_ref, ddt_ref, da_ref, dh0_ref, dh_scr = rest
        k = pl.program_id(1)

        @pl.when(k == 0)
        def _():
            dh_scr[...] = dhl_ref[0]
            da_ref[...] = jnp.zeros_like(da_ref)

        dH = dh_scr[...]
        h = hs_ref[0, 0]
        X = x_ref[...].astype(F32)
        Bm, Cm = b_ref[...], c_ref[...]
        dtT_v = dt_ref[...]
        a_v = a_ref[...]
        mask, Mf, csT, totT, last = _ssd_common(dtT_v, a_v, reverse)
        toendT = jnp.exp(totT - csT)
        cols = _to_cols(jnp.concatenate([dtT_v, csT, jnp.exp(csT), toendT], axis=0))
        lane_head = lax.broadcasted_iota(jnp.int32, (Q, JP), 1) // P
        dt_e = _expand(cols, 0, lane_head, J)
        ecs_e = _expand(cols, 2 * J, lane_head, J)
        toend_e = _expand(cols, 3 * J, lane_head, J)
        xdt = (X * dt_e).astype(ACT)
        dYb = dy_ref[...]
        dYf = dYb.astype(F32)
        dYe = (dYf * ecs_e).astype(ACT)
        h_b = h.astype(ACT)
        dH_b = dH.astype(ACT)
        CB = lax.dot_general(Cm, Bm, NT, preferred_element_type=F32)
        dxdt_off = lax.dot_general(Bm, dH_b, NT, preferred_element_type=F32) * toend_e
        dCB = jnp.zeros((Q, Q), F32)
        lh = lane_head[:, :QW]
        lane_q = lax.broadcasted_iota(jnp.int32, (Q, Q), 1)
        sub_j = lax.broadcasted_iota(jnp.int32, (J, Q), 0)
        e_rows = jnp.zeros((Q, Q), F32)
        e_cols = jnp.zeros((J, Q), F32)
        diag = []
        for q in range(JP // QW):
            xq = xdt[:, q * QW:(q + 1) * QW]
            dyq = dYb[:, q * QW:(q + 1) * QW]
            acc = jnp.zeros((Q, QW), F32)
            for jj in range(HQ):
                j = q * HQ + jj
                seg = cols[:, J + j:J + j + 1] - csT[j:j + 1, :]
                L = jnp.exp(jnp.where(mask, seg, NEG))
                Mf_j = CB * L
                dyj = jnp.where(lh == jj, dyq, jnp.zeros_like(dyq))
                acc = acc + lax.dot_general(Mf_j.astype(ACT), dyj, TN, preferred_element_type=F32)
                dM = lax.dot_general(dyj, xq, NT, preferred_element_type=F32)
                dCB = dCB + dM * L
                E = dM * Mf_j
                e_rows = jnp.where(lane_q == j, jnp.sum(E, axis=1, keepdims=True), e_rows)
                e_cols = jnp.where(sub_j == j, jnp.sum(E, axis=0, keepdims=True), e_cols)
            diag.append(acc)
        dxdt = dxdt_off + jnp.concatenate(diag, axis=1)
        dCB_b = dCB.astype(ACT)
        dC = jnp.dot(dCB_b, Bm, preferred_element_type=F32) + jnp.dot(dYe, h_b, preferred_element_type=F32)
        xw = (X * dt_e * toend_e).astype(ACT)
        dB = lax.dot_general(dCB_b, Cm, TN, preferred_element_type=F32) + jnp.dot(xw, dH_b, preferred_element_type=F32)
        dHc = lax.dot_general(dYe, Cm, TN, preferred_element_type=F32)
        for j in range(J):
            rows = slice(j * P, (j + 1) * P)
            dh_scr[rows, :] = dH[rows, :] * jnp.exp(totT[j:j + 1, :]) + dHc[rows, :]
        dh0_ref[0] = dh_scr[...]

        yo = lax.dot_general(Cm, h_b, NT, preferred_element_type=F32) * ecs_e
        Z = jnp.concatenate([dYf * yo, X * dxdt, X * dxdt_off], axis=0)
        sel = (lax.broadcasted_iota(jnp.int32, (JP, 128), 0) // P
               == lax.broadcasted_iota(jnp.int32, (JP, 128), 1)).astype(ACT)
        Zh = Z.astype(ACT)
        Zl = (Z - Zh.astype(F32)).astype(ACT)
        bs = jnp.dot(Zh, sel, preferred_element_type=F32) + jnp.dot(Zl, sel, preferred_element_type=F32)
        RT_ = bs[0:Q].T[0:J]
        UT_ = bs[Q:2 * Q].T[0:J]
        UoT = bs[2 * Q:3 * Q].T[0:J]
        hsum = jnp.sum(jnp.sum((dH * h).reshape(J, P, N), axis=1), axis=1, keepdims=True)
        dtot = jnp.sum(UoT * dtT_v, axis=1, keepdims=True) + jnp.exp(totT) * hsum
        lane = lax.broadcasted_iota(jnp.int32, (J, Q), 1)
        dcsT = e_rows.T[0:J] - e_cols + RT_ - UoT * dtT_v + jnp.where(lane == last, dtot, 0.0)
        ddaT = jnp.dot(dcsT, Mf, precision=HIGHEST, preferred_element_type=F32)
        ddt_ref[...] = ddaT * a_v + UT_
        da_ref[...] += ddaT * dtT_v
        dX = dxdt * dt_e
        if has_add:
            dX = dX + adx_ref[...].astype(F32)
            dB = dB + adb_ref[...].astype(F32)
            dC = dC + adc_ref[...].astype(F32)
        dx_ref[...] = dX.astype(ACT)
        db_ref[...] = dB.astype(ACT)
        dc_ref[...] = dC.astype(ACT)

    xspec = pl.BlockSpec((Q, JP), lambda g, k: (ci(k), g))
    nspec = pl.BlockSpec((Q, N), lambda g, k: (ci(k), g))
    hspec = pl.BlockSpec((1, JP, N), lambda g, k: (g, 0, 0))
    in_specs = [xspec,
                pl.BlockSpec((Q, N), lambda g, k: (ci(k), di // N + g)),
                pl.BlockSpec((Q, N), lambda g, k: (ci(k), di // N + G + g)),
                pl.BlockSpec((J, Q), lambda g, k: (g, ci(k))),
                pl.BlockSpec((J, 1), lambda g, k: (g, 0)),
                xspec,
                pl.BlockSpec((1, 1, JP, N), lambda g, k: (ci(k), g, 0, 0)),
                hspec]
    operands = [xbc, xbc, xbc, dtT, a, dy, hs, dh_last]
    if has_add:
        in_specs += [xspec, nspec, nspec]
        operands += list(add)
    H = G * J
    return pl.pallas_call(
        body, name=name, grid=(G, nc),
        out_shape=(jax.ShapeDtypeStruct((T, di), ACT), jax.ShapeDtypeStruct((T, G * N), ACT),
                   jax.ShapeDtypeStruct((T, G * N), ACT), jax.ShapeDtypeStruct((H, T), F32),
                   jax.ShapeDtypeStruct((H, Q), F32), jax.ShapeDtypeStruct((G, JP, N), F32)),
        in_specs=in_specs,
        out_specs=(xspec, nspec, nspec,
                   pl.BlockSpec((J, Q), lambda g, k: (g, ci(k))),
                   pl.BlockSpec((J, Q), lambda g, k: (g, 0)),
                   hspec),
        scratch_shapes=[pltpu.VMEM((JP, N), F32)],
        compiler_params=_params(("arbitrary", "arbitrary")),
    )(*operands)


def _mod_fwd(craw, w, b, *, name):
    R, D = craw.shape
    NL = w.shape[1]
    tn = _tile(NL, 512, 128)

    def body(c_ref, w_ref, b_ref, o_ref):
        o_ref[...] = jnp.dot(_silu(c_ref[...]), w_ref[...], preferred_element_type=F32) + b_ref[...]

    return pl.pallas_call(
        body, name=name, grid=(NL // tn,), out_shape=jax.ShapeDtypeStruct((R, NL), F32),
        in_specs=[pl.BlockSpec((R, D), lambda j: (0, 0)), pl.BlockSpec((D, tn), lambda j: (0, j)),
                  pl.BlockSpec((1, tn), lambda j: (0, j))],
        out_specs=pl.BlockSpec((R, tn), lambda j: (0, j)),
        compiler_params=_params(("parallel",)),
    )(craw, w, b)


def _mod_bwd(craw, w, dm, *, name):
    R, D = craw.shape
    NL = w.shape[1]
    tn = _tile(NL, 512, 128)

    def body(c_ref, w_ref, dm_ref, dw_ref, dc_ref):
        first = pl.program_id(0) == 0
        cf = c_ref[...]
        dmv = dm_ref[...]
        dw_ref[...] = lax.dot_general(_silu(cf), dmv, TN, preferred_element_type=F32)
        part = lax.dot_general(dmv, w_ref[...], NT, preferred_element_type=F32) * _dsilu(cf)

        @pl.when(first)
        def _():
            dc_ref[...] = part

        @pl.when(jnp.logical_not(first))
        def _():
            dc_ref[...] += part

    return pl.pallas_call(
        body, name=name, grid=(NL // tn,),
        out_shape=(jax.ShapeDtypeStruct((D, NL), F32), jax.ShapeDtypeStruct((R, D), F32)),
        in_specs=[pl.BlockSpec((R, D), lambda j: (0, 0)), pl.BlockSpec((D, tn), lambda j: (0, j)),
                  pl.BlockSpec((R, tn), lambda j: (0, j))],
        out_specs=(pl.BlockSpec((D, tn), lambda j: (0, j)), pl.BlockSpec((R, D), lambda j: (0, 0))),
        compiler_params=_params(("arbitrary",)),
    )(craw, w, dm)


def _pad_rows(a, rows):
    return jnp.concatenate([a, jnp.zeros((rows - a.shape[0],) + a.shape[1:], a.dtype)], axis=0)


def _gather_cols(shard, name):
    g = _exchange(shard.astype(ACT), name=name, gather=True)
    return jnp.transpose(g, (1, 0, 2)).reshape(shard.shape[0], N_DEV * shard.shape[1])


def _gather_rows(shard, name):
    g = _exchange(shard.astype(ACT), name=name, gather=True)
    return g.reshape(N_DEV * shard.shape[0], shard.shape[1])


def _scatter_cols(full, name):
    K, n8 = full.shape
    blocks = jnp.transpose(full.reshape(K, N_DEV, n8 // N_DEV), (1, 0, 2)).astype(ACT)
    return _exchange(blocks, name=name, gather=False)


def _scatter_rows(full, name):
    K8, n = full.shape
    return _exchange(full.reshape(N_DEV, K8 // N_DEV, n).astype(ACT), name=name, gather=False)


def kernel(x, c, ctx, c_ctx, w_mod, b_mod, norm_mix, w_in, ssm_conv_w, ssm_conv_b, dt_bias, a_log, d_skip, ssm_norm, cf_conv_w, cf_conv_b, cf_ln_g, cf_ln_b, w_proj_a, w_proj_b, w_out, norm_ffn, w_ffn_gate, w_ffn_up, w_ffn_down, norm_final, loss_target, m_c_ctx, m_w_mod, m_b_mod, m_norm_mix, m_w_in, m_ssm_conv_w, m_ssm_conv_b, m_dt_bias, m_a_log, m_d_skip, m_ssm_norm, m_cf_conv_w, m_cf_conv_b, m_cf_ln_g, m_cf_ln_b, m_w_proj_a, m_w_proj_b, m_w_out, m_norm_ffn, m_w_ffn_gate, m_w_ffn_up, m_w_ffn_down, m_norm_final, v_c_ctx, v_w_mod, v_b_mod, v_norm_mix, v_w_in, v_ssm_conv_w, v_ssm_conv_b, v_dt_bias, v_a_log, v_d_skip, v_ssm_norm, v_cf_conv_w, v_cf_conv_b, v_cf_ln_g, v_cf_ln_b, v_w_proj_a, v_w_proj_b, v_w_out, v_norm_ffn, v_w_ffn_gate, v_w_ffn_up, v_w_ffn_down, v_norm_final):
    args = dict(locals())
    me = 4 * lax.axis_index("x") + 2 * lax.axis_index("y") + lax.axis_index("c")
    T, D = x.shape[1], x.shape[2]
    DI = ssm_norm.shape[1]
    H = DI // HEAD_DIM
    G, J, N = GROUPS, H // GROUPS, STATE
    JP = J * HEAD_DIM
    GN = G * N
    CONV = DI + 2 * GN
    x0 = x[0]
    ctx0 = ctx[0]
    target = loss_target[0]

    win = _gather_cols(w_in[0], "gather_w_in")
    o_xbc, o_dt, o_glu, o_gates = DI, DI + CONV, DI + CONV + 2 * H, DI + CONV + 2 * H + 2 * D
    w_z, w_xbc, w_dt = win[:, :o_xbc], win[:, o_xbc:o_dt], win[:, o_dt:o_glu]
    w_u, w_v, w_gates = win[:, o_glu:o_glu + D], win[:, o_glu + D:o_gates], win[:, o_gates:]
    w_gate = _gather_cols(w_ffn_gate[0], "gather_w_gate")
    w_up = _gather_cols(w_ffn_up[0], "gather_w_up")
    w_down = _gather_rows(w_ffn_down[0], "gather_w_down")
    w_pa = _gather_rows(w_proj_a[0], "gather_w_pa")
    w_pb = _gather_rows(w_proj_b[0], "gather_w_pb")
    w_o = _gather_rows(w_out[0], "gather_w_out")
    k5 = ssm_conv_w.shape[1]
    k31 = cf_conv_w.shape[1]
    cw5 = _exchange(_pad_rows(ssm_conv_w[0], 8), name="gather_conv5", gather=True)
    cw5 = jnp.transpose(cw5, (1, 0, 2)).reshape(8, CONV)
    cw31 = _exchange(_pad_rows(cf_conv_w[0], 32), name="gather_conv31", gather=True)
    cw31 = jnp.transpose(cw31, (1, 0, 2)).reshape(32, D)

    c_all = _exchange(_pad_rows(c, 8), name="gather_c", gather=True)[:, 0, :]
    craw = jnp.concatenate([c_all, c_ctx[None, :], jnp.zeros((7, D), F32)], axis=0)
    NL = w_mod.shape[2]
    b_loc = lax.dynamic_slice(b_mod, (0, me * NL), (1, NL))
    m_loc = _mod_fwd(craw, w_mod[0], b_loc, name="mod_fwd")
    m_all = jnp.transpose(_exchange(m_loc, name="gather_mod", gather=True), (1, 0, 2)).reshape(16, N_DEV * NL)
    m_me = lax.dynamic_slice(m_all, (me, 0), (1, 6 * D))
    sh1, sc1, g1, sh2, sc2, g2 = [m_me[:, i * D:(i + 1) * D] for i in range(6)]
    csh1, csc1 = m_all[8:9, 0:D], m_all[8:9, D:2 * D]

    a_neg = -jnp.exp(a_log[0])
    a_f, a_b = a_neg[0][:, None], a_neg[1][:, None]
    dtb = dt_bias[0].reshape(2 * H, 1)
    dskip_e = jnp.repeat(d_skip[0], HEAD_DIM)[None, :]

    def front(h, tag, full):
        out = {}
        out["xbc_raw"] = _mm(h, w_xbc, "nn", name="mm_xbc_" + tag, out_dtype=ACT)
        dt_raw = _mm(h, w_dt, "nn", name="mm_dt_" + tag, out_dtype=F32)
        out["rawT"] = dt_raw.T
        if full:
            out["z"] = _mm(h, w_z, "nn", name="mm_z_" + tag, out_dtype=ACT)
            out["u"] = _mm(h, w_u, "nn", name="mm_u_" + tag, out_dtype=ACT)
            out["v"] = _mm(h, w_v, "nn", name="mm_v_" + tag, out_dtype=ACT)
            out["gates"] = _mm(h, w_gates, "nn", name="mm_gates_" + tag, out_dtype=ACT)
        out["xbc"] = _conv5_silu_fwd(out["xbc_raw"], cw5, ssm_conv_b, name="conv5_fwd_" + tag)
        out["dtT"] = _dt_fwd(out["rawT"], dtb, name="dt_fwd_" + tag)
        return out

    hc = _norm_mod_fwd(ctx0, norm_mix, csh1, csc1, name="norm_mod_ctx")
    fc = front(hc, "ctx", False)
    zero_state = jnp.zeros((G, JP, N), F32)
    _, hs_cf, h_f = _ssd_fwd(fc["xbc"], fc["dtT"][:H], a_f, zero_state, reverse=False, name="ssd_fwd_ctx_f", di=DI)
    _, hs_cb, h_b = _ssd_fwd(fc["xbc"], fc["dtT"][H:], a_b, zero_state, reverse=True, name="ssd_fwd_ctx_b", di=DI)

    hx = _norm_mod_fwd(x0, norm_mix, sh1, sc1, name="norm_mod_x")
    fx = front(hx, "x", True)
    y_f, hs_f, _ = _ssd_fwd(fx["xbc"], fx["dtT"][:H], a_f, h_f, reverse=False, name="ssd_fwd_x_f", di=DI)
    y_b, hs_b, _ = _ssd_fwd(fx["xbc"], fx["dtT"][H:], a_b, h_b, reverse=True, name="ssd_fwd_x_b", di=DI)
    ya_in = _gate_norm_fwd(y_f, y_b, fx["xbc"], fx["z"], dskip_e, ssm_norm, name="gate_norm_fwd")
    ya = _mm(ya_in, w_pa, "nn", name="mm_proj_a", out_dtype=ACT)
    conv_out = _glu_conv_fwd(fx["u"], fx["v"], cw31, cf_conv_b, name="glu_conv_fwd")
    cf = _ln_silu_fwd(conv_out, cf_ln_g, cf_ln_b, name="ln_silu_fwd")
    yb = _mm(cf, w_pb, "nn", name="mm_proj_b", out_dtype=ACT)
    merged = _merge_fwd(ya, yb, fx["gates"], name="merge_fwd")
    o_mix = _mm(merged, w_o, "nn", name="mm_out", out_dtype=ACT)

    x1, h2 = _resid_norm_mod_fwd(x0, o_mix, g1, norm_ffn, sh2, sc2, name="resid_norm_mod")
    gate = _mm(h2, w_gate, "nn", name="mm_gate", out_dtype=ACT)
    up = _mm(h2, w_up, "nn", name="mm_up", out_dtype=ACT)
    act = _swiglu_fwd(gate, up, name="swiglu_fwd")
    dn = _mm(act, w_down, "nn", name="mm_down", out_dtype=ACT)

    loss_part, dx2, d_dn, g_norm_final, d_g2 = _final_fwd_bwd(x1, dn, g2, norm_final[None, :], target, name="final")
    loss = lax.psum(loss_part[0, 0], AXES)

    d_act = _mm(d_dn, w_down, "nt", name="mm_d_act", out_dtype=ACT)
    gw_down = _mm(act, d_dn, "tn", name="mm_gw_down", out_dtype=F32)
    d_gate, d_up = _swiglu_bwd(gate, up, d_act, name="swiglu_bwd")
    gw_gate = _mm(h2, d_gate, "tn", name="mm_gw_gate", out_dtype=F32)
    gw_up = _mm(h2, d_up, "tn", name="mm_gw_up", out_dtype=F32)
    d_h2 = _mm(d_gate, w_gate, "nt", name="mm_d_h2_gate", out_dtype=F32)
    d_h2 = _mm(d_up, w_up, "nt", name="mm_d_h2_up", out_dtype=F32, add=d_h2)
    dx1, d_sh2, d_sc2, g_norm_ffn, d_o, d_g1 = _norm_mod_bwd(
        x1, norm_ffn, sc2, d_h2, name="norm_mod_bwd_ffn", dres=dx2, o=o_mix, g=g1)

    d_merged = _mm(d_o, w_o, "nt", name="mm_d_merged", out_dtype=ACT)
    gw_out = _mm(merged, d_o, "tn", name="mm_gw_out", out_dtype=F32)
    d_ya, d_yb, d_gates = _merge_bwd(d_merged, ya, yb, fx["gates"], name="merge_bwd")
    gw_pa = _mm(ya_in, d_ya, "tn", name="mm_gw_pa", out_dtype=F32)
    gw_pb = _mm(cf, d_yb, "tn", name="mm_gw_pb", out_dtype=F32)
    d_ya_in = _mm(d_ya, w_pa, "nt", name="mm_d_ya_in", out_dtype=ACT)
    d_cf = _mm(d_yb, w_pb, "nt", name="mm_d_cf", out_dtype=ACT)
    d_conv, g_ln_g, g_ln_b = _ln_silu_bwd(conv_out, cf_ln_g, cf_ln_b, d_cf, name="ln_silu_bwd")
    d_u, d_v, g_cw31, g_cb31 = _glu_conv_bwd(fx["u"], fx["v"], cw31, d_conv, name="glu_conv_bwd")
    d_y, d_z, dxs_skip, g_ssm_norm, g_dskip_e = _gate_norm_bwd(
        d_ya_in, y_f, y_b, fx["xbc"], fx["z"], dskip_e, ssm_norm, name="gate_norm_bwd")

    zero_bc = jnp.zeros((T, GN), ACT)
    r1 = _ssd_bwd(fx["xbc"], fx["dtT"][:H], a_f, d_y, hs_f, zero_state, (dxs_skip, zero_bc, zero_bc),
                  reverse=False, name="ssd_bwd_x_f", di=DI)
    r2 = _ssd_bwd(fx["xbc"], fx["dtT"][H:], a_b, d_y, hs_b, zero_state, r1[:3],
                  reverse=True, name="ssd_bwd_x_b", di=DI)
    Tc = ctx0.shape[0]
    zero_yc = jnp.zeros((Tc, DI), ACT)
    r3 = _ssd_bwd(fc["xbc"], fc["dtT"][:H], a_f, zero_yc, hs_cf, r1[5], None,
                  reverse=False, name="ssd_bwd_ctx_f", di=DI)
    r4 = _ssd_bwd(fc["xbc"], fc["dtT"][H:], a_b, zero_yc, hs_cb, r2[5], r3[:3],
                  reverse=True, name="ssd_bwd_ctx_b", di=DI)

    def back(f, rf, rb, tag):
        d_xbc = jnp.concatenate([rb[0], rb[1], rb[2]], axis=1)
        d_xbc_raw, g_w5, g_b5 = _conv5_silu_bwd(f["xbc_raw"], cw5, ssm_conv_b, d_xbc, name="conv5_bwd_" + tag)
        ddtT = jnp.concatenate([rf[3], rb[3]], axis=0)
        d_rawT, g_dtb = _dt_bwd(f["rawT"], dtb, ddtT, name="dt_bwd_" + tag)
        g_a = jnp.stack([jnp.sum(rf[4], axis=1), jnp.sum(rb[4], axis=1)])
        return d_xbc_raw, d_rawT.T.astype(ACT), g_w5, g_b5, g_dtb, g_a

    dx_xbc_raw, dx_dt_raw, gx_w5, gx_b5, gx_dtb, gx_a = back(fx, r1, r2, "x")
    dc_xbc_raw, dc_dt_raw, gc_w5, gc_b5, gc_dtb, gc_a = back(fc, r3, r4, "ctx")

    gw_xbc = _mm(hc, dc_xbc_raw, "tn", name="mm_gw_xbc_ctx", out_dtype=F32)
    gw_xbc = _mm(hx, dx_xbc_raw, "tn", name="mm_gw_xbc", out_dtype=F32, add=gw_xbc)
    gw_dt = _mm(hc, dc_dt_raw, "tn", name="mm_gw_dt_ctx", out_dtype=F32)
    gw_dt = _mm(hx, dx_dt_raw, "tn", name="mm_gw_dt", out_dtype=F32, add=gw_dt)
    gw_z = _mm(hx, d_z, "tn", name="mm_gw_z", out_dtype=F32)
    gw_u = _mm(hx, d_u, "tn", name="mm_gw_u", out_dtype=F32)
    gw_v = _mm(hx, d_v, "tn", name="mm_gw_v", out_dtype=F32)
    gw_gates = _mm(hx, d_gates, "tn", name="mm_gw_gates", out_dtype=F32)
    gw_in = jnp.concatenate([gw_z, gw_xbc, gw_dt, gw_u, gw_v, gw_gates], axis=1)

    d_hx = _mm(d_z, w_z, "nt", name="mm_d_hx_z", out_dtype=F32)
    d_hx = _mm(dx_xbc_raw, w_xbc, "nt", name="mm_d_hx_xbc", out_dtype=F32, add=d_hx)
    d_hx = _mm(dx_dt_raw, w_dt, "nt", name="mm_d_hx_dt", out_dtype=F32, add=d_hx)
    d_hx = _mm(d_u, w_u, "nt", name="mm_d_hx_u", out_dtype=F32, add=d_hx)
    d_hx = _mm(d_v, w_v, "nt", name="mm_d_hx_v", out_dtype=F32, add=d_hx)
    d_hx = _mm(d_gates, w_gates, "nt", name="mm_d_hx_gates", out_dtype=F32, add=d_hx)
    grad_x, d_sh1, d_sc1, gx_norm_mix = _norm_mod_bwd(x0, norm_mix, sc1, d_hx, name="norm_mod_bwd_x", dres=dx1)
    d_hc = _mm(dc_xbc_raw, w_xbc, "nt", name="mm_d_hc_xbc", out_dtype=F32)
    d_hc = _mm(dc_dt_raw, w_dt, "nt", name="mm_d_hc_dt", out_dtype=F32, add=d_hc)
    _, d_csh1, d_csc1, gc_norm_mix = _norm_mod_bwd(ctx0, norm_mix, csc1, d_hc, name="norm_mod_bwd_ctx")

    zD = jnp.zeros((1, D), F32)
    dm_me = jnp.concatenate([d_sh1, d_sc1, d_g1, d_sh2, d_sc2, d_g2], axis=1)
    dm_ctx = jnp.concatenate([d_csh1, d_csc1, zD, zD, zD, zD], axis=1)
    rows16 = lax.broadcasted_iota(jnp.int32, (16, 1), 0)
    dm_rows = jnp.where(rows16 == me, dm_me, 0.0) + jnp.where(rows16 == 8, dm_ctx, 0.0)
    dm_sum = _sum_slots(_exchange(dm_rows, name="gather_dm", gather=True), name="sum_dm")
    g_b_mod = _colsum(dm_sum, name="colsum_dm")
    dm_loc = lax.dynamic_slice(dm_sum, (0, me * NL), (16, NL))
    g_w_mod, dcraw = _mod_bwd(craw, w_mod[0], dm_loc, name="mod_bwd")

    small = [
        ("c_ctx", dcraw[8]), ("norm_mix", gx_norm_mix + gc_norm_mix),
        ("ssm_conv_w", (gx_w5 + gc_w5)[:k5]), ("ssm_conv_b", gx_b5 + gc_b5),
        ("dt_bias", gx_dtb + gc_dtb), ("a_log", (gx_a + gc_a) * a_neg),
        ("d_skip", jnp.sum(g_dskip_e.reshape(H, HEAD_DIM), axis=1)), ("ssm_norm", g_ssm_norm),
        ("cf_conv_w", g_cw31[:k31]), ("cf_conv_b", g_cb31), ("cf_ln_g", g_ln_g), ("cf_ln_b", g_ln_b),
        ("norm_ffn", g_norm_ffn), ("norm_final", g_norm_final),
    ]
    flat = jnp.concatenate([v.reshape(-1) for _, v in small])
    n_small = flat.shape[0]
    rows_small = -(-n_small // 1024) * 8
    flat = jnp.concatenate([flat, jnp.zeros((rows_small * 128 - n_small,), F32)]).reshape(rows_small, 128)
    summed = _sum_slots(_exchange(flat, name="gather_small", gather=True), name="sum_small").reshape(-1)
    g_small = {}
    pos = 0
    for nm, v in small:
        g_small[nm] = summed[pos:pos + v.size].reshape(v.shape)
        pos += v.size
    g_small["b_mod"] = g_b_mod
    n5, n31 = ssm_conv_w.shape[2], cf_conv_w.shape[2]
    g_small["ssm_conv_w"] = lax.dynamic_slice(g_small["ssm_conv_w"], (0, me * n5), (k5, n5))
    g_small["cf_conv_w"] = lax.dynamic_slice(g_small["cf_conv_w"], (0, me * n31), (k31, n31))

    grads, deltas, new_m, new_v = {}, {}, {}, {}

    def adam2d(nm, parts):
        shape = args[nm].shape
        R, C = shape[-2], shape[-1]
        g, d, m2, v2 = _adamw(parts, args[nm].reshape(R, C), args["m_" + nm].reshape(R, C),
                              args["v_" + nm].reshape(R, C), name="adamw_" + nm)
        grads[nm], deltas[nm], new_m[nm], new_v[nm] = [t.reshape(shape) for t in (g, d, m2, v2)]

    adam2d("w_mod", g_w_mod[None])
    adam2d("w_in", _scatter_cols(gw_in, "scatter_w_in"))
    adam2d("w_ffn_gate", _scatter_cols(gw_gate, "scatter_w_gate"))
    adam2d("w_ffn_up", _scatter_cols(gw_up, "scatter_w_up"))
    adam2d("w_ffn_down", _scatter_rows(gw_down, "scatter_w_down"))
    adam2d("w_proj_a", _scatter_rows(gw_pa, "scatter_w_pa"))
    adam2d("w_proj_b", _scatter_rows(gw_pb, "scatter_w_pb"))
    adam2d("w_out", _scatter_rows(gw_out, "scatter_w_out"))

    small_names = ["c_ctx", "b_mod", "norm_mix", "ssm_conv_w", "ssm_conv_b", "dt_bias", "a_log", "d_skip", "ssm_norm",
                   "cf_conv_w", "cf_conv_b", "cf_ln_g", "cf_ln_b", "norm_ffn", "norm_final"]

    def pack(vals):
        f = jnp.concatenate([t.reshape(-1) for t in vals])
        rows = -(-f.shape[0] // 1024) * 8
        return jnp.concatenate([f, jnp.zeros((rows * 128 - f.shape[0],), F32)]).reshape(rows, 128)

    pg = pack([g_small[nm] for nm in small_names])
    pw = pack([args[nm] for nm in small_names])
    pm = pack([args["m_" + nm] for nm in small_names])
    pv = pack([args["v_" + nm] for nm in small_names])
    outs = _adamw(pg[None], pw, pm, pv, name="adamw_small")
    pos = 0
    for nm in small_names:
        shape = args[nm].shape
        size = math.prod(shape)
        vals = [t.reshape(-1)[pos:pos + size].reshape(shape) for t in outs]
        grads[nm], deltas[nm], new_m[nm], new_v[nm] = vals
        pos += size

    order = ["c_ctx", "w_mod", "b_mod", "norm_mix", "w_in", "ssm_conv_w", "ssm_conv_b", "dt_bias", "a_log", "d_skip",
             "ssm_norm", "cf_conv_w", "cf_conv_b", "cf_ln_g", "cf_ln_b", "w_proj_a", "w_proj_b", "w_out", "norm_ffn",
             "w_ffn_gate", "w_ffn_up", "w_ffn_down", "norm_final"]
    return (loss, grad_x[None], *[grads[n] for n in order], *[deltas[n] for n in order],
            *[new_m[n] for n in order], *[new_v[n] for n in order])
```

```python
import functools
import math

import jax
import jax.numpy as jnp
from jax import lax
from jax.experimental import pallas as pl
from jax.experimental.pallas import tpu as pltpu

F32 = jnp.float32
ACT = jnp.bfloat16
HIGHEST = lax.Precision.HIGHEST
MESH = pl.DeviceIdType.MESH
AXES = ("x", "y", "c")
N_DEV = 8

GRID_W = 64
CHUNK = 128
HEAD_DIM = 64
GROUPS = 8
STATE = 128
EPS = 1e-6
ADAM_LR = 0.001
ADAM_B1 = 0.9
ADAM_B2 = 0.999
ADAM_EPS = 1e-08
ADAM_WD = 0.01
ADAM_STEP = 10

V7X_VMEM_LIMIT = 56 * 1024 * 1024
NEG = -1e30

NN = (((1,), (0,)), ((), ()))
NT = (((1,), (1,)), ((), ()))
TN = (((0,), (0,)), ((), ()))


def _tile(n, target, quantum):
    best = None
    t = quantum
    while t <= min(n, target):
        if n % t == 0:
            best = t
        t += quantum
    return n if best is None else best


def _params(sem=None):
    kw = dict(vmem_limit_bytes=V7X_VMEM_LIMIT)
    if sem is not None:
        kw["dimension_semantics"] = sem
    return pltpu.CompilerParams(**kw)


def _silu(v):
    return v * jax.nn.sigmoid(v)


def _dsilu(v):
    s = jax.nn.sigmoid(v)
    return s * (1.0 + v * (1.0 - s))


def _exchange(x, *, name, gather):
    shape = x.shape[-2:]

    def body(x_ref, o_ref, send_sems, recv_sems, loc_sem):
        ix, iy, ic = lax.axis_index("x"), lax.axis_index("y"), lax.axis_index("c")
        me = 4 * ix + 2 * iy + ic

        def src(d):
            return x_ref if gather else x_ref.at[d]

        def remote(k, slot, peer_xyz, src_ref):
            return pltpu.make_async_remote_copy(
                src_ref=src_ref, dst_ref=o_ref.at[slot], send_sem=send_sems.at[k], recv_sem=recv_sems.at[k],
                device_id=peer_xyz, device_id_type=MESH)

        local = pltpu.make_async_copy(src(me), o_ref.at[me], loc_sem)
        local.start()
        sends, peers = [], []
        for k in range(1, N_DEV):
            px = 1 - ix if k & 4 else ix
            py = 1 - iy if k & 2 else iy
            pc = 1 - ic if k & 1 else ic
            peer = 4 * px + 2 * py + pc
            cp = remote(k - 1, me, (px, py, pc), src(peer))
            cp.start()
            sends.append(cp)
            peers.append((peer, (px, py, pc)))
        for k in range(1, N_DEV):
            peer, xyz = peers[k - 1]
            remote(k - 1, peer, xyz, src(peer)).wait_recv()
        for cp in sends:
            cp.wait_send()
        local.wait()

    return pl.pallas_call(
        body, name=name,
        out_shape=jax.ShapeDtypeStruct((N_DEV,) + shape, x.dtype),
        in_specs=[pl.BlockSpec(memory_space=pl.ANY)],
        out_specs=pl.BlockSpec(memory_space=pl.ANY),
        scratch_shapes=[pltpu.SemaphoreType.DMA((N_DEV - 1,)), pltpu.SemaphoreType.DMA((N_DEV - 1,)),
                        pltpu.SemaphoreType.DMA],
    )(x)


HBM_SPEC = pl.BlockSpec(memory_space=pltpu.HBM)
SEM_SPEC = pl.BlockSpec(memory_space=pltpu.SEMAPHORE)
ANY_SPEC = pl.BlockSpec(memory_space=pl.ANY)
DATAFLOW = pltpu.SideEffectType.DATAFLOW_SIDE_EFFECTING


def _peer(k):
    ix, iy, ic = lax.axis_index("x"), lax.axis_index("y"), lax.axis_index("c")
    px = 1 - ix if k & 4 else ix
    py = 1 - iy if k & 2 else iy
    pc = 1 - ic if k & 1 else ic
    return (px, py, pc), 4 * px + 2 * py + pc


def _exchange_start(x, after, *, name, gather):
    shape = x.shape[-2:]

    def body(after_ref, x_ref, land_ref, send_sem, recv_sem, x_thru, land_thru, token, loc_sem):
        _, me = _peer(0)

        def src(d):
            return x_ref if gather else x_ref.at[d]

        local = pltpu.make_async_copy(src(me), land_ref.at[me], loc_sem)
        local.start()
        for k in range(1, N_DEV):
            xyz, peer = _peer(k)
            pltpu.make_async_remote_copy(
                src_ref=src(peer), dst_ref=land_ref.at[me], send_sem=send_sem, recv_sem=recv_sem,
                device_id=xyz, device_id_type=MESH).start()
        local.wait()
        token[...] = jnp.zeros_like(token)

    land = lax.empty((N_DEV,) + shape, x.dtype)
    return pl.pallas_call(
        body, name=name,
        out_shape=(pltpu.SemaphoreType.DMA(()), pltpu.SemaphoreType.DMA(()), pltpu.HBM(x.shape, x.dtype),
                   pltpu.HBM((N_DEV,) + shape, x.dtype), jax.ShapeDtypeStruct((8, 128), F32)),
        in_specs=(ANY_SPEC, HBM_SPEC, HBM_SPEC),
        out_specs=(SEM_SPEC, SEM_SPEC, HBM_SPEC, HBM_SPEC, pl.BlockSpec(memory_space=pltpu.VMEM)),
        input_output_aliases={1: 2, 2: 3},
        scratch_shapes=[pltpu.SemaphoreType.DMA],
        compiler_params=pltpu.CompilerParams(has_side_effects=DATAFLOW),
    )(after, pltpu.with_memory_space_constraint(x, pltpu.HBM), pltpu.with_memory_space_constraint(land, pltpu.HBM))


def _exchange_wait(started, after, *, name):
    send_sem, recv_sem, x_thru, land_thru, _ = started

    def body(x_ref, land_ref, send_sem, recv_sem, after_ref, x_dead, got_ref):
        xyz, _ = _peer(0)
        seven = land_ref.at[pl.ds(0, N_DEV - 1)]
        cp = pltpu.make_async_remote_copy(src_ref=seven, dst_ref=seven, send_sem=send_sem, recv_sem=recv_sem,
                                          device_id=xyz, device_id_type=MESH)
        cp.wait_send()
        cp.wait_recv()

    return pl.pallas_call(
        body, name=name,
        out_shape=(pltpu.HBM(x_thru.shape, x_thru.dtype), pltpu.HBM(land_thru.shape, land_thru.dtype)),
        in_specs=(HBM_SPEC, HBM_SPEC, SEM_SPEC, SEM_SPEC, ANY_SPEC),
        out_specs=(HBM_SPEC, HBM_SPEC),
        input_output_aliases={0: 0, 1: 1},
        compiler_params=pltpu.CompilerParams(has_side_effects=DATAFLOW),
    )(x_thru, land_thru, send_sem, recv_sem, after)[1]


def _sum_slots(x, *, name):
    n, R, C = x.shape
    tr = _tile(R, 256, 8)

    def body(x_ref, o_ref):
        acc = x_ref[0].astype(F32)
        for d in range(1, n):
            acc = acc + x_ref[d].astype(F32)
        o_ref[...] = acc

    return pl.pallas_call(
        body, name=name, grid=(R // tr,),
        out_shape=jax.ShapeDtypeStruct((R, C), F32),
        in_specs=[pl.BlockSpec((n, tr, C), lambda i: (0, i, 0))],
        out_specs=pl.BlockSpec((tr, C), lambda i: (i, 0)),
        compiler_params=_params(("parallel",)),
    )(x)


def _colsum(x, *, name):
    R, C = x.shape

    def body(x_ref, o_ref):
        o_ref[...] = jnp.sum(x_ref[...], axis=0, keepdims=True)

    return pl.pallas_call(
        body, name=name, out_shape=jax.ShapeDtypeStruct((1, C), F32),
        in_specs=[pl.BlockSpec((R, C), lambda: (0, 0))], out_specs=pl.BlockSpec((1, C), lambda: (0, 0)),
        compiler_params=_params(),
    )(x)


def _adamw(parts, w, m, v, *, name):
    n, R, C = parts.shape
    tr = _tile(R, 128, 8)
    c1 = 1.0 - ADAM_B1 ** ADAM_STEP
    c2 = 1.0 - ADAM_B2 ** ADAM_STEP

    def body(p_ref, w_ref, m_ref, v_ref, g_ref, d_ref, nm_ref, nv_ref):
        g = p_ref[0].astype(F32)
        for d in range(1, n):
            g = g + p_ref[d].astype(F32)
        mn = ADAM_B1 * m_ref[...] + (1.0 - ADAM_B1) * g
        vn = ADAM_B2 * v_ref[...] + (1.0 - ADAM_B2) * (g * g)
        g_ref[...] = g
        nm_ref[...] = mn
        nv_ref[...] = vn
        d_ref[...] = -ADAM_LR * ((mn / c1) / (jnp.sqrt(vn / c2) + ADAM_EPS) + ADAM_WD * w_ref[...])

    spec = pl.BlockSpec((tr, C), lambda i: (i, 0))
    shp = jax.ShapeDtypeStruct((R, C), F32)
    return pl.pallas_call(
        body, name=name, grid=(R // tr,), out_shape=(shp, shp, shp, shp),
        in_specs=[pl.BlockSpec((n, tr, C), lambda i: (0, i, 0)), spec, spec, spec],
        out_specs=(spec, spec, spec, spec),
        compiler_params=_params(("parallel",)),
    )(parts, w, m, v)


MM_VMEM_BUDGET = 40 * 1024 * 1024
MM_TK_MAX = 2048
MXU_WIDTH = 256


def _divisors(n, quantum, cap):
    return [t for t in range(quantum, min(n, cap) + 1, quantum) if n % t == 0] or [n]


def _mm_tiles(M, N, K, mode, a_bytes, b_bytes, o_bytes, has_add):
    tk = max(_divisors(K, 128, MM_TK_MAX))
    nk = K // tk
    best = None
    for tm in _divisors(M, 128 if mode == "tn" else 8, 1024):
        for tn in _divisors(N, 128, 3072):
            need = 2 * (tm * tk * a_bytes + tk * tn * b_bytes) + 2 * tm * tn * o_bytes + tm * tn * 4
            need += tm * tn * 4 if nk > 1 else 0
            need += 2 * tm * tn * 4 if has_add else 0
            if need > MM_VMEM_BUDGET:
                continue
            score = (tn % MXU_WIDTH == 0 or tn == N, tm * tn, tm)
            if best is None or score > best[0]:
                best = (score, tm, tn)
    assert best is not None, (M, N, K)
    return best[1], best[2], tk


def _mm(a, b, mode, *, name, out_dtype, add=None, after=None):
    if mode == "nn":
        (M, K), (K2, N) = a.shape, b.shape
    elif mode == "nt":
        (M, K), (N, K2) = a.shape, b.shape
    else:
        (K, M), (K2, N) = a.shape, b.shape
    assert K == K2, (name, a.shape, b.shape)
    tm, tn, tk = _mm_tiles(M, N, K, mode, a.dtype.itemsize, b.dtype.itemsize, jnp.dtype(out_dtype).itemsize,
                           add is not None)
    nk = K // tk
    dims = {"nn": NN, "nt": NT, "tn": TN}[mode]

    a_spec = {"nn": pl.BlockSpec((tm, tk), lambda i, j, k: (i, k)),
              "nt": pl.BlockSpec((tm, tk), lambda i, j, k: (i, k)),
              "tn": pl.BlockSpec((tk, tm), lambda i, j, k: (k, i))}[mode]
    b_spec = {"nn": pl.BlockSpec((tk, tn), lambda i, j, k: (k, j)),
              "nt": pl.BlockSpec((tn, tk), lambda i, j, k: (j, k)),
              "tn": pl.BlockSpec((tk, tn), lambda i, j, k: (k, j))}[mode]
    o_spec = pl.BlockSpec((tm, tn), lambda i, j, k: (i, j))

    def body(a_ref, b_ref, *rest):
        rest = list(rest)
        add_ref = rest.pop(0) if add is not None else None
        if after is not None:
            rest.pop(0)
        o_ref = rest.pop(0)
        part = lax.dot_general(a_ref[...].astype(ACT), b_ref[...].astype(ACT), dims, preferred_element_type=F32)

        def finish(r):
            if add is not None:
                r = r + add_ref[...].astype(F32)
            o_ref[...] = r.astype(out_dtype)

        if nk == 1:
            finish(part)
            return
        acc = rest.pop(0)
        k = pl.program_id(2)

        @pl.when(k == 0)
        def _():
            acc[...] = part

        @pl.when(jnp.logical_and(k > 0, k < nk - 1))
        def _():
            acc[...] += part

        @pl.when(k == nk - 1)
        def _():
            finish(acc[...] + part)

    operands = [a, b] + ([] if add is None else [add])
    in_specs = [a_spec, b_spec] + ([] if add is None else [o_spec])
    if after is not None:
        operands.append(after)
        in_specs.append(ANY_SPEC)
    return pl.pallas_call(
        body, name=name, grid=(M // tm, N // tn, nk),
        out_shape=jax.ShapeDtypeStruct((M, N), out_dtype),
        in_specs=in_specs, out_specs=o_spec,
        scratch_shapes=[pltpu.VMEM((tm, tn), F32)] if nk > 1 else [],
        compiler_params=_params(("parallel", "parallel", "arbitrary")),
    )(*operands)


def _row(tr, cols, blk=0):
    return pl.BlockSpec((tr, cols), lambda i: (i, blk))


def _vec(cols):
    return pl.BlockSpec((1, cols), lambda i: (0, 0))


def _rms(xf):
    return lax.rsqrt(jnp.mean(xf * xf, axis=-1, keepdims=True) + EPS)


def _rms_bwd(dxhat, xhat, r):
    return r * (dxhat - xhat * jnp.mean(dxhat * xhat, axis=-1, keepdims=True))


def _acc_rows(ref, val, first):
    s = jnp.sum(val, axis=0, keepdims=True)

    @pl.when(first)
    def _():
        ref[...] = s

    @pl.when(jnp.logical_not(first))
    def _():
        ref[...] += s


def _norm_mod_fwd(x, nw, shift, scale, *, name):
    T, D = x.shape
    tr = _tile(T, 256, 8)

    def body(x_ref, nw_ref, sh_ref, sc_ref, o_ref):
        xf = x_ref[...]
        n = xf * _rms(xf) * nw_ref[...]
        o_ref[...] = (n * (1.0 + sc_ref[...]) + sh_ref[...]).astype(ACT)

    return pl.pallas_call(
        body, name=name, grid=(T // tr,), out_shape=jax.ShapeDtypeStruct((T, D), ACT),
        in_specs=[_row(tr, D), _vec(D), _vec(D), _vec(D)], out_specs=_row(tr, D),
        compiler_params=_params(("parallel",)),
    )(x, nw, shift, scale)


def _resid_norm_mod_fwd(x, o, g, nw, shift, scale, *, name):
    T, D = x.shape
    tr = _tile(T, 256, 8)

    def body(x_ref, o_ref, g_ref, nw_ref, sh_ref, sc_ref, x1_ref, h_ref):
        x1 = x_ref[...] + g_ref[...] * o_ref[...].astype(F32)
        x1_ref[...] = x1
        n = x1 * _rms(x1) * nw_ref[...]
        h_ref[...] = (n * (1.0 + sc_ref[...]) + sh_ref[...]).astype(ACT)

    return pl.pallas_call(
        body, name=name, grid=(T // tr,),
        out_shape=(jax.ShapeDtypeStruct((T, D), F32), jax.ShapeDtypeStruct((T, D), ACT)),
        in_specs=[_row(tr, D), _row(tr, D), _vec(D), _vec(D), _vec(D), _vec(D)],
        out_specs=(_row(tr, D), _row(tr, D)),
        compiler_params=_params(("parallel",)),
    )(x, o, g, nw, shift, scale)


def _final_fwd_bwd(x1, dn, g2, nw, target, *, name):
    T, D = x1.shape
    tr = _tile(T, 256, 8)

    def body(x1_ref, dn_ref, g_ref, nw_ref, t_ref, loss_ref, dx_ref, ddn_ref, dnw_ref, dg_ref):
        first = pl.program_id(0) == 0
        dn_f = dn_ref[...].astype(F32)
        x2 = x1_ref[...] + g_ref[...] * dn_f
        r = _rms(x2)
        xhat = x2 * r
        err = xhat * nw_ref[...] - t_ref[...]
        part = 0.5 * jnp.sum(jnp.mean(err * err, axis=-1, keepdims=True), axis=0, keepdims=True)

        @pl.when(first)
        def _():
            loss_ref[...] = part

        @pl.when(jnp.logical_not(first))
        def _():
            loss_ref[...] += part

        dy = err * (1.0 / D)
        _acc_rows(dnw_ref, dy * xhat, first)
        dx2 = _rms_bwd(dy * nw_ref[...], xhat, r)
        dx_ref[...] = dx2
        ddn_ref[...] = (g_ref[...] * dx2).astype(ACT)
        _acc_rows(dg_ref, dx2 * dn_f, first)

    vec = jax.ShapeDtypeStruct((1, D), F32)
    return pl.pallas_call(
        body, name=name, grid=(T // tr,),
        out_shape=(jax.ShapeDtypeStruct((1, 1), F32), jax.ShapeDtypeStruct((T, D), F32),
                   jax.ShapeDtypeStruct((T, D), ACT), vec, vec),
        in_specs=[_row(tr, D), _row(tr, D), _vec(D), _vec(D), _row(tr, D)],
        out_specs=(pl.BlockSpec((1, 1), lambda i: (0, 0)), _row(tr, D), _row(tr, D), _vec(D), _vec(D)),
        compiler_params=_params(("arbitrary",)),
    )(x1, dn, g2, nw, target)


def _norm_mod_bwd(xin, nw, scale, dh, *, name, dres=None, o=None, g=None):
    T, D = xin.shape
    tr = _tile(T, 256, 8)
    has_res, has_o = dres is not None, o is not None

    def body(*refs):
        refs = list(refs)
        x_ref, nw_ref, sc_ref, dh_ref = refs[:4]
        pos = 4
        dres_ref = o_ref = g_ref = None
        if has_res:
            dres_ref = refs[pos]
            pos += 1
        if has_o:
            o_ref, g_ref = refs[pos], refs[pos + 1]
            pos += 2
        dx_ref, dsh_ref, dsc_ref, dnw_ref = refs[pos:pos + 4]
        pos += 4
        first = pl.program_id(0) == 0
        xf = x_ref[...]
        r = _rms(xf)
        xhat = xf * r
        n = xhat * nw_ref[...]
        dhf = dh_ref[...].astype(F32)
        _acc_rows(dsh_ref, dhf, first)
        _acc_rows(dsc_ref, dhf * n, first)
        dn = dhf * (1.0 + sc_ref[...])
        _acc_rows(dnw_ref, dn * xhat, first)
        dx = _rms_bwd(dn * nw_ref[...], xhat, r)
        if has_res:
            dx = dx + dres_ref[...]
        dx_ref[...] = dx
        if has_o:
            do_ref, dg_ref = refs[pos], refs[pos + 1]
            do_ref[...] = (g_ref[...] * dx).astype(ACT)
            _acc_rows(dg_ref, dx * o_ref[...].astype(F32), first)

    vec = jax.ShapeDtypeStruct((1, D), F32)
    operands = [xin, nw, scale, dh]
    in_specs = [_row(tr, D), _vec(D), _vec(D), _row(tr, D)]
    if has_res:
        operands.append(dres)
        in_specs.append(_row(tr, D))
    if has_o:
        operands += [o, g]
        in_specs += [_row(tr, D), _vec(D)]
    out_shape = [jax.ShapeDtypeStruct((T, D), F32), vec, vec, vec]
    out_specs = [_row(tr, D), _vec(D), _vec(D), _vec(D)]
    if has_o:
        out_shape += [jax.ShapeDtypeStruct((T, D), ACT), vec]
        out_specs += [_row(tr, D), _vec(D)]
    return pl.pallas_call(
        body, name=name, grid=(T // tr,), out_shape=tuple(out_shape),
        in_specs=in_specs, out_specs=tuple(out_specs),
        compiler_params=_params(("arbitrary",)),
    )(*operands)


def _swiglu_fwd(gate, up, *, name):
    T, F = gate.shape
    tr = _tile(T, 256, 8)

    def body(g_ref, u_ref, o_ref):
        o_ref[...] = (_silu(g_ref[...].astype(F32)) * u_ref[...].astype(F32)).astype(ACT)

    return pl.pallas_call(
        body, name=name, grid=(T // tr,), out_shape=jax.ShapeDtypeStruct((T, F), ACT),
        in_specs=[_row(tr, F), _row(tr, F)], out_specs=_row(tr, F),
        compiler_params=_params(("parallel",)),
    )(gate, up)


def _swiglu_bwd(gate, up, dact, *, name):
    T, F = gate.shape
    tr = _tile(T, 256, 8)

    def body(g_ref, u_ref, d_ref, dg_ref, du_ref):
        gf, uf, df = g_ref[...].astype(F32), u_ref[...].astype(F32), d_ref[...].astype(F32)
        dg_ref[...] = (df * uf * _dsilu(gf)).astype(ACT)
        du_ref[...] = (df * _silu(gf)).astype(ACT)

    shp = jax.ShapeDtypeStruct((T, F), ACT)
    return pl.pallas_call(
        body, name=name, grid=(T // tr,), out_shape=(shp, shp),
        in_specs=[_row(tr, F)] * 3, out_specs=(_row(tr, F), _row(tr, F)),
        compiler_params=_params(("parallel",)),
    )(gate, up, dact)


def _merge_fwd(ya, yb, gates, *, name):
    T, D = ya.shape
    tr = _tile(T, 256, 8)

    def body(a_ref, b_ref, g_ref, o_ref):
        ga = g_ref[:, :D].astype(F32)
        gb = g_ref[:, D:].astype(F32)
        o_ref[...] = (jax.nn.sigmoid(ga) * a_ref[...].astype(F32)
                      + jax.nn.sigmoid(gb) * b_ref[...].astype(F32)).astype(ACT)

    return pl.pallas_call(
        body, name=name, grid=(T // tr,), out_shape=jax.ShapeDtypeStruct((T, D), ACT),
        in_specs=[_row(tr, D), _row(tr, D), _row(tr, 2 * D)], out_specs=_row(tr, D),
        compiler_params=_params(("parallel",)),
    )(ya, yb, gates)


def _merge_bwd(dmer, ya, yb, gates, *, name):
    T, D = ya.shape
    tr = _tile(T, 256, 8)

    def body(d_ref, a_ref, b_ref, g_ref, da_ref, db_ref, dg_ref):
        d = d_ref[...].astype(F32)
        sa = jax.nn.sigmoid(g_ref[:, :D].astype(F32))
        sb = jax.nn.sigmoid(g_ref[:, D:].astype(F32))
        da_ref[...] = (d * sa).astype(ACT)
        db_ref[...] = (d * sb).astype(ACT)
        dg_ref[:, :D] = (d * a_ref[...].astype(F32) * sa * (1.0 - sa)).astype(ACT)
        dg_ref[:, D:] = (d * b_ref[...].astype(F32) * sb * (1.0 - sb)).astype(ACT)

    shp = jax.ShapeDtypeStruct((T, D), ACT)
    return pl.pallas_call(
        body, name=name, grid=(T // tr,), out_shape=(shp, shp, jax.ShapeDtypeStruct((T, 2 * D), ACT)),
        in_specs=[_row(tr, D), _row(tr, D), _row(tr, D), _row(tr, 2 * D)],
        out_specs=(_row(tr, D), _row(tr, D), _row(tr, 2 * D)),
        compiler_params=_params(("parallel",)),
    )(dmer, ya, yb, gates)


def _gate_norm_fwd(yf, yb, xbc, z, dskip, nw, *, name):
    T, DI = z.shape
    tr = _tile(T, 128, 8)

    def body(yf_ref, yb_ref, xs_ref, z_ref, ds_ref, nw_ref, o_ref):
        y = yf_ref[...].astype(F32) + yb_ref[...].astype(F32) + ds_ref[...] * xs_ref[...].astype(F32)
        gz = y * _silu(z_ref[...].astype(F32))
        o_ref[...] = (gz * _rms(gz) * nw_ref[...]).astype(ACT)

    return pl.pallas_call(
        body, name=name, grid=(T // tr,), out_shape=jax.ShapeDtypeStruct((T, DI), ACT),
        in_specs=[_row(tr, DI), _row(tr, DI), _row(tr, DI), _row(tr, DI), _vec(DI), _vec(DI)],
        out_specs=_row(tr, DI),
        compiler_params=_params(("parallel",)),
    )(yf, yb, xbc, z, dskip, nw)


def _gate_norm_bwd(dout, yf, yb, xbc, z, dskip, nw, *, name):
    T, DI = z.shape
    tr = _tile(T, 128, 8)

    def body(do_ref, yf_ref, yb_ref, xs_ref, z_ref, ds_ref, nw_ref, dy_ref, dz_ref, dxs_ref, dnw_ref, dds_ref):
        first = pl.program_id(0) == 0
        xs = xs_ref[...].astype(F32)
        zf = z_ref[...].astype(F32)
        y = yf_ref[...].astype(F32) + yb_ref[...].astype(F32) + ds_ref[...] * xs
        sz = _silu(zf)
        gz = y * sz
        r = _rms(gz)
        ghat = gz * r
        do = do_ref[...].astype(F32)
        _acc_rows(dnw_ref, do * ghat, first)
        dgz = _rms_bwd(do * nw_ref[...], ghat, r)
        dy = dgz * sz
        dy_ref[...] = dy.astype(ACT)
        dz_ref[...] = (dgz * y * _dsilu(zf)).astype(ACT)
        dxs_ref[...] = (dy * ds_ref[...]).astype(ACT)
        _acc_rows(dds_ref, dy * xs, first)

    shp = jax.ShapeDtypeStruct((T, DI), ACT)
    vec = jax.ShapeDtypeStruct((1, DI), F32)
    return pl.pallas_call(
        body, name=name, grid=(T // tr,), out_shape=(shp, shp, shp, vec, vec),
        in_specs=[_row(tr, DI)] * 5 + [_vec(DI), _vec(DI)],
        out_specs=(_row(tr, DI), _row(tr, DI), _row(tr, DI), _vec(DI), _vec(DI)),
        compiler_params=_params(("arbitrary",)),
    )(dout, yf, yb, xbc, z, dskip, nw)


def _ln_silu_fwd(x, g, b, *, name):
    T, D = x.shape
    tr = _tile(T, 256, 8)

    def body(x_ref, g_ref, b_ref, o_ref):
        xf = x_ref[...].astype(F32)
        xc = xf - jnp.mean(xf, axis=-1, keepdims=True)
        rstd = lax.rsqrt(jnp.mean(xc * xc, axis=-1, keepdims=True) + EPS)
        o_ref[...] = _silu(xc * rstd * g_ref[...] + b_ref[...]).astype(ACT)

    return pl.pallas_call(
        body, name=name, grid=(T // tr,), out_shape=jax.ShapeDtypeStruct((T, D), ACT),
        in_specs=[_row(tr, D), _vec(D), _vec(D)], out_specs=_row(tr, D),
        compiler_params=_params(("parallel",)),
    )(x, g, b)


def _ln_silu_bwd(x, g, b, dcf, *, name):
    T, D = x.shape
    tr = _tile(T, 256, 8)

    def body(x_ref, g_ref, b_ref, d_ref, dx_ref, dg_ref, db_ref):
        first = pl.program_id(0) == 0
        xf = x_ref[...].astype(F32)
        xc = xf - jnp.mean(xf, axis=-1, keepdims=True)
        rstd = lax.rsqrt(jnp.mean(xc * xc, axis=-1, keepdims=True) + EPS)
        xhat = xc * rstd
        dyln = d_ref[...].astype(F32) * _dsilu(xhat * g_ref[...] + b_ref[...])
        _acc_rows(dg_ref, dyln * xhat, first)
        _acc_rows(db_ref, dyln, first)
        dxh = dyln * g_ref[...]
        dx = rstd * (dxh - jnp.mean(dxh, axis=-1, keepdims=True)
                     - xhat * jnp.mean(dxh * xhat, axis=-1, keepdims=True))
        dx_ref[...] = dx.astype(ACT)

    vec = jax.ShapeDtypeStruct((1, D), F32)
    return pl.pallas_call(
        body, name=name, grid=(T // tr,), out_shape=(jax.ShapeDtypeStruct((T, D), ACT), vec, vec),
        in_specs=[_row(tr, D), _vec(D), _vec(D), _row(tr, D)],
        out_specs=(_row(tr, D), _vec(D), _vec(D)),
        compiler_params=_params(("arbitrary",)),
    )(x, g, b, dcf)


CONV_CW = 128
CONV_RT = 256
SEQ_PAD = 8


def _window(ext, off, n):
    if off % 8 == 0:
        return ext[off:off + n]
    return pltpu.roll(ext, ext.shape[0] - off, 0)[:n]


def _sum8(v):
    R, C = v.shape
    return jnp.sum(v.reshape(R // 8, 8, C), axis=0)


def _conv5_silu_fwd(x, w, b, *, name):
    T, C = x.shape
    K = 5
    cw, rt = CONV_CW, _tile(T, CONV_RT, 8)
    half = K // 2

    def body(x_ref, w_ref, b_ref, o_ref, pad):
        zeros = jnp.zeros((SEQ_PAD, cw), F32)
        pad[0:SEQ_PAD, :] = zeros
        pad[T + SEQ_PAD:T + 2 * SEQ_PAD, :] = zeros

        def fill(i, c):
            base = pl.multiple_of(i * rt, rt)
            pad[pl.ds(base + SEQ_PAD, rt), :] = x_ref[pl.ds(base, rt), :].astype(F32)
            return c

        lax.fori_loop(0, T // rt, fill, 0)
        wv = w_ref[...]
        bias = b_ref[...]

        def step(i, c):
            base = pl.multiple_of(i * rt, rt)
            ext = pad[pl.ds(base, rt + 2 * SEQ_PAD), :]
            acc = jnp.zeros((rt, cw), F32) + bias
            for k in range(K):
                acc = acc + wv[k:k + 1, :] * _window(ext, SEQ_PAD + k - half, rt)
            o_ref[pl.ds(base, rt), :] = _silu(acc).astype(ACT)
            return c

        lax.fori_loop(0, T // rt, step, 0)

    return pl.pallas_call(
        body, name=name, grid=(C // cw,), out_shape=jax.ShapeDtypeStruct((T, C), ACT),
        in_specs=[pl.BlockSpec((T, cw), lambda j: (0, j)), pl.BlockSpec((8, cw), lambda j: (0, j)),
                  pl.BlockSpec((1, cw), lambda j: (0, j))],
        out_specs=pl.BlockSpec((T, cw), lambda j: (0, j)),
        scratch_shapes=[pltpu.VMEM((T + 2 * SEQ_PAD, cw), F32)],
        compiler_params=_params(("parallel",)),
    )(x, w, b)


def _conv5_silu_bwd(x, w, b, dout, *, name):
    T, C = x.shape
    K = 5
    cw, rt = CONV_CW, _tile(T, CONV_RT, 8)
    half = K // 2

    def body(x_ref, w_ref, b_ref, d_ref, dx_ref, dw_ref, db_ref, pad, dpad, wacc):
        zeros = jnp.zeros((SEQ_PAD, cw), F32)
        for p in (pad, dpad):
            p[0:SEQ_PAD, :] = zeros
            p[T + SEQ_PAD:T + 2 * SEQ_PAD, :] = zeros
        wacc[...] = jnp.zeros_like(wacc)

        def fill(i, c):
            base = pl.multiple_of(i * rt, rt)
            pad[pl.ds(base + SEQ_PAD, rt), :] = x_ref[pl.ds(base, rt), :].astype(F32)
            return c

        lax.fori_loop(0, T // rt, fill, 0)
        wv = w_ref[...]
        bias = b_ref[...]

        def step1(i, c):
            base = pl.multiple_of(i * rt, rt)
            ext = pad[pl.ds(base, rt + 2 * SEQ_PAD), :]
            wins = [_window(ext, SEQ_PAD + k - half, rt) for k in range(K)]
            pre = jnp.zeros((rt, cw), F32) + bias
            for k in range(K):
                pre = pre + wv[k:k + 1, :] * wins[k]
            dpre = d_ref[pl.ds(base, rt), :].astype(F32) * _dsilu(pre)
            dpad[pl.ds(base + SEQ_PAD, rt), :] = dpre
            for k in range(K):
                wacc[k] += _sum8(dpre * wins[k])
            wacc[K] += _sum8(dpre)
            return c

        lax.fori_loop(0, T // rt, step1, 0)

        def step2(i, c):
            base = pl.multiple_of(i * rt, rt)
            ext = dpad[pl.ds(base, rt + 2 * SEQ_PAD), :]
            acc = jnp.zeros((rt, cw), F32)
            for k in range(K):
                acc = acc + wv[k:k + 1, :] * _window(ext, SEQ_PAD - (k - half), rt)
            dx_ref[pl.ds(base, rt), :] = acc.astype(ACT)
            return c

        lax.fori_loop(0, T // rt, step2, 0)
        rows = [jnp.sum(wacc[k], axis=0, keepdims=True) for k in range(K)]
        rows += [jnp.zeros((1, cw), F32)] * (8 - K)
        dw_ref[...] = jnp.concatenate(rows, axis=0)
        db_ref[...] = jnp.sum(wacc[K], axis=0, keepdims=True)

    return pl.pallas_call(
        body, name=name, grid=(C // cw,),
        out_shape=(jax.ShapeDtypeStruct((T, C), ACT), jax.ShapeDtypeStruct((8, C), F32),
                   jax.ShapeDtypeStruct((1, C), F32)),
        in_specs=[pl.BlockSpec((T, cw), lambda j: (0, j)), pl.BlockSpec((8, cw), lambda j: (0, j)),
                  pl.BlockSpec((1, cw), lambda j: (0, j)), pl.BlockSpec((T, cw), lambda j: (0, j))],
        out_specs=(pl.BlockSpec((T, cw), lambda j: (0, j)), pl.BlockSpec((8, cw), lambda j: (0, j)),
                   pl.BlockSpec((1, cw), lambda j: (0, j))),
        scratch_shapes=[pltpu.VMEM((T + 2 * SEQ_PAD, cw), F32), pltpu.VMEM((T + 2 * SEQ_PAD, cw), F32),
                        pltpu.VMEM((K + 1, 8, cw), F32)],
        compiler_params=_params(("parallel",)),
    )(x, w, b, dout)


def _glu_conv_fwd(u, v, w, b, *, name):
    T, C = u.shape
    K = 31
    KP = w.shape[0]
    cw, rt = CONV_CW, _tile(T, CONV_RT, GRID_W)
    half = K // 2
    P = half * GRID_W

    def body(u_ref, v_ref, w_ref, b_ref, o_ref, pad):
        zeros = jnp.zeros((P, cw), F32)
        pad[0:P, :] = zeros
        pad[T + P:T + 2 * P, :] = zeros

        def fill(i, c):
            base = pl.multiple_of(i * rt, rt)
            uf = u_ref[pl.ds(base, rt), :].astype(F32)
            vf = v_ref[pl.ds(base, rt), :].astype(F32)
            pad[pl.ds(base + P, rt), :] = uf * jax.nn.sigmoid(vf)
            return c

        lax.fori_loop(0, T // rt, fill, 0)
        wv = w_ref[...]
        bias = b_ref[...]

        def step(i, c):
            base = pl.multiple_of(i * rt, rt)
            acc = jnp.zeros((rt, cw), F32) + bias
            for k in range(K):
                acc = acc + wv[k:k + 1, :] * pad[pl.ds(base + k * GRID_W, rt), :]
            o_ref[pl.ds(base, rt), :] = acc.astype(ACT)
            return c

        lax.fori_loop(0, T // rt, step, 0)

    col = pl.BlockSpec((T, cw), lambda j: (0, j))
    return pl.pallas_call(
        body, name=name, grid=(C // cw,), out_shape=jax.ShapeDtypeStruct((T, C), ACT),
        in_specs=[col, col, pl.BlockSpec((KP, cw), lambda j: (0, j)), pl.BlockSpec((1, cw), lambda j: (0, j))],
        out_specs=col,
        scratch_shapes=[pltpu.VMEM((T + 2 * P, cw), F32)],
        compiler_params=_params(("parallel",)),
    )(u, v, w, b)


def _glu_conv_bwd(u, v, w, dout, *, name):
    T, C = u.shape
    K = 31
    KP = w.shape[0]
    cw, rt = CONV_CW, _tile(T, CONV_RT, GRID_W)
    half = K // 2
    P = half * GRID_W

    def body(u_ref, v_ref, w_ref, d_ref, du_ref, dv_ref, dw_ref, db_ref, pad, dpad, wacc):
        zeros = jnp.zeros((P, cw), F32)
        for p in (pad, dpad):
            p[0:P, :] = zeros
            p[T + P:T + 2 * P, :] = zeros
        wacc[...] = jnp.zeros_like(wacc)

        def fill(i, c):
            base = pl.multiple_of(i * rt, rt)
            uf = u_ref[pl.ds(base, rt), :].astype(F32)
            vf = v_ref[pl.ds(base, rt), :].astype(F32)
            pad[pl.ds(base + P, rt), :] = uf * jax.nn.sigmoid(vf)
            dpad[pl.ds(base + P, rt), :] = d_ref[pl.ds(base, rt), :].astype(F32)
            return c

        lax.fori_loop(0, T // rt, fill, 0)
        wv = w_ref[...]

        def step(i, c):
            base = pl.multiple_of(i * rt, rt)
            d = dpad[pl.ds(base + P, rt), :]
            dg = jnp.zeros((rt, cw), F32)
            for k in range(K):
                wacc[k] += _sum8(d * pad[pl.ds(base + k * GRID_W, rt), :])
                dg = dg + wv[k:k + 1, :] * dpad[pl.ds(base + (K - 1 - k) * GRID_W, rt), :]
            wacc[K] += _sum8(d)
            uf = u_ref[pl.ds(base, rt), :].astype(F32)
            sv = jax.nn.sigmoid(v_ref[pl.ds(base, rt), :].astype(F32))
            du_ref[pl.ds(base, rt), :] = (dg * sv).astype(ACT)
            dv_ref[pl.ds(base, rt), :] = (dg * uf * sv * (1.0 - sv)).astype(ACT)
            return c

        lax.fori_loop(0, T // rt, step, 0)
        rows = [jnp.sum(wacc[k], axis=0, keepdims=True) for k in range(K)]
        rows += [jnp.zeros((1, cw), F32)] * (KP - K)
        dw_ref[...] = jnp.concatenate(rows, axis=0)
        db_ref[...] = jnp.sum(wacc[K], axis=0, keepdims=True)

    col = pl.BlockSpec((T, cw), lambda j: (0, j))
    shp = jax.ShapeDtypeStruct((T, C), ACT)
    return pl.pallas_call(
        body, name=name, grid=(C // cw,),
        out_shape=(shp, shp, jax.ShapeDtypeStruct((KP, C), F32), jax.ShapeDtypeStruct((1, C), F32)),
        in_specs=[col, col, pl.BlockSpec((KP, cw), lambda j: (0, j)), col],
        out_specs=(col, col, pl.BlockSpec((KP, cw), lambda j: (0, j)), pl.BlockSpec((1, cw), lambda j: (0, j))),
        scratch_shapes=[pltpu.VMEM((T + 2 * P, cw), F32), pltpu.VMEM((T + 2 * P, cw), F32),
                        pltpu.VMEM((K + 1, 8, cw), F32)],
        compiler_params=_params(("parallel",)),
    )(u, v, w, dout)


def _dt_fwd(rawT, bias, *, name):
    H2, T = rawT.shape
    tc = _tile(T, 2048, 128)

    def body(r_ref, b_ref, o_ref):
        v = r_ref[...] + b_ref[...]
        o_ref[...] = jnp.maximum(v, 0.0) + jnp.log(1.0 + jnp.exp(-jnp.abs(v)))

    return pl.pallas_call(
        body, name=name, grid=(T // tc,), out_shape=jax.ShapeDtypeStruct((H2, T), F32),
        in_specs=[pl.BlockSpec((H2, tc), lambda i: (0, i)), pl.BlockSpec((H2, 1), lambda i: (0, 0))],
        out_specs=pl.BlockSpec((H2, tc), lambda i: (0, i)),
        compiler_params=_params(("parallel",)),
    )(rawT, bias)


def _dt_bwd(rawT, bias, ddtT, *, name):
    H2, T = rawT.shape
    tc = _tile(T, 2048, 128)

    def body(r_ref, b_ref, d_ref, o_ref, db_ref):
        first = pl.program_id(0) == 0
        dr = d_ref[...] * jax.nn.sigmoid(r_ref[...] + b_ref[...])
        o_ref[...] = dr
        s = jnp.sum(dr, axis=1, keepdims=True)

        @pl.when(first)
        def _():
            db_ref[...] = s

        @pl.when(jnp.logical_not(first))
        def _():
            db_ref[...] += s

    return pl.pallas_call(
        body, name=name, grid=(T // tc,),
        out_shape=(jax.ShapeDtypeStruct((H2, T), F32), jax.ShapeDtypeStruct((H2, 1), F32)),
        in_specs=[pl.BlockSpec((H2, tc), lambda i: (0, i)), pl.BlockSpec((H2, 1), lambda i: (0, 0)),
                  pl.BlockSpec((H2, tc), lambda i: (0, i))],
        out_specs=(pl.BlockSpec((H2, tc), lambda i: (0, i)), pl.BlockSpec((H2, 1), lambda i: (0, 0))),
        compiler_params=_params(("arbitrary",)),
    )(rawT, bias, ddtT)


def _ssd_common(dtT, a, reverse):
    J, Q = dtT.shape
    li = lax.broadcasted_iota(jnp.int32, (Q, Q), 0)
    si = lax.broadcasted_iota(jnp.int32, (Q, Q), 1)
    mask = (si >= li) if reverse else (si <= li)
    Mf = mask.astype(F32)
    daT = dtT * a
    csT = lax.dot_general(daT, Mf, NT, precision=HIGHEST, preferred_element_type=F32)
    last = 0 if reverse else Q - 1
    totT = csT[:, last:last + 1]
    return mask, Mf, csT, totT, last


def _to_cols(rows):
    R, Q = rows.shape
    if R < 128:
        rows = jnp.concatenate([rows, jnp.zeros((128 - R, Q), F32)], axis=0)
    return rows.T


def _expand(cols, base, lane_head, J):
    out = jnp.zeros(lane_head.shape, F32)
    for j in range(J):
        out = jnp.where(lane_head == j, cols[:, base + j:base + j + 1], out)
    return out


def _ssd_fwd(xbc, dtT, a, h0, *, reverse, name, di):
    T = xbc.shape[0]
    G, JP, N = h0.shape
    J, P, Q = JP // HEAD_DIM, HEAD_DIM, CHUNK
    nc = T // Q
    QW = 256
    HQ = QW // P

    def ci(k):
        return nc - 1 - k if reverse else k

    def body(x_ref, b_ref, c_ref, dt_ref, a_ref, h0_ref, y_ref, hs_ref, hl_ref, h_scr):
        k = pl.program_id(1)

        @pl.when(k == 0)
        def _():
            h_scr[...] = h0_ref[0]

        h = h_scr[...]
        hs_ref[0, 0] = h
        X = x_ref[...].astype(F32)
        Bm, Cm = b_ref[...], c_ref[...]
        dtT_v = dt_ref[...]
        mask, _, csT, totT, _ = _ssd_common(dtT_v, a_ref[...], reverse)
        cols = _to_cols(jnp.concatenate([dtT_v, csT, jnp.exp(csT), dtT_v * jnp.exp(totT - csT)], axis=0))
        lane_head = lax.broadcasted_iota(jnp.int32, (Q, JP), 1) // P
        xdt = (X * _expand(cols, 0, lane_head, J)).astype(ACT)
        CB = lax.dot_general(Cm, Bm, NT, preferred_element_type=F32)
        yo = lax.dot_general(Cm, h.astype(ACT), NT, preferred_element_type=F32) * _expand(cols, 2 * J, lane_head, J)
        lh = lane_head[:, :QW]
        for q in range(JP // QW):
            xq = xdt[:, q * QW:(q + 1) * QW]
            acc = yo[:, q * QW:(q + 1) * QW]
            for jj in range(HQ):
                j = q * HQ + jj
                seg = cols[:, J + j:J + j + 1] - csT[j:j + 1, :]
                Mj = (CB * jnp.exp(jnp.where(mask, seg, NEG))).astype(ACT)
                acc = acc + jnp.dot(Mj, jnp.where(lh == jj, xq, jnp.zeros_like(xq)), preferred_element_type=F32)
            y_ref[:, q * QW:(q + 1) * QW] = acc.astype(ACT)
        xw = (X * _expand(cols, 3 * J, lane_head, J)).astype(ACT)
        upd = lax.dot_general(xw, Bm, TN, preferred_element_type=F32)
        for j in range(J):
            rows = slice(j * P, (j + 1) * P)
            h_scr[rows, :] = h[rows, :] * jnp.exp(totT[j:j + 1, :]) + upd[rows, :]

        @pl.when(k == nc - 1)
        def _():
            hl_ref[0] = h_scr[...]

    nb = N // 128
    return pl.pallas_call(
        body, name=name, grid=(G, nc),
        out_shape=(jax.ShapeDtypeStruct((T, di), ACT), jax.ShapeDtypeStruct((nc, G, JP, N), F32),
                   jax.ShapeDtypeStruct((G, JP, N), F32)),
        in_specs=[pl.BlockSpec((Q, JP), lambda g, k: (ci(k), g)),
                  pl.BlockSpec((Q, N), lambda g, k: (ci(k), di // N + g)),
                  pl.BlockSpec((Q, N), lambda g, k: (ci(k), di // N + G + g)),
                  pl.BlockSpec((J, Q), lambda g, k: (g, ci(k))),
                  pl.BlockSpec((J, 1), lambda g, k: (g, 0)),
                  pl.BlockSpec((1, JP, N), lambda g, k: (g, 0, 0))],
        out_specs=(pl.BlockSpec((Q, JP), lambda g, k: (ci(k), g)),
                   pl.BlockSpec((1, 1, JP, N), lambda g, k: (ci(k), g, 0, 0)),
                   pl.BlockSpec((1, JP, N), lambda g, k: (g, 0, 0))),
        scratch_shapes=[pltpu.VMEM((JP, N), F32)],
        compiler_params=_params(("arbitrary", "arbitrary")),
    )(xbc, xbc, xbc, dtT, a, h0)


def _ssd_bwd(xbc, dtT, a, dy, hs, dh_last, add, *, reverse, name, di):
    T = xbc.shape[0]
    G, JP, N = dh_last.shape
    J, P, Q = JP // HEAD_DIM, HEAD_DIM, CHUNK
    nc = T // Q
    QW = 256
    HQ = QW // P
    has_add = add is not None

    def ci(k):
        return k if reverse else nc - 1 - k

    def body(x_ref, b_ref, c_ref, dt_ref, a_ref, dy_ref, hs_ref, dhl_ref, *rest):
        if has_add:
            adx_ref, adb_ref, adc_ref = rest[:3]
            rest = rest[3:]
        dx_ref, db_ref, dc_ref, ddt_ref, da_ref, dh0_ref, dh_scr = rest
        k = pl.program_id(1)

        @pl.when(k == 0)
        def _():
            dh_scr[...] = dhl_ref[0]
            da_ref[...] = jnp.zeros_like(da_ref)

        dH = dh_scr[...]
        h = hs_ref[0, 0]
        X = x_ref[...].astype(F32)
        Bm, Cm = b_ref[...], c_ref[...]
        dtT_v = dt_ref[...]
        a_v = a_ref[...]
        mask, Mf, csT, totT, last = _ssd_common(dtT_v, a_v, reverse)
        toendT = jnp.exp(totT - csT)
        cols = _to_cols(jnp.concatenate([dtT_v, csT, jnp.exp(csT), toendT], axis=0))
        lane_head = lax.broadcasted_iota(jnp.int32, (Q, JP), 1) // P
        dt_e = _expand(cols, 0, lane_head, J)
        ecs_e = _expand(cols, 2 * J, lane_head, J)
        toend_e = _expand(cols, 3 * J, lane_head, J)
        xdt = (X * dt_e).astype(ACT)
        dYb = dy_ref[...]
        dYf = dYb.astype(F32)
        dYe = (dYf * ecs_e).astype(ACT)
        h_b = h.astype(ACT)
        dH_b = dH.astype(ACT)
        CB = lax.dot_general(Cm, Bm, NT, preferred_element_type=F32)
        dxdt_off = lax.dot_general(Bm, dH_b, NT, preferred_element_type=F32) * toend_e
        dCB = jnp.zeros((Q, Q), F32)
        lh = lane_head[:, :QW]
        lane_q = lax.broadcasted_iota(jnp.int32, (Q, Q), 1)
        sub_j = lax.broadcasted_iota(jnp.int32, (J, Q), 0)
        e_rows = jnp.zeros((Q, Q), F32)
        e_cols = jnp.zeros((J, Q), F32)
        diag = []
        for q in range(JP // QW):
            xq = xdt[:, q * QW:(q + 1) * QW]
            dyq = dYb[:, q * QW:(q + 1) * QW]
            acc = jnp.zeros((Q, QW), F32)
            for jj in range(HQ):
                j = q * HQ + jj
                seg = cols[:, J + j:J + j + 1] - csT[j:j + 1, :]
                L = jnp.exp(jnp.where(mask, seg, NEG))
                Mf_j = CB * L
                dyj = jnp.where(lh == jj, dyq, jnp.zeros_like(dyq))
                acc = acc + lax.dot_general(Mf_j.astype(ACT), dyj, TN, preferred_element_type=F32)
                dM = lax.dot_general(dyj, xq, NT, preferred_element_type=F32)
                dCB = dCB + dM * L
                E = dM * Mf_j
                e_rows = jnp.where(lane_q == j, jnp.sum(E, axis=1, keepdims=True), e_rows)
                e_cols = jnp.where(sub_j == j, jnp.sum(E, axis=0, keepdims=True), e_cols)
            diag.append(acc)
        dxdt = dxdt_off + jnp.concatenate(diag, axis=1)
        dCB_b = dCB.astype(ACT)
        dC = jnp.dot(dCB_b, Bm, preferred_element_type=F32) + jnp.dot(dYe, h_b, preferred_element_type=F32)
        xw = (X * dt_e * toend_e).astype(ACT)
        dB = lax.dot_general(dCB_b, Cm, TN, preferred_element_type=F32) + jnp.dot(xw, dH_b, preferred_element_type=F32)
        dHc = lax.dot_general(dYe, Cm, TN, preferred_element_type=F32)
        for j in range(J):
            rows = slice(j * P, (j + 1) * P)
            dh_scr[rows, :] = dH[rows, :] * jnp.exp(totT[j:j + 1, :]) + dHc[rows, :]
        dh0_ref[0] = dh_scr[...]

        yo = lax.dot_general(Cm, h_b, NT, preferred_element_type=F32) * ecs_e
        Z = jnp.concatenate([dYf * yo, X * dxdt, X * dxdt_off], axis=0)
        sel = (lax.broadcasted_iota(jnp.int32, (JP, 128), 0) // P
               == lax.broadcasted_iota(jnp.int32, (JP, 128), 1)).astype(ACT)
        Zh = Z.astype(ACT)
        Zl = (Z - Zh.astype(F32)).astype(ACT)
        bs = jnp.dot(Zh, sel, preferred_element_type=F32) + jnp.dot(Zl, sel, preferred_element_type=F32)
        RT_ = bs[0:Q].T[0:J]
        UT_ = bs[Q:2 * Q].T[0:J]
        UoT = bs[2 * Q:3 * Q].T[0:J]
        hsum = jnp.sum(jnp.sum((dH * h).reshape(J, P, N), axis=1), axis=1, keepdims=True)
        dtot = jnp.sum(UoT * dtT_v, axis=1, keepdims=True) + jnp.exp(totT) * hsum
        lane = lax.broadcasted_iota(jnp.int32, (J, Q), 1)
        dcsT = e_rows.T[0:J] - e_cols + RT_ - UoT * dtT_v + jnp.where(lane == last, dtot, 0.0)
        ddaT = jnp.dot(dcsT, Mf, precision=HIGHEST, preferred_element_type=F32)
        ddt_ref[...] = ddaT * a_v + UT_
        da_ref[...] += ddaT * dtT_v
        dX = dxdt * dt_e
        if has_add:
            dX = dX + adx_ref[...].astype(F32)
            dB = dB + adb_ref[...].astype(F32)
            dC = dC + adc_ref[...].astype(F32)
        dx_ref[...] = dX.astype(ACT)
        db_ref[...] = dB.astype(ACT)
        dc_ref[...] = dC.astype(ACT)

    xspec = pl.BlockSpec((Q, JP), lambda g, k: (ci(k), g))
    nspec = pl.BlockSpec((Q, N), lambda g, k: (ci(k), g))
    hspec = pl.BlockSpec((1, JP, N), lambda g, k: (g, 0, 0))
    in_specs = [xspec,
                pl.BlockSpec((Q, N), lambda g, k: (ci(k), di // N + g)),
                pl.BlockSpec((Q, N), lambda g, k: (ci(k), di // N + G + g)),
                pl.BlockSpec((J, Q), lambda g, k: (g, ci(k))),
                pl.BlockSpec((J, 1), lambda g, k: (g, 0)),
                xspec,
                pl.BlockSpec((1, 1, JP, N), lambda g, k: (ci(k), g, 0, 0)),
                hspec]
    operands = [xbc, xbc, xbc, dtT, a, dy, hs, dh_last]
    if has_add:
        in_specs += [xspec, nspec, nspec]
        operands += list(add)
    H = G * J
    return pl.pallas_call(
        body, name=name, grid=(G, nc),
        out_shape=(jax.ShapeDtypeStruct((T, di), ACT), jax.ShapeDtypeStruct((T, G * N), ACT),
                   jax.ShapeDtypeStruct((T, G * N), ACT), jax.ShapeDtypeStruct((H, T), F32),
                   jax.ShapeDtypeStruct((H, Q), F32), jax.ShapeDtypeStruct((G, JP, N), F32)),
        in_specs=in_specs,
        out_specs=(xspec, nspec, nspec,
                   pl.BlockSpec((J, Q), lambda g, k: (g, ci(k))),
                   pl.BlockSpec((J, Q), lambda g, k: (g, 0)),
                   hspec),
        scratch_shapes=[pltpu.VMEM((JP, N), F32)],
        compiler_params=_params(("arbitrary", "arbitrary")),
    )(*operands)


def _mod_fwd(craw, w, b, *, name):
    R, D = craw.shape
    NL = w.shape[1]
    tn = _tile(NL, 512, 128)

    def body(c_ref, w_ref, b_ref, o_ref):
        o_ref[...] = jnp.dot(_silu(c_ref[...]), w_ref[...], preferred_element_type=F32) + b_ref[...]

    return pl.pallas_call(
        body, name=name, grid=(NL // tn,), out_shape=jax.ShapeDtypeStruct((R, NL), F32),
        in_specs=[pl.BlockSpec((R, D), lambda j: (0, 0)), pl.BlockSpec((D, tn), lambda j: (0, j)),
                  pl.BlockSpec((1, tn), lambda j: (0, j))],
        out_specs=pl.BlockSpec((R, tn), lambda j: (0, j)),
        compiler_params=_params(("parallel",)),
    )(craw, w, b)


def _mod_bwd(craw, w, dm, *, name):
    R, D = craw.shape
    NL = w.shape[1]
    tn = _tile(NL, 512, 128)

    def body(c_ref, w_ref, dm_ref, dw_ref, dc_ref):
        first = pl.program_id(0) == 0
        cf = c_ref[...]
        dmv = dm_ref[...]
        dw_ref[...] = lax.dot_general(_silu(cf), dmv, TN, preferred_element_type=F32)
        part = lax.dot_general(dmv, w_ref[...], NT, preferred_element_type=F32) * _dsilu(cf)

        @pl.when(first)
        def _():
            dc_ref[...] = part

        @pl.when(jnp.logical_not(first))
        def _():
            dc_ref[...] += part

    return pl.pallas_call(
        body, name=name, grid=(NL // tn,),
        out_shape=(jax.ShapeDtypeStruct((D, NL), F32), jax.ShapeDtypeStruct((R, D), F32)),
        in_specs=[pl.BlockSpec((R, D), lambda j: (0, 0)), pl.BlockSpec((D, tn), lambda j: (0, j)),
                  pl.BlockSpec((R, tn), lambda j: (0, j))],
        out_specs=(pl.BlockSpec((D, tn), lambda j: (0, j)), pl.BlockSpec((R, D), lambda j: (0, 0))),
        compiler_params=_params(("arbitrary",)),
    )(craw, w, dm)


def _pad_rows(a, rows):
    return jnp.concatenate([a, jnp.zeros((rows - a.shape[0],) + a.shape[1:], a.dtype)], axis=0)


def _cols_whole(g):
    return jnp.transpose(g, (1, 0, 2)).reshape(g.shape[1], N_DEV * g.shape[2])


def _rows_whole(g):
    return g.reshape(N_DEV * g.shape[1], g.shape[2])


def _col_blocks(full):
    K, n8 = full.shape
    return jnp.transpose(full.reshape(K, N_DEV, n8 // N_DEV), (1, 0, 2)).astype(ACT)


def _row_blocks(full):
    K8, n = full.shape
    return full.reshape(N_DEV, K8 // N_DEV, n).astype(ACT)


def kernel(x, c, ctx, c_ctx, w_mod, b_mod, norm_mix, w_in, ssm_conv_w, ssm_conv_b, dt_bias, a_log, d_skip, ssm_norm, cf_conv_w, cf_conv_b, cf_ln_g, cf_ln_b, w_proj_a, w_proj_b, w_out, norm_ffn, w_ffn_gate, w_ffn_up, w_ffn_down, norm_final, loss_target, m_c_ctx, m_w_mod, m_b_mod, m_norm_mix, m_w_in, m_ssm_conv_w, m_ssm_conv_b, m_dt_bias, m_a_log, m_d_skip, m_ssm_norm, m_cf_conv_w, m_cf_conv_b, m_cf_ln_g, m_cf_ln_b, m_w_proj_a, m_w_proj_b, m_w_out, m_norm_ffn, m_w_ffn_gate, m_w_ffn_up, m_w_ffn_down, m_norm_final, v_c_ctx, v_w_mod, v_b_mod, v_norm_mix, v_w_in, v_ssm_conv_w, v_ssm_conv_b, v_dt_bias, v_a_log, v_d_skip, v_ssm_norm, v_cf_conv_w, v_cf_conv_b, v_cf_ln_g, v_cf_ln_b, v_w_proj_a, v_w_proj_b, v_w_out, v_norm_ffn, v_w_ffn_gate, v_w_ffn_up, v_w_ffn_down, v_norm_final):
    args = dict(locals())
    me = 4 * lax.axis_index("x") + 2 * lax.axis_index("y") + lax.axis_index("c")
    T, D = x.shape[1], x.shape[2]
    DI = ssm_norm.shape[1]
    H = DI // HEAD_DIM
    G, J, N = GROUPS, H // GROUPS, STATE
    JP = J * HEAD_DIM
    GN = G * N
    CONV = DI + 2 * GN
    x0 = x[0]
    ctx0 = ctx[0]
    target = loss_target[0]

    st_in = _exchange_start(w_in[0].astype(ACT), c, name="gather_start_w_in", gather=True)
    k5 = ssm_conv_w.shape[1]
    k31 = cf_conv_w.shape[1]
    cw5 = _exchange(_pad_rows(ssm_conv_w[0], 8), name="gather_conv5", gather=True)
    cw5 = jnp.transpose(cw5, (1, 0, 2)).reshape(8, CONV)
    cw31 = _exchange(_pad_rows(cf_conv_w[0], 32), name="gather_conv31", gather=True)
    cw31 = jnp.transpose(cw31, (1, 0, 2)).reshape(32, D)

    c_all = _exchange(_pad_rows(c, 8), name="gather_c", gather=True)[:, 0, :]
    craw = jnp.concatenate([c_all, c_ctx[None, :], jnp.zeros((7, D), F32)], axis=0)
    NL = w_mod.shape[2]
    b_loc = lax.dynamic_slice(b_mod, (0, me * NL), (1, NL))
    m_loc = _mod_fwd(craw, w_mod[0], b_loc, name="mod_fwd")
    m_all = jnp.transpose(_exchange(m_loc, name="gather_mod", gather=True), (1, 0, 2)).reshape(16, N_DEV * NL)
    m_me = lax.dynamic_slice(m_all, (me, 0), (1, 6 * D))
    sh1, sc1, g1, sh2, sc2, g2 = [m_me[:, i * D:(i + 1) * D] for i in range(6)]
    csh1, csc1 = m_all[8:9, 0:D], m_all[8:9, D:2 * D]

    got_in = _exchange_wait(st_in, m_all, name="gather_wait_w_in")
    win = _cols_whole(got_in)
    o_xbc, o_dt, o_glu, o_gates = DI, DI + CONV, DI + CONV + 2 * H, DI + CONV + 2 * H + 2 * D
    w_z, w_xbc, w_dt = win[:, :o_xbc], win[:, o_xbc:o_dt], win[:, o_dt:o_glu]
    w_u, w_v, w_gates = win[:, o_glu:o_glu + D], win[:, o_glu + D:o_gates], win[:, o_gates:]
    st_pa = _exchange_start(w_proj_a[0].astype(ACT), got_in, name="gather_start_w_pa", gather=True)
    st_pb = _exchange_start(w_proj_b[0].astype(ACT), st_pa[4], name="gather_start_w_pb", gather=True)
    st_o = _exchange_start(w_out[0].astype(ACT), st_pb[4], name="gather_start_w_out", gather=True)
    st_gate = _exchange_start(w_ffn_gate[0].astype(ACT), st_o[4], name="gather_start_w_gate", gather=True)
    st_up = _exchange_start(w_ffn_up[0].astype(ACT), st_gate[4], name="gather_start_w_up", gather=True)
    st_down = _exchange_start(w_ffn_down[0].astype(ACT), st_up[4], name="gather_start_w_down", gather=True)
    started = st_down[4]

    a_neg = -jnp.exp(a_log[0])
    a_f, a_b = a_neg[0][:, None], a_neg[1][:, None]
    dtb = dt_bias[0].reshape(2 * H, 1)
    dskip_e = jnp.repeat(d_skip[0], HEAD_DIM)[None, :]

    def front(h, tag, full, after=None):
        out = {}
        out["xbc_raw"] = _mm(h, w_xbc, "nn", name="mm_xbc_" + tag, out_dtype=ACT, after=after)
        dt_raw = _mm(h, w_dt, "nn", name="mm_dt_" + tag, out_dtype=F32)
        out["rawT"] = dt_raw.T
        if full:
            out["z"] = _mm(h, w_z, "nn", name="mm_z_" + tag, out_dtype=ACT)
            out["u"] = _mm(h, w_u, "nn", name="mm_u_" + tag, out_dtype=ACT)
            out["v"] = _mm(h, w_v, "nn", name="mm_v_" + tag, out_dtype=ACT)
            out["gates"] = _mm(h, w_gates, "nn", name="mm_gates_" + tag, out_dtype=ACT)
        out["xbc"] = _conv5_silu_fwd(out["xbc_raw"], cw5, ssm_conv_b, name="conv5_fwd_" + tag)
        out["dtT"] = _dt_fwd(out["rawT"], dtb, name="dt_fwd_" + tag)
        return out

    hc = _norm_mod_fwd(ctx0, norm_mix, csh1, csc1, name="norm_mod_ctx")
    fc = front(hc, "ctx", False, after=started)
    zero_state = jnp.zeros((G, JP, N), F32)
    _, hs_cf, h_f = _ssd_fwd(fc["xbc"], fc["dtT"][:H], a_f, zero_state, reverse=False, name="ssd_fwd_ctx_f", di=DI)
    _, hs_cb, h_b = _ssd_fwd(fc["xbc"], fc["dtT"][H:], a_b, zero_state, reverse=True, name="ssd_fwd_ctx_b", di=DI)

    hx = _norm_mod_fwd(x0, norm_mix, sh1, sc1, name="norm_mod_x")
    fx = front(hx, "x", True)
    y_f, hs_f, _ = _ssd_fwd(fx["xbc"], fx["dtT"][:H], a_f, h_f, reverse=False, name="ssd_fwd_x_f", di=DI)
    y_b, hs_b, _ = _ssd_fwd(fx["xbc"], fx["dtT"][H:], a_b, h_b, reverse=True, name="ssd_fwd_x_b", di=DI)
    ya_in = _gate_norm_fwd(y_f, y_b, fx["xbc"], fx["z"], dskip_e, ssm_norm, name="gate_norm_fwd")
    w_pa = _rows_whole(_exchange_wait(st_pa, ya_in, name="gather_wait_w_pa"))
    ya = _mm(ya_in, w_pa, "nn", name="mm_proj_a", out_dtype=ACT)
    conv_out = _glu_conv_fwd(fx["u"], fx["v"], cw31, cf_conv_b, name="glu_conv_fwd")
    cf = _ln_silu_fwd(conv_out, cf_ln_g, cf_ln_b, name="ln_silu_fwd")
    w_pb = _rows_whole(_exchange_wait(st_pb, cf, name="gather_wait_w_pb"))
    yb = _mm(cf, w_pb, "nn", name="mm_proj_b", out_dtype=ACT)
    merged = _merge_fwd(ya, yb, fx["gates"], name="merge_fwd")
    w_o = _rows_whole(_exchange_wait(st_o, merged, name="gather_wait_w_out"))
    o_mix = _mm(merged, w_o, "nn", name="mm_out", out_dtype=ACT)

    x1, h2 = _resid_norm_mod_fwd(x0, o_mix, g1, norm_ffn, sh2, sc2, name="resid_norm_mod")
    w_gate = _cols_whole(_exchange_wait(st_gate, h2, name="gather_wait_w_gate"))
    gate = _mm(h2, w_gate, "nn", name="mm_gate", out_dtype=ACT)
    w_up = _cols_whole(_exchange_wait(st_up, gate, name="gather_wait_w_up"))
    up = _mm(h2, w_up, "nn", name="mm_up", out_dtype=ACT)
    act = _swiglu_fwd(gate, up, name="swiglu_fwd")
    w_down = _rows_whole(_exchange_wait(st_down, act, name="gather_wait_w_down"))
    dn = _mm(act, w_down, "nn", name="mm_down", out_dtype=ACT)

    loss_part, dx2, d_dn, g_norm_final, d_g2 = _final_fwd_bwd(x1, dn, g2, norm_final[None, :], target, name="final")
    loss = lax.psum(loss_part[0, 0], AXES)

    d_act = _mm(d_dn, w_down, "nt", name="mm_d_act", out_dtype=ACT)
    gw_down = _mm(act, d_dn, "tn", name="mm_gw_down", out_dtype=F32)
    sc_down = _exchange_start(_row_blocks(gw_down), gw_down, name="scatter_start_w_down", gather=False)
    d_gate, d_up = _swiglu_bwd(gate, up, d_act, name="swiglu_bwd")
    gw_gate = _mm(h2, d_gate, "tn", name="mm_gw_gate", out_dtype=F32, after=sc_down[4])
    sc_gate = _exchange_start(_col_blocks(gw_gate), sc_down[4], name="scatter_start_w_gate", gather=False)
    gw_up = _mm(h2, d_up, "tn", name="mm_gw_up", out_dtype=F32, after=sc_gate[4])
    sc_up = _exchange_start(_col_blocks(gw_up), sc_gate[4], name="scatter_start_w_up", gather=False)
    d_h2 = _mm(d_gate, w_gate, "nt", name="mm_d_h2_gate", out_dtype=F32, after=sc_up[4])
    d_h2 = _mm(d_up, w_up, "nt", name="mm_d_h2_up", out_dtype=F32, add=d_h2)
    dx1, d_sh2, d_sc2, g_norm_ffn, d_o, d_g1 = _norm_mod_bwd(
        x1, norm_ffn, sc2, d_h2, name="norm_mod_bwd_ffn", dres=dx2, o=o_mix, g=g1)

    d_merged = _mm(d_o, w_o, "nt", name="mm_d_merged", out_dtype=ACT)
    gw_out = _mm(merged, d_o, "tn", name="mm_gw_out", out_dtype=F32)
    sc_out = _exchange_start(_row_blocks(gw_out), sc_up[4], name="scatter_start_w_out", gather=False)
    d_ya, d_yb, d_gates = _merge_bwd(d_merged, ya, yb, fx["gates"], name="merge_bwd")
    gw_pa = _mm(ya_in, d_ya, "tn", name="mm_gw_pa", out_dtype=F32, after=sc_out[4])
    sc_pa = _exchange_start(_row_blocks(gw_pa), sc_out[4], name="scatter_start_w_pa", gather=False)
    gw_pb = _mm(cf, d_yb, "tn", name="mm_gw_pb", out_dtype=F32, after=sc_pa[4])
    sc_pb = _exchange_start(_row_blocks(gw_pb), sc_pa[4], name="scatter_start_w_pb", gather=False)
    d_ya_in = _mm(d_ya, w_pa, "nt", name="mm_d_ya_in", out_dtype=ACT, after=sc_pb[4])
    d_cf = _mm(d_yb, w_pb, "nt", name="mm_d_cf", out_dtype=ACT)
    d_conv, g_ln_g, g_ln_b = _ln_silu_bwd(conv_out, cf_ln_g, cf_ln_b, d_cf, name="ln_silu_bwd")
    d_u, d_v, g_cw31, g_cb31 = _glu_conv_bwd(fx["u"], fx["v"], cw31, d_conv, name="glu_conv_bwd")
    d_y, d_z, dxs_skip, g_ssm_norm, g_dskip_e = _gate_norm_bwd(
        d_ya_in, y_f, y_b, fx["xbc"], fx["z"], dskip_e, ssm_norm, name="gate_norm_bwd")

    zero_bc = jnp.zeros((T, GN), ACT)
    r1 = _ssd_bwd(fx["xbc"], fx["dtT"][:H], a_f, d_y, hs_f, zero_state, (dxs_skip, zero_bc, zero_bc),
                  reverse=False, name="ssd_bwd_x_f", di=DI)
    r2 = _ssd_bwd(fx["xbc"], fx["dtT"][H:], a_b, d_y, hs_b, zero_state, r1[:3],
                  reverse=True, name="ssd_bwd_x_b", di=DI)
    Tc = ctx0.shape[0]
    zero_yc = jnp.zeros((Tc, DI), ACT)
    r3 = _ssd_bwd(fc["xbc"], fc["dtT"][:H], a_f, zero_yc, hs_cf, r1[5], None,
                  reverse=False, name="ssd_bwd_ctx_f", di=DI)
    r4 = _ssd_bwd(fc["xbc"], fc["dtT"][H:], a_b, zero_yc, hs_cb, r2[5], r3[:3],
                  reverse=True, name="ssd_bwd_ctx_b", di=DI)

    def back(f, rf, rb, tag):
        d_xbc = jnp.concatenate([rb[0], rb[1], rb[2]], axis=1)
        d_xbc_raw, g_w5, g_b5 = _conv5_silu_bwd(f["xbc_raw"], cw5, ssm_conv_b, d_xbc, name="conv5_bwd_" + tag)
        ddtT = jnp.concatenate([rf[3], rb[3]], axis=0)
        d_rawT, g_dtb = _dt_bwd(f["rawT"], dtb, ddtT, name="dt_bwd_" + tag)
        g_a = jnp.stack([jnp.sum(rf[4], axis=1), jnp.sum(rb[4], axis=1)])
        return d_xbc_raw, d_rawT.T.astype(ACT), g_w5, g_b5, g_dtb, g_a

    dx_xbc_raw, dx_dt_raw, gx_w5, gx_b5, gx_dtb, gx_a = back(fx, r1, r2, "x")
    dc_xbc_raw, dc_dt_raw, gc_w5, gc_b5, gc_dtb, gc_a = back(fc, r3, r4, "ctx")

    gw_xbc = _mm(hc, dc_xbc_raw, "tn", name="mm_gw_xbc_ctx", out_dtype=F32)
    gw_xbc = _mm(hx, dx_xbc_raw, "tn", name="mm_gw_xbc", out_dtype=F32, add=gw_xbc)
    gw_dt = _mm(hc, dc_dt_raw, "tn", name="mm_gw_dt_ctx", out_dtype=F32)
    gw_dt = _mm(hx, dx_dt_raw, "tn", name="mm_gw_dt", out_dtype=F32, add=gw_dt)
    gw_z = _mm(hx, d_z, "tn", name="mm_gw_z", out_dtype=F32)
    gw_u = _mm(hx, d_u, "tn", name="mm_gw_u", out_dtype=F32)
    gw_v = _mm(hx, d_v, "tn", name="mm_gw_v", out_dtype=F32)
    gw_gates = _mm(hx, d_gates, "tn", name="mm_gw_gates", out_dtype=F32)
    gw_in = jnp.concatenate([gw_z, gw_xbc, gw_dt, gw_u, gw_v, gw_gates], axis=1)
    sc_in = _exchange_start(_col_blocks(gw_in), sc_pb[4], name="scatter_start_w_in", gather=False)

    d_hx = _mm(d_z, w_z, "nt", name="mm_d_hx_z", out_dtype=F32, after=sc_in[4])
    d_hx = _mm(dx_xbc_raw, w_xbc, "nt", name="mm_d_hx_xbc", out_dtype=F32, add=d_hx)
    d_hx = _mm(dx_dt_raw, w_dt, "nt", name="mm_d_hx_dt", out_dtype=F32, add=d_hx)
    d_hx = _mm(d_u, w_u, "nt", name="mm_d_hx_u", out_dtype=F32, add=d_hx)
    d_hx = _mm(d_v, w_v, "nt", name="mm_d_hx_v", out_dtype=F32, add=d_hx)
    d_hx = _mm(d_gates, w_gates, "nt", name="mm_d_hx_gates", out_dtype=F32, add=d_hx)
    grad_x, d_sh1, d_sc1, gx_norm_mix = _norm_mod_bwd(x0, norm_mix, sc1, d_hx, name="norm_mod_bwd_x", dres=dx1)
    d_hc = _mm(dc_xbc_raw, w_xbc, "nt", name="mm_d_hc_xbc", out_dtype=F32)
    d_hc = _mm(dc_dt_raw, w_dt, "nt", name="mm_d_hc_dt", out_dtype=F32, add=d_hc)
    _, d_csh1, d_csc1, gc_norm_mix = _norm_mod_bwd(ctx0, norm_mix, csc1, d_hc, name="norm_mod_bwd_ctx")

    zD = jnp.zeros((1, D), F32)
    dm_me = jnp.concatenate([d_sh1, d_sc1, d_g1, d_sh2, d_sc2, d_g2], axis=1)
    dm_ctx = jnp.concatenate([d_csh1, d_csc1, zD, zD, zD, zD], axis=1)
    rows16 = lax.broadcasted_iota(jnp.int32, (16, 1), 0)
    dm_rows = jnp.where(rows16 == me, dm_me, 0.0) + jnp.where(rows16 == 8, dm_ctx, 0.0)
    dm_sum = _sum_slots(_exchange(dm_rows, name="gather_dm", gather=True), name="sum_dm")
    g_b_mod = _colsum(dm_sum, name="colsum_dm")
    dm_loc = lax.dynamic_slice(dm_sum, (0, me * NL), (16, NL))
    g_w_mod, dcraw = _mod_bwd(craw, w_mod[0], dm_loc, name="mod_bwd")

    small = [
        ("c_ctx", dcraw[8]), ("norm_mix", gx_norm_mix + gc_norm_mix),
        ("ssm_conv_w", (gx_w5 + gc_w5)[:k5]), ("ssm_conv_b", gx_b5 + gc_b5),
        ("dt_bias", gx_dtb + gc_dtb), ("a_log", (gx_a + gc_a) * a_neg),
        ("d_skip", jnp.sum(g_dskip_e.reshape(H, HEAD_DIM), axis=1)), ("ssm_norm", g_ssm_norm),
        ("cf_conv_w", g_cw31[:k31]), ("cf_conv_b", g_cb31), ("cf_ln_g", g_ln_g), ("cf_ln_b", g_ln_b),
        ("norm_ffn", g_norm_ffn), ("norm_final", g_norm_final),
    ]
    flat = jnp.concatenate([v.reshape(-1) for _, v in small])
    n_small = flat.shape[0]
    rows_small = -(-n_small // 1024) * 8
    flat = jnp.concatenate([flat, jnp.zeros((rows_small * 128 - n_small,), F32)]).reshape(rows_small, 128)
    summed = _sum_slots(_exchange(flat, name="gather_small", gather=True), name="sum_small").reshape(-1)
    g_small = {}
    pos = 0
    for nm, v in small:
        g_small[nm] = summed[pos:pos + v.size].reshape(v.shape)
        pos += v.size
    g_small["b_mod"] = g_b_mod
    n5, n31 = ssm_conv_w.shape[2], cf_conv_w.shape[2]
    g_small["ssm_conv_w"] = lax.dynamic_slice(g_small["ssm_conv_w"], (0, me * n5), (k5, n5))
    g_small["cf_conv_w"] = lax.dynamic_slice(g_small["cf_conv_w"], (0, me * n31), (k31, n31))

    grads, deltas, new_m, new_v = {}, {}, {}, {}

    def adam2d(nm, parts):
        shape = args[nm].shape
        R, C = shape[-2], shape[-1]
        g, d, m2, v2 = _adamw(parts, args[nm].reshape(R, C), args["m_" + nm].reshape(R, C),
                              args["v_" + nm].reshape(R, C), name="adamw_" + nm)
        grads[nm], deltas[nm], new_m[nm], new_v[nm] = [t.reshape(shape) for t in (g, d, m2, v2)]

    adam2d("w_mod", g_w_mod[None])
    behind = grad_x
    for nm, st in (("w_ffn_down", sc_down), ("w_ffn_gate", sc_gate), ("w_ffn_up", sc_up), ("w_out", sc_out),
                   ("w_proj_a", sc_pa), ("w_proj_b", sc_pb), ("w_in", sc_in)):
        parts = _exchange_wait(st, behind, name="scatter_wait_" + nm)
        adam2d(nm, parts)
        behind = grads[nm]

    small_names = ["c_ctx", "b_mod", "norm_mix", "ssm_conv_w", "ssm_conv_b", "dt_bias", "a_log", "d_skip", "ssm_norm",
                   "cf_conv_w", "cf_conv_b", "cf_ln_g", "cf_ln_b", "norm_ffn", "norm_final"]

    def pack(vals):
        f = jnp.concatenate([t.reshape(-1) for t in vals])
        rows = -(-f.shape[0] // 1024) * 8
        return jnp.concatenate([f, jnp.zeros((rows * 128 - f.shape[0],), F32)]).reshape(rows, 128)

    pg = pack([g_small[nm] for nm in small_names])
    pw = pack([args[nm] for nm in small_names])
    pm = pack([args["m_" + nm] for nm in small_names])
    pv = pack([args["v_" + nm] for nm in small_names])
    outs = _adamw(pg[None], pw, pm, pv, name="adamw_small")
    pos = 0
    for nm in small_names:
        shape = args[nm].shape
        size = math.prod(shape)
        vals = [t.reshape(-1)[pos:pos + size].reshape(shape) for t in outs]
        grads[nm], deltas[nm], new_m[nm], new_v[nm] = vals
        pos += size

    order = ["c_ctx", "w_mod", "b_mod", "norm_mix", "w_in", "ssm_conv_w", "ssm_conv_b", "dt_bias", "a_log", "d_skip",
             "ssm_norm", "cf_conv_w", "cf_conv_b", "cf_ln_g", "cf_ln_b", "w_proj_a", "w_proj_b", "w_out", "norm_ffn",
             "w_ffn_gate", "w_ffn_up", "w_ffn_down", "norm_final"]
    return (loss, grad_x[None], *[grads[n] for n in order], *[deltas[n] for n in order],
            *[new_m[n] for n in order], *[new_v[n] for n in order])
```

```python
import functools
import math

import jax
import jax.numpy as jnp
from jax import lax
from jax.experimental import pallas as pl
from jax.experimental.pallas import tpu as pltpu

F32 = jnp.float32
ACT = jnp.bfloat16
HIGHEST = lax.Precision.HIGHEST
MESH = pl.DeviceIdType.MESH
AXES = ("x", "y", "c")
N_DEV = 8

GRID_W = 64
CHUNK = 128
SSD_GROUPS_PER_STEP = 2
HEAD_DIM = 64
GROUPS = 8
STATE = 128
EPS = 1e-6
ADAM_LR = 0.001
ADAM_B1 = 0.9
ADAM_B2 = 0.999
ADAM_EPS = 1e-08
ADAM_WD = 0.01
ADAM_STEP = 10

V7X_VMEM_LIMIT = 56 * 1024 * 1024
NEG = -1e30

NN = (((1,), (0,)), ((), ()))
NT = (((1,), (1,)), ((), ()))
TN = (((0,), (0,)), ((), ()))


def _tile(n, target, quantum):
    best = None
    t = quantum
    while t <= min(n, target):
        if n % t == 0:
            best = t
        t += quantum
    return n if best is None else best


def _params(sem=None):
    kw = dict(vmem_limit_bytes=V7X_VMEM_LIMIT)
    if sem is not None:
        kw["dimension_semantics"] = sem
    return pltpu.CompilerParams(**kw)


def _silu(v):
    return v * jax.nn.sigmoid(v)


def _dsilu(v):
    s = jax.nn.sigmoid(v)
    return s * (1.0 + v * (1.0 - s))


def _exchange(x, *, name, gather):
    shape = x.shape[-2:]

    def body(x_ref, o_ref, send_sems, recv_sems, loc_sem):
        ix, iy, ic = lax.axis_index("x"), lax.axis_index("y"), lax.axis_index("c")
        me = 4 * ix + 2 * iy + ic

        def src(d):
            return x_ref if gather else x_ref.at[d]

        def remote(k, slot, peer_xyz, src_ref):
            return pltpu.make_async_remote_copy(
                src_ref=src_ref, dst_ref=o_ref.at[slot], send_sem=send_sems.at[k], recv_sem=recv_sems.at[k],
                device_id=peer_xyz, device_id_type=MESH)

        local = pltpu.make_async_copy(src(me), o_ref.at[me], loc_sem)
        local.start()
        sends, peers = [], []
        for k in range(1, N_DEV):
            px = 1 - ix if k & 4 else ix
            py = 1 - iy if k & 2 else iy
            pc = 1 - ic if k & 1 else ic
            peer = 4 * px + 2 * py + pc
            cp = remote(k - 1, me, (px, py, pc), src(peer))
            cp.start()
            sends.append(cp)
            peers.append((peer, (px, py, pc)))
        for k in range(1, N_DEV):
            peer, xyz = peers[k - 1]
            remote(k - 1, peer, xyz, src(peer)).wait_recv()
        for cp in sends:
            cp.wait_send()
        local.wait()

    return pl.pallas_call(
        body, name=name,
        out_shape=jax.ShapeDtypeStruct((N_DEV,) + shape, x.dtype),
        in_specs=[pl.BlockSpec(memory_space=pl.ANY)],
        out_specs=pl.BlockSpec(memory_space=pl.ANY),
        scratch_shapes=[pltpu.SemaphoreType.DMA((N_DEV - 1,)), pltpu.SemaphoreType.DMA((N_DEV - 1,)),
                        pltpu.SemaphoreType.DMA],
    )(x)


HBM_SPEC = pl.BlockSpec(memory_space=pltpu.HBM)
SEM_SPEC = pl.BlockSpec(memory_space=pltpu.SEMAPHORE)
ANY_SPEC = pl.BlockSpec(memory_space=pl.ANY)
DATAFLOW = pltpu.SideEffectType.DATAFLOW_SIDE_EFFECTING


def _peer(k):
    ix, iy, ic = lax.axis_index("x"), lax.axis_index("y"), lax.axis_index("c")
    px = 1 - ix if k & 4 else ix
    py = 1 - iy if k & 2 else iy
    pc = 1 - ic if k & 1 else ic
    return (px, py, pc), 4 * px + 2 * py + pc


def _exchange_start(x, after, *, name, gather):
    shape = x.shape[-2:]

    def body(after_ref, x_ref, land_ref, send_sem, recv_sem, x_thru, land_thru, token, loc_sem):
        _, me = _peer(0)

        def src(d):
            return x_ref if gather else x_ref.at[d]

        local = pltpu.make_async_copy(src(me), land_ref.at[me], loc_sem)
        local.start()
        local.wait()
        for k in range(1, N_DEV):
            xyz, peer = _peer(k)
            pltpu.make_async_remote_copy(
                src_ref=src(peer), dst_ref=land_ref.at[me], send_sem=send_sem, recv_sem=recv_sem,
                device_id=xyz, device_id_type=MESH).start()
        token[...] = jnp.zeros_like(token)

    land = lax.empty((N_DEV,) + shape, x.dtype)
    return pl.pallas_call(
        body, name=name,
        out_shape=(pltpu.SemaphoreType.DMA(()), pltpu.SemaphoreType.DMA(()), pltpu.HBM(x.shape, x.dtype),
                   pltpu.HBM((N_DEV,) + shape, x.dtype), jax.ShapeDtypeStruct((8, 128), F32)),
        in_specs=(ANY_SPEC, HBM_SPEC, HBM_SPEC),
        out_specs=(SEM_SPEC, SEM_SPEC, HBM_SPEC, HBM_SPEC, pl.BlockSpec(memory_space=pltpu.VMEM)),
        input_output_aliases={1: 2, 2: 3},
        scratch_shapes=[pltpu.SemaphoreType.DMA],
        compiler_params=pltpu.CompilerParams(has_side_effects=DATAFLOW),
    )(after, pltpu.with_memory_space_constraint(x, pltpu.HBM), pltpu.with_memory_space_constraint(land, pltpu.HBM))


def _exchange_wait(started, after, *, name):
    send_sem, recv_sem, x_thru, land_thru, _ = started

    def body(x_ref, land_ref, send_sem, recv_sem, after_ref, x_dead, got_ref):
        xyz, _ = _peer(0)
        seven = land_ref.at[pl.ds(0, N_DEV - 1)]
        cp = pltpu.make_async_remote_copy(src_ref=seven, dst_ref=seven, send_sem=send_sem, recv_sem=recv_sem,
                                          device_id=xyz, device_id_type=MESH)
        cp.wait_send()
        cp.wait_recv()

    return pl.pallas_call(
        body, name=name,
        out_shape=(pltpu.HBM(x_thru.shape, x_thru.dtype), pltpu.HBM(land_thru.shape, land_thru.dtype)),
        in_specs=(HBM_SPEC, HBM_SPEC, SEM_SPEC, SEM_SPEC, ANY_SPEC),
        out_specs=(HBM_SPEC, HBM_SPEC),
        input_output_aliases={0: 0, 1: 1},
        compiler_params=pltpu.CompilerParams(has_side_effects=DATAFLOW),
    )(x_thru, land_thru, send_sem, recv_sem, after)[1]


def _sum_slots(x, *, name):
    n, R, C = x.shape
    tr = _tile(R, 256, 8)

    def body(x_ref, o_ref):
        acc = x_ref[0].astype(F32)
        for d in range(1, n):
            acc = acc + x_ref[d].astype(F32)
        o_ref[...] = acc

    return pl.pallas_call(
        body, name=name, grid=(R // tr,),
        out_shape=jax.ShapeDtypeStruct((R, C), F32),
        in_specs=[pl.BlockSpec((n, tr, C), lambda i: (0, i, 0))],
        out_specs=pl.BlockSpec((tr, C), lambda i: (i, 0)),
        compiler_params=_params(("parallel",)),
    )(x)


def _colsum(x, *, name):
    R, C = x.shape

    def body(x_ref, o_ref):
        o_ref[...] = jnp.sum(x_ref[...], axis=0, keepdims=True)

    return pl.pallas_call(
        body, name=name, out_shape=jax.ShapeDtypeStruct((1, C), F32),
        in_specs=[pl.BlockSpec((R, C), lambda: (0, 0))], out_specs=pl.BlockSpec((1, C), lambda: (0, 0)),
        compiler_params=_params(),
    )(x)


def _adamw(parts, w, m, v, *, name):
    n, R, C = parts.shape
    tr = _tile(R, 128, 8)
    c1 = 1.0 - ADAM_B1 ** ADAM_STEP
    c2 = 1.0 - ADAM_B2 ** ADAM_STEP

    def body(p_ref, w_ref, m_ref, v_ref, g_ref, d_ref, nm_ref, nv_ref):
        g = p_ref[0].astype(F32)
        for d in range(1, n):
            g = g + p_ref[d].astype(F32)
        mn = ADAM_B1 * m_ref[...] + (1.0 - ADAM_B1) * g
        vn = ADAM_B2 * v_ref[...] + (1.0 - ADAM_B2) * (g * g)
        g_ref[...] = g
        nm_ref[...] = mn
        nv_ref[...] = vn
        d_ref[...] = -ADAM_LR * ((mn / c1) / (jnp.sqrt(vn / c2) + ADAM_EPS) + ADAM_WD * w_ref[...])

    spec = pl.BlockSpec((tr, C), lambda i: (i, 0))
    shp = jax.ShapeDtypeStruct((R, C), F32)
    return pl.pallas_call(
        body, name=name, grid=(R // tr,), out_shape=(shp, shp, shp, shp),
        in_specs=[pl.BlockSpec((n, tr, C), lambda i: (0, i, 0)), spec, spec, spec],
        out_specs=(spec, spec, spec, spec),
        compiler_params=_params(("parallel",)),
    )(parts, w, m, v)


MM_VMEM_BUDGET = 40 * 1024 * 1024
MM_TK_MAX = 2048
MXU_WIDTH = 256


def _divisors(n, quantum, cap):
    return [t for t in range(quantum, min(n, cap) + 1, quantum) if n % t == 0] or [n]


def _mm_tiles(M, N, K, mode, a_bytes, b_bytes, o_bytes, has_add):
    tk = max(_divisors(K, 128, MM_TK_MAX))
    nk = K // tk
    best = None
    for tm in _divisors(M, 128 if mode == "tn" else 8, 1024):
        for tn in _divisors(N, 128, 3072):
            need = 2 * (tm * tk * a_bytes + tk * tn * b_bytes) + 2 * tm * tn * o_bytes + tm * tn * 4
            need += tm * tn * 4 if nk > 1 else 0
            need += 2 * tm * tn * 4 if has_add else 0
            if need > MM_VMEM_BUDGET:
                continue
            score = (tn % MXU_WIDTH == 0 or tn == N, tm * tn, tm)
            if best is None or score > best[0]:
                best = (score, tm, tn)
    assert best is not None, (M, N, K)
    return best[1], best[2], tk


def _mm(a, b, mode, *, name, out_dtype, add=None, after=None):
    if mode == "nn":
        (M, K), (K2, N) = a.shape, b.shape
    elif mode == "nt":
        (M, K), (N, K2) = a.shape, b.shape
    else:
        (K, M), (K2, N) = a.shape, b.shape
    assert K == K2, (name, a.shape, b.shape)
    tm, tn, tk = _mm_tiles(M, N, K, mode, a.dtype.itemsize, b.dtype.itemsize, jnp.dtype(out_dtype).itemsize,
                           add is not None)
    nk = K // tk
    dims = {"nn": NN, "nt": NT, "tn": TN}[mode]

    a_spec = {"nn": pl.BlockSpec((tm, tk), lambda i, j, k: (i, k)),
              "nt": pl.BlockSpec((tm, tk), lambda i, j, k: (i, k)),
              "tn": pl.BlockSpec((tk, tm), lambda i, j, k: (k, i))}[mode]
    b_spec = {"nn": pl.BlockSpec((tk, tn), lambda i, j, k: (k, j)),
              "nt": pl.BlockSpec((tn, tk), lambda i, j, k: (j, k)),
              "tn": pl.BlockSpec((tk, tn), lambda i, j, k: (k, j))}[mode]
    o_spec = pl.BlockSpec((tm, tn), lambda i, j, k: (i, j))

    def body(a_ref, b_ref, *rest):
        rest = list(rest)
        add_ref = rest.pop(0) if add is not None else None
        if after is not None:
            rest.pop(0)
        o_ref = rest.pop(0)
        part = lax.dot_general(a_ref[...].astype(ACT), b_ref[...].astype(ACT), dims, preferred_element_type=F32)

        def finish(r):
            if add is not None:
                r = r + add_ref[...].astype(F32)
            o_ref[...] = r.astype(out_dtype)

        if nk == 1:
            finish(part)
            return
        acc = rest.pop(0)
        k = pl.program_id(2)

        @pl.when(k == 0)
        def _():
            acc[...] = part

        @pl.when(jnp.logical_and(k > 0, k < nk - 1))
        def _():
            acc[...] += part

        @pl.when(k == nk - 1)
        def _():
            finish(acc[...] + part)

    operands = [a, b] + ([] if add is None else [add])
    in_specs = [a_spec, b_spec] + ([] if add is None else [o_spec])
    if after is not None:
        operands.append(after)
        in_specs.append(ANY_SPEC)
    return pl.pallas_call(
        body, name=name, grid=(M // tm, N // tn, nk),
        out_shape=jax.ShapeDtypeStruct((M, N), out_dtype),
        in_specs=in_specs, out_specs=o_spec,
        scratch_shapes=[pltpu.VMEM((tm, tn), F32)] if nk > 1 else [],
        compiler_params=_params(("parallel", "parallel", "arbitrary")),
    )(*operands)


def _row(tr, cols, blk=0):
    return pl.BlockSpec((tr, cols), lambda i: (i, blk))


def _vec(cols):
    return pl.BlockSpec((1, cols), lambda i: (0, 0))


def _rms(xf):
    return lax.rsqrt(jnp.mean(xf * xf, axis=-1, keepdims=True) + EPS)


def _rms_bwd(dxhat, xhat, r):
    return r * (dxhat - xhat * jnp.mean(dxhat * xhat, axis=-1, keepdims=True))


def _acc_rows(ref, val, first):
    s = jnp.sum(val, axis=0, keepdims=True)

    @pl.when(first)
    def _():
        ref[...] = s

    @pl.when(jnp.logical_not(first))
    def _():
        ref[...] += s


def _norm_mod_fwd(x, nw, shift, scale, *, name):
    T, D = x.shape
    tr = _tile(T, 256, 8)

    def body(x_ref, nw_ref, sh_ref, sc_ref, o_ref):
        xf = x_ref[...]
        n = xf * _rms(xf) * nw_ref[...]
        o_ref[...] = (n * (1.0 + sc_ref[...]) + sh_ref[...]).astype(ACT)

    return pl.pallas_call(
        body, name=name, grid=(T // tr,), out_shape=jax.ShapeDtypeStruct((T, D), ACT),
        in_specs=[_row(tr, D), _vec(D), _vec(D), _vec(D)], out_specs=_row(tr, D),
        compiler_params=_params(("parallel",)),
    )(x, nw, shift, scale)


def _resid_norm_mod_fwd(x, o, g, nw, shift, scale, *, name):
    T, D = x.shape
    tr = _tile(T, 256, 8)

    def body(x_ref, o_ref, g_ref, nw_ref, sh_ref, sc_ref, x1_ref, h_ref):
        x1 = x_ref[...] + g_ref[...] * o_ref[...].astype(F32)
        x1_ref[...] = x1
        n = x1 * _rms(x1) * nw_ref[...]
        h_ref[...] = (n * (1.0 + sc_ref[...]) + sh_ref[...]).astype(ACT)

    return pl.pallas_call(
        body, name=name, grid=(T // tr,),
        out_shape=(jax.ShapeDtypeStruct((T, D), F32), jax.ShapeDtypeStruct((T, D), ACT)),
        in_specs=[_row(tr, D), _row(tr, D), _vec(D), _vec(D), _vec(D), _vec(D)],
        out_specs=(_row(tr, D), _row(tr, D)),
        compiler_params=_params(("parallel",)),
    )(x, o, g, nw, shift, scale)


def _final_fwd_bwd(x1, dn, g2, nw, target, *, name):
    T, D = x1.shape
    tr = _tile(T, 256, 8)

    def body(x1_ref, dn_ref, g_ref, nw_ref, t_ref, loss_ref, dx_ref, ddn_ref, dnw_ref, dg_ref):
        first = pl.program_id(0) == 0
        dn_f = dn_ref[...].astype(F32)
        x2 = x1_ref[...] + g_ref[...] * dn_f
        r = _rms(x2)
        xhat = x2 * r
        err = xhat * nw_ref[...] - t_ref[...]
        part = 0.5 * jnp.sum(jnp.mean(err * err, axis=-1, keepdims=True), axis=0, keepdims=True)

        @pl.when(first)
        def _():
            loss_ref[...] = part

        @pl.when(jnp.logical_not(first))
        def _():
            loss_ref[...] += part

        dy = err * (1.0 / D)
        _acc_rows(dnw_ref, dy * xhat, first)
        dx2 = _rms_bwd(dy * nw_ref[...], xhat, r)
        dx_ref[...] = dx2
        ddn_ref[...] = (g_ref[...] * dx2).astype(ACT)
        _acc_rows(dg_ref, dx2 * dn_f, first)

    vec = jax.ShapeDtypeStruct((1, D), F32)
    return pl.pallas_call(
        body, name=name, grid=(T // tr,),
        out_shape=(jax.ShapeDtypeStruct((1, 1), F32), jax.ShapeDtypeStruct((T, D), F32),
                   jax.ShapeDtypeStruct((T, D), ACT), vec, vec),
        in_specs=[_row(tr, D), _row(tr, D), _vec(D), _vec(D), _row(tr, D)],
        out_specs=(pl.BlockSpec((1, 1), lambda i: (0, 0)), _row(tr, D), _row(tr, D), _vec(D), _vec(D)),
        compiler_params=_params(("arbitrary",)),
    )(x1, dn, g2, nw, target)


def _norm_mod_bwd(xin, nw, scale, dh, *, name, dres=None, o=None, g=None):
    T, D = xin.shape
    tr = _tile(T, 256, 8)
    has_res, has_o = dres is not None, o is not None

    def body(*refs):
        refs = list(refs)
        x_ref, nw_ref, sc_ref, dh_ref = refs[:4]
        pos = 4
        dres_ref = o_ref = g_ref = None
        if has_res:
            dres_ref = refs[pos]
            pos += 1
        if has_o:
            o_ref, g_ref = refs[pos], refs[pos + 1]
            pos += 2
        dx_ref, dsh_ref, dsc_ref, dnw_ref = refs[pos:pos + 4]
        pos += 4
        first = pl.program_id(0) == 0
        xf = x_ref[...]
        r = _rms(xf)
        xhat = xf * r
        n = xhat * nw_ref[...]
        dhf = dh_ref[...].astype(F32)
        _acc_rows(dsh_ref, dhf, first)
        _acc_rows(dsc_ref, dhf * n, first)
        dn = dhf * (1.0 + sc_ref[...])
        _acc_rows(dnw_ref, dn * xhat, first)
        dx = _rms_bwd(dn * nw_ref[...], xhat, r)
        if has_res:
            dx = dx + dres_ref[...]
        dx_ref[...] = dx
        if has_o:
            do_ref, dg_ref = refs[pos], refs[pos + 1]
            do_ref[...] = (g_ref[...] * dx).astype(ACT)
            _acc_rows(dg_ref, dx * o_ref[...].astype(F32), first)

    vec = jax.ShapeDtypeStruct((1, D), F32)
    operands = [xin, nw, scale, dh]
    in_specs = [_row(tr, D), _vec(D), _vec(D), _row(tr, D)]
    if has_res:
        operands.append(dres)
        in_specs.append(_row(tr, D))
    if has_o:
        operands += [o, g]
        in_specs += [_row(tr, D), _vec(D)]
    out_shape = [jax.ShapeDtypeStruct((T, D), F32), vec, vec, vec]
    out_specs = [_row(tr, D), _vec(D), _vec(D), _vec(D)]
    if has_o:
        out_shape += [jax.ShapeDtypeStruct((T, D), ACT), vec]
        out_specs += [_row(tr, D), _vec(D)]
    return pl.pallas_call(
        body, name=name, grid=(T // tr,), out_shape=tuple(out_shape),
        in_specs=in_specs, out_specs=tuple(out_specs),
        compiler_params=_params(("arbitrary",)),
    )(*operands)


def _swiglu_fwd(gate, up, *, name):
    T, F = gate.shape
    tr = _tile(T, 256, 8)

    def body(g_ref, u_ref, o_ref):
        o_ref[...] = (_silu(g_ref[...].astype(F32)) * u_ref[...].astype(F32)).astype(ACT)

    return pl.pallas_call(
        body, name=name, grid=(T // tr,), out_shape=jax.ShapeDtypeStruct((T, F), ACT),
        in_specs=[_row(tr, F), _row(tr, F)], out_specs=_row(tr, F),
        compiler_params=_params(("parallel",)),
    )(gate, up)


def _swiglu_bwd(gate, up, dact, *, name):
    T, F = gate.shape
    tr = _tile(T, 256, 8)

    def body(g_ref, u_ref, d_ref, dg_ref, du_ref):
        gf, uf, df = g_ref[...].astype(F32), u_ref[...].astype(F32), d_ref[...].astype(F32)
        dg_ref[...] = (df * uf * _dsilu(gf)).astype(ACT)
        du_ref[...] = (df * _silu(gf)).astype(ACT)

    shp = jax.ShapeDtypeStruct((T, F), ACT)
    return pl.pallas_call(
        body, name=name, grid=(T // tr,), out_shape=(shp, shp),
        in_specs=[_row(tr, F)] * 3, out_specs=(_row(tr, F), _row(tr, F)),
        compiler_params=_params(("parallel",)),
    )(gate, up, dact)


def _merge_fwd(ya, yb, gates, *, name):
    T, D = ya.shape
    tr = _tile(T, 256, 8)

    def body(a_ref, b_ref, g_ref, o_ref):
        ga = g_ref[:, :D].astype(F32)
        gb = g_ref[:, D:].astype(F32)
        o_ref[...] = (jax.nn.sigmoid(ga) * a_ref[...].astype(F32)
                      + jax.nn.sigmoid(gb) * b_ref[...].astype(F32)).astype(ACT)

    return pl.pallas_call(
        body, name=name, grid=(T // tr,), out_shape=jax.ShapeDtypeStruct((T, D), ACT),
        in_specs=[_row(tr, D), _row(tr, D), _row(tr, 2 * D)], out_specs=_row(tr, D),
        compiler_params=_params(("parallel",)),
    )(ya, yb, gates)


def _merge_bwd(dmer, ya, yb, gates, *, name):
    T, D = ya.shape
    tr = _tile(T, 256, 8)

    def body(d_ref, a_ref, b_ref, g_ref, da_ref, db_ref, dg_ref):
        d = d_ref[...].astype(F32)
        sa = jax.nn.sigmoid(g_ref[:, :D].astype(F32))
        sb = jax.nn.sigmoid(g_ref[:, D:].astype(F32))
        da_ref[...] = (d * sa).astype(ACT)
        db_ref[...] = (d * sb).astype(ACT)
        dg_ref[:, :D] = (d * a_ref[...].astype(F32) * sa * (1.0 - sa)).astype(ACT)
        dg_ref[:, D:] = (d * b_ref[...].astype(F32) * sb * (1.0 - sb)).astype(ACT)

    shp = jax.ShapeDtypeStruct((T, D), ACT)
    return pl.pallas_call(
        body, name=name, grid=(T // tr,), out_shape=(shp, shp, jax.ShapeDtypeStruct((T, 2 * D), ACT)),
        in_specs=[_row(tr, D), _row(tr, D), _row(tr, D), _row(tr, 2 * D)],
        out_specs=(_row(tr, D), _row(tr, D), _row(tr, 2 * D)),
        compiler_params=_params(("parallel",)),
    )(dmer, ya, yb, gates)


def _gate_norm_fwd(yf, yb, xbc, z, dskip, nw, *, name):
    T, DI = z.shape
    tr = _tile(T, 128, 8)

    def body(yf_ref, yb_ref, xs_ref, z_ref, ds_ref, nw_ref, o_ref):
        y = yf_ref[...].astype(F32) + yb_ref[...].astype(F32) + ds_ref[...] * xs_ref[...].astype(F32)
        gz = y * _silu(z_ref[...].astype(F32))
        o_ref[...] = (gz * _rms(gz) * nw_ref[...]).astype(ACT)

    return pl.pallas_call(
        body, name=name, grid=(T // tr,), out_shape=jax.ShapeDtypeStruct((T, DI), ACT),
        in_specs=[_row(tr, DI), _row(tr, DI), _row(tr, DI), _row(tr, DI), _vec(DI), _vec(DI)],
        out_specs=_row(tr, DI),
        compiler_params=_params(("parallel",)),
    )(yf, yb, xbc, z, dskip, nw)


def _gate_norm_bwd(dout, yf, yb, xbc, z, dskip, nw, *, name):
    T, DI = z.shape
    tr = _tile(T, 128, 8)

    def body(do_ref, yf_ref, yb_ref, xs_ref, z_ref, ds_ref, nw_ref, dy_ref, dz_ref, dxs_ref, dnw_ref, dds_ref):
        first = pl.program_id(0) == 0
        xs = xs_ref[...].astype(F32)
        zf = z_ref[...].astype(F32)
        y = yf_ref[...].astype(F32) + yb_ref[...].astype(F32) + ds_ref[...] * xs
        sz = _silu(zf)
        gz = y * sz
        r = _rms(gz)
        ghat = gz * r
        do = do_ref[...].astype(F32)
        _acc_rows(dnw_ref, do * ghat, first)
        dgz = _rms_bwd(do * nw_ref[...], ghat, r)
        dy = dgz * sz
        dy_ref[...] = dy.astype(ACT)
        dz_ref[...] = (dgz * y * _dsilu(zf)).astype(ACT)
        dxs_ref[...] = (dy * ds_ref[...]).astype(ACT)
        _acc_rows(dds_ref, dy * xs, first)

    shp = jax.ShapeDtypeStruct((T, DI), ACT)
    vec = jax.ShapeDtypeStruct((1, DI), F32)
    return pl.pallas_call(
        body, name=name, grid=(T // tr,), out_shape=(shp, shp, shp, vec, vec),
        in_specs=[_row(tr, DI)] * 5 + [_vec(DI), _vec(DI)],
        out_specs=(_row(tr, DI), _row(tr, DI), _row(tr, DI), _vec(DI), _vec(DI)),
        compiler_params=_params(("arbitrary",)),
    )(dout, yf, yb, xbc, z, dskip, nw)


def _ln_silu_fwd(x, g, b, *, name):
    T, D = x.shape
    tr = _tile(T, 256, 8)

    def body(x_ref, g_ref, b_ref, o_ref):
        xf = x_ref[...].astype(F32)
        xc = xf - jnp.mean(xf, axis=-1, keepdims=True)
        rstd = lax.rsqrt(jnp.mean(xc * xc, axis=-1, keepdims=True) + EPS)
        o_ref[...] = _silu(xc * rstd * g_ref[...] + b_ref[...]).astype(ACT)

    return pl.pallas_call(
        body, name=name, grid=(T // tr,), out_shape=jax.ShapeDtypeStruct((T, D), ACT),
        in_specs=[_row(tr, D), _vec(D), _vec(D)], out_specs=_row(tr, D),
        compiler_params=_params(("parallel",)),
    )(x, g, b)


def _ln_silu_bwd(x, g, b, dcf, *, name):
    T, D = x.shape
    tr = _tile(T, 256, 8)

    def body(x_ref, g_ref, b_ref, d_ref, dx_ref, dg_ref, db_ref):
        first = pl.program_id(0) == 0
        xf = x_ref[...].astype(F32)
        xc = xf - jnp.mean(xf, axis=-1, keepdims=True)
        rstd = lax.rsqrt(jnp.mean(xc * xc, axis=-1, keepdims=True) + EPS)
        xhat = xc * rstd
        dyln = d_ref[...].astype(F32) * _dsilu(xhat * g_ref[...] + b_ref[...])
        _acc_rows(dg_ref, dyln * xhat, first)
        _acc_rows(db_ref, dyln, first)
        dxh = dyln * g_ref[...]
        dx = rstd * (dxh - jnp.mean(dxh, axis=-1, keepdims=True)
                     - xhat * jnp.mean(dxh * xhat, axis=-1, keepdims=True))
        dx_ref[...] = dx.astype(ACT)

    vec = jax.ShapeDtypeStruct((1, D), F32)
    return pl.pallas_call(
        body, name=name, grid=(T // tr,), out_shape=(jax.ShapeDtypeStruct((T, D), ACT), vec, vec),
        in_specs=[_row(tr, D), _vec(D), _vec(D), _row(tr, D)],
        out_specs=(_row(tr, D), _vec(D), _vec(D)),
        compiler_params=_params(("arbitrary",)),
    )(x, g, b, dcf)


CONV_CW = 128
CONV_RT = 256
SEQ_PAD = 8


def _window(ext, off, n):
    if off % 8 == 0:
        return ext[off:off + n]
    return pltpu.roll(ext, ext.shape[0] - off, 0)[:n]


def _sum8(v):
    R, C = v.shape
    return jnp.sum(v.reshape(R // 8, 8, C), axis=0)


def _conv5_silu_fwd(x, w, b, *, name):
    T, C = x.shape
    K = 5
    cw, rt = CONV_CW, _tile(T, CONV_RT, 8)
    half = K // 2

    def body(x_ref, w_ref, b_ref, o_ref, pad):
        zeros = jnp.zeros((SEQ_PAD, cw), F32)
        pad[0:SEQ_PAD, :] = zeros
        pad[T + SEQ_PAD:T + 2 * SEQ_PAD, :] = zeros

        def fill(i, c):
            base = pl.multiple_of(i * rt, rt)
            pad[pl.ds(base + SEQ_PAD, rt), :] = x_ref[pl.ds(base, rt), :].astype(F32)
            return c

        lax.fori_loop(0, T // rt, fill, 0)
        wv = w_ref[...]
        bias = b_ref[...]

        def step(i, c):
            base = pl.multiple_of(i * rt, rt)
            ext = pad[pl.ds(base, rt + 2 * SEQ_PAD), :]
            acc = jnp.zeros((rt, cw), F32) + bias
            for k in range(K):
                acc = acc + wv[k:k + 1, :] * _window(ext, SEQ_PAD + k - half, rt)
            o_ref[pl.ds(base, rt), :] = _silu(acc).astype(ACT)
            return c

        lax.fori_loop(0, T // rt, step, 0)

    return pl.pallas_call(
        body, name=name, grid=(C // cw,), out_shape=jax.ShapeDtypeStruct((T, C), ACT),
        in_specs=[pl.BlockSpec((T, cw), lambda j: (0, j)), pl.BlockSpec((8, cw), lambda j: (0, j)),
                  pl.BlockSpec((1, cw), lambda j: (0, j))],
        out_specs=pl.BlockSpec((T, cw), lambda j: (0, j)),
        scratch_shapes=[pltpu.VMEM((T + 2 * SEQ_PAD, cw), F32)],
        compiler_params=_params(("parallel",)),
    )(x, w, b)


def _conv5_silu_bwd(x, w, b, dout, *, name):
    T, C = x.shape
    K = 5
    cw, rt = CONV_CW, _tile(T, CONV_RT, 8)
    half = K // 2

    def body(x_ref, w_ref, b_ref, d_ref, dx_ref, dw_ref, db_ref, pad, dpad, wacc):
        zeros = jnp.zeros((SEQ_PAD, cw), F32)
        for p in (pad, dpad):
            p[0:SEQ_PAD, :] = zeros
            p[T + SEQ_PAD:T + 2 * SEQ_PAD, :] = zeros
        wacc[...] = jnp.zeros_like(wacc)

        def fill(i, c):
            base = pl.multiple_of(i * rt, rt)
            pad[pl.ds(base + SEQ_PAD, rt), :] = x_ref[pl.ds(base, rt), :].astype(F32)
            return c

        lax.fori_loop(0, T // rt, fill, 0)
        wv = w_ref[...]
        bias = b_ref[...]

        def step1(i, c):
            base = pl.multiple_of(i * rt, rt)
            ext = pad[pl.ds(base, rt + 2 * SEQ_PAD), :]
            wins = [_window(ext, SEQ_PAD + k - half, rt) for k in range(K)]
            pre = jnp.zeros((rt, cw), F32) + bias
            for k in range(K):
                pre = pre + wv[k:k + 1, :] * wins[k]
            dpre = d_ref[pl.ds(base, rt), :].astype(F32) * _dsilu(pre)
            dpad[pl.ds(base + SEQ_PAD, rt), :] = dpre
            for k in range(K):
                wacc[k] += _sum8(dpre * wins[k])
            wacc[K] += _sum8(dpre)
            return c

        lax.fori_loop(0, T // rt, step1, 0)

        def step2(i, c):
            base = pl.multiple_of(i * rt, rt)
            ext = dpad[pl.ds(base, rt + 2 * SEQ_PAD), :]
            acc = jnp.zeros((rt, cw), F32)
            for k in range(K):
                acc = acc + wv[k:k + 1, :] * _window(ext, SEQ_PAD - (k - half), rt)
            dx_ref[pl.ds(base, rt), :] = acc.astype(ACT)
            return c

        lax.fori_loop(0, T // rt, step2, 0)
        rows = [jnp.sum(wacc[k], axis=0, keepdims=True) for k in range(K)]
        rows += [jnp.zeros((1, cw), F32)] * (8 - K)
        dw_ref[...] = jnp.concatenate(rows, axis=0)
        db_ref[...] = jnp.sum(wacc[K], axis=0, keepdims=True)

    return pl.pallas_call(
        body, name=name, grid=(C // cw,),
        out_shape=(jax.ShapeDtypeStruct((T, C), ACT), jax.ShapeDtypeStruct((8, C), F32),
                   jax.ShapeDtypeStruct((1, C), F32)),
        in_specs=[pl.BlockSpec((T, cw), lambda j: (0, j)), pl.BlockSpec((8, cw), lambda j: (0, j)),
                  pl.BlockSpec((1, cw), lambda j: (0, j)), pl.BlockSpec((T, cw), lambda j: (0, j))],
        out_specs=(pl.BlockSpec((T, cw), lambda j: (0, j)), pl.BlockSpec((8, cw), lambda j: (0, j)),
                   pl.BlockSpec((1, cw), lambda j: (0, j))),
        scratch_shapes=[pltpu.VMEM((T + 2 * SEQ_PAD, cw), F32), pltpu.VMEM((T + 2 * SEQ_PAD, cw), F32),
                        pltpu.VMEM((K + 1, 8, cw), F32)],
        compiler_params=_params(("parallel",)),
    )(x, w, b, dout)


def _glu_conv_fwd(u, v, w, b, *, name):
    T, C = u.shape
    K = 31
    KP = w.shape[0]
    cw, rt = CONV_CW, _tile(T, CONV_RT, GRID_W)
    half = K // 2
    P = half * GRID_W

    def body(u_ref, v_ref, w_ref, b_ref, o_ref, pad):
        zeros = jnp.zeros((P, cw), F32)
        pad[0:P, :] = zeros
        pad[T + P:T + 2 * P, :] = zeros

        def fill(i, c):
            base = pl.multiple_of(i * rt, rt)
            uf = u_ref[pl.ds(base, rt), :].astype(F32)
            vf = v_ref[pl.ds(base, rt), :].astype(F32)
            pad[pl.ds(base + P, rt), :] = uf * jax.nn.sigmoid(vf)
            return c

        lax.fori_loop(0, T // rt, fill, 0)
        wv = w_ref[...]
        bias = b_ref[...]

        def step(i, c):
            base = pl.multiple_of(i * rt, rt)
            acc = jnp.zeros((rt, cw), F32) + bias
            for k in range(K):
                acc = acc + wv[k:k + 1, :] * pad[pl.ds(base + k * GRID_W, rt), :]
            o_ref[pl.ds(base, rt), :] = acc.astype(ACT)
            return c

        lax.fori_loop(0, T // rt, step, 0)

    col = pl.BlockSpec((T, cw), lambda j: (0, j))
    return pl.pallas_call(
        body, name=name, grid=(C // cw,), out_shape=jax.ShapeDtypeStruct((T, C), ACT),
        in_specs=[col, col, pl.BlockSpec((KP, cw), lambda j: (0, j)), pl.BlockSpec((1, cw), lambda j: (0, j))],
        out_specs=col,
        scratch_shapes=[pltpu.VMEM((T + 2 * P, cw), F32)],
        compiler_params=_params(("parallel",)),
    )(u, v, w, b)


def _glu_conv_bwd(u, v, w, dout, *, name):
    T, C = u.shape
    K = 31
    KP = w.shape[0]
    cw, rt = CONV_CW, _tile(T, CONV_RT, GRID_W)
    half = K // 2
    P = half * GRID_W

    def body(u_ref, v_ref, w_ref, d_ref, du_ref, dv_ref, dw_ref, db_ref, pad, dpad, wacc):
        zeros = jnp.zeros((P, cw), F32)
        for p in (pad, dpad):
            p[0:P, :] = zeros
            p[T + P:T + 2 * P, :] = zeros
        wacc[...] = jnp.zeros_like(wacc)

        def fill(i, c):
            base = pl.multiple_of(i * rt, rt)
            uf = u_ref[pl.ds(base, rt), :].astype(F32)
            vf = v_ref[pl.ds(base, rt), :].astype(F32)
            pad[pl.ds(base + P, rt), :] = uf * jax.nn.sigmoid(vf)
            dpad[pl.ds(base + P, rt), :] = d_ref[pl.ds(base, rt), :].astype(F32)
            return c

        lax.fori_loop(0, T // rt, fill, 0)
        wv = w_ref[...]

        def step(i, c):
            base = pl.multiple_of(i * rt, rt)
            d = dpad[pl.ds(base + P, rt), :]
            dg = jnp.zeros((rt, cw), F32)
            for k in range(K):
                wacc[k] += _sum8(d * pad[pl.ds(base + k * GRID_W, rt), :])
                dg = dg + wv[k:k + 1, :] * dpad[pl.ds(base + (K - 1 - k) * GRID_W, rt), :]
            wacc[K] += _sum8(d)
            uf = u_ref[pl.ds(base, rt), :].astype(F32)
            sv = jax.nn.sigmoid(v_ref[pl.ds(base, rt), :].astype(F32))
            du_ref[pl.ds(base, rt), :] = (dg * sv).astype(ACT)
            dv_ref[pl.ds(base, rt), :] = (dg * uf * sv * (1.0 - sv)).astype(ACT)
            return c

        lax.fori_loop(0, T // rt, step, 0)
        rows = [jnp.sum(wacc[k], axis=0, keepdims=True) for k in range(K)]
        rows += [jnp.zeros((1, cw), F32)] * (KP - K)
        dw_ref[...] = jnp.concatenate(rows, axis=0)
        db_ref[...] = jnp.sum(wacc[K], axis=0, keepdims=True)

    col = pl.BlockSpec((T, cw), lambda j: (0, j))
    shp = jax.ShapeDtypeStruct((T, C), ACT)
    return pl.pallas_call(
        body, name=name, grid=(C // cw,),
        out_shape=(shp, shp, jax.ShapeDtypeStruct((KP, C), F32), jax.ShapeDtypeStruct((1, C), F32)),
        in_specs=[col, col, pl.BlockSpec((KP, cw), lambda j: (0, j)), col],
        out_specs=(col, col, pl.BlockSpec((KP, cw), lambda j: (0, j)), pl.BlockSpec((1, cw), lambda j: (0, j))),
        scratch_shapes=[pltpu.VMEM((T + 2 * P, cw), F32), pltpu.VMEM((T + 2 * P, cw), F32),
                        pltpu.VMEM((K + 1, 8, cw), F32)],
        compiler_params=_params(("parallel",)),
    )(u, v, w, dout)


def _dt_fwd(rawT, bias, *, name):
    H2, T = rawT.shape
    tc = _tile(T, 2048, 128)

    def body(r_ref, b_ref, o_ref):
        v = r_ref[...] + b_ref[...]
        o_ref[...] = jnp.maximum(v, 0.0) + jnp.log(1.0 + jnp.exp(-jnp.abs(v)))

    return pl.pallas_call(
        body, name=name, grid=(T // tc,), out_shape=jax.ShapeDtypeStruct((H2, T), F32),
        in_specs=[pl.BlockSpec((H2, tc), lambda i: (0, i)), pl.BlockSpec((H2, 1), lambda i: (0, 0))],
        out_specs=pl.BlockSpec((H2, tc), lambda i: (0, i)),
        compiler_params=_params(("parallel",)),
    )(rawT, bias)


def _dt_bwd(rawT, bias, ddtT, *, name):
    H2, T = rawT.shape
    tc = _tile(T, 2048, 128)

    def body(r_ref, b_ref, d_ref, o_ref, db_ref):
        first = pl.program_id(0) == 0
        dr = d_ref[...] * jax.nn.sigmoid(r_ref[...] + b_ref[...])
        o_ref[...] = dr
        s = jnp.sum(dr, axis=1, keepdims=True)

        @pl.when(first)
        def _():
            db_ref[...] = s

        @pl.when(jnp.logical_not(first))
        def _():
            db_ref[...] += s

    return pl.pallas_call(
        body, name=name, grid=(T // tc,),
        out_shape=(jax.ShapeDtypeStruct((H2, T), F32), jax.ShapeDtypeStruct((H2, 1), F32)),
        in_specs=[pl.BlockSpec((H2, tc), lambda i: (0, i)), pl.BlockSpec((H2, 1), lambda i: (0, 0)),
                  pl.BlockSpec((H2, tc), lambda i: (0, i))],
        out_specs=(pl.BlockSpec((H2, tc), lambda i: (0, i)), pl.BlockSpec((H2, 1), lambda i: (0, 0))),
        compiler_params=_params(("arbitrary",)),
    )(rawT, bias, ddtT)


def _ssd_common(dtT, a, reverse):
    J, Q = dtT.shape
    li = lax.broadcasted_iota(jnp.int32, (Q, Q), 0)
    si = lax.broadcasted_iota(jnp.int32, (Q, Q), 1)
    mask = (si >= li) if reverse else (si <= li)
    Mf = mask.astype(F32)
    daT = dtT * a
    csT = lax.dot_general(daT, Mf, NT, precision=HIGHEST, preferred_element_type=F32)
    last = 0 if reverse else Q - 1
    totT = csT[:, last:last + 1]
    return mask, Mf, csT, totT, last


def _to_cols(rows):
    R, Q = rows.shape
    if R < 128:
        rows = jnp.concatenate([rows, jnp.zeros((128 - R, Q), F32)], axis=0)
    return rows.T


def _ssd_fwd(xbc, dtT, a, h0, *, reverse, name, di):
    T = xbc.shape[0]
    G, JP, N = h0.shape
    J, P, Q = JP // HEAD_DIM, HEAD_DIM, CHUNK
    nc = T // Q
    QW = 256
    HQ = QW // P

    def ci(k):
        return nc - 1 - k if reverse else k

    GB = SSD_GROUPS_PER_STEP

    def body(x_ref, b_ref, c_ref, dt_ref, a_ref, h0_ref, y_ref, hs_ref, hl_ref, h_scr):
        k = pl.program_id(1)

        @pl.when(k == 0)
        def _():
            h_scr[...] = h0_ref[...]

        lh = lax.broadcasted_iota(jnp.int32, (Q, QW), 1) // P

        def scale_heads(vT, rowsT):
            return jnp.concatenate([vT[j * P:(j + 1) * P, :] * rowsT[j:j + 1, :] for j in range(J)], axis=0)

        for gi in range(GB):
            h = h_scr[gi]
            hs_ref[0, gi] = h
            Xb = x_ref[:, gi * JP:(gi + 1) * JP]
            Bm, Cm = b_ref[:, gi * N:(gi + 1) * N], c_ref[:, gi * N:(gi + 1) * N]
            dtT_v = dt_ref[gi * J:(gi + 1) * J, :]
            mask, _, csT, totT, _ = _ssd_common(dtT_v, a_ref[gi * J:(gi + 1) * J, :], reverse)
            cs = _to_cols(csT)
            CB = lax.dot_general(Cm, Bm, NT, preferred_element_type=F32)
            yoT = lax.dot_general(h.astype(ACT), Cm, NT, preferred_element_type=F32)
            yo = scale_heads(yoT, jnp.exp(csT)).T
            for q in range(JP // QW):
                xq = Xb[:, q * QW:(q + 1) * QW]
                acc = yo[:, q * QW:(q + 1) * QW]
                for jj in range(HQ):
                    j = q * HQ + jj
                    seg = cs[:, j:j + 1] - csT[j:j + 1, :]
                    Mj = (CB * jnp.exp(jnp.where(mask, seg, NEG)) * dtT_v[j:j + 1, :]).astype(ACT)
                    acc = acc + jnp.dot(Mj, jnp.where(lh == jj, xq, jnp.zeros_like(xq)),
                                        preferred_element_type=F32)
                y_ref[:, gi * JP + q * QW:gi * JP + (q + 1) * QW] = acc.astype(ACT)
            xwT = scale_heads(Xb.astype(F32).T, dtT_v * jnp.exp(totT - csT)).astype(ACT)
            upd = jnp.dot(xwT, Bm, preferred_element_type=F32)
            for j in range(J):
                rows = slice(j * P, (j + 1) * P)
                h_scr[gi, rows, :] = h[rows, :] * jnp.exp(totT[j:j + 1, :]) + upd[rows, :]

        @pl.when(k == nc - 1)
        def _():
            hl_ref[...] = h_scr[...]

    GN = G * N
    return pl.pallas_call(
        body, name=name, grid=(G // GB, nc),
        out_shape=(jax.ShapeDtypeStruct((T, di), ACT), jax.ShapeDtypeStruct((nc, G, JP, N), F32),
                   jax.ShapeDtypeStruct((G, JP, N), F32)),
        in_specs=[pl.BlockSpec((Q, GB * JP), lambda g, k: (ci(k), g)),
                  pl.BlockSpec((Q, GB * N), lambda g, k: (ci(k), di // (GB * N) + g)),
                  pl.BlockSpec((Q, GB * N), lambda g, k: (ci(k), (di + GN) // (GB * N) + g)),
                  pl.BlockSpec((GB * J, Q), lambda g, k: (g, ci(k))),
                  pl.BlockSpec((GB * J, 1), lambda g, k: (g, 0)),
                  pl.BlockSpec((GB, JP, N), lambda g, k: (g, 0, 0))],
        out_specs=(pl.BlockSpec((Q, GB * JP), lambda g, k: (ci(k), g)),
                   pl.BlockSpec((1, GB, JP, N), lambda g, k: (ci(k), g, 0, 0)),
                   pl.BlockSpec((GB, JP, N), lambda g, k: (g, 0, 0))),
        scratch_shapes=[pltpu.VMEM((GB, JP, N), F32)],
        compiler_params=_params(("arbitrary", "arbitrary")),
    )(xbc, xbc, xbc, dtT, a, h0)


def _ssd_bwd(xbc, dtT, a, dy, hs, dh_last, add, *, reverse, name, di):
    T = xbc.shape[0]
    G, JP, N = dh_last.shape
    J, P, Q = JP // HEAD_DIM, HEAD_DIM, CHUNK
    nc = T // Q
    QW = 256
    HQ = QW // P
    has_add = add is not None

    def ci(k):
        return k if reverse else nc - 1 - k

    GB = SSD_GROUPS_PER_STEP

    def body(*refs):
        for gi in range(GB):
            wide = lambda r, w: r.at[:, pl.ds(gi * w, w)]
            x_ref, b_ref, c_ref, dt_ref, a_ref, dy_ref, hs_ref, dhl_ref = refs[:8]
            views = [wide(x_ref, JP), wide(b_ref, N), wide(c_ref, N), dt_ref.at[pl.ds(gi * J, J)],
                     a_ref.at[pl.ds(gi * J, J)], wide(dy_ref, JP), hs_ref.at[:, pl.ds(gi, 1)],
                     dhl_ref.at[pl.ds(gi, 1)]]
            rest = refs[8:]
            if has_add:
                views += [wide(rest[0], JP), wide(rest[1], N), wide(rest[2], N)]
                rest = rest[3:]
            dx_ref, db_ref, dc_ref, ddt_ref, da_ref, dh0_ref, dh_scr = rest
            views += [wide(dx_ref, JP), wide(db_ref, N), wide(dc_ref, N), ddt_ref.at[pl.ds(gi * J, J)],
                      da_ref.at[pl.ds(gi * J, J)], dh0_ref.at[pl.ds(gi, 1)], dh_scr.at[gi]]
            group_body(*views)

    def group_body(x_ref, b_ref, c_ref, dt_ref, a_ref, dy_ref, hs_ref, dhl_ref, *rest):
        if has_add:
            adx_ref, adb_ref, adc_ref = rest[:3]
            rest = rest[3:]
        dx_ref, db_ref, dc_ref, ddt_ref, da_ref, dh0_ref, dh_scr = rest
        k = pl.program_id(1)

        @pl.when(k == 0)
        def _():
            dh_scr[...] = dhl_ref[0]
            da_ref[...] = jnp.zeros_like(da_ref)

        def scale_heads(vT, rowsT):
            return jnp.concatenate([vT[j * P:(j + 1) * P, :] * rowsT[j:j + 1, :] for j in range(J)], axis=0)

        def head_sums(vT):
            return jnp.sum(vT.reshape(J, P, Q), axis=1)

        dH = dh_scr[...]
        h = hs_ref[0, 0]
        Bm, Cm = b_ref[...], c_ref[...]
        dtT_v = dt_ref[...]
        a_v = a_ref[...]
        mask, Mf, csT, totT, last = _ssd_common(dtT_v, a_v, reverse)
        ecsT = jnp.exp(csT)
        toendT = jnp.exp(totT - csT)
        cs = _to_cols(csT)
        dYb = dy_ref[...]
        XT = x_ref[...].astype(F32).T
        dYT = dYb.astype(F32).T
        xdtT = scale_heads(XT, dtT_v).astype(ACT)
        dYT_b = dYT.astype(ACT)
        dYeT = scale_heads(dYT, ecsT).astype(ACT)
        h_b = h.astype(ACT)
        dH_b = dH.astype(ACT)
        CB = lax.dot_general(Cm, Bm, NT, preferred_element_type=F32)
        dxdt_offT = scale_heads(lax.dot_general(dH_b, Bm, NT, preferred_element_type=F32), toendT)
        dCB = jnp.zeros((Q, Q), F32)
        lh = lax.broadcasted_iota(jnp.int32, (Q, QW), 1) // P
        sh = lax.broadcasted_iota(jnp.int32, (QW, Q), 0) // P
        lane_q = lax.broadcasted_iota(jnp.int32, (Q, Q), 1)
        sub_j = lax.broadcasted_iota(jnp.int32, (J, Q), 0)
        e_rows = jnp.zeros((Q, Q), F32)
        e_cols = jnp.zeros((J, Q), F32)
        diag = []
        for q in range(JP // QW):
            xq = xdtT[q * QW:(q + 1) * QW, :]
            dyq = dYb[:, q * QW:(q + 1) * QW]
            dyTq = dYT_b[q * QW:(q + 1) * QW, :]
            acc = jnp.zeros((QW, Q), F32)
            for jj in range(HQ):
                j = q * HQ + jj
                seg = cs[:, j:j + 1] - csT[j:j + 1, :]
                L = jnp.exp(jnp.where(mask, seg, NEG))
                Mf_j = CB * L
                dyj = jnp.where(lh == jj, dyq, jnp.zeros_like(dyq))
                dyTj = jnp.where(sh == jj, dyTq, jnp.zeros_like(dyTq))
                acc = acc + jnp.dot(dyTj, Mf_j.astype(ACT), preferred_element_type=F32)
                dM = jnp.dot(dyj, xq, preferred_element_type=F32)
                dCB = dCB + dM * L
                E = dM * Mf_j
                e_rows = jnp.where(lane_q == j, jnp.sum(E, axis=1, keepdims=True), e_rows)
                e_cols = jnp.where(sub_j == j, jnp.sum(E, axis=0, keepdims=True), e_cols)
            diag.append(acc)
        dxdtT = dxdt_offT + jnp.concatenate(diag, axis=0)
        dCB_b = dCB.astype(ACT)
        dC = (jnp.dot(dCB_b, Bm, preferred_element_type=F32)
              + lax.dot_general(dYeT, h_b, TN, preferred_element_type=F32))
        xwT = scale_heads(XT, dtT_v * toendT).astype(ACT)
        dB = (lax.dot_general(dCB_b, Cm, TN, preferred_element_type=F32)
              + lax.dot_general(xwT, dH_b, TN, preferred_element_type=F32))
        dHc = jnp.dot(dYeT, Cm, preferred_element_type=F32)
        for j in range(J):
            rows = slice(j * P, (j + 1) * P)
            dh_scr[rows, :] = dH[rows, :] * jnp.exp(totT[j:j + 1, :]) + dHc[rows, :]
        dh0_ref[0] = dh_scr[...]

        yoT = scale_heads(lax.dot_general(h_b, Cm, NT, preferred_element_type=F32), ecsT)
        RT_ = head_sums(dYT * yoT)
        UT_ = head_sums(XT * dxdtT)
        UoT = head_sums(XT * dxdt_offT)
        hsum = jnp.sum(jnp.sum((dH * h).reshape(J, P, N), axis=1), axis=1, keepdims=True)
        dtot = jnp.sum(UoT * dtT_v, axis=1, keepdims=True) + jnp.exp(totT) * hsum
        lane = lax.broadcasted_iota(jnp.int32, (J, Q), 1)
        dcsT = e_rows.T[0:J] - e_cols + RT_ - UoT * dtT_v + jnp.where(lane == last, dtot, 0.0)
        ddaT = jnp.dot(dcsT, Mf, precision=HIGHEST, preferred_element_type=F32)
        ddt_ref[...] = ddaT * a_v + UT_
        da_ref[...] += ddaT * dtT_v
        dX = scale_heads(dxdtT, dtT_v).T
        if has_add:
            dX = dX + adx_ref[...].astype(F32)
            dB = dB + adb_ref[...].astype(F32)
            dC = dC + adc_ref[...].astype(F32)
        dx_ref[...] = dX.astype(ACT)
        db_ref[...] = dB.astype(ACT)
        dc_ref[...] = dC.astype(ACT)

    GN = G * N
    xspec = pl.BlockSpec((Q, GB * JP), lambda g, k: (ci(k), g))
    nspec = pl.BlockSpec((Q, GB * N), lambda g, k: (ci(k), g))
    hspec = pl.BlockSpec((GB, JP, N), lambda g, k: (g, 0, 0))
    in_specs = [xspec,
                pl.BlockSpec((Q, GB * N), lambda g, k: (ci(k), di // (GB * N) + g)),
                pl.BlockSpec((Q, GB * N), lambda g, k: (ci(k), (di + GN) // (GB * N) + g)),
                pl.BlockSpec((GB * J, Q), lambda g, k: (g, ci(k))),
                pl.BlockSpec((GB * J, 1), lambda g, k: (g, 0)),
                xspec,
                pl.BlockSpec((1, GB, JP, N), lambda g, k: (ci(k), g, 0, 0)),
                hspec]
    operands = [xbc, xbc, xbc, dtT, a, dy, hs, dh_last]
    if has_add:
        in_specs += [xspec, nspec, nspec]
        operands += list(add)
    H = G * J
    return pl.pallas_call(
        body, name=name, grid=(G // GB, nc),
        out_shape=(jax.ShapeDtypeStruct((T, di), ACT), jax.ShapeDtypeStruct((T, GN), ACT),
                   jax.ShapeDtypeStruct((T, GN), ACT), jax.ShapeDtypeStruct((H, T), F32),
                   jax.ShapeDtypeStruct((H, Q), F32), jax.ShapeDtypeStruct((G, JP, N), F32)),
        in_specs=in_specs,
        out_specs=(xspec, nspec, nspec,
                   pl.BlockSpec((GB * J, Q), lambda g, k: (g, ci(k))),
                   pl.BlockSpec((GB * J, Q), lambda g, k: (g, 0)),
                   hspec),
        scratch_shapes=[pltpu.VMEM((GB, JP, N), F32)],
        compiler_params=_params(("arbitrary", "arbitrary")),
    )(*operands)


def _mod_fwd(craw, w, b, *, name):
    R, D = craw.shape
    NL = w.shape[1]
    tn = _tile(NL, 512, 128)

    def body(c_ref, w_ref, b_ref, o_ref):
        o_ref[...] = jnp.dot(_silu(c_ref[...]), w_ref[...], preferred_element_type=F32) + b_ref[...]

    return pl.pallas_call(
        body, name=name, grid=(NL // tn,), out_shape=jax.ShapeDtypeStruct((R, NL), F32),
        in_specs=[pl.BlockSpec((R, D), lambda j: (0, 0)), pl.BlockSpec((D, tn), lambda j: (0, j)),
                  pl.BlockSpec((1, tn), lambda j: (0, j))],
        out_specs=pl.BlockSpec((R, tn), lambda j: (0, j)),
        compiler_params=_params(("parallel",)),
    )(craw, w, b)


def _mod_bwd(craw, w, dm, *, name):
    R, D = craw.shape
    NL = w.shape[1]
    tn = _tile(NL, 512, 128)

    def body(c_ref, w_ref, dm_ref, dw_ref, dc_ref):
        first = pl.program_id(0) == 0
        cf = c_ref[...]
        dmv = dm_ref[...]
        dw_ref[...] = lax.dot_general(_silu(cf), dmv, TN, preferred_element_type=F32)
        part = lax.dot_general(dmv, w_ref[...], NT, preferred_element_type=F32) * _dsilu(cf)

        @pl.when(first)
        def _():
            dc_ref[...] = part

        @pl.when(jnp.logical_not(first))
        def _():
            dc_ref[...] += part

    return pl.pallas_call(
        body, name=name, grid=(NL // tn,),
        out_shape=(jax.ShapeDtypeStruct((D, NL), F32), jax.ShapeDtypeStruct((R, D), F32)),
        in_specs=[pl.BlockSpec((R, D), lambda j: (0, 0)), pl.BlockSpec((D, tn), lambda j: (0, j)),
                  pl.BlockSpec((R, tn), lambda j: (0, j))],
        out_specs=(pl.BlockSpec((D, tn), lambda j: (0, j)), pl.BlockSpec((R, D), lambda j: (0, 0))),
        compiler_params=_params(("arbitrary",)),
    )(craw, w, dm)


def _pad_rows(a, rows):
    return jnp.concatenate([a, jnp.zeros((rows - a.shape[0],) + a.shape[1:], a.dtype)], axis=0)


def _cols_whole(g):
    return jnp.transpose(g, (1, 0, 2)).reshape(g.shape[1], N_DEV * g.shape[2])


def _rows_whole(g):
    return g.reshape(N_DEV * g.shape[1], g.shape[2])


def _col_blocks(full):
    K, n8 = full.shape
    return jnp.transpose(full.reshape(K, N_DEV, n8 // N_DEV), (1, 0, 2)).astype(ACT)


def _row_blocks(full):
    K8, n = full.shape
    return full.reshape(N_DEV, K8 // N_DEV, n).astype(ACT)


def kernel(x, c, ctx, c_ctx, w_mod, b_mod, norm_mix, w_in, ssm_conv_w, ssm_conv_b, dt_bias, a_log, d_skip, ssm_norm, cf_conv_w, cf_conv_b, cf_ln_g, cf_ln_b, w_proj_a, w_proj_b, w_out, norm_ffn, w_ffn_gate, w_ffn_up, w_ffn_down, norm_final, loss_target, m_c_ctx, m_w_mod, m_b_mod, m_norm_mix, m_w_in, m_ssm_conv_w, m_ssm_conv_b, m_dt_bias, m_a_log, m_d_skip, m_ssm_norm, m_cf_conv_w, m_cf_conv_b, m_cf_ln_g, m_cf_ln_b, m_w_proj_a, m_w_proj_b, m_w_out, m_norm_ffn, m_w_ffn_gate, m_w_ffn_up, m_w_ffn_down, m_norm_final, v_c_ctx, v_w_mod, v_b_mod, v_norm_mix, v_w_in, v_ssm_conv_w, v_ssm_conv_b, v_dt_bias, v_a_log, v_d_skip, v_ssm_norm, v_cf_conv_w, v_cf_conv_b, v_cf_ln_g, v_cf_ln_b, v_w_proj_a, v_w_proj_b, v_w_out, v_norm_ffn, v_w_ffn_gate, v_w_ffn_up, v_w_ffn_down, v_norm_final):
    args = dict(locals())
    me = 4 * lax.axis_index("x") + 2 * lax.axis_index("y") + lax.axis_index("c")
    T, D = x.shape[1], x.shape[2]
    DI = ssm_norm.shape[1]
    H = DI // HEAD_DIM
    G, J, N = GROUPS, H // GROUPS, STATE
    JP = J * HEAD_DIM
    GN = G * N
    CONV = DI + 2 * GN
    x0 = x[0]
    ctx0 = ctx[0]
    target = loss_target[0]

    st_in = _exchange_start(w_in[0].astype(ACT), c, name="gather_start_w_in", gather=True)
    k5 = ssm_conv_w.shape[1]
    k31 = cf_conv_w.shape[1]
    cw5 = _exchange(_pad_rows(ssm_conv_w[0], 8), name="gather_conv5", gather=True)
    cw5 = jnp.transpose(cw5, (1, 0, 2)).reshape(8, CONV)
    cw31 = _exchange(_pad_rows(cf_conv_w[0], 32), name="gather_conv31", gather=True)
    cw31 = jnp.transpose(cw31, (1, 0, 2)).reshape(32, D)

    c_all = _exchange(_pad_rows(c, 8), name="gather_c", gather=True)[:, 0, :]
    craw = jnp.concatenate([c_all, c_ctx[None, :], jnp.zeros((7, D), F32)], axis=0)
    NL = w_mod.shape[2]
    b_loc = lax.dynamic_slice(b_mod, (0, me * NL), (1, NL))
    m_loc = _mod_fwd(craw, w_mod[0], b_loc, name="mod_fwd")
    m_all = jnp.transpose(_exchange(m_loc, name="gather_mod", gather=True), (1, 0, 2)).reshape(16, N_DEV * NL)
    m_me = lax.dynamic_slice(m_all, (me, 0), (1, 6 * D))
    sh1, sc1, g1, sh2, sc2, g2 = [m_me[:, i * D:(i + 1) * D] for i in range(6)]
    csh1, csc1 = m_all[8:9, 0:D], m_all[8:9, D:2 * D]

    got_in = _exchange_wait(st_in, m_all, name="gather_wait_w_in")
    win = _cols_whole(got_in)
    o_xbc, o_dt, o_glu, o_gates = DI, DI + CONV, DI + CONV + 2 * H, DI + CONV + 2 * H + 2 * D
    w_z, w_xbc, w_dt = win[:, :o_xbc], win[:, o_xbc:o_dt], win[:, o_dt:o_glu]
    w_u, w_v, w_gates = win[:, o_glu:o_glu + D], win[:, o_glu + D:o_gates], win[:, o_gates:]
    st_pa = _exchange_start(w_proj_a[0].astype(ACT), got_in, name="gather_start_w_pa", gather=True)
    st_pb = _exchange_start(w_proj_b[0].astype(ACT), st_pa[4], name="gather_start_w_pb", gather=True)
    st_o = _exchange_start(w_out[0].astype(ACT), st_pb[4], name="gather_start_w_out", gather=True)
    st_gate = _exchange_start(w_ffn_gate[0].astype(ACT), st_o[4], name="gather_start_w_gate", gather=True)
    st_up = _exchange_start(w_ffn_up[0].astype(ACT), st_gate[4], name="gather_start_w_up", gather=True)
    st_down = _exchange_start(w_ffn_down[0].astype(ACT), st_up[4], name="gather_start_w_down", gather=True)
    started = st_down[4]

    a_neg = -jnp.exp(a_log[0])
    a_f, a_b = a_neg[0][:, None], a_neg[1][:, None]
    dtb = dt_bias[0].reshape(2 * H, 1)
    dskip_e = jnp.repeat(d_skip[0], HEAD_DIM)[None, :]

    def front(h, tag, full, after=None):
        out = {}
        out["xbc_raw"] = _mm(h, w_xbc, "nn", name="mm_xbc_" + tag, out_dtype=ACT, after=after)
        dt_raw = _mm(h, w_dt, "nn", name="mm_dt_" + tag, out_dtype=F32)
        out["rawT"] = dt_raw.T
        if full:
            out["z"] = _mm(h, w_z, "nn", name="mm_z_" + tag, out_dtype=ACT)
            out["u"] = _mm(h, w_u, "nn", name="mm_u_" + tag, out_dtype=ACT)
            out["v"] = _mm(h, w_v, "nn", name="mm_v_" + tag, out_dtype=ACT)
            out["gates"] = _mm(h, w_gates, "nn", name="mm_gates_" + tag, out_dtype=ACT)
        out["xbc"] = _conv5_silu_fwd(out["xbc_raw"], cw5, ssm_conv_b, name="conv5_fwd_" + tag)
        out["dtT"] = _dt_fwd(out["rawT"], dtb, name="dt_fwd_" + tag)
        return out

    hc = _norm_mod_fwd(ctx0, norm_mix, csh1, csc1, name="norm_mod_ctx")
    fc = front(hc, "ctx", False, after=started)
    zero_state = jnp.zeros((G, JP, N), F32)
    _, hs_cf, h_f = _ssd_fwd(fc["xbc"], fc["dtT"][:H], a_f, zero_state, reverse=False, name="ssd_fwd_ctx_f", di=DI)
    _, hs_cb, h_b = _ssd_fwd(fc["xbc"], fc["dtT"][H:], a_b, zero_state, reverse=True, name="ssd_fwd_ctx_b", di=DI)

    hx = _norm_mod_fwd(x0, norm_mix, sh1, sc1, name="norm_mod_x")
    fx = front(hx, "x", True)
    y_f, hs_f, _ = _ssd_fwd(fx["xbc"], fx["dtT"][:H], a_f, h_f, reverse=False, name="ssd_fwd_x_f", di=DI)
    y_b, hs_b, _ = _ssd_fwd(fx["xbc"], fx["dtT"][H:], a_b, h_b, reverse=True, name="ssd_fwd_x_b", di=DI)
    ya_in = _gate_norm_fwd(y_f, y_b, fx["xbc"], fx["z"], dskip_e, ssm_norm, name="gate_norm_fwd")
    w_pa = _rows_whole(_exchange_wait(st_pa, ya_in, name="gather_wait_w_pa"))
    ya = _mm(ya_in, w_pa, "nn", name="mm_proj_a", out_dtype=ACT)
    conv_out = _glu_conv_fwd(fx["u"], fx["v"], cw31, cf_conv_b, name="glu_conv_fwd")
    cf = _ln_silu_fwd(conv_out, cf_ln_g, cf_ln_b, name="ln_silu_fwd")
    w_pb = _rows_whole(_exchange_wait(st_pb, cf, name="gather_wait_w_pb"))
    yb = _mm(cf, w_pb, "nn", name="mm_proj_b", out_dtype=ACT)
    merged = _merge_fwd(ya, yb, fx["gates"], name="merge_fwd")
    w_o = _rows_whole(_exchange_wait(st_o, merged, name="gather_wait_w_out"))
    o_mix = _mm(merged, w_o, "nn", name="mm_out", out_dtype=ACT)

    x1, h2 = _resid_norm_mod_fwd(x0, o_mix, g1, norm_ffn, sh2, sc2, name="resid_norm_mod")
    w_gate = _cols_whole(_exchange_wait(st_gate, h2, name="gather_wait_w_gate"))
    gate = _mm(h2, w_gate, "nn", name="mm_gate", out_dtype=ACT)
    w_up = _cols_whole(_exchange_wait(st_up, gate, name="gather_wait_w_up"))
    up = _mm(h2, w_up, "nn", name="mm_up", out_dtype=ACT)
    act = _swiglu_fwd(gate, up, name="swiglu_fwd")
    w_down = _rows_whole(_exchange_wait(st_down, act, name="gather_wait_w_down"))
    dn = _mm(act, w_down, "nn", name="mm_down", out_dtype=ACT)

    loss_part, dx2, d_dn, g_norm_final, d_g2 = _final_fwd_bwd(x1, dn, g2, norm_final[None, :], target, name="final")
    loss = lax.psum(loss_part[0, 0], AXES)

    d_act = _mm(d_dn, w_down, "nt", name="mm_d_act", out_dtype=ACT)
    gw_down = _mm(act, d_dn, "tn", name="mm_gw_down", out_dtype=F32)
    sc_down = _exchange_start(_row_blocks(gw_down), gw_down, name="scatter_start_w_down", gather=False)
    d_gate, d_up = _swiglu_bwd(gate, up, d_act, name="swiglu_bwd")
    gw_gate = _mm(h2, d_gate, "tn", name="mm_gw_gate", out_dtype=F32, after=sc_down[4])
    sc_gate = _exchange_start(_col_blocks(gw_gate), sc_down[4], name="scatter_start_w_gate", gather=False)
    gw_up = _mm(h2, d_up, "tn", name="mm_gw_up", out_dtype=F32, after=sc_gate[4])
    sc_up = _exchange_start(_col_blocks(gw_up), sc_gate[4], name="scatter_start_w_up", gather=False)
    d_h2 = _mm(d_gate, w_gate, "nt", name="mm_d_h2_gate", out_dtype=F32, after=sc_up[4])
    d_h2 = _mm(d_up, w_up, "nt", name="mm_d_h2_up", out_dtype=F32, add=d_h2)
    dx1, d_sh2, d_sc2, g_norm_ffn, d_o, d_g1 = _norm_mod_bwd(
        x1, norm_ffn, sc2, d_h2, name="norm_mod_bwd_ffn", dres=dx2, o=o_mix, g=g1)

    d_merged = _mm(d_o, w_o, "nt", name="mm_d_merged", out_dtype=ACT)
    gw_out = _mm(merged, d_o, "tn", name="mm_gw_out", out_dtype=F32)
    sc_out = _exchange_start(_row_blocks(gw_out), sc_up[4], name="scatter_start_w_out", gather=False)
    d_ya, d_yb, d_gates = _merge_bwd(d_merged, ya, yb, fx["gates"], name="merge_bwd")
    gw_pa = _mm(ya_in, d_ya, "tn", name="mm_gw_pa", out_dtype=F32, after=sc_out[4])
    sc_pa = _exchange_start(_row_blocks(gw_pa), sc_out[4], name="scatter_start_w_pa", gather=False)
    gw_pb = _mm(cf, d_yb, "tn", name="mm_gw_pb", out_dtype=F32, after=sc_pa[4])
    sc_pb = _exchange_start(_row_blocks(gw_pb), sc_pa[4], name="scatter_start_w_pb", gather=False)
    d_ya_in = _mm(d_ya, w_pa, "nt", name="mm_d_ya_in", out_dtype=ACT, after=sc_pb[4])
    d_cf = _mm(d_yb, w_pb, "nt", name="mm_d_cf", out_dtype=ACT)
    d_conv, g_ln_g, g_ln_b = _ln_silu_bwd(conv_out, cf_ln_g, cf_ln_b, d_cf, name="ln_silu_bwd")
    d_u, d_v, g_cw31, g_cb31 = _glu_conv_bwd(fx["u"], fx["v"], cw31, d_conv, name="glu_conv_bwd")
    d_y, d_z, dxs_skip, g_ssm_norm, g_dskip_e = _gate_norm_bwd(
        d_ya_in, y_f, y_b, fx["xbc"], fx["z"], dskip_e, ssm_norm, name="gate_norm_bwd")

    zero_bc = jnp.zeros((T, GN), ACT)
    r1 = _ssd_bwd(fx["xbc"], fx["dtT"][:H], a_f, d_y, hs_f, zero_state, (dxs_skip, zero_bc, zero_bc),
                  reverse=False, name="ssd_bwd_x_f", di=DI)
    r2 = _ssd_bwd(fx["xbc"], fx["dtT"][H:], a_b, d_y, hs_b, zero_state, r1[:3],
                  reverse=True, name="ssd_bwd_x_b", di=DI)
    Tc = ctx0.shape[0]
    zero_yc = jnp.zeros((Tc, DI), ACT)
    r3 = _ssd_bwd(fc["xbc"], fc["dtT"][:H], a_f, zero_yc, hs_cf, r1[5], None,
                  reverse=False, name="ssd_bwd_ctx_f", di=DI)
    r4 = _ssd_bwd(fc["xbc"], fc["dtT"][H:], a_b, zero_yc, hs_cb, r2[5], r3[:3],
                  reverse=True, name="ssd_bwd_ctx_b", di=DI)

    def back(f, rf, rb, tag):
        d_xbc = jnp.concatenate([rb[0], rb[1], rb[2]], axis=1)
        d_xbc_raw, g_w5, g_b5 = _conv5_silu_bwd(f["xbc_raw"], cw5, ssm_conv_b, d_xbc, name="conv5_bwd_" + tag)
        ddtT = jnp.concatenate([rf[3], rb[3]], axis=0)
        d_rawT, g_dtb = _dt_bwd(f["rawT"], dtb, ddtT, name="dt_bwd_" + tag)
        g_a = jnp.stack([jnp.sum(rf[4], axis=1), jnp.sum(rb[4], axis=1)])
        return d_xbc_raw, d_rawT.T.astype(ACT), g_w5, g_b5, g_dtb, g_a

    dx_xbc_raw, dx_dt_raw, gx_w5, gx_b5, gx_dtb, gx_a = back(fx, r1, r2, "x")
    dc_xbc_raw, dc_dt_raw, gc_w5, gc_b5, gc_dtb, gc_a = back(fc, r3, r4, "ctx")

    gw_xbc = _mm(hc, dc_xbc_raw, "tn", name="mm_gw_xbc_ctx", out_dtype=F32)
    gw_xbc = _mm(hx, dx_xbc_raw, "tn", name="mm_gw_xbc", out_dtype=F32, add=gw_xbc)
    gw_dt = _mm(hc, dc_dt_raw, "tn", name="mm_gw_dt_ctx", out_dtype=F32)
    gw_dt = _mm(hx, dx_dt_raw, "tn", name="mm_gw_dt", out_dtype=F32, add=gw_dt)
    gw_z = _mm(hx, d_z, "tn", name="mm_gw_z", out_dtype=F32)
    gw_u = _mm(hx, d_u, "tn", name="mm_gw_u", out_dtype=F32)
    gw_v = _mm(hx, d_v, "tn", name="mm_gw_v", out_dtype=F32)
    gw_gates = _mm(hx, d_gates, "tn", name="mm_gw_gates", out_dtype=F32)
    gw_in = jnp.concatenate([gw_z, gw_xbc, gw_dt, gw_u, gw_v, gw_gates], axis=1)
    sc_in = _exchange_start(_col_blocks(gw_in), sc_pb[4], name="scatter_start_w_in", gather=False)

    d_hx = _mm(d_z, w_z, "nt", name="mm_d_hx_z", out_dtype=F32, after=sc_in[4])
    d_hx = _mm(dx_xbc_raw, w_xbc, "nt", name="mm_d_hx_xbc", out_dtype=F32, add=d_hx)
    d_hx = _mm(dx_dt_raw, w_dt, "nt", name="mm_d_hx_dt", out_dtype=F32, add=d_hx)
    d_hx = _mm(d_u, w_u, "nt", name="mm_d_hx_u", out_dtype=F32, add=d_hx)
    d_hx = _mm(d_v, w_v, "nt", name="mm_d_hx_v", out_dtype=F32, add=d_hx)
    d_hx = _mm(d_gates, w_gates, "nt", name="mm_d_hx_gates", out_dtype=F32, add=d_hx)
    grad_x, d_sh1, d_sc1, gx_norm_mix = _norm_mod_bwd(x0, norm_mix, sc1, d_hx, name="norm_mod_bwd_x", dres=dx1)
    d_hc = _mm(dc_xbc_raw, w_xbc, "nt", name="mm_d_hc_xbc", out_dtype=F32)
    d_hc = _mm(dc_dt_raw, w_dt, "nt", name="mm_d_hc_dt", out_dtype=F32, add=d_hc)
    _, d_csh1, d_csc1, gc_norm_mix = _norm_mod_bwd(ctx0, norm_mix, csc1, d_hc, name="norm_mod_bwd_ctx")

    zD = jnp.zeros((1, D), F32)
    dm_me = jnp.concatenate([d_sh1, d_sc1, d_g1, d_sh2, d_sc2, d_g2], axis=1)
    dm_ctx = jnp.concatenate([d_csh1, d_csc1, zD, zD, zD, zD], axis=1)
    rows16 = lax.broadcasted_iota(jnp.int32, (16, 1), 0)
    dm_rows = jnp.where(rows16 == me, dm_me, 0.0) + jnp.where(rows16 == 8, dm_ctx, 0.0)
    dm_sum = _sum_slots(_exchange(dm_rows, name="gather_dm", gather=True), name="sum_dm")
    g_b_mod = _colsum(dm_sum, name="colsum_dm")
    dm_loc = lax.dynamic_slice(dm_sum, (0, me * NL), (16, NL))
    g_w_mod, dcraw = _mod_bwd(craw, w_mod[0], dm_loc, name="mod_bwd")

    small = [
        ("c_ctx", dcraw[8]), ("norm_mix", gx_norm_mix + gc_norm_mix),
        ("ssm_conv_w", (gx_w5 + gc_w5)[:k5]), ("ssm_conv_b", gx_b5 + gc_b5),
        ("dt_bias", gx_dtb + gc_dtb), ("a_log", (gx_a + gc_a) * a_neg),
        ("d_skip", jnp.sum(g_dskip_e.reshape(H, HEAD_DIM), axis=1)), ("ssm_norm", g_ssm_norm),
        ("cf_conv_w", g_cw31[:k31]), ("cf_conv_b", g_cb31), ("cf_ln_g", g_ln_g), ("cf_ln_b", g_ln_b),
        ("norm_ffn", g_norm_ffn), ("norm_final", g_norm_final),
    ]
    flat = jnp.concatenate([v.reshape(-1) for _, v in small])
    n_small = flat.shape[0]
    rows_small = -(-n_small // 1024) * 8
    flat = jnp.concatenate([flat, jnp.zeros((rows_small * 128 - n_small,), F32)]).reshape(rows_small, 128)
    summed = _sum_slots(_exchange(flat, name="gather_small", gather=True), name="sum_small").reshape(-1)
    g_small = {}
    pos = 0
    for nm, v in small:
        g_small[nm] = summed[pos:pos + v.size].reshape(v.shape)
        pos += v.size
    g_small["b_mod"] = g_b_mod
    n5, n31 = ssm_conv_w.shape[2], cf_conv_w.shape[2]
    g_small["ssm_conv_w"] = lax.dynamic_slice(g_small["ssm_conv_w"], (0, me * n5), (k5, n5))
    g_small["cf_conv_w"] = lax.dynamic_slice(g_small["cf_conv_w"], (0, me * n31), (k31, n31))

    grads, deltas, new_m, new_v = {}, {}, {}, {}

    def adam2d(nm, parts):
        shape = args[nm].shape
        R, C = shape[-2], shape[-1]
        g, d, m2, v2 = _adamw(parts, args[nm].reshape(R, C), args["m_" + nm].reshape(R, C),
                              args["v_" + nm].reshape(R, C), name="adamw_" + nm)
        grads[nm], deltas[nm], new_m[nm], new_v[nm] = [t.reshape(shape) for t in (g, d, m2, v2)]

    adam2d("w_mod", g_w_mod[None])
    behind = grad_x
    for nm, st in (("w_ffn_down", sc_down), ("w_ffn_gate", sc_gate), ("w_ffn_up", sc_up), ("w_out", sc_out),
                   ("w_proj_a", sc_pa), ("w_proj_b", sc_pb), ("w_in", sc_in)):
        parts = _exchange_wait(st, behind, name="scatter_wait_" + nm)
        adam2d(nm, parts)
        behind = grads[nm]

    small_names = ["c_ctx", "b_mod", "norm_mix", "ssm_conv_w", "ssm_conv_b", "dt_bias", "a_log", "d_skip", "ssm_norm",
                   "cf_conv_w", "cf_conv_b", "cf_ln_g", "cf_ln_b", "norm_ffn", "norm_final"]

    def pack(vals):
        f = jnp.concatenate([t.reshape(-1) for t in vals])
        rows = -(-f.shape[0] // 1024) * 8
        return jnp.concatenate([f, jnp.zeros((rows * 128 - f.shape[0],), F32)]).reshape(rows, 128)

    pg = pack([g_small[nm] for nm in small_names])
    pw = pack([args[nm] for nm in small_names])
    pm = pack([args["m_" + nm] for nm in small_names])
    pv = pack([args["v_" + nm] for nm in small_names])
    outs = _adamw(pg[None], pw, pm, pv, name="adamw_small")
    pos = 0
    for nm in small_names:
        shape = args[nm].shape
        size = math.prod(shape)
        vals = [t.reshape(-1)[pos:pos + size].reshape(shape) for t in outs]
        grads[nm], deltas[nm], new_m[nm], new_v[nm] = vals
        pos += size

    order = ["c_ctx", "w_mod", "b_mod", "norm_mix", "w_in", "ssm_conv_w", "ssm_conv_b", "dt_bias", "a_log", "d_skip",
             "ssm_norm", "cf_conv_w", "cf_conv_b", "cf_ln_g", "cf_ln_b", "w_proj_a", "w_proj_b", "w_out", "norm_ffn",
             "w_ffn_gate", "w_ffn_up", "w_ffn_down", "norm_final"]
    return (loss, grad_x[None], *[grads[n] for n in order], *[deltas[n] for n in order],
            *[new_m[n] for n in order], *[new_v[n] for n in order])
```

```python
import functools
import math

import jax
import jax.numpy as jnp
from jax import lax
from jax.experimental import pallas as pl
from jax.experimental.pallas import tpu as pltpu
from jax.experimental.pallas import tpu_sc as plsc

F32 = jnp.float32
ACT = jnp.bfloat16
HIGHEST = lax.Precision.HIGHEST
MESH = pl.DeviceIdType.MESH
AXES = ("x", "y", "c")
N_DEV = 8

GRID_W = 64
CHUNK = 128
SSD_GROUPS_PER_STEP = 2
HEAD_DIM = 64
GROUPS = 8
STATE = 128
EPS = 1e-6
ADAM_LR = 0.001
ADAM_B1 = 0.9
ADAM_B2 = 0.999
ADAM_EPS = 1e-08
ADAM_WD = 0.01
ADAM_STEP = 10

V7X_VMEM_LIMIT = 56 * 1024 * 1024
NEG = -1e30

NN = (((1,), (0,)), ((), ()))
NT = (((1,), (1,)), ((), ()))
TN = (((0,), (0,)), ((), ()))


def _tile(n, target, quantum):
    best = None
    t = quantum
    while t <= min(n, target):
        if n % t == 0:
            best = t
        t += quantum
    return n if best is None else best


def _params(sem=None):
    kw = dict(vmem_limit_bytes=V7X_VMEM_LIMIT)
    if sem is not None:
        kw["dimension_semantics"] = sem
    return pltpu.CompilerParams(**kw)


def _silu(v):
    return v * jax.nn.sigmoid(v)


def _dsilu(v):
    s = jax.nn.sigmoid(v)
    return s * (1.0 + v * (1.0 - s))


def _exchange(x, *, name, gather):
    shape = x.shape[-2:]

    def body(x_ref, o_ref, send_sems, recv_sems, loc_sem):
        ix, iy, ic = lax.axis_index("x"), lax.axis_index("y"), lax.axis_index("c")
        me = 4 * ix + 2 * iy + ic

        def src(d):
            return x_ref if gather else x_ref.at[d]

        def remote(k, slot, peer_xyz, src_ref):
            return pltpu.make_async_remote_copy(
                src_ref=src_ref, dst_ref=o_ref.at[slot], send_sem=send_sems.at[k], recv_sem=recv_sems.at[k],
                device_id=peer_xyz, device_id_type=MESH)

        local = pltpu.make_async_copy(src(me), o_ref.at[me], loc_sem)
        local.start()
        sends, peers = [], []
        for k in range(1, N_DEV):
            px = 1 - ix if k & 4 else ix
            py = 1 - iy if k & 2 else iy
            pc = 1 - ic if k & 1 else ic
            peer = 4 * px + 2 * py + pc
            cp = remote(k - 1, me, (px, py, pc), src(peer))
            cp.start()
            sends.append(cp)
            peers.append((peer, (px, py, pc)))
        for k in range(1, N_DEV):
            peer, xyz = peers[k - 1]
            remote(k - 1, peer, xyz, src(peer)).wait_recv()
        for cp in sends:
            cp.wait_send()
        local.wait()

    return pl.pallas_call(
        body, name=name,
        out_shape=jax.ShapeDtypeStruct((N_DEV,) + shape, x.dtype),
        in_specs=[pl.BlockSpec(memory_space=pl.ANY)],
        out_specs=pl.BlockSpec(memory_space=pl.ANY),
        scratch_shapes=[pltpu.SemaphoreType.DMA((N_DEV - 1,)), pltpu.SemaphoreType.DMA((N_DEV - 1,)),
                        pltpu.SemaphoreType.DMA],
    )(x)


HBM_SPEC = pl.BlockSpec(memory_space=pltpu.HBM)
SEM_SPEC = pl.BlockSpec(memory_space=pltpu.SEMAPHORE)
ANY_SPEC = pl.BlockSpec(memory_space=pl.ANY)
DATAFLOW = pltpu.SideEffectType.DATAFLOW_SIDE_EFFECTING


def _peer(k):
    ix, iy, ic = lax.axis_index("x"), lax.axis_index("y"), lax.axis_index("c")
    px = 1 - ix if k & 4 else ix
    py = 1 - iy if k & 2 else iy
    pc = 1 - ic if k & 1 else ic
    return (px, py, pc), 4 * px + 2 * py + pc


def _exchange_start(x, after, *, name, gather):
    shape = x.shape[-2:]

    def body(after_ref, x_ref, land_ref, send_sem, recv_sem, x_thru, land_thru, token, loc_sem):
        _, me = _peer(0)

        def src(d):
            return x_ref if gather else x_ref.at[d]

        local = pltpu.make_async_copy(src(me), land_ref.at[me], loc_sem)
        local.start()
        local.wait()
        for k in range(1, N_DEV):
            xyz, peer = _peer(k)
            pltpu.make_async_remote_copy(
                src_ref=src(peer), dst_ref=land_ref.at[me], send_sem=send_sem, recv_sem=recv_sem,
                device_id=xyz, device_id_type=MESH).start()
        token[...] = jnp.zeros_like(token)

    land = lax.empty((N_DEV,) + shape, x.dtype)
    return pl.pallas_call(
        body, name=name,
        out_shape=(pltpu.SemaphoreType.DMA(()), pltpu.SemaphoreType.DMA(()), pltpu.HBM(x.shape, x.dtype),
                   pltpu.HBM((N_DEV,) + shape, x.dtype), jax.ShapeDtypeStruct((8, 128), F32)),
        in_specs=(ANY_SPEC, HBM_SPEC, HBM_SPEC),
        out_specs=(SEM_SPEC, SEM_SPEC, HBM_SPEC, HBM_SPEC, pl.BlockSpec(memory_space=pltpu.VMEM)),
        input_output_aliases={1: 2, 2: 3},
        scratch_shapes=[pltpu.SemaphoreType.DMA],
        compiler_params=pltpu.CompilerParams(has_side_effects=DATAFLOW),
    )(after, pltpu.with_memory_space_constraint(x, pltpu.HBM), pltpu.with_memory_space_constraint(land, pltpu.HBM))


def _exchange_wait(started, after, *, name):
    send_sem, recv_sem, x_thru, land_thru, _ = started

    def body(x_ref, land_ref, send_sem, recv_sem, after_ref, x_dead, got_ref):
        xyz, _ = _peer(0)
        seven = land_ref.at[pl.ds(0, N_DEV - 1)]
        cp = pltpu.make_async_remote_copy(src_ref=seven, dst_ref=seven, send_sem=send_sem, recv_sem=recv_sem,
                                          device_id=xyz, device_id_type=MESH)
        cp.wait_send()
        cp.wait_recv()

    return pl.pallas_call(
        body, name=name,
        out_shape=(pltpu.HBM(x_thru.shape, x_thru.dtype), pltpu.HBM(land_thru.shape, land_thru.dtype)),
        in_specs=(HBM_SPEC, HBM_SPEC, SEM_SPEC, SEM_SPEC, ANY_SPEC),
        out_specs=(HBM_SPEC, HBM_SPEC),
        input_output_aliases={0: 0, 1: 1},
        compiler_params=pltpu.CompilerParams(has_side_effects=DATAFLOW),
    )(x_thru, land_thru, send_sem, recv_sem, after)[1]


def _exchange_seq(x, *, name, gather, collective_id):
    shape = x.shape[-2:]
    x_ref = jax.new_ref(x, memory_space=pltpu.MemorySpace.HBM)
    out_ref = jax.empty_ref(jax.ShapeDtypeStruct((N_DEV,) + shape, x.dtype), memory_space=pltpu.MemorySpace.HBM)

    @pl.kernel(mesh=plsc.ScalarSubcoreMesh(axis_name="seq", num_cores=1), name=name,
               scratch_types=(pltpu.SemaphoreType.DMA, pltpu.SemaphoreType.DMA, pltpu.SemaphoreType.DMA),
               compiler_params=pltpu.CompilerParams(collective_id=collective_id))
    def launch(send_sem, recv_sem, loc_sem):
        barrier = pltpu.get_barrier_semaphore()
        for k in range(1, N_DEV):
            xyz, _ = _peer(k)
            pl.semaphore_signal(barrier, inc=1, device_id=xyz, device_id_type=MESH)
        pl.semaphore_wait(barrier, N_DEV - 1)
        mine, me = _peer(0)

        def src(d):
            return x_ref if gather else x_ref.at[d]

        local = pltpu.make_async_copy(src(me), out_ref.at[me], loc_sem)
        local.start()
        for k in range(1, N_DEV):
            xyz, peer = _peer(k)
            pltpu.make_async_remote_copy(
                src_ref=src(peer), dst_ref=out_ref.at[me], send_sem=send_sem, recv_sem=recv_sem,
                device_id=xyz, device_id_type=MESH).start()
        seven = out_ref.at[pl.ds(0, N_DEV - 1)]
        pltpu.make_async_remote_copy(src_ref=seven, dst_ref=seven, send_sem=send_sem, recv_sem=recv_sem,
                                     device_id=mine, device_id_type=MESH).wait()
        local.wait()

    launch()
    return out_ref[...]


def _sum_slots(x, *, name):
    n, R, C = x.shape
    tr = _tile(R, 256, 8)

    def body(x_ref, o_ref):
        acc = x_ref[0].astype(F32)
        for d in range(1, n):
            acc = acc + x_ref[d].astype(F32)
        o_ref[...] = acc

    return pl.pallas_call(
        body, name=name, grid=(R // tr,),
        out_shape=jax.ShapeDtypeStruct((R, C), F32),
        in_specs=[pl.BlockSpec((n, tr, C), lambda i: (0, i, 0))],
        out_specs=pl.BlockSpec((tr, C), lambda i: (i, 0)),
        compiler_params=_params(("parallel",)),
    )(x)


def _colsum(x, *, name):
    R, C = x.shape

    def body(x_ref, o_ref):
        o_ref[...] = jnp.sum(x_ref[...], axis=0, keepdims=True)

    return pl.pallas_call(
        body, name=name, out_shape=jax.ShapeDtypeStruct((1, C), F32),
        in_specs=[pl.BlockSpec((R, C), lambda: (0, 0))], out_specs=pl.BlockSpec((1, C), lambda: (0, 0)),
        compiler_params=_params(),
    )(x)


def _adamw(parts, w, m, v, *, name):
    n, R, C = parts.shape
    tr = _tile(R, 128, 8)
    c1 = 1.0 - ADAM_B1 ** ADAM_STEP
    c2 = 1.0 - ADAM_B2 ** ADAM_STEP

    def body(p_ref, w_ref, m_ref, v_ref, g_ref, d_ref, nm_ref, nv_ref):
        g = p_ref[0].astype(F32)
        for d in range(1, n):
            g = g + p_ref[d].astype(F32)
        mn = ADAM_B1 * m_ref[...] + (1.0 - ADAM_B1) * g
        vn = ADAM_B2 * v_ref[...] + (1.0 - ADAM_B2) * (g * g)
        g_ref[...] = g
        nm_ref[...] = mn
        nv_ref[...] = vn
        d_ref[...] = -ADAM_LR * ((mn / c1) / (jnp.sqrt(vn / c2) + ADAM_EPS) + ADAM_WD * w_ref[...])

    spec = pl.BlockSpec((tr, C), lambda i: (i, 0))
    shp = jax.ShapeDtypeStruct((R, C), F32)
    return pl.pallas_call(
        body, name=name, grid=(R // tr,), out_shape=(shp, shp, shp, shp),
        in_specs=[pl.BlockSpec((n, tr, C), lambda i: (0, i, 0)), spec, spec, spec],
        out_specs=(spec, spec, spec, spec),
        compiler_params=_params(("parallel",)),
    )(parts, w, m, v)


MM_VMEM_BUDGET = 40 * 1024 * 1024
MM_TK_MAX = 2048
MXU_WIDTH = 256


def _divisors(n, quantum, cap):
    return [t for t in range(quantum, min(n, cap) + 1, quantum) if n % t == 0] or [n]


def _mm_tiles(M, N, K, mode, a_bytes, b_bytes, o_bytes, has_add):
    tk = max(_divisors(K, 128, MM_TK_MAX))
    nk = K // tk
    best = None
    for tm in _divisors(M, 128 if mode == "tn" else 8, 1024):
        for tn in _divisors(N, 128, 3072):
            need = 2 * (tm * tk * a_bytes + tk * tn * b_bytes) + 2 * tm * tn * o_bytes + tm * tn * 4
            need += tm * tn * 4 if nk > 1 else 0
            need += 2 * tm * tn * 4 if has_add else 0
            if need > MM_VMEM_BUDGET:
                continue
            score = (tn % MXU_WIDTH == 0 or tn == N, tm * tn, tm)
            if best is None or score > best[0]:
                best = (score, tm, tn)
    assert best is not None, (M, N, K)
    return best[1], best[2], tk


def _mm(a, b, mode, *, name, out_dtype, add=None, after=None):
    if mode == "nn":
        (M, K), (K2, N) = a.shape, b.shape
    elif mode == "nt":
        (M, K), (N, K2) = a.shape, b.shape
    else:
        (K, M), (K2, N) = a.shape, b.shape
    assert K == K2, (name, a.shape, b.shape)
    tm, tn, tk = _mm_tiles(M, N, K, mode, a.dtype.itemsize, b.dtype.itemsize, jnp.dtype(out_dtype).itemsize,
                           add is not None)
    nk = K // tk
    dims = {"nn": NN, "nt": NT, "tn": TN}[mode]

    a_spec = {"nn": pl.BlockSpec((tm, tk), lambda i, j, k: (i, k)),
              "nt": pl.BlockSpec((tm, tk), lambda i, j, k: (i, k)),
              "tn": pl.BlockSpec((tk, tm), lambda i, j, k: (k, i))}[mode]
    b_spec = {"nn": pl.BlockSpec((tk, tn), lambda i, j, k: (k, j)),
              "nt": pl.BlockSpec((tn, tk), lambda i, j, k: (j, k)),
              "tn": pl.BlockSpec((tk, tn), lambda i, j, k: (k, j))}[mode]
    o_spec = pl.BlockSpec((tm, tn), lambda i, j, k: (i, j))

    def body(a_ref, b_ref, *rest):
        rest = list(rest)
        add_ref = rest.pop(0) if add is not None else None
        if after is not None:
            rest.pop(0)
        o_ref = rest.pop(0)
        part = lax.dot_general(a_ref[...].astype(ACT), b_ref[...].astype(ACT), dims, preferred_element_type=F32)

        def finish(r):
            if add is not None:
                r = r + add_ref[...].astype(F32)
            o_ref[...] = r.astype(out_dtype)

        if nk == 1:
            finish(part)
            return
        acc = rest.pop(0)
        k = pl.program_id(2)

        @pl.when(k == 0)
        def _():
            acc[...] = part

        @pl.when(jnp.logical_and(k > 0, k < nk - 1))
        def _():
            acc[...] += part

        @pl.when(k == nk - 1)
        def _():
            finish(acc[...] + part)

    operands = [a, b] + ([] if add is None else [add])
    in_specs = [a_spec, b_spec] + ([] if add is None else [o_spec])
    if after is not None:
        operands.append(after)
        in_specs.append(ANY_SPEC)
    return pl.pallas_call(
        body, name=name, grid=(M // tm, N // tn, nk),
        out_shape=jax.ShapeDtypeStruct((M, N), out_dtype),
        in_specs=in_specs, out_specs=o_spec,
        scratch_shapes=[pltpu.VMEM((tm, tn), F32)] if nk > 1 else [],
        compiler_params=_params(("parallel", "parallel", "arbitrary")),
    )(*operands)


def _row(tr, cols, blk=0):
    return pl.BlockSpec((tr, cols), lambda i: (i, blk))


def _vec(cols):
    return pl.BlockSpec((1, cols), lambda i: (0, 0))


def _rms(xf):
    return lax.rsqrt(jnp.mean(xf * xf, axis=-1, keepdims=True) + EPS)


def _rms_bwd(dxhat, xhat, r):
    return r * (dxhat - xhat * jnp.mean(dxhat * xhat, axis=-1, keepdims=True))


def _acc_rows(ref, val, first):
    s = jnp.sum(val, axis=0, keepdims=True)

    @pl.when(first)
    def _():
        ref[...] = s

    @pl.when(jnp.logical_not(first))
    def _():
        ref[...] += s


def _norm_mod_fwd(x, nw, shift, scale, *, name):
    T, D = x.shape
    tr = _tile(T, 256, 8)

    def body(x_ref, nw_ref, sh_ref, sc_ref, o_ref):
        xf = x_ref[...]
        n = xf * _rms(xf) * nw_ref[...]
        o_ref[...] = (n * (1.0 + sc_ref[...]) + sh_ref[...]).astype(ACT)

    return pl.pallas_call(
        body, name=name, grid=(T // tr,), out_shape=jax.ShapeDtypeStruct((T, D), ACT),
        in_specs=[_row(tr, D), _vec(D), _vec(D), _vec(D)], out_specs=_row(tr, D),
        compiler_params=_params(("parallel",)),
    )(x, nw, shift, scale)


def _resid_norm_mod_fwd(x, o, g, nw, shift, scale, *, name):
    T, D = x.shape
    tr = _tile(T, 256, 8)

    def body(x_ref, o_ref, g_ref, nw_ref, sh_ref, sc_ref, x1_ref, h_ref):
        x1 = x_ref[...] + g_ref[...] * o_ref[...].astype(F32)
        x1_ref[...] = x1
        n = x1 * _rms(x1) * nw_ref[...]
        h_ref[...] = (n * (1.0 + sc_ref[...]) + sh_ref[...]).astype(ACT)

    return pl.pallas_call(
        body, name=name, grid=(T // tr,),
        out_shape=(jax.ShapeDtypeStruct((T, D), F32), jax.ShapeDtypeStruct((T, D), ACT)),
        in_specs=[_row(tr, D), _row(tr, D), _vec(D), _vec(D), _vec(D), _vec(D)],
        out_specs=(_row(tr, D), _row(tr, D)),
        compiler_params=_params(("parallel",)),
    )(x, o, g, nw, shift, scale)


def _final_fwd_bwd(x1, dn, g2, nw, target, *, name):
    T, D = x1.shape
    tr = _tile(T, 256, 8)

    def body(x1_ref, dn_ref, g_ref, nw_ref, t_ref, loss_ref, dx_ref, ddn_ref, dnw_ref, dg_ref):
        first = pl.program_id(0) == 0
        dn_f = dn_ref[...].astype(F32)
        x2 = x1_ref[...] + g_ref[...] * dn_f
        r = _rms(x2)
        xhat = x2 * r
        err = xhat * nw_ref[...] - t_ref[...]
        part = 0.5 * jnp.sum(jnp.mean(err * err, axis=-1, keepdims=True), axis=0, keepdims=True)

        @pl.when(first)
        def _():
            loss_ref[...] = part

        @pl.when(jnp.logical_not(first))
        def _():
            loss_ref[...] += part

        dy = err * (1.0 / D)
        _acc_rows(dnw_ref, dy * xhat, first)
        dx2 = _rms_bwd(dy * nw_ref[...], xhat, r)
        dx_ref[...] = dx2
        ddn_ref[...] = (g_ref[...] * dx2).astype(ACT)
        _acc_rows(dg_ref, dx2 * dn_f, first)

    vec = jax.ShapeDtypeStruct((1, D), F32)
    return pl.pallas_call(
        body, name=name, grid=(T // tr,),
        out_shape=(jax.ShapeDtypeStruct((1, 1), F32), jax.ShapeDtypeStruct((T, D), F32),
                   jax.ShapeDtypeStruct((T, D), ACT), vec, vec),
        in_specs=[_row(tr, D), _row(tr, D), _vec(D), _vec(D), _row(tr, D)],
        out_specs=(pl.BlockSpec((1, 1), lambda i: (0, 0)), _row(tr, D), _row(tr, D), _vec(D), _vec(D)),
        compiler_params=_params(("arbitrary",)),
    )(x1, dn, g2, nw, target)


def _norm_mod_bwd(xin, nw, scale, dh, *, name, dres=None, o=None, g=None):
    T, D = xin.shape
    tr = _tile(T, 256, 8)
    has_res, has_o = dres is not None, o is not None

    def body(*refs):
        refs = list(refs)
        x_ref, nw_ref, sc_ref, dh_ref = refs[:4]
        pos = 4
        dres_ref = o_ref = g_ref = None
        if has_res:
            dres_ref = refs[pos]
            pos += 1
        if has_o:
            o_ref, g_ref = refs[pos], refs[pos + 1]
            pos += 2
        dx_ref, dsh_ref, dsc_ref, dnw_ref = refs[pos:pos + 4]
        pos += 4
        first = pl.program_id(0) == 0
        xf = x_ref[...]
        r = _rms(xf)
        xhat = xf * r
        n = xhat * nw_ref[...]
        dhf = dh_ref[...].astype(F32)
        _acc_rows(dsh_ref, dhf, first)
        _acc_rows(dsc_ref, dhf * n, first)
        dn = dhf * (1.0 + sc_ref[...])
        _acc_rows(dnw_ref, dn * xhat, first)
        dx = _rms_bwd(dn * nw_ref[...], xhat, r)
        if has_res:
            dx = dx + dres_ref[...]
        dx_ref[...] = dx
        if has_o:
            do_ref, dg_ref = refs[pos], refs[pos + 1]
            do_ref[...] = (g_ref[...] * dx).astype(ACT)
            _acc_rows(dg_ref, dx * o_ref[...].astype(F32), first)

    vec = jax.ShapeDtypeStruct((1, D), F32)
    operands = [xin, nw, scale, dh]
    in_specs = [_row(tr, D), _vec(D), _vec(D), _row(tr, D)]
    if has_res:
        operands.append(dres)
        in_specs.append(_row(tr, D))
    if has_o:
        operands += [o, g]
        in_specs += [_row(tr, D), _vec(D)]
    out_shape = [jax.ShapeDtypeStruct((T, D), F32), vec, vec, vec]
    out_specs = [_row(tr, D), _vec(D), _vec(D), _vec(D)]
    if has_o:
        out_shape += [jax.ShapeDtypeStruct((T, D), ACT), vec]
        out_specs += [_row(tr, D), _vec(D)]
    return pl.pallas_call(
        body, name=name, grid=(T // tr,), out_shape=tuple(out_shape),
        in_specs=in_specs, out_specs=tuple(out_specs),
        compiler_params=_params(("arbitrary",)),
    )(*operands)


def _swiglu_fwd(gate, up, *, name):
    T, F = gate.shape
    tr = _tile(T, 256, 8)

    def body(g_ref, u_ref, o_ref):
        o_ref[...] = (_silu(g_ref[...].astype(F32)) * u_ref[...].astype(F32)).astype(ACT)

    return pl.pallas_call(
        body, name=name, grid=(T // tr,), out_shape=jax.ShapeDtypeStruct((T, F), ACT),
        in_specs=[_row(tr, F), _row(tr, F)], out_specs=_row(tr, F),
        compiler_params=_params(("parallel",)),
    )(gate, up)


def _swiglu_bwd(gate, up, dact, *, name):
    T, F = gate.shape
    tr = _tile(T, 256, 8)

    def body(g_ref, u_ref, d_ref, dg_ref, du_ref):
        gf, uf, df = g_ref[...].astype(F32), u_ref[...].astype(F32), d_ref[...].astype(F32)
        dg_ref[...] = (df * uf * _dsilu(gf)).astype(ACT)
        du_ref[...] = (df * _silu(gf)).astype(ACT)

    shp = jax.ShapeDtypeStruct((T, F), ACT)
    return pl.pallas_call(
        body, name=name, grid=(T // tr,), out_shape=(shp, shp),
        in_specs=[_row(tr, F)] * 3, out_specs=(_row(tr, F), _row(tr, F)),
        compiler_params=_params(("parallel",)),
    )(gate, up, dact)


def _merge_fwd(ya, yb, gates, *, name):
    T, D = ya.shape
    tr = _tile(T, 256, 8)

    def body(a_ref, b_ref, g_ref, o_ref):
        ga = g_ref[:, :D].astype(F32)
        gb = g_ref[:, D:].astype(F32)
        o_ref[...] = (jax.nn.sigmoid(ga) * a_ref[...].astype(F32)
                      + jax.nn.sigmoid(gb) * b_ref[...].astype(F32)).astype(ACT)

    return pl.pallas_call(
        body, name=name, grid=(T // tr,), out_shape=jax.ShapeDtypeStruct((T, D), ACT),
        in_specs=[_row(tr, D), _row(tr, D), _row(tr, 2 * D)], out_specs=_row(tr, D),
        compiler_params=_params(("parallel",)),
    )(ya, yb, gates)


def _merge_bwd(dmer, ya, yb, gates, *, name):
    T, D = ya.shape
    tr = _tile(T, 256, 8)

    def body(d_ref, a_ref, b_ref, g_ref, da_ref, db_ref, dg_ref):
        d = d_ref[...].astype(F32)
        sa = jax.nn.sigmoid(g_ref[:, :D].astype(F32))
        sb = jax.nn.sigmoid(g_ref[:, D:].astype(F32))
        da_ref[...] = (d * sa).astype(ACT)
        db_ref[...] = (d * sb).astype(ACT)
        dg_ref[:, :D] = (d * a_ref[...].astype(F32) * sa * (1.0 - sa)).astype(ACT)
        dg_ref[:, D:] = (d * b_ref[...].astype(F32) * sb * (1.0 - sb)).astype(ACT)

    shp = jax.ShapeDtypeStruct((T, D), ACT)
    return pl.pallas_call(
        body, name=name, grid=(T // tr,), out_shape=(shp, shp, jax.ShapeDtypeStruct((T, 2 * D), ACT)),
        in_specs=[_row(tr, D), _row(tr, D), _row(tr, D), _row(tr, 2 * D)],
        out_specs=(_row(tr, D), _row(tr, D), _row(tr, 2 * D)),
        compiler_params=_params(("parallel",)),
    )(dmer, ya, yb, gates)


def _gate_norm_fwd(yf, yb, xbc, z, dskip, nw, *, name):
    T, DI = z.shape
    tr = _tile(T, 128, 8)

    def body(yf_ref, yb_ref, xs_ref, z_ref, ds_ref, nw_ref, o_ref):
        y = yf_ref[...].astype(F32) + yb_ref[...].astype(F32) + ds_ref[...] * xs_ref[...].astype(F32)
        gz = y * _silu(z_ref[...].astype(F32))
        o_ref[...] = (gz * _rms(gz) * nw_ref[...]).astype(ACT)

    return pl.pallas_call(
        body, name=name, grid=(T // tr,), out_shape=jax.ShapeDtypeStruct((T, DI), ACT),
        in_specs=[_row(tr, DI), _row(tr, DI), _row(tr, DI), _row(tr, DI), _vec(DI), _vec(DI)],
        out_specs=_row(tr, DI),
        compiler_params=_params(("parallel",)),
    )(yf, yb, xbc, z, dskip, nw)


def _gate_norm_bwd(dout, yf, yb, xbc, z, dskip, nw, *, name):
    T, DI = z.shape
    tr = _tile(T, 128, 8)

    def body(do_ref, yf_ref, yb_ref, xs_ref, z_ref, ds_ref, nw_ref, dy_ref, dz_ref, dxs_ref, dnw_ref, dds_ref):
        first = pl.program_id(0) == 0
        xs = xs_ref[...].astype(F32)
        zf = z_ref[...].astype(F32)
        y = yf_ref[...].astype(F32) + yb_ref[...].astype(F32) + ds_ref[...] * xs
        sz = _silu(zf)
        gz = y * sz
        r = _rms(gz)
        ghat = gz * r
        do = do_ref[...].astype(F32)
        _acc_rows(dnw_ref, do * ghat, first)
        dgz = _rms_bwd(do * nw_ref[...], ghat, r)
        dy = dgz * sz
        dy_ref[...] = dy.astype(ACT)
        dz_ref[...] = (dgz * y * _dsilu(zf)).astype(ACT)
        dxs_ref[...] = (dy * ds_ref[...]).astype(ACT)
        _acc_rows(dds_ref, dy * xs, first)

    shp = jax.ShapeDtypeStruct((T, DI), ACT)
    vec = jax.ShapeDtypeStruct((1, DI), F32)
    return pl.pallas_call(
        body, name=name, grid=(T // tr,), out_shape=(shp, shp, shp, vec, vec),
        in_specs=[_row(tr, DI)] * 5 + [_vec(DI), _vec(DI)],
        out_specs=(_row(tr, DI), _row(tr, DI), _row(tr, DI), _vec(DI), _vec(DI)),
        compiler_params=_params(("arbitrary",)),
    )(dout, yf, yb, xbc, z, dskip, nw)


def _ln_silu_fwd(x, g, b, *, name):
    T, D = x.shape
    tr = _tile(T, 256, 8)

    def body(x_ref, g_ref, b_ref, o_ref):
        xf = x_ref[...].astype(F32)
        xc = xf - jnp.mean(xf, axis=-1, keepdims=True)
        rstd = lax.rsqrt(jnp.mean(xc * xc, axis=-1, keepdims=True) + EPS)
        o_ref[...] = _silu(xc * rstd * g_ref[...] + b_ref[...]).astype(ACT)

    return pl.pallas_call(
        body, name=name, grid=(T // tr,), out_shape=jax.ShapeDtypeStruct((T, D), ACT),
        in_specs=[_row(tr, D), _vec(D), _vec(D)], out_specs=_row(tr, D),
        compiler_params=_params(("parallel",)),
    )(x, g, b)


def _ln_silu_bwd(x, g, b, dcf, *, name):
    T, D = x.shape
    tr = _tile(T, 256, 8)

    def body(x_ref, g_ref, b_ref, d_ref, dx_ref, dg_ref, db_ref):
        first = pl.program_id(0) == 0
        xf = x_ref[...].astype(F32)
        xc = xf - jnp.mean(xf, axis=-1, keepdims=True)
        rstd = lax.rsqrt(jnp.mean(xc * xc, axis=-1, keepdims=True) + EPS)
        xhat = xc * rstd
        dyln = d_ref[...].astype(F32) * _dsilu(xhat * g_ref[...] + b_ref[...])
        _acc_rows(dg_ref, dyln * xhat, first)
        _acc_rows(db_ref, dyln, first)
        dxh = dyln * g_ref[...]
        dx = rstd * (dxh - jnp.mean(dxh, axis=-1, keepdims=True)
                     - xhat * jnp.mean(dxh * xhat, axis=-1, keepdims=True))
        dx_ref[...] = dx.astype(ACT)

    vec = jax.ShapeDtypeStruct((1, D), F32)
    return pl.pallas_call(
        body, name=name, grid=(T // tr,), out_shape=(jax.ShapeDtypeStruct((T, D), ACT), vec, vec),
        in_specs=[_row(tr, D), _vec(D), _vec(D), _row(tr, D)],
        out_specs=(_row(tr, D), _vec(D), _vec(D)),
        compiler_params=_params(("arbitrary",)),
    )(x, g, b, dcf)


CONV_CW = 128
CONV_RT = 256
SEQ_PAD = 8


def _window(ext, off, n):
    if off % 8 == 0:
        return ext[off:off + n]
    return pltpu.roll(ext, ext.shape[0] - off, 0)[:n]


def _sum8(v):
    R, C = v.shape
    return jnp.sum(v.reshape(R // 8, 8, C), axis=0)


def _conv5_silu_fwd(x, w, b, *, name):
    T, C = x.shape
    K = 5
    cw, rt = CONV_CW, _tile(T, CONV_RT, 8)
    half = K // 2

    def body(x_ref, w_ref, b_ref, o_ref, pad):
        zeros = jnp.zeros((SEQ_PAD, cw), F32)
        pad[0:SEQ_PAD, :] = zeros
        pad[T + SEQ_PAD:T + 2 * SEQ_PAD, :] = zeros

        def fill(i, c):
            base = pl.multiple_of(i * rt, rt)
            pad[pl.ds(base + SEQ_PAD, rt), :] = x_ref[pl.ds(base, rt), :].astype(F32)
            return c

        lax.fori_loop(0, T // rt, fill, 0)
        wv = w_ref[...]
        bias = b_ref[...]

        def step(i, c):
            base = pl.multiple_of(i * rt, rt)
            ext = pad[pl.ds(base, rt + 2 * SEQ_PAD), :]
            acc = jnp.zeros((rt, cw), F32) + bias
            for k in range(K):
                acc = acc + wv[k:k + 1, :] * _window(ext, SEQ_PAD + k - half, rt)
            o_ref[pl.ds(base, rt), :] = _silu(acc).astype(ACT)
            return c

        lax.fori_loop(0, T // rt, step, 0)

    return pl.pallas_call(
        body, name=name, grid=(C // cw,), out_shape=jax.ShapeDtypeStruct((T, C), ACT),
        in_specs=[pl.BlockSpec((T, cw), lambda j: (0, j)), pl.BlockSpec((8, cw), lambda j: (0, j)),
                  pl.BlockSpec((1, cw), lambda j: (0, j))],
        out_specs=pl.BlockSpec((T, cw), lambda j: (0, j)),
        scratch_shapes=[pltpu.VMEM((T + 2 * SEQ_PAD, cw), F32)],
        compiler_params=_params(("parallel",)),
    )(x, w, b)


def _conv5_silu_bwd(x, w, b, dout, *, name):
    T, C = x.shape
    K = 5
    cw, rt = CONV_CW, _tile(T, CONV_RT, 8)
    half = K // 2

    def body(x_ref, w_ref, b_ref, d_ref, dx_ref, dw_ref, db_ref, pad, dpad, wacc):
        zeros = jnp.zeros((SEQ_PAD, cw), F32)
        for p in (pad, dpad):
            p[0:SEQ_PAD, :] = zeros
            p[T + SEQ_PAD:T + 2 * SEQ_PAD, :] = zeros
        wacc[...] = jnp.zeros_like(wacc)

        def fill(i, c):
            base = pl.multiple_of(i * rt, rt)
            pad[pl.ds(base + SEQ_PAD, rt), :] = x_ref[pl.ds(base, rt), :].astype(F32)
            return c

        lax.fori_loop(0, T // rt, fill, 0)
        wv = w_ref[...]
        bias = b_ref[...]

        def step1(i, c):
            base = pl.multiple_of(i * rt, rt)
            ext = pad[pl.ds(base, rt + 2 * SEQ_PAD), :]
            wins = [_window(ext, SEQ_PAD + k - half, rt) for k in range(K)]
            pre = jnp.zeros((rt, cw), F32) + bias
            for k in range(K):
                pre = pre + wv[k:k + 1, :] * wins[k]
            dpre = d_ref[pl.ds(base, rt), :].astype(F32) * _dsilu(pre)
            dpad[pl.ds(base + SEQ_PAD, rt), :] = dpre
            for k in range(K):
                wacc[k] += _sum8(dpre * wins[k])
            wacc[K] += _sum8(dpre)
            return c

        lax.fori_loop(0, T // rt, step1, 0)

        def step2(i, c):
            base = pl.multiple_of(i * rt, rt)
            ext = dpad[pl.ds(base, rt + 2 * SEQ_PAD), :]
            acc = jnp.zeros((rt, cw), F32)
            for k in range(K):
                acc = acc + wv[k:k + 1, :] * _window(ext, SEQ_PAD - (k - half), rt)
            dx_ref[pl.ds(base, rt), :] = acc.astype(ACT)
            return c

        lax.fori_loop(0, T // rt, step2, 0)
        rows = [jnp.sum(wacc[k], axis=0, keepdims=True) for k in range(K)]
        rows += [jnp.zeros((1, cw), F32)] * (8 - K)
        dw_ref[...] = jnp.concatenate(rows, axis=0)
        db_ref[...] = jnp.sum(wacc[K], axis=0, keepdims=True)

    return pl.pallas_call(
        body, name=name, grid=(C // cw,),
        out_shape=(jax.ShapeDtypeStruct((T, C), ACT), jax.ShapeDtypeStruct((8, C), F32),
                   jax.ShapeDtypeStruct((1, C), F32)),
        in_specs=[pl.BlockSpec((T, cw), lambda j: (0, j)), pl.BlockSpec((8, cw), lambda j: (0, j)),
                  pl.BlockSpec((1, cw), lambda j: (0, j)), pl.BlockSpec((T, cw), lambda j: (0, j))],
        out_specs=(pl.BlockSpec((T, cw), lambda j: (0, j)), pl.BlockSpec((8, cw), lambda j: (0, j)),
                   pl.BlockSpec((1, cw), lambda j: (0, j))),
        scratch_shapes=[pltpu.VMEM((T + 2 * SEQ_PAD, cw), F32), pltpu.VMEM((T + 2 * SEQ_PAD, cw), F32),
                        pltpu.VMEM((K + 1, 8, cw), F32)],
        compiler_params=_params(("parallel",)),
    )(x, w, b, dout)


def _glu_conv_fwd(u, v, w, b, *, name):
    T, C = u.shape
    K = 31
    KP = w.shape[0]
    cw, rt = CONV_CW, _tile(T, CONV_RT, GRID_W)
    half = K // 2
    P = half * GRID_W

    def body(u_ref, v_ref, w_ref, b_ref, o_ref, pad):
        zeros = jnp.zeros((P, cw), F32)
        pad[0:P, :] = zeros
        pad[T + P:T + 2 * P, :] = zeros

        def fill(i, c):
            base = pl.multiple_of(i * rt, rt)
            uf = u_ref[pl.ds(base, rt), :].astype(F32)
            vf = v_ref[pl.ds(base, rt), :].astype(F32)
            pad[pl.ds(base + P, rt), :] = uf * jax.nn.sigmoid(vf)
            return c

        lax.fori_loop(0, T // rt, fill, 0)
        wv = w_ref[...]
        bias = b_ref[...]

        def step(i, c):
            base = pl.multiple_of(i * rt, rt)
            acc = jnp.zeros((rt, cw), F32) + bias
            for k in range(K):
                acc = acc + wv[k:k + 1, :] * pad[pl.ds(base + k * GRID_W, rt), :]
            o_ref[pl.ds(base, rt), :] = acc.astype(ACT)
            return c

        lax.fori_loop(0, T // rt, step, 0)

    col = pl.BlockSpec((T, cw), lambda j: (0, j))
    return pl.pallas_call(
        body, name=name, grid=(C // cw,), out_shape=jax.ShapeDtypeStruct((T, C), ACT),
        in_specs=[col, col, pl.BlockSpec((KP, cw), lambda j: (0, j)), pl.BlockSpec((1, cw), lambda j: (0, j))],
        out_specs=col,
        scratch_shapes=[pltpu.VMEM((T + 2 * P, cw), F32)],
        compiler_params=_params(("parallel",)),
    )(u, v, w, b)


def _glu_conv_bwd(u, v, w, dout, *, name):
    T, C = u.shape
    K = 31
    KP = w.shape[0]
    cw, rt = CONV_CW, _tile(T, CONV_RT, GRID_W)
    half = K // 2
    P = half * GRID_W

    def body(u_ref, v_ref, w_ref, d_ref, du_ref, dv_ref, dw_ref, db_ref, pad, dpad, wacc):
        zeros = jnp.zeros((P, cw), F32)
        for p in (pad, dpad):
            p[0:P, :] = zeros
            p[T + P:T + 2 * P, :] = zeros
        wacc[...] = jnp.zeros_like(wacc)

        def fill(i, c):
            base = pl.multiple_of(i * rt, rt)
            uf = u_ref[pl.ds(base, rt), :].astype(F32)
            vf = v_ref[pl.ds(base, rt), :].astype(F32)
            pad[pl.ds(base + P, rt), :] = uf * jax.nn.sigmoid(vf)
            dpad[pl.ds(base + P, rt), :] = d_ref[pl.ds(base, rt), :].astype(F32)
            return c

        lax.fori_loop(0, T // rt, fill, 0)
        wv = w_ref[...]

        def step(i, c):
            base = pl.multiple_of(i * rt, rt)
            d = dpad[pl.ds(base + P, rt), :]
            dg = jnp.zeros((rt, cw), F32)
            for k in range(K):
                wacc[k] += _sum8(d * pad[pl.ds(base + k * GRID_W, rt), :])
                dg = dg + wv[k:k + 1, :] * dpad[pl.ds(base + (K - 1 - k) * GRID_W, rt), :]
            wacc[K] += _sum8(d)
            uf = u_ref[pl.ds(base, rt), :].astype(F32)
            sv = jax.nn.sigmoid(v_ref[pl.ds(base, rt), :].astype(F32))
            du_ref[pl.ds(base, rt), :] = (dg * sv).astype(ACT)
            dv_ref[pl.ds(base, rt), :] = (dg * uf * sv * (1.0 - sv)).astype(ACT)
            return c

        lax.fori_loop(0, T // rt, step, 0)
        rows = [jnp.sum(wacc[k], axis=0, keepdims=True) for k in range(K)]
        rows += [jnp.zeros((1, cw), F32)] * (KP - K)
        dw_ref[...] = jnp.concatenate(rows, axis=0)
        db_ref[...] = jnp.sum(wacc[K], axis=0, keepdims=True)

    col = pl.BlockSpec((T, cw), lambda j: (0, j))
    shp = jax.ShapeDtypeStruct((T, C), ACT)
    return pl.pallas_call(
        body, name=name, grid=(C // cw,),
        out_shape=(shp, shp, jax.ShapeDtypeStruct((KP, C), F32), jax.ShapeDtypeStruct((1, C), F32)),
        in_specs=[col, col, pl.BlockSpec((KP, cw), lambda j: (0, j)), col],
        out_specs=(col, col, pl.BlockSpec((KP, cw), lambda j: (0, j)), pl.BlockSpec((1, cw), lambda j: (0, j))),
        scratch_shapes=[pltpu.VMEM((T + 2 * P, cw), F32), pltpu.VMEM((T + 2 * P, cw), F32),
                        pltpu.VMEM((K + 1, 8, cw), F32)],
        compiler_params=_params(("parallel",)),
    )(u, v, w, dout)


def _dt_fwd(rawT, bias, *, name):
    H2, T = rawT.shape
    tc = _tile(T, 2048, 128)

    def body(r_ref, b_ref, o_ref):
        v = r_ref[...] + b_ref[...]
        o_ref[...] = jnp.maximum(v, 0.0) + jnp.log(1.0 + jnp.exp(-jnp.abs(v)))

    return pl.pallas_call(
        body, name=name, grid=(T // tc,), out_shape=jax.ShapeDtypeStruct((H2, T), F32),
        in_specs=[pl.BlockSpec((H2, tc), lambda i: (0, i)), pl.BlockSpec((H2, 1), lambda i: (0, 0))],
        out_specs=pl.BlockSpec((H2, tc), lambda i: (0, i)),
        compiler_params=_params(("parallel",)),
    )(rawT, bias)


def _dt_bwd(rawT, bias, ddtT, *, name):
    H2, T = rawT.shape
    tc = _tile(T, 2048, 128)

    def body(r_ref, b_ref, d_ref, o_ref, db_ref):
        first = pl.program_id(0) == 0
        dr = d_ref[...] * jax.nn.sigmoid(r_ref[...] + b_ref[...])
        o_ref[...] = dr
        s = jnp.sum(dr, axis=1, keepdims=True)

        @pl.when(first)
        def _():
            db_ref[...] = s

        @pl.when(jnp.logical_not(first))
        def _():
            db_ref[...] += s

    return pl.pallas_call(
        body, name=name, grid=(T // tc,),
        out_shape=(jax.ShapeDtypeStruct((H2, T), F32), jax.ShapeDtypeStruct((H2, 1), F32)),
        in_specs=[pl.BlockSpec((H2, tc), lambda i: (0, i)), pl.BlockSpec((H2, 1), lambda i: (0, 0)),
                  pl.BlockSpec((H2, tc), lambda i: (0, i))],
        out_specs=(pl.BlockSpec((H2, tc), lambda i: (0, i)), pl.BlockSpec((H2, 1), lambda i: (0, 0))),
        compiler_params=_params(("arbitrary",)),
    )(rawT, bias, ddtT)


def _ssd_common(dtT, a, reverse):
    J, Q = dtT.shape
    li = lax.broadcasted_iota(jnp.int32, (Q, Q), 0)
    si = lax.broadcasted_iota(jnp.int32, (Q, Q), 1)
    mask = (si >= li) if reverse else (si <= li)
    Mf = mask.astype(F32)
    daT = dtT * a
    csT = lax.dot_general(daT, Mf, NT, precision=HIGHEST, preferred_element_type=F32)
    last = 0 if reverse else Q - 1
    totT = csT[:, last:last + 1]
    return mask, Mf, csT, totT, last


def _to_cols(rows):
    R, Q = rows.shape
    if R < 128:
        rows = jnp.concatenate([rows, jnp.zeros((128 - R, Q), F32)], axis=0)
    return rows.T


def _ssd_fwd(xbc, dtT, a, h0, *, reverse, name, di):
    T = xbc.shape[0]
    G, JP, N = h0.shape
    J, P, Q = JP // HEAD_DIM, HEAD_DIM, CHUNK
    nc = T // Q
    QW = 256
    HQ = QW // P

    def ci(k):
        return nc - 1 - k if reverse else k

    GB = SSD_GROUPS_PER_STEP

    def body(x_ref, b_ref, c_ref, dt_ref, a_ref, h0_ref, y_ref, hs_ref, hl_ref, h_scr):
        k = pl.program_id(1)

        @pl.when(k == 0)
        def _():
            h_scr[...] = h0_ref[...]

        lh = lax.broadcasted_iota(jnp.int32, (Q, QW), 1) // P

        def scale_heads(vT, rowsT):
            return jnp.concatenate([vT[j * P:(j + 1) * P, :] * rowsT[j:j + 1, :] for j in range(J)], axis=0)

        for gi in range(GB):
            h = h_scr[gi]
            hs_ref[0, gi] = h
            Xb = x_ref[:, gi * JP:(gi + 1) * JP]
            Bm, Cm = b_ref[:, gi * N:(gi + 1) * N], c_ref[:, gi * N:(gi + 1) * N]
            dtT_v = dt_ref[gi * J:(gi + 1) * J, :]
            mask, _, csT, totT, _ = _ssd_common(dtT_v, a_ref[gi * J:(gi + 1) * J, :], reverse)
            cs = _to_cols(csT)
            CB = lax.dot_general(Cm, Bm, NT, preferred_element_type=F32)
            yoT = lax.dot_general(h.astype(ACT), Cm, NT, preferred_element_type=F32)
            yo = scale_heads(yoT, jnp.exp(csT)).T
            for q in range(JP // QW):
                xq = Xb[:, q * QW:(q + 1) * QW]
                acc = yo[:, q * QW:(q + 1) * QW]
                for jj in range(HQ):
                    j = q * HQ + jj
                    seg = cs[:, j:j + 1] - csT[j:j + 1, :]
                    Mj = (CB * jnp.exp(jnp.where(mask, seg, NEG)) * dtT_v[j:j + 1, :]).astype(ACT)
                    acc = acc + jnp.dot(Mj, jnp.where(lh == jj, xq, jnp.zeros_like(xq)),
                                        preferred_element_type=F32)
                y_ref[:, gi * JP + q * QW:gi * JP + (q + 1) * QW] = acc.astype(ACT)
            xwT = scale_heads(Xb.astype(F32).T, dtT_v * jnp.exp(totT - csT)).astype(ACT)
            upd = jnp.dot(xwT, Bm, preferred_element_type=F32)
            for j in range(J):
                rows = slice(j * P, (j + 1) * P)
                h_scr[gi, rows, :] = h[rows, :] * jnp.exp(totT[j:j + 1, :]) + upd[rows, :]

        @pl.when(k == nc - 1)
        def _():
            hl_ref[...] = h_scr[...]

    GN = G * N
    return pl.pallas_call(
        body, name=name, grid=(G // GB, nc),
        out_shape=(jax.ShapeDtypeStruct((T, di), ACT), jax.ShapeDtypeStruct((nc, G, JP, N), F32),
                   jax.ShapeDtypeStruct((G, JP, N), F32)),
        in_specs=[pl.BlockSpec((Q, GB * JP), lambda g, k: (ci(k), g)),
                  pl.BlockSpec((Q, GB * N), lambda g, k: (ci(k), di // (GB * N) + g)),
                  pl.BlockSpec((Q, GB * N), lambda g, k: (ci(k), (di + GN) // (GB * N) + g)),
                  pl.BlockSpec((GB * J, Q), lambda g, k: (g, ci(k))),
                  pl.BlockSpec((GB * J, 1), lambda g, k: (g, 0)),
                  pl.BlockSpec((GB, JP, N), lambda g, k: (g, 0, 0))],
        out_specs=(pl.BlockSpec((Q, GB * JP), lambda g, k: (ci(k), g)),
                   pl.BlockSpec((1, GB, JP, N), lambda g, k: (ci(k), g, 0, 0)),
                   pl.BlockSpec((GB, JP, N), lambda g, k: (g, 0, 0))),
        scratch_shapes=[pltpu.VMEM((GB, JP, N), F32)],
        compiler_params=_params(("arbitrary", "arbitrary")),
    )(xbc, xbc, xbc, dtT, a, h0)


def _ssd_bwd(xbc, dtT, a, dy, hs, dh_last, add, *, reverse, name, di):
    T = xbc.shape[0]
    G, JP, N = dh_last.shape
    J, P, Q = JP // HEAD_DIM, HEAD_DIM, CHUNK
    nc = T // Q
    QW = 256
    HQ = QW // P
    has_add = add is not None

    def ci(k):
        return k if reverse else nc - 1 - k

    GB = SSD_GROUPS_PER_STEP

    def body(*refs):
        for gi in range(GB):
            wide = lambda r, w: r.at[:, pl.ds(gi * w, w)]
            x_ref, b_ref, c_ref, dt_ref, a_ref, dy_ref, hs_ref, dhl_ref = refs[:8]
            views = [wide(x_ref, JP), wide(b_ref, N), wide(c_ref, N), dt_ref.at[pl.ds(gi * J, J)],
                     a_ref.at[pl.ds(gi * J, J)], wide(dy_ref, JP), hs_ref.at[:, pl.ds(gi, 1)],
                     dhl_ref.at[pl.ds(gi, 1)]]
            rest = refs[8:]
            if has_add:
                views += [wide(rest[0], JP), wide(rest[1], N), wide(rest[2], N)]
                rest = rest[3:]
            dx_ref, db_ref, dc_ref, ddt_ref, da_ref, dh0_ref, dh_scr = rest
            views += [wide(dx_ref, JP), wide(db_ref, N), wide(dc_ref, N), ddt_ref.at[pl.ds(gi * J, J)],
                      da_ref.at[pl.ds(gi * J, J)], dh0_ref.at[pl.ds(gi, 1)], dh_scr.at[gi]]
            group_body(*views)

    def group_body(x_ref, b_ref, c_ref, dt_ref, a_ref, dy_ref, hs_ref, dhl_ref, *rest):
        if has_add:
            adx_ref, adb_ref, adc_ref = rest[:3]
            rest = rest[3:]
        dx_ref, db_ref, dc_ref, ddt_ref, da_ref, dh0_ref, dh_scr = rest
        k = pl.program_id(1)

        @pl.when(k == 0)
        def _():
            dh_scr[...] = dhl_ref[0]
            da_ref[...] = jnp.zeros_like(da_ref)

        def scale_heads(vT, rowsT):
            return jnp.concatenate([vT[j * P:(j + 1) * P, :] * rowsT[j:j + 1, :] for j in range(J)], axis=0)

        def head_sums(vT):
            return jnp.sum(vT.reshape(J, P, Q), axis=1)

        dH = dh_scr[...]
        h = hs_ref[0, 0]
        Bm, Cm = b_ref[...], c_ref[...]
        dtT_v = dt_ref[...]
        a_v = a_ref[...]
        mask, Mf, csT, totT, last = _ssd_common(dtT_v, a_v, reverse)
        ecsT = jnp.exp(csT)
        toendT = jnp.exp(totT - csT)
        cs = _to_cols(csT)
        dYb = dy_ref[...]
        XT = x_ref[...].astype(F32).T
        dYT = dYb.astype(F32).T
        xdtT = scale_heads(XT, dtT_v).astype(ACT)
        dYT_b = dYT.astype(ACT)
        dYeT = scale_heads(dYT, ecsT).astype(ACT)
        h_b = h.astype(ACT)
        dH_b = dH.astype(ACT)
        CB = lax.dot_general(Cm, Bm, NT, preferred_element_type=F32)
        dxdt_offT = scale_heads(lax.dot_general(dH_b, Bm, NT, preferred_element_type=F32), toendT)
        dCB = jnp.zeros((Q, Q), F32)
        lh = lax.broadcasted_iota(jnp.int32, (Q, QW), 1) // P
        sh = lax.broadcasted_iota(jnp.int32, (QW, Q), 0) // P
        lane_q = lax.broadcasted_iota(jnp.int32, (Q, Q), 1)
        sub_j = lax.broadcasted_iota(jnp.int32, (J, Q), 0)
        e_rows = jnp.zeros((Q, Q), F32)
        e_cols = jnp.zeros((J, Q), F32)
        diag = []
        for q in range(JP // QW):
            xq = xdtT[q * QW:(q + 1) * QW, :]
            dyq = dYb[:, q * QW:(q + 1) * QW]
            dyTq = dYT_b[q * QW:(q + 1) * QW, :]
            acc = jnp.zeros((QW, Q), F32)
            for jj in range(HQ):
                j = q * HQ + jj
                seg = cs[:, j:j + 1] - csT[j:j + 1, :]
                L = jnp.exp(jnp.where(mask, seg, NEG))
                Mf_j = CB * L
                dyj = jnp.where(lh == jj, dyq, jnp.zeros_like(dyq))
                dyTj = jnp.where(sh == jj, dyTq, jnp.zeros_like(dyTq))
                acc = acc + jnp.dot(dyTj, Mf_j.astype(ACT), preferred_element_type=F32)
                dM = jnp.dot(dyj, xq, preferred_element_type=F32)
                dCB = dCB + dM * L
                E = dM * Mf_j
                e_rows = jnp.where(lane_q == j, jnp.sum(E, axis=1, keepdims=True), e_rows)
                e_cols = jnp.where(sub_j == j, jnp.sum(E, axis=0, keepdims=True), e_cols)
            diag.append(acc)
        dxdtT = dxdt_offT + jnp.concatenate(diag, axis=0)
        dCB_b = dCB.astype(ACT)
        dC = (jnp.dot(dCB_b, Bm, preferred_element_type=F32)
              + lax.dot_general(dYeT, h_b, TN, preferred_element_type=F32))
        xwT = scale_heads(XT, dtT_v * toendT).astype(ACT)
        dB = (lax.dot_general(dCB_b, Cm, TN, preferred_element_type=F32)
              + lax.dot_general(xwT, dH_b, TN, preferred_element_type=F32))
        dHc = jnp.dot(dYeT, Cm, preferred_element_type=F32)
        for j in range(J):
            rows = slice(j * P, (j + 1) * P)
            dh_scr[rows, :] = dH[rows, :] * jnp.exp(totT[j:j + 1, :]) + dHc[rows, :]
        dh0_ref[0] = dh_scr[...]

        yoT = scale_heads(lax.dot_general(h_b, Cm, NT, preferred_element_type=F32), ecsT)
        RT_ = head_sums(dYT * yoT)
        UT_ = head_sums(XT * dxdtT)
        UoT = head_sums(XT * dxdt_offT)
        hsum = jnp.sum(jnp.sum((dH * h).reshape(J, P, N), axis=1), axis=1, keepdims=True)
        dtot = jnp.sum(UoT * dtT_v, axis=1, keepdims=True) + jnp.exp(totT) * hsum
        lane = lax.broadcasted_iota(jnp.int32, (J, Q), 1)
        dcsT = e_rows.T[0:J] - e_cols + RT_ - UoT * dtT_v + jnp.where(lane == last, dtot, 0.0)
        ddaT = jnp.dot(dcsT, Mf, precision=HIGHEST, preferred_element_type=F32)
        ddt_ref[...] = ddaT * a_v + UT_
        da_ref[...] += ddaT * dtT_v
        dX = scale_heads(dxdtT, dtT_v).T
        if has_add:
            dX = dX + adx_ref[...].astype(F32)
            dB = dB + adb_ref[...].astype(F32)
            dC = dC + adc_ref[...].astype(F32)
        dx_ref[...] = dX.astype(ACT)
        db_ref[...] = dB.astype(ACT)
        dc_ref[...] = dC.astype(ACT)

    GN = G * N
    xspec = pl.BlockSpec((Q, GB * JP), lambda g, k: (ci(k), g))
    nspec = pl.BlockSpec((Q, GB * N), lambda g, k: (ci(k), g))
    hspec = pl.BlockSpec((GB, JP, N), lambda g, k: (g, 0, 0))
    in_specs = [xspec,
                pl.BlockSpec((Q, GB * N), lambda g, k: (ci(k), di // (GB * N) + g)),
                pl.BlockSpec((Q, GB * N), lambda g, k: (ci(k), (di + GN) // (GB * N) + g)),
                pl.BlockSpec((GB * J, Q), lambda g, k: (g, ci(k))),
                pl.BlockSpec((GB * J, 1), lambda g, k: (g, 0)),
                xspec,
                pl.BlockSpec((1, GB, JP, N), lambda g, k: (ci(k), g, 0, 0)),
                hspec]
    operands = [xbc, xbc, xbc, dtT, a, dy, hs, dh_last]
    if has_add:
        in_specs += [xspec, nspec, nspec]
        operands += list(add)
    H = G * J
    return pl.pallas_call(
        body, name=name, grid=(G // GB, nc),
        out_shape=(jax.ShapeDtypeStruct((T, di), ACT), jax.ShapeDtypeStruct((T, GN), ACT),
                   jax.ShapeDtypeStruct((T, GN), ACT), jax.ShapeDtypeStruct((H, T), F32),
                   jax.ShapeDtypeStruct((H, Q), F32), jax.ShapeDtypeStruct((G, JP, N), F32)),
        in_specs=in_specs,
        out_specs=(xspec, nspec, nspec,
                   pl.BlockSpec((GB * J, Q), lambda g, k: (g, ci(k))),
                   pl.BlockSpec((GB * J, Q), lambda g, k: (g, 0)),
                   hspec),
        scratch_shapes=[pltpu.VMEM((GB, JP, N), F32)],
        compiler_params=_params(("arbitrary", "arbitrary")),
    )(*operands)


def _mod_fwd(craw, w, b, *, name):
    R, D = craw.shape
    NL = w.shape[1]
    tn = _tile(NL, 512, 128)

    def body(c_ref, w_ref, b_ref, o_ref):
        o_ref[...] = jnp.dot(_silu(c_ref[...]), w_ref[...], preferred_element_type=F32) + b_ref[...]

    return pl.pallas_call(
        body, name=name, grid=(NL // tn,), out_shape=jax.ShapeDtypeStruct((R, NL), F32),
        in_specs=[pl.BlockSpec((R, D), lambda j: (0, 0)), pl.BlockSpec((D, tn), lambda j: (0, j)),
                  pl.BlockSpec((1, tn), lambda j: (0, j))],
        out_specs=pl.BlockSpec((R, tn), lambda j: (0, j)),
        compiler_params=_params(("parallel",)),
    )(craw, w, b)


def _mod_bwd(craw, w, dm, *, name):
    R, D = craw.shape
    NL = w.shape[1]
    tn = _tile(NL, 512, 128)

    def body(c_ref, w_ref, dm_ref, dw_ref, dc_ref):
        first = pl.program_id(0) == 0
        cf = c_ref[...]
        dmv = dm_ref[...]
        dw_ref[...] = lax.dot_general(_silu(cf), dmv, TN, preferred_element_type=F32)
        part = lax.dot_general(dmv, w_ref[...], NT, preferred_element_type=F32) * _dsilu(cf)

        @pl.when(first)
        def _():
            dc_ref[...] = part

        @pl.when(jnp.logical_not(first))
        def _():
            dc_ref[...] += part

    return pl.pallas_call(
        body, name=name, grid=(NL // tn,),
        out_shape=(jax.ShapeDtypeStruct((D, NL), F32), jax.ShapeDtypeStruct((R, D), F32)),
        in_specs=[pl.BlockSpec((R, D), lambda j: (0, 0)), pl.BlockSpec((D, tn), lambda j: (0, j)),
                  pl.BlockSpec((R, tn), lambda j: (0, j))],
        out_specs=(pl.BlockSpec((D, tn), lambda j: (0, j)), pl.BlockSpec((R, D), lambda j: (0, 0))),
        compiler_params=_params(("arbitrary",)),
    )(craw, w, dm)


def _pad_rows(a, rows):
    return jnp.concatenate([a, jnp.zeros((rows - a.shape[0],) + a.shape[1:], a.dtype)], axis=0)


def _cols_whole(g):
    return jnp.transpose(g, (1, 0, 2)).reshape(g.shape[1], N_DEV * g.shape[2])


def _rows_whole(g):
    return g.reshape(N_DEV * g.shape[1], g.shape[2])


def _col_blocks(full):
    K, n8 = full.shape
    return jnp.transpose(full.reshape(K, N_DEV, n8 // N_DEV), (1, 0, 2)).astype(ACT)


def _row_blocks(full):
    K8, n = full.shape
    return full.reshape(N_DEV, K8 // N_DEV, n).astype(ACT)


def kernel(x, c, ctx, c_ctx, w_mod, b_mod, norm_mix, w_in, ssm_conv_w, ssm_conv_b, dt_bias, a_log, d_skip, ssm_norm, cf_conv_w, cf_conv_b, cf_ln_g, cf_ln_b, w_proj_a, w_proj_b, w_out, norm_ffn, w_ffn_gate, w_ffn_up, w_ffn_down, norm_final, loss_target, m_c_ctx, m_w_mod, m_b_mod, m_norm_mix, m_w_in, m_ssm_conv_w, m_ssm_conv_b, m_dt_bias, m_a_log, m_d_skip, m_ssm_norm, m_cf_conv_w, m_cf_conv_b, m_cf_ln_g, m_cf_ln_b, m_w_proj_a, m_w_proj_b, m_w_out, m_norm_ffn, m_w_ffn_gate, m_w_ffn_up, m_w_ffn_down, m_norm_final, v_c_ctx, v_w_mod, v_b_mod, v_norm_mix, v_w_in, v_ssm_conv_w, v_ssm_conv_b, v_dt_bias, v_a_log, v_d_skip, v_ssm_norm, v_cf_conv_w, v_cf_conv_b, v_cf_ln_g, v_cf_ln_b, v_w_proj_a, v_w_proj_b, v_w_out, v_norm_ffn, v_w_ffn_gate, v_w_ffn_up, v_w_ffn_down, v_norm_final):
    args = dict(locals())
    me = 4 * lax.axis_index("x") + 2 * lax.axis_index("y") + lax.axis_index("c")
    T, D = x.shape[1], x.shape[2]
    DI = ssm_norm.shape[1]
    H = DI // HEAD_DIM
    G, J, N = GROUPS, H // GROUPS, STATE
    JP = J * HEAD_DIM
    GN = G * N
    CONV = DI + 2 * GN
    x0 = x[0]
    ctx0 = ctx[0]
    target = loss_target[0]

    st_in = _exchange_start(w_in[0].astype(ACT), c, name="gather_start_w_in", gather=True)
    k5 = ssm_conv_w.shape[1]
    k31 = cf_conv_w.shape[1]
    cw5 = _exchange(_pad_rows(ssm_conv_w[0], 8), name="gather_conv5", gather=True)
    cw5 = jnp.transpose(cw5, (1, 0, 2)).reshape(8, CONV)
    cw31 = _exchange(_pad_rows(cf_conv_w[0], 32), name="gather_conv31", gather=True)
    cw31 = jnp.transpose(cw31, (1, 0, 2)).reshape(32, D)

    c_all = _exchange(_pad_rows(c, 8), name="gather_c", gather=True)[:, 0, :]
    craw = jnp.concatenate([c_all, c_ctx[None, :], jnp.zeros((7, D), F32)], axis=0)
    NL = w_mod.shape[2]
    b_loc = lax.dynamic_slice(b_mod, (0, me * NL), (1, NL))
    m_loc = _mod_fwd(craw, w_mod[0], b_loc, name="mod_fwd")
    m_all = jnp.transpose(_exchange(m_loc, name="gather_mod", gather=True), (1, 0, 2)).reshape(16, N_DEV * NL)
    m_me = lax.dynamic_slice(m_all, (me, 0), (1, 6 * D))
    sh1, sc1, g1, sh2, sc2, g2 = [m_me[:, i * D:(i + 1) * D] for i in range(6)]
    csh1, csc1 = m_all[8:9, 0:D], m_all[8:9, D:2 * D]

    got_in = _exchange_wait(st_in, m_all, name="gather_wait_w_in")
    win = _cols_whole(got_in)
    o_xbc, o_dt, o_glu, o_gates = DI, DI + CONV, DI + CONV + 2 * H, DI + CONV + 2 * H + 2 * D
    w_z, w_xbc, w_dt = win[:, :o_xbc], win[:, o_xbc:o_dt], win[:, o_dt:o_glu]
    w_u, w_v, w_gates = win[:, o_glu:o_glu + D], win[:, o_glu + D:o_gates], win[:, o_gates:]
    behind_in = (got_in[0, 0, 0] * 0).astype(ACT)

    def gather_seq(shard, name, cid):
        return _exchange_seq(shard.astype(ACT) + behind_in, name=name, gather=True, collective_id=cid)

    w_pa = _rows_whole(gather_seq(w_proj_a[0], "gather_w_pa", 1))
    w_pb = _rows_whole(gather_seq(w_proj_b[0], "gather_w_pb", 2))
    w_o = _rows_whole(gather_seq(w_out[0], "gather_w_out", 3))
    w_gate = _cols_whole(gather_seq(w_ffn_gate[0], "gather_w_gate", 4))
    w_up = _cols_whole(gather_seq(w_ffn_up[0], "gather_w_up", 5))
    w_down = _rows_whole(gather_seq(w_ffn_down[0], "gather_w_down", 6))

    a_neg = -jnp.exp(a_log[0])
    a_f, a_b = a_neg[0][:, None], a_neg[1][:, None]
    dtb = dt_bias[0].reshape(2 * H, 1)
    dskip_e = jnp.repeat(d_skip[0], HEAD_DIM)[None, :]

    def front(h, tag, full, after=None):
        out = {}
        out["xbc_raw"] = _mm(h, w_xbc, "nn", name="mm_xbc_" + tag, out_dtype=ACT, after=after)
        dt_raw = _mm(h, w_dt, "nn", name="mm_dt_" + tag, out_dtype=F32)
        out["rawT"] = dt_raw.T
        if full:
            out["z"] = _mm(h, w_z, "nn", name="mm_z_" + tag, out_dtype=ACT)
            out["u"] = _mm(h, w_u, "nn", name="mm_u_" + tag, out_dtype=ACT)
            out["v"] = _mm(h, w_v, "nn", name="mm_v_" + tag, out_dtype=ACT)
            out["gates"] = _mm(h, w_gates, "nn", name="mm_gates_" + tag, out_dtype=ACT)
        out["xbc"] = _conv5_silu_fwd(out["xbc_raw"], cw5, ssm_conv_b, name="conv5_fwd_" + tag)
        out["dtT"] = _dt_fwd(out["rawT"], dtb, name="dt_fwd_" + tag)
        return out

    hc = _norm_mod_fwd(ctx0, norm_mix, csh1, csc1, name="norm_mod_ctx")
    fc = front(hc, "ctx", False)
    zero_state = jnp.zeros((G, JP, N), F32)
    _, hs_cf, h_f = _ssd_fwd(fc["xbc"], fc["dtT"][:H], a_f, zero_state, reverse=False, name="ssd_fwd_ctx_f", di=DI)
    _, hs_cb, h_b = _ssd_fwd(fc["xbc"], fc["dtT"][H:], a_b, zero_state, reverse=True, name="ssd_fwd_ctx_b", di=DI)

    hx = _norm_mod_fwd(x0, norm_mix, sh1, sc1, name="norm_mod_x")
    fx = front(hx, "x", True)
    y_f, hs_f, _ = _ssd_fwd(fx["xbc"], fx["dtT"][:H], a_f, h_f, reverse=False, name="ssd_fwd_x_f", di=DI)
    y_b, hs_b, _ = _ssd_fwd(fx["xbc"], fx["dtT"][H:], a_b, h_b, reverse=True, name="ssd_fwd_x_b", di=DI)
    ya_in = _gate_norm_fwd(y_f, y_b, fx["xbc"], fx["z"], dskip_e, ssm_norm, name="gate_norm_fwd")
    ya = _mm(ya_in, w_pa, "nn", name="mm_proj_a", out_dtype=ACT)
    conv_out = _glu_conv_fwd(fx["u"], fx["v"], cw31, cf_conv_b, name="glu_conv_fwd")
    cf = _ln_silu_fwd(conv_out, cf_ln_g, cf_ln_b, name="ln_silu_fwd")
    yb = _mm(cf, w_pb, "nn", name="mm_proj_b", out_dtype=ACT)
    merged = _merge_fwd(ya, yb, fx["gates"], name="merge_fwd")
    o_mix = _mm(merged, w_o, "nn", name="mm_out", out_dtype=ACT)

    x1, h2 = _resid_norm_mod_fwd(x0, o_mix, g1, norm_ffn, sh2, sc2, name="resid_norm_mod")
    gate = _mm(h2, w_gate, "nn", name="mm_gate", out_dtype=ACT)
    up = _mm(h2, w_up, "nn", name="mm_up", out_dtype=ACT)
    act = _swiglu_fwd(gate, up, name="swiglu_fwd")
    dn = _mm(act, w_down, "nn", name="mm_down", out_dtype=ACT)

    loss_part, dx2, d_dn, g_norm_final, d_g2 = _final_fwd_bwd(x1, dn, g2, norm_final[None, :], target, name="final")
    loss = lax.psum(loss_part[0, 0], AXES)

    d_act = _mm(d_dn, w_down, "nt", name="mm_d_act", out_dtype=ACT)
    gw_down = _mm(act, d_dn, "tn", name="mm_gw_down", out_dtype=F32)
    parts = {}
    parts["w_ffn_down"] = _exchange_seq(_row_blocks(gw_down), name="scatter_w_down", gather=False, collective_id=7)
    d_gate, d_up = _swiglu_bwd(gate, up, d_act, name="swiglu_bwd")
    gw_gate = _mm(h2, d_gate, "tn", name="mm_gw_gate", out_dtype=F32)
    parts["w_ffn_gate"] = _exchange_seq(_col_blocks(gw_gate), name="scatter_w_gate", gather=False, collective_id=8)
    gw_up = _mm(h2, d_up, "tn", name="mm_gw_up", out_dtype=F32)
    parts["w_ffn_up"] = _exchange_seq(_col_blocks(gw_up), name="scatter_w_up", gather=False, collective_id=9)
    d_h2 = _mm(d_gate, w_gate, "nt", name="mm_d_h2_gate", out_dtype=F32)
    d_h2 = _mm(d_up, w_up, "nt", name="mm_d_h2_up", out_dtype=F32, add=d_h2)
    dx1, d_sh2, d_sc2, g_norm_ffn, d_o, d_g1 = _norm_mod_bwd(
        x1, norm_ffn, sc2, d_h2, name="norm_mod_bwd_ffn", dres=dx2, o=o_mix, g=g1)

    d_merged = _mm(d_o, w_o, "nt", name="mm_d_merged", out_dtype=ACT)
    gw_out = _mm(merged, d_o, "tn", name="mm_gw_out", out_dtype=F32)
    parts["w_out"] = _exchange_seq(_row_blocks(gw_out), name="scatter_w_out", gather=False, collective_id=10)
    d_ya, d_yb, d_gates = _merge_bwd(d_merged, ya, yb, fx["gates"], name="merge_bwd")
    gw_pa = _mm(ya_in, d_ya, "tn", name="mm_gw_pa", out_dtype=F32)
    parts["w_proj_a"] = _exchange_seq(_row_blocks(gw_pa), name="scatter_w_pa", gather=False, collective_id=11)
    gw_pb = _mm(cf, d_yb, "tn", name="mm_gw_pb", out_dtype=F32)
    parts["w_proj_b"] = _exchange_seq(_row_blocks(gw_pb), name="scatter_w_pb", gather=False, collective_id=12)
    d_ya_in = _mm(d_ya, w_pa, "nt", name="mm_d_ya_in", out_dtype=ACT)
    d_cf = _mm(d_yb, w_pb, "nt", name="mm_d_cf", out_dtype=ACT)
    d_conv, g_ln_g, g_ln_b = _ln_silu_bwd(conv_out, cf_ln_g, cf_ln_b, d_cf, name="ln_silu_bwd")
    d_u, d_v, g_cw31, g_cb31 = _glu_conv_bwd(fx["u"], fx["v"], cw31, d_conv, name="glu_conv_bwd")
    d_y, d_z, dxs_skip, g_ssm_norm, g_dskip_e = _gate_norm_bwd(
        d_ya_in, y_f, y_b, fx["xbc"], fx["z"], dskip_e, ssm_norm, name="gate_norm_bwd")

    zero_bc = jnp.zeros((T, GN), ACT)
    r1 = _ssd_bwd(fx["xbc"], fx["dtT"][:H], a_f, d_y, hs_f, zero_state, (dxs_skip, zero_bc, zero_bc),
                  reverse=False, name="ssd_bwd_x_f", di=DI)
    r2 = _ssd_bwd(fx["xbc"], fx["dtT"][H:], a_b, d_y, hs_b, zero_state, r1[:3],
                  reverse=True, name="ssd_bwd_x_b", di=DI)
    Tc = ctx0.shape[0]
    zero_yc = jnp.zeros((Tc, DI), ACT)
    r3 = _ssd_bwd(fc["xbc"], fc["dtT"][:H], a_f, zero_yc, hs_cf, r1[5], None,
                  reverse=False, name="ssd_bwd_ctx_f", di=DI)
    r4 = _ssd_bwd(fc["xbc"], fc["dtT"][H:], a_b, zero_yc, hs_cb, r2[5], r3[:3],
                  reverse=True, name="ssd_bwd_ctx_b", di=DI)

    def back(f, rf, rb, tag):
        d_xbc = jnp.concatenate([rb[0], rb[1], rb[2]], axis=1)
        d_xbc_raw, g_w5, g_b5 = _conv5_silu_bwd(f["xbc_raw"], cw5, ssm_conv_b, d_xbc, name="conv5_bwd_" + tag)
        ddtT = jnp.concatenate([rf[3], rb[3]], axis=0)
        d_rawT, g_dtb = _dt_bwd(f["rawT"], dtb, ddtT, name="dt_bwd_" + tag)
        g_a = jnp.stack([jnp.sum(rf[4], axis=1), jnp.sum(rb[4], axis=1)])
        return d_xbc_raw, d_rawT.T.astype(ACT), g_w5, g_b5, g_dtb, g_a

    dx_xbc_raw, dx_dt_raw, gx_w5, gx_b5, gx_dtb, gx_a = back(fx, r1, r2, "x")
    dc_xbc_raw, dc_dt_raw, gc_w5, gc_b5, gc_dtb, gc_a = back(fc, r3, r4, "ctx")

    gw_xbc = _mm(hc, dc_xbc_raw, "tn", name="mm_gw_xbc_ctx", out_dtype=F32)
    gw_xbc = _mm(hx, dx_xbc_raw, "tn", name="mm_gw_xbc", out_dtype=F32, add=gw_xbc)
    gw_dt = _mm(hc, dc_dt_raw, "tn", name="mm_gw_dt_ctx", out_dtype=F32)
    gw_dt = _mm(hx, dx_dt_raw, "tn", name="mm_gw_dt", out_dtype=F32, add=gw_dt)
    gw_z = _mm(hx, d_z, "tn", name="mm_gw_z", out_dtype=F32)
    gw_u = _mm(hx, d_u, "tn", name="mm_gw_u", out_dtype=F32)
    gw_v = _mm(hx, d_v, "tn", name="mm_gw_v", out_dtype=F32)
    gw_gates = _mm(hx, d_gates, "tn", name="mm_gw_gates", out_dtype=F32)
    gw_in = jnp.concatenate([gw_z, gw_xbc, gw_dt, gw_u, gw_v, gw_gates], axis=1)
    parts["w_in"] = _exchange_seq(_col_blocks(gw_in), name="scatter_w_in", gather=False, collective_id=13)

    d_hx = _mm(d_z, w_z, "nt", name="mm_d_hx_z", out_dtype=F32)
    d_hx = _mm(dx_xbc_raw, w_xbc, "nt", name="mm_d_hx_xbc", out_dtype=F32, add=d_hx)
    d_hx = _mm(dx_dt_raw, w_dt, "nt", name="mm_d_hx_dt", out_dtype=F32, add=d_hx)
    d_hx = _mm(d_u, w_u, "nt", name="mm_d_hx_u", out_dtype=F32, add=d_hx)
    d_hx = _mm(d_v, w_v, "nt", name="mm_d_hx_v", out_dtype=F32, add=d_hx)
    d_hx = _mm(d_gates, w_gates, "nt", name="mm_d_hx_gates", out_dtype=F32, add=d_hx)
    grad_x, d_sh1, d_sc1, gx_norm_mix = _norm_mod_bwd(x0, norm_mix, sc1, d_hx, name="norm_mod_bwd_x", dres=dx1)
    d_hc = _mm(dc_xbc_raw, w_xbc, "nt", name="mm_d_hc_xbc", out_dtype=F32)
    d_hc = _mm(dc_dt_raw, w_dt, "nt", name="mm_d_hc_dt", out_dtype=F32, add=d_hc)
    _, d_csh1, d_csc1, gc_norm_mix = _norm_mod_bwd(ctx0, norm_mix, csc1, d_hc, name="norm_mod_bwd_ctx")

    zD = jnp.zeros((1, D), F32)
    dm_me = jnp.concatenate([d_sh1, d_sc1, d_g1, d_sh2, d_sc2, d_g2], axis=1)
    dm_ctx = jnp.concatenate([d_csh1, d_csc1, zD, zD, zD, zD], axis=1)
    rows16 = lax.broadcasted_iota(jnp.int32, (16, 1), 0)
    dm_rows = jnp.where(rows16 == me, dm_me, 0.0) + jnp.where(rows16 == 8, dm_ctx, 0.0)
    dm_sum = _sum_slots(_exchange(dm_rows, name="gather_dm", gather=True), name="sum_dm")
    g_b_mod = _colsum(dm_sum, name="colsum_dm")
    dm_loc = lax.dynamic_slice(dm_sum, (0, me * NL), (16, NL))
    g_w_mod, dcraw = _mod_bwd(craw, w_mod[0], dm_loc, name="mod_bwd")

    small = [
        ("c_ctx", dcraw[8]), ("norm_mix", gx_norm_mix + gc_norm_mix),
        ("ssm_conv_w", (gx_w5 + gc_w5)[:k5]), ("ssm_conv_b", gx_b5 + gc_b5),
        ("dt_bias", gx_dtb + gc_dtb), ("a_log", (gx_a + gc_a) * a_neg),
        ("d_skip", jnp.sum(g_dskip_e.reshape(H, HEAD_DIM), axis=1)), ("ssm_norm", g_ssm_norm),
        ("cf_conv_w", g_cw31[:k31]), ("cf_conv_b", g_cb31), ("cf_ln_g", g_ln_g), ("cf_ln_b", g_ln_b),
        ("norm_ffn", g_norm_ffn), ("norm_final", g_norm_final),
    ]
    flat = jnp.concatenate([v.reshape(-1) for _, v in small])
    n_small = flat.shape[0]
    rows_small = -(-n_small // 1024) * 8
    flat = jnp.concatenate([flat, jnp.zeros((rows_small * 128 - n_small,), F32)]).reshape(rows_small, 128)
    summed = _sum_slots(_exchange(flat, name="gather_small", gather=True), name="sum_small").reshape(-1)
    g_small = {}
    pos = 0
    for nm, v in small:
        g_small[nm] = summed[pos:pos + v.size].reshape(v.shape)
        pos += v.size
    g_small["b_mod"] = g_b_mod
    n5, n31 = ssm_conv_w.shape[2], cf_conv_w.shape[2]
    g_small["ssm_conv_w"] = lax.dynamic_slice(g_small["ssm_conv_w"], (0, me * n5), (k5, n5))
    g_small["cf_conv_w"] = lax.dynamic_slice(g_small["cf_conv_w"], (0, me * n31), (k31, n31))

    grads, deltas, new_m, new_v = {}, {}, {}, {}

    def adam2d(nm, parts):
        shape = args[nm].shape
        R, C = shape[-2], shape[-1]
        g, d, m2, v2 = _adamw(parts, args[nm].reshape(R, C), args["m_" + nm].reshape(R, C),
                              args["v_" + nm].reshape(R, C), name="adamw_" + nm)
        grads[nm], deltas[nm], new_m[nm], new_v[nm] = [t.reshape(shape) for t in (g, d, m2, v2)]

    adam2d("w_mod", g_w_mod[None])
    for nm in ("w_ffn_down", "w_ffn_gate", "w_ffn_up", "w_out", "w_proj_a", "w_proj_b", "w_in"):
        adam2d(nm, parts[nm])

    small_names = ["c_ctx", "b_mod", "norm_mix", "ssm_conv_w", "ssm_conv_b", "dt_bias", "a_log", "d_skip", "ssm_norm",
                   "cf_conv_w", "cf_conv_b", "cf_ln_g", "cf_ln_b", "norm_ffn", "norm_final"]

    def pack(vals):
        f = jnp.concatenate([t.reshape(-1) for t in vals])
        rows = -(-f.shape[0] // 1024) * 8
        return jnp.concatenate([f, jnp.zeros((rows * 128 - f.shape[0],), F32)]).reshape(rows, 128)

    pg = pack([g_small[nm] for nm in small_names])
    pw = pack([args[nm] for nm in small_names])
    pm = pack([args["m_" + nm] for nm in small_names])
    pv = pack([args["v_" + nm] for nm in small_names])
    outs = _adamw(pg[None], pw, pm, pv, name="adamw_small")
    pos = 0
    for nm in small_names:
        shape = args[nm].shape
        size = math.prod(shape)
        vals = [t.reshape(-1)[pos:pos + size].reshape(shape) for t in outs]
        grads[nm], deltas[nm], new_m[nm], new_v[nm] = vals
        pos += size

    order = ["c_ctx", "w_mod", "b_mod", "norm_mix", "w_in", "ssm_conv_w", "ssm_conv_b", "dt_bias", "a_log", "d_skip",
             "ssm_norm", "cf_conv_w", "cf_conv_b", "cf_ln_g", "cf_ln_b", "w_proj_a", "w_proj_b", "w_out", "norm_ffn",
             "w_ffn_gate", "w_ffn_up", "w_ffn_down", "norm_final"]
    return (loss, grad_x[None], *[grads[n] for n in order], *[deltas[n] for n in order],
            *[new_m[n] for n in order], *[new_v[n] for n in order])
```

```python
import functools
import math

import jax
import jax.numpy as jnp
from jax import lax
from jax.experimental import pallas as pl
from jax.experimental.pallas import tpu as pltpu
from jax.experimental.pallas import tpu_sc as plsc

F32 = jnp.float32
ACT = jnp.bfloat16
HIGHEST = lax.Precision.HIGHEST
MESH = pl.DeviceIdType.MESH
AXES = ("x", "y", "c")
N_DEV = 8

GRID_W = 64
CHUNK = 128
SSD_GROUPS_PER_STEP = 2
HEAD_DIM = 64
GROUPS = 8
STATE = 128
EPS = 1e-6
ADAM_LR = 0.001
ADAM_B1 = 0.9
ADAM_B2 = 0.999
ADAM_EPS = 1e-08
ADAM_WD = 0.01
ADAM_STEP = 10

V7X_VMEM_LIMIT = 56 * 1024 * 1024
NEG = -1e30

NN = (((1,), (0,)), ((), ()))
NT = (((1,), (1,)), ((), ()))
TN = (((0,), (0,)), ((), ()))


def _tile(n, target, quantum):
    best = None
    t = quantum
    while t <= min(n, target):
        if n % t == 0:
            best = t
        t += quantum
    return n if best is None else best


def _params(sem=None):
    kw = dict(vmem_limit_bytes=V7X_VMEM_LIMIT)
    if sem is not None:
        kw["dimension_semantics"] = sem
    return pltpu.CompilerParams(**kw)


def _silu(v):
    return v * jax.nn.sigmoid(v)


def _dsilu(v):
    s = jax.nn.sigmoid(v)
    return s * (1.0 + v * (1.0 - s))


def _exchange(x, *, name, gather):
    shape = x.shape[-2:]

    def body(x_ref, o_ref, send_sems, recv_sems, loc_sem):
        ix, iy, ic = lax.axis_index("x"), lax.axis_index("y"), lax.axis_index("c")
        me = 4 * ix + 2 * iy + ic

        def src(d):
            return x_ref if gather else x_ref.at[d]

        def remote(k, slot, peer_xyz, src_ref):
            return pltpu.make_async_remote_copy(
                src_ref=src_ref, dst_ref=o_ref.at[slot], send_sem=send_sems.at[k], recv_sem=recv_sems.at[k],
                device_id=peer_xyz, device_id_type=MESH)

        local = pltpu.make_async_copy(src(me), o_ref.at[me], loc_sem)
        local.start()
        sends, peers = [], []
        for k in range(1, N_DEV):
            px = 1 - ix if k & 4 else ix
            py = 1 - iy if k & 2 else iy
            pc = 1 - ic if k & 1 else ic
            peer = 4 * px + 2 * py + pc
            cp = remote(k - 1, me, (px, py, pc), src(peer))
            cp.start()
            sends.append(cp)
            peers.append((peer, (px, py, pc)))
        for k in range(1, N_DEV):
            peer, xyz = peers[k - 1]
            remote(k - 1, peer, xyz, src(peer)).wait_recv()
        for cp in sends:
            cp.wait_send()
        local.wait()

    return pl.pallas_call(
        body, name=name,
        out_shape=jax.ShapeDtypeStruct((N_DEV,) + shape, x.dtype),
        in_specs=[pl.BlockSpec(memory_space=pl.ANY)],
        out_specs=pl.BlockSpec(memory_space=pl.ANY),
        scratch_shapes=[pltpu.SemaphoreType.DMA((N_DEV - 1,)), pltpu.SemaphoreType.DMA((N_DEV - 1,)),
                        pltpu.SemaphoreType.DMA],
    )(x)


HBM_SPEC = pl.BlockSpec(memory_space=pltpu.HBM)
SEM_SPEC = pl.BlockSpec(memory_space=pltpu.SEMAPHORE)
ANY_SPEC = pl.BlockSpec(memory_space=pl.ANY)
DATAFLOW = pltpu.SideEffectType.DATAFLOW_SIDE_EFFECTING


def _peer(k):
    ix, iy, ic = lax.axis_index("x"), lax.axis_index("y"), lax.axis_index("c")
    px = 1 - ix if k & 4 else ix
    py = 1 - iy if k & 2 else iy
    pc = 1 - ic if k & 1 else ic
    return (px, py, pc), 4 * px + 2 * py + pc


def _exchange_start(x, after, *, name, gather):
    shape = x.shape[-2:]

    def body(after_ref, x_ref, land_ref, send_sem, recv_sem, x_thru, land_thru, token, loc_sem):
        _, me = _peer(0)

        def src(d):
            return x_ref if gather else x_ref.at[d]

        local = pltpu.make_async_copy(src(me), land_ref.at[me], loc_sem)
        local.start()
        local.wait()
        for k in range(1, N_DEV):
            xyz, peer = _peer(k)
            pltpu.make_async_remote_copy(
                src_ref=src(peer), dst_ref=land_ref.at[me], send_sem=send_sem, recv_sem=recv_sem,
                device_id=xyz, device_id_type=MESH).start()
        token[...] = jnp.zeros_like(token)

    land = lax.empty((N_DEV,) + shape, x.dtype)
    return pl.pallas_call(
        body, name=name,
        out_shape=(pltpu.SemaphoreType.DMA(()), pltpu.SemaphoreType.DMA(()), pltpu.HBM(x.shape, x.dtype),
                   pltpu.HBM((N_DEV,) + shape, x.dtype), jax.ShapeDtypeStruct((8, 128), F32)),
        in_specs=(ANY_SPEC, HBM_SPEC, HBM_SPEC),
        out_specs=(SEM_SPEC, SEM_SPEC, HBM_SPEC, HBM_SPEC, pl.BlockSpec(memory_space=pltpu.VMEM)),
        input_output_aliases={1: 2, 2: 3},
        scratch_shapes=[pltpu.SemaphoreType.DMA],
        compiler_params=pltpu.CompilerParams(has_side_effects=DATAFLOW),
    )(after, pltpu.with_memory_space_constraint(x, pltpu.HBM), pltpu.with_memory_space_constraint(land, pltpu.HBM))


def _exchange_wait(started, after, *, name):
    send_sem, recv_sem, x_thru, land_thru, _ = started

    def body(x_ref, land_ref, send_sem, recv_sem, after_ref, x_dead, got_ref):
        xyz, _ = _peer(0)
        seven = land_ref.at[pl.ds(0, N_DEV - 1)]
        cp = pltpu.make_async_remote_copy(src_ref=seven, dst_ref=seven, send_sem=send_sem, recv_sem=recv_sem,
                                          device_id=xyz, device_id_type=MESH)
        cp.wait_send()
        cp.wait_recv()

    return pl.pallas_call(
        body, name=name,
        out_shape=(pltpu.HBM(x_thru.shape, x_thru.dtype), pltpu.HBM(land_thru.shape, land_thru.dtype)),
        in_specs=(HBM_SPEC, HBM_SPEC, SEM_SPEC, SEM_SPEC, ANY_SPEC),
        out_specs=(HBM_SPEC, HBM_SPEC),
        input_output_aliases={0: 0, 1: 1},
        compiler_params=pltpu.CompilerParams(has_side_effects=DATAFLOW),
    )(x_thru, land_thru, send_sem, recv_sem, after)[1]


def _exchange_seq(x, *, name, gather, collective_id):
    shape = x.shape[-2:]
    x_ref = jax.new_ref(x, memory_space=pltpu.MemorySpace.HBM)
    out_ref = jax.empty_ref(jax.ShapeDtypeStruct((N_DEV,) + shape, x.dtype), memory_space=pltpu.MemorySpace.HBM)

    @pl.kernel(mesh=plsc.ScalarSubcoreMesh(axis_name="seq", num_cores=1), name=name,
               scratch_types=(pltpu.SemaphoreType.DMA, pltpu.SemaphoreType.DMA, pltpu.SemaphoreType.DMA),
               compiler_params=pltpu.CompilerParams(collective_id=collective_id))
    def launch(send_sem, recv_sem, loc_sem):
        barrier = pltpu.get_barrier_semaphore()
        for k in range(1, N_DEV):
            xyz, _ = _peer(k)
            pl.semaphore_signal(barrier, inc=1, device_id=xyz, device_id_type=MESH)
        pl.semaphore_wait(barrier, N_DEV - 1)
        mine, me = _peer(0)

        def src(d):
            return x_ref if gather else x_ref.at[d]

        local = pltpu.make_async_copy(src(me), out_ref.at[me], loc_sem)
        local.start()
        for k in range(1, N_DEV):
            xyz, peer = _peer(k)
            pltpu.make_async_remote_copy(
                src_ref=src(peer), dst_ref=out_ref.at[me], send_sem=send_sem, recv_sem=recv_sem,
                device_id=xyz, device_id_type=MESH).start()
        seven = out_ref.at[pl.ds(0, N_DEV - 1)]
        pltpu.make_async_remote_copy(src_ref=seven, dst_ref=seven, send_sem=send_sem, recv_sem=recv_sem,
                                     device_id=mine, device_id_type=MESH).wait()
        local.wait()

    launch()
    return out_ref[...]


def _gather_seq(x, *, name, collective_id):
    x_ref = jax.new_ref(x, memory_space=pltpu.MemorySpace.HBM)
    out_ref = jax.empty_ref(jax.ShapeDtypeStruct((N_DEV,) + x.shape, x.dtype), memory_space=pltpu.MemorySpace.HBM)

    @pl.kernel(mesh=plsc.ScalarSubcoreMesh(axis_name="seq", num_cores=1), name=name,
               scratch_types=(pltpu.SemaphoreType.DMA((N_DEV - 1,)), pltpu.SemaphoreType.DMA((N_DEV - 1,)),
                              pltpu.SemaphoreType.DMA),
               compiler_params=pltpu.CompilerParams(collective_id=collective_id))
    def launch(send_sems, recv_sems, loc_sem):
        ix, iy, ic = lax.axis_index("x"), lax.axis_index("y"), lax.axis_index("c")
        me, sibling = (ix, iy, ic), (ix, iy, 1 - ic)
        chips = [(1 - ix, iy), (ix, 1 - iy), (1 - ix, 1 - iy)]
        writers = [sibling] + [(*chip, ic) for chip in chips]
        barrier = pltpu.get_barrier_semaphore()
        for peer in writers:
            pl.semaphore_signal(barrier, inc=1, device_id=peer, device_id_type=MESH)
        pl.semaphore_wait(barrier, len(writers))

        def rows(px, py, pc):
            return out_ref.at[4 * px + 2 * py + pc]

        def copy(k, block, to, src=None):
            return pltpu.make_async_remote_copy(
                src_ref=rows(*block) if src is None else src, dst_ref=rows(*block),
                send_sem=send_sems.at[k], recv_sem=recv_sems.at[k], device_id=to, device_id_type=MESH)

        mine = pltpu.make_async_copy(x_ref, rows(*me), loc_sem)
        mine.start()
        first = [copy(0, me, sibling, src=x_ref)]
        first += [copy(1 + j, me, (*chip, ic), src=x_ref) for j, chip in enumerate(chips)]
        for cp in first:
            cp.start()
        passed = [copy(4 + j, (*chip, ic), sibling) for j, chip in enumerate(chips)]
        for j, chip in enumerate(chips):
            copy(1 + j, (*chip, ic), me).wait_recv()
            passed[j].start()
        copy(0, sibling, me).wait_recv()
        for j, chip in enumerate(chips):
            copy(4 + j, (*chip, 1 - ic), me).wait_recv()
        for cp in first + passed:
            cp.wait_send()
        mine.wait()

    launch()
    return out_ref[...]


def _sum_slots(x, *, name):
    n, R, C = x.shape
    tr = _tile(R, 256, 8)

    def body(x_ref, o_ref):
        acc = x_ref[0].astype(F32)
        for d in range(1, n):
            acc = acc + x_ref[d].astype(F32)
        o_ref[...] = acc

    return pl.pallas_call(
        body, name=name, grid=(R // tr,),
        out_shape=jax.ShapeDtypeStruct((R, C), F32),
        in_specs=[pl.BlockSpec((n, tr, C), lambda i: (0, i, 0))],
        out_specs=pl.BlockSpec((tr, C), lambda i: (i, 0)),
        compiler_params=_params(("parallel",)),
    )(x)


def _colsum(x, *, name):
    R, C = x.shape

    def body(x_ref, o_ref):
        o_ref[...] = jnp.sum(x_ref[...], axis=0, keepdims=True)

    return pl.pallas_call(
        body, name=name, out_shape=jax.ShapeDtypeStruct((1, C), F32),
        in_specs=[pl.BlockSpec((R, C), lambda: (0, 0))], out_specs=pl.BlockSpec((1, C), lambda: (0, 0)),
        compiler_params=_params(),
    )(x)


def _adamw(parts, w, m, v, *, name):
    n, R, C = parts.shape
    tr = _tile(R, 128, 8)
    c1 = 1.0 - ADAM_B1 ** ADAM_STEP
    c2 = 1.0 - ADAM_B2 ** ADAM_STEP

    def body(p_ref, w_ref, m_ref, v_ref, g_ref, d_ref, nm_ref, nv_ref):
        g = p_ref[0].astype(F32)
        for d in range(1, n):
            g = g + p_ref[d].astype(F32)
        mn = ADAM_B1 * m_ref[...] + (1.0 - ADAM_B1) * g
        vn = ADAM_B2 * v_ref[...] + (1.0 - ADAM_B2) * (g * g)
        g_ref[...] = g
        nm_ref[...] = mn
        nv_ref[...] = vn
        d_ref[...] = -ADAM_LR * ((mn / c1) / (jnp.sqrt(vn / c2) + ADAM_EPS) + ADAM_WD * w_ref[...])

    spec = pl.BlockSpec((tr, C), lambda i: (i, 0))
    shp = jax.ShapeDtypeStruct((R, C), F32)
    return pl.pallas_call(
        body, name=name, grid=(R // tr,), out_shape=(shp, shp, shp, shp),
        in_specs=[pl.BlockSpec((n, tr, C), lambda i: (0, i, 0)), spec, spec, spec],
        out_specs=(spec, spec, spec, spec),
        compiler_params=_params(("parallel",)),
    )(parts, w, m, v)


MM_VMEM_BUDGET = 40 * 1024 * 1024
MM_TK_MAX = 2816
MXU_WIDTH = 256


def _divisors(n, quantum, cap):
    return [t for t in range(quantum, min(n, cap) + 1, quantum) if n % t == 0] or [n]


def _mm_tiles(M, N, K, mode, a_bytes, b_bytes, o_bytes, has_add):
    tk = max(_divisors(K, 128, MM_TK_MAX))
    nk = K // tk
    best = None
    for tm in _divisors(M, 128 if mode == "tn" else 8, 1024):
        for tn in _divisors(N, 128, 3072):
            need = 2 * (tm * tk * a_bytes + tk * tn * b_bytes) + 2 * tm * tn * o_bytes + tm * tn * 4
            need += tm * tn * 4 if nk > 1 else 0
            need += 2 * tm * tn * 4 if has_add else 0
            if need > MM_VMEM_BUDGET:
                continue
            score = (tn % MXU_WIDTH == 0 or tn == N, tm * tn, tm)
            if best is None or score > best[0]:
                best = (score, tm, tn)
    assert best is not None, (M, N, K)
    return best[1], best[2], tk


def _mm(a, b, mode, *, name, out_dtype, add=None, after=None):
    if mode == "nn":
        (M, K), (K2, N) = a.shape, b.shape
    elif mode == "nt":
        (M, K), (N, K2) = a.shape, b.shape
    else:
        (K, M), (K2, N) = a.shape, b.shape
    assert K == K2, (name, a.shape, b.shape)
    tm, tn, tk = _mm_tiles(M, N, K, mode, a.dtype.itemsize, b.dtype.itemsize, jnp.dtype(out_dtype).itemsize,
                           add is not None)
    nk = K // tk
    dims = {"nn": NN, "nt": NT, "tn": TN}[mode]

    a_spec = {"nn": pl.BlockSpec((tm, tk), lambda i, j, k: (i, k)),
              "nt": pl.BlockSpec((tm, tk), lambda i, j, k: (i, k)),
              "tn": pl.BlockSpec((tk, tm), lambda i, j, k: (k, i))}[mode]
    b_spec = {"nn": pl.BlockSpec((tk, tn), lambda i, j, k: (k, j)),
              "nt": pl.BlockSpec((tn, tk), lambda i, j, k: (j, k)),
              "tn": pl.BlockSpec((tk, tn), lambda i, j, k: (k, j))}[mode]
    o_spec = pl.BlockSpec((tm, tn), lambda i, j, k: (i, j))

    def body(a_ref, b_ref, *rest):
        rest = list(rest)
        add_ref = rest.pop(0) if add is not None else None
        if after is not None:
            rest.pop(0)
        o_ref = rest.pop(0)
        part = lax.dot_general(a_ref[...].astype(ACT), b_ref[...].astype(ACT), dims, preferred_element_type=F32)

        def finish(r):
            if add is not None:
                r = r + add_ref[...].astype(F32)
            o_ref[...] = r.astype(out_dtype)

        if nk == 1:
            finish(part)
            return
        acc = rest.pop(0)
        k = pl.program_id(2)

        @pl.when(k == 0)
        def _():
            acc[...] = part

        @pl.when(jnp.logical_and(k > 0, k < nk - 1))
        def _():
            acc[...] += part

        @pl.when(k == nk - 1)
        def _():
            finish(acc[...] + part)

    operands = [a, b] + ([] if add is None else [add])
    in_specs = [a_spec, b_spec] + ([] if add is None else [o_spec])
    if after is not None:
        operands.append(after)
        in_specs.append(ANY_SPEC)
    return pl.pallas_call(
        body, name=name, grid=(M // tm, N // tn, nk),
        out_shape=jax.ShapeDtypeStruct((M, N), out_dtype),
        in_specs=in_specs, out_specs=o_spec,
        scratch_shapes=[pltpu.VMEM((tm, tn), F32)] if nk > 1 else [],
        compiler_params=_params(("parallel", "parallel", "arbitrary")),
    )(*operands)


def _row(tr, cols, blk=0):
    return pl.BlockSpec((tr, cols), lambda i: (i, blk))


def _vec(cols):
    return pl.BlockSpec((1, cols), lambda i: (0, 0))


def _rms(xf):
    return lax.rsqrt(jnp.mean(xf * xf, axis=-1, keepdims=True) + EPS)


def _rms_bwd(dxhat, xhat, r):
    return r * (dxhat - xhat * jnp.mean(dxhat * xhat, axis=-1, keepdims=True))


def _acc_rows(ref, val, first):
    s = jnp.sum(val, axis=0, keepdims=True)

    @pl.when(first)
    def _():
        ref[...] = s

    @pl.when(jnp.logical_not(first))
    def _():
        ref[...] += s


def _norm_mod_fwd(x, nw, shift, scale, *, name):
    T, D = x.shape
    tr = _tile(T, 256, 8)

    def body(x_ref, nw_ref, sh_ref, sc_ref, o_ref):
        xf = x_ref[...]
        n = xf * _rms(xf) * nw_ref[...]
        o_ref[...] = (n * (1.0 + sc_ref[...]) + sh_ref[...]).astype(ACT)

    return pl.pallas_call(
        body, name=name, grid=(T // tr,), out_shape=jax.ShapeDtypeStruct((T, D), ACT),
        in_specs=[_row(tr, D), _vec(D), _vec(D), _vec(D)], out_specs=_row(tr, D),
        compiler_params=_params(("parallel",)),
    )(x, nw, shift, scale)


def _resid_norm_mod_fwd(x, o, g, nw, shift, scale, *, name):
    T, D = x.shape
    tr = _tile(T, 256, 8)

    def body(x_ref, o_ref, g_ref, nw_ref, sh_ref, sc_ref, x1_ref, h_ref):
        x1 = x_ref[...] + g_ref[...] * o_ref[...].astype(F32)
        x1_ref[...] = x1
        n = x1 * _rms(x1) * nw_ref[...]
        h_ref[...] = (n * (1.0 + sc_ref[...]) + sh_ref[...]).astype(ACT)

    return pl.pallas_call(
        body, name=name, grid=(T // tr,),
        out_shape=(jax.ShapeDtypeStruct((T, D), F32), jax.ShapeDtypeStruct((T, D), ACT)),
        in_specs=[_row(tr, D), _row(tr, D), _vec(D), _vec(D), _vec(D), _vec(D)],
        out_specs=(_row(tr, D), _row(tr, D)),
        compiler_params=_params(("parallel",)),
    )(x, o, g, nw, shift, scale)


def _final_fwd_bwd(x1, dn, g2, nw, target, *, name):
    T, D = x1.shape
    tr = _tile(T, 256, 8)

    def body(x1_ref, dn_ref, g_ref, nw_ref, t_ref, loss_ref, dx_ref, ddn_ref, dnw_ref, dg_ref):
        first = pl.program_id(0) == 0
        dn_f = dn_ref[...].astype(F32)
        x2 = x1_ref[...] + g_ref[...] * dn_f
        r = _rms(x2)
        xhat = x2 * r
        err = xhat * nw_ref[...] - t_ref[...]
        part = 0.5 * jnp.sum(jnp.mean(err * err, axis=-1, keepdims=True), axis=0, keepdims=True)

        @pl.when(first)
        def _():
            loss_ref[...] = part

        @pl.when(jnp.logical_not(first))
        def _():
            loss_ref[...] += part

        dy = err * (1.0 / D)
        _acc_rows(dnw_ref, dy * xhat, first)
        dx2 = _rms_bwd(dy * nw_ref[...], xhat, r)
        dx_ref[...] = dx2
        ddn_ref[...] = (g_ref[...] * dx2).astype(ACT)
        _acc_rows(dg_ref, dx2 * dn_f, first)

    vec = jax.ShapeDtypeStruct((1, D), F32)
    return pl.pallas_call(
        body, name=name, grid=(T // tr,),
        out_shape=(jax.ShapeDtypeStruct((1, 1), F32), jax.ShapeDtypeStruct((T, D), F32),
                   jax.ShapeDtypeStruct((T, D), ACT), vec, vec),
        in_specs=[_row(tr, D), _row(tr, D), _vec(D), _vec(D), _row(tr, D)],
        out_specs=(pl.BlockSpec((1, 1), lambda i: (0, 0)), _row(tr, D), _row(tr, D), _vec(D), _vec(D)),
        compiler_params=_params(("arbitrary",)),
    )(x1, dn, g2, nw, target)


def _norm_mod_bwd(xin, nw, scale, dh, *, name, dres=None, o=None, g=None):
    T, D = xin.shape
    tr = _tile(T, 256, 8)
    has_res, has_o = dres is not None, o is not None

    def body(*refs):
        refs = list(refs)
        x_ref, nw_ref, sc_ref, dh_ref = refs[:4]
        pos = 4
        dres_ref = o_ref = g_ref = None
        if has_res:
            dres_ref = refs[pos]
            pos += 1
        if has_o:
            o_ref, g_ref = refs[pos], refs[pos + 1]
            pos += 2
        dx_ref, dsh_ref, dsc_ref, dnw_ref = refs[pos:pos + 4]
        pos += 4
        first = pl.program_id(0) == 0
        xf = x_ref[...]
        r = _rms(xf)
        xhat = xf * r
        n = xhat * nw_ref[...]
        dhf = dh_ref[...].astype(F32)
        _acc_rows(dsh_ref, dhf, first)
        _acc_rows(dsc_ref, dhf * n, first)
        dn = dhf * (1.0 + sc_ref[...])
        _acc_rows(dnw_ref, dn * xhat, first)
        dx = _rms_bwd(dn * nw_ref[...], xhat, r)
        if has_res:
            dx = dx + dres_ref[...]
        dx_ref[...] = dx
        if has_o:
            do_ref, dg_ref = refs[pos], refs[pos + 1]
            do_ref[...] = (g_ref[...] * dx).astype(ACT)
            _acc_rows(dg_ref, dx * o_ref[...].astype(F32), first)

    vec = jax.ShapeDtypeStruct((1, D), F32)
    operands = [xin, nw, scale, dh]
    in_specs = [_row(tr, D), _vec(D), _vec(D), _row(tr, D)]
    if has_res:
        operands.append(dres)
        in_specs.append(_row(tr, D))
    if has_o:
        operands += [o, g]
        in_specs += [_row(tr, D), _vec(D)]
    out_shape = [jax.ShapeDtypeStruct((T, D), F32), vec, vec, vec]
    out_specs = [_row(tr, D), _vec(D), _vec(D), _vec(D)]
    if has_o:
        out_shape += [jax.ShapeDtypeStruct((T, D), ACT), vec]
        out_specs += [_row(tr, D), _vec(D)]
    return pl.pallas_call(
        body, name=name, grid=(T // tr,), out_shape=tuple(out_shape),
        in_specs=in_specs, out_specs=tuple(out_specs),
        compiler_params=_params(("arbitrary",)),
    )(*operands)


def _swiglu_fwd(gu, *, name):
    T, F = gu.shape[0], gu.shape[1] // 2
    tr = _tile(T, 256, 8)

    def body(g_ref, u_ref, o_ref):
        o_ref[...] = (_silu(g_ref[...].astype(F32)) * u_ref[...].astype(F32)).astype(ACT)

    return pl.pallas_call(
        body, name=name, grid=(T // tr,), out_shape=jax.ShapeDtypeStruct((T, F), ACT),
        in_specs=[_row(tr, F, 0), _row(tr, F, 1)], out_specs=_row(tr, F),
        compiler_params=_params(("parallel",)),
    )(gu, gu)


def _swiglu_bwd(gu, dact, *, name):
    T, F = gu.shape[0], gu.shape[1] // 2
    tr = _tile(T, 256, 8)

    def body(g_ref, u_ref, d_ref, o_ref):
        gf, uf, df = g_ref[...].astype(F32), u_ref[...].astype(F32), d_ref[...].astype(F32)
        o_ref[:, :F] = (df * uf * _dsilu(gf)).astype(ACT)
        o_ref[:, F:] = (df * _silu(gf)).astype(ACT)

    return pl.pallas_call(
        body, name=name, grid=(T // tr,), out_shape=jax.ShapeDtypeStruct((T, 2 * F), ACT),
        in_specs=[_row(tr, F, 0), _row(tr, F, 1), _row(tr, F)], out_specs=_row(tr, 2 * F),
        compiler_params=_params(("parallel",)),
    )(gu, gu, dact)


def _merge_fwd(ya, yb, gates, *, name):
    T, D = ya.shape
    tr = _tile(T, 256, 8)

    def body(a_ref, b_ref, g_ref, o_ref):
        ga = g_ref[:, :D].astype(F32)
        gb = g_ref[:, D:].astype(F32)
        o_ref[...] = (jax.nn.sigmoid(ga) * a_ref[...].astype(F32)
                      + jax.nn.sigmoid(gb) * b_ref[...].astype(F32)).astype(ACT)

    return pl.pallas_call(
        body, name=name, grid=(T // tr,), out_shape=jax.ShapeDtypeStruct((T, D), ACT),
        in_specs=[_row(tr, D), _row(tr, D), _row(tr, 2 * D)], out_specs=_row(tr, D),
        compiler_params=_params(("parallel",)),
    )(ya, yb, gates)


def _merge_bwd(dmer, ya, yb, gates, *, name):
    T, D = ya.shape
    tr = _tile(T, 256, 8)

    def body(d_ref, a_ref, b_ref, g_ref, da_ref, db_ref, dg_ref):
        d = d_ref[...].astype(F32)
        sa = jax.nn.sigmoid(g_ref[:, :D].astype(F32))
        sb = jax.nn.sigmoid(g_ref[:, D:].astype(F32))
        da_ref[...] = (d * sa).astype(ACT)
        db_ref[...] = (d * sb).astype(ACT)
        dg_ref[:, :D] = (d * a_ref[...].astype(F32) * sa * (1.0 - sa)).astype(ACT)
        dg_ref[:, D:] = (d * b_ref[...].astype(F32) * sb * (1.0 - sb)).astype(ACT)

    shp = jax.ShapeDtypeStruct((T, D), ACT)
    return pl.pallas_call(
        body, name=name, grid=(T // tr,), out_shape=(shp, shp, jax.ShapeDtypeStruct((T, 2 * D), ACT)),
        in_specs=[_row(tr, D), _row(tr, D), _row(tr, D), _row(tr, 2 * D)],
        out_specs=(_row(tr, D), _row(tr, D), _row(tr, 2 * D)),
        compiler_params=_params(("parallel",)),
    )(dmer, ya, yb, gates)


def _gate_norm_fwd(yf, yb, xbc, z, dskip, nw, *, name):
    T, DI = z.shape
    tr = _tile(T, 128, 8)

    def body(yf_ref, yb_ref, xs_ref, z_ref, ds_ref, nw_ref, o_ref):
        y = yf_ref[...].astype(F32) + yb_ref[...].astype(F32) + ds_ref[...] * xs_ref[...].astype(F32)
        gz = y * _silu(z_ref[...].astype(F32))
        o_ref[...] = (gz * _rms(gz) * nw_ref[...]).astype(ACT)

    return pl.pallas_call(
        body, name=name, grid=(T // tr,), out_shape=jax.ShapeDtypeStruct((T, DI), ACT),
        in_specs=[_row(tr, DI), _row(tr, DI), _row(tr, DI), _row(tr, DI), _vec(DI), _vec(DI)],
        out_specs=_row(tr, DI),
        compiler_params=_params(("parallel",)),
    )(yf, yb, xbc, z, dskip, nw)


def _gate_norm_bwd(dout, yf, yb, xbc, z, dskip, nw, *, name):
    T, DI = z.shape
    tr = _tile(T, 128, 8)

    def body(do_ref, yf_ref, yb_ref, xs_ref, z_ref, ds_ref, nw_ref, dy_ref, dz_ref, dxs_ref, dnw_ref, dds_ref):
        first = pl.program_id(0) == 0
        xs = xs_ref[...].astype(F32)
        zf = z_ref[...].astype(F32)
        y = yf_ref[...].astype(F32) + yb_ref[...].astype(F32) + ds_ref[...] * xs
        sz = _silu(zf)
        gz = y * sz
        r = _rms(gz)
        ghat = gz * r
        do = do_ref[...].astype(F32)
        _acc_rows(dnw_ref, do * ghat, first)
        dgz = _rms_bwd(do * nw_ref[...], ghat, r)
        dy = dgz * sz
        dy_ref[...] = dy.astype(ACT)
        dz_ref[...] = (dgz * y * _dsilu(zf)).astype(ACT)
        dxs_ref[...] = (dy * ds_ref[...]).astype(ACT)
        _acc_rows(dds_ref, dy * xs, first)

    shp = jax.ShapeDtypeStruct((T, DI), ACT)
    vec = jax.ShapeDtypeStruct((1, DI), F32)
    return pl.pallas_call(
        body, name=name, grid=(T // tr,), out_shape=(shp, shp, shp, vec, vec),
        in_specs=[_row(tr, DI)] * 5 + [_vec(DI), _vec(DI)],
        out_specs=(_row(tr, DI), _row(tr, DI), _row(tr, DI), _vec(DI), _vec(DI)),
        compiler_params=_params(("arbitrary",)),
    )(dout, yf, yb, xbc, z, dskip, nw)


def _ln_silu_fwd(x, g, b, *, name):
    T, D = x.shape
    tr = _tile(T, 256, 8)

    def body(x_ref, g_ref, b_ref, o_ref):
        xf = x_ref[...].astype(F32)
        xc = xf - jnp.mean(xf, axis=-1, keepdims=True)
        rstd = lax.rsqrt(jnp.mean(xc * xc, axis=-1, keepdims=True) + EPS)
        o_ref[...] = _silu(xc * rstd * g_ref[...] + b_ref[...]).astype(ACT)

    return pl.pallas_call(
        body, name=name, grid=(T // tr,), out_shape=jax.ShapeDtypeStruct((T, D), ACT),
        in_specs=[_row(tr, D), _vec(D), _vec(D)], out_specs=_row(tr, D),
        compiler_params=_params(("parallel",)),
    )(x, g, b)


def _ln_silu_bwd(x, g, b, dcf, *, name):
    T, D = x.shape
    tr = _tile(T, 256, 8)

    def body(x_ref, g_ref, b_ref, d_ref, dx_ref, dg_ref, db_ref):
        first = pl.program_id(0) == 0
        xf = x_ref[...].astype(F32)
        xc = xf - jnp.mean(xf, axis=-1, keepdims=True)
        rstd = lax.rsqrt(jnp.mean(xc * xc, axis=-1, keepdims=True) + EPS)
        xhat = xc * rstd
        dyln = d_ref[...].astype(F32) * _dsilu(xhat * g_ref[...] + b_ref[...])
        _acc_rows(dg_ref, dyln * xhat, first)
        _acc_rows(db_ref, dyln, first)
        dxh = dyln * g_ref[...]
        dx = rstd * (dxh - jnp.mean(dxh, axis=-1, keepdims=True)
                     - xhat * jnp.mean(dxh * xhat, axis=-1, keepdims=True))
        dx_ref[...] = dx.astype(ACT)

    vec = jax.ShapeDtypeStruct((1, D), F32)
    return pl.pallas_call(
        body, name=name, grid=(T // tr,), out_shape=(jax.ShapeDtypeStruct((T, D), ACT), vec, vec),
        in_specs=[_row(tr, D), _vec(D), _vec(D), _row(tr, D)],
        out_specs=(_row(tr, D), _vec(D), _vec(D)),
        compiler_params=_params(("arbitrary",)),
    )(x, g, b, dcf)


CONV_CW = 128
CONV_RT = 256
SEQ_PAD = 8


def _window(ext, off, n):
    if off % 8 == 0:
        return ext[off:off + n]
    return pltpu.roll(ext, ext.shape[0] - off, 0)[:n]


def _sum8(v):
    R, C = v.shape
    return jnp.sum(v.reshape(R // 8, 8, C), axis=0)


def _conv5_silu_fwd(x, w, b, *, name):
    T, C = x.shape
    K = 5
    cw, rt = CONV_CW, _tile(T, CONV_RT, 8)
    half = K // 2

    def body(x_ref, w_ref, b_ref, o_ref, pad):
        zeros = jnp.zeros((SEQ_PAD, cw), F32)
        pad[0:SEQ_PAD, :] = zeros
        pad[T + SEQ_PAD:T + 2 * SEQ_PAD, :] = zeros

        def fill(i, c):
            base = pl.multiple_of(i * rt, rt)
            pad[pl.ds(base + SEQ_PAD, rt), :] = x_ref[pl.ds(base, rt), :].astype(F32)
            return c

        lax.fori_loop(0, T // rt, fill, 0)
        wv = w_ref[...]
        bias = b_ref[...]

        def step(i, c):
            base = pl.multiple_of(i * rt, rt)
            ext = pad[pl.ds(base, rt + 2 * SEQ_PAD), :]
            acc = jnp.zeros((rt, cw), F32) + bias
            for k in range(K):
                acc = acc + wv[k:k + 1, :] * _window(ext, SEQ_PAD + k - half, rt)
            o_ref[pl.ds(base, rt), :] = _silu(acc).astype(ACT)
            return c

        lax.fori_loop(0, T // rt, step, 0)

    return pl.pallas_call(
        body, name=name, grid=(C // cw,), out_shape=jax.ShapeDtypeStruct((T, C), ACT),
        in_specs=[pl.BlockSpec((T, cw), lambda j: (0, j)), pl.BlockSpec((8, cw), lambda j: (0, j)),
                  pl.BlockSpec((1, cw), lambda j: (0, j))],
        out_specs=pl.BlockSpec((T, cw), lambda j: (0, j)),
        scratch_shapes=[pltpu.VMEM((T + 2 * SEQ_PAD, cw), F32)],
        compiler_params=_params(("parallel",)),
    )(x, w, b)


def _conv5_silu_bwd(x, w, b, dout, *, name):
    T, C = x.shape
    K = 5
    cw, rt = CONV_CW, _tile(T, CONV_RT, 8)
    half = K // 2

    def body(x_ref, w_ref, b_ref, d_ref, dx_ref, dw_ref, db_ref, pad, dpad, wacc):
        zeros = jnp.zeros((SEQ_PAD, cw), F32)
        for p in (pad, dpad):
            p[0:SEQ_PAD, :] = zeros
            p[T + SEQ_PAD:T + 2 * SEQ_PAD, :] = zeros
        wacc[...] = jnp.zeros_like(wacc)

        def fill(i, c):
            base = pl.multiple_of(i * rt, rt)
            pad[pl.ds(base + SEQ_PAD, rt), :] = x_ref[pl.ds(base, rt), :].astype(F32)
            return c

        lax.fori_loop(0, T // rt, fill, 0)
        wv = w_ref[...]
        bias = b_ref[...]

        def step1(i, c):
            base = pl.multiple_of(i * rt, rt)
            ext = pad[pl.ds(base, rt + 2 * SEQ_PAD), :]
            wins = [_window(ext, SEQ_PAD + k - half, rt) for k in range(K)]
            pre = jnp.zeros((rt, cw), F32) + bias
            for k in range(K):
                pre = pre + wv[k:k + 1, :] * wins[k]
            dpre = d_ref[pl.ds(base, rt), :].astype(F32) * _dsilu(pre)
            dpad[pl.ds(base + SEQ_PAD, rt), :] = dpre
            for k in range(K):
                wacc[k] += _sum8(dpre * wins[k])
            wacc[K] += _sum8(dpre)
            return c

        lax.fori_loop(0, T // rt, step1, 0)

        def step2(i, c):
            base = pl.multiple_of(i * rt, rt)
            ext = dpad[pl.ds(base, rt + 2 * SEQ_PAD), :]
            acc = jnp.zeros((rt, cw), F32)
            for k in range(K):
                acc = acc + wv[k:k + 1, :] * _window(ext, SEQ_PAD - (k - half), rt)
            dx_ref[pl.ds(base, rt), :] = acc.astype(ACT)
            return c

        lax.fori_loop(0, T // rt, step2, 0)
        rows = [jnp.sum(wacc[k], axis=0, keepdims=True) for k in range(K)]
        rows += [jnp.zeros((1, cw), F32)] * (8 - K)
        dw_ref[...] = jnp.concatenate(rows, axis=0)
        db_ref[...] = jnp.sum(wacc[K], axis=0, keepdims=True)

    return pl.pallas_call(
        body, name=name, grid=(C // cw,),
        out_shape=(jax.ShapeDtypeStruct((T, C), ACT), jax.ShapeDtypeStruct((8, C), F32),
                   jax.ShapeDtypeStruct((1, C), F32)),
        in_specs=[pl.BlockSpec((T, cw), lambda j: (0, j)), pl.BlockSpec((8, cw), lambda j: (0, j)),
                  pl.BlockSpec((1, cw), lambda j: (0, j)), pl.BlockSpec((T, cw), lambda j: (0, j))],
        out_specs=(pl.BlockSpec((T, cw), lambda j: (0, j)), pl.BlockSpec((8, cw), lambda j: (0, j)),
                   pl.BlockSpec((1, cw), lambda j: (0, j))),
        scratch_shapes=[pltpu.VMEM((T + 2 * SEQ_PAD, cw), F32), pltpu.VMEM((T + 2 * SEQ_PAD, cw), F32),
                        pltpu.VMEM((K + 1, 8, cw), F32)],
        compiler_params=_params(("parallel",)),
    )(x, w, b, dout)


def _glu_conv_fwd(u, v, w, b, *, name):
    T, C = u.shape
    K = 31
    KP = w.shape[0]
    cw, rt = CONV_CW, _tile(T, CONV_RT, GRID_W)
    half = K // 2
    P = half * GRID_W

    def body(u_ref, v_ref, w_ref, b_ref, o_ref, pad):
        zeros = jnp.zeros((P, cw), F32)
        pad[0:P, :] = zeros
        pad[T + P:T + 2 * P, :] = zeros

        def fill(i, c):
            base = pl.multiple_of(i * rt, rt)
            uf = u_ref[pl.ds(base, rt), :].astype(F32)
            vf = v_ref[pl.ds(base, rt), :].astype(F32)
            pad[pl.ds(base + P, rt), :] = uf * jax.nn.sigmoid(vf)
            return c

        lax.fori_loop(0, T // rt, fill, 0)
        wv = w_ref[...]
        bias = b_ref[...]

        def step(i, c):
            base = pl.multiple_of(i * rt, rt)
            acc = jnp.zeros((rt, cw), F32) + bias
            for k in range(K):
                acc = acc + wv[k:k + 1, :] * pad[pl.ds(base + k * GRID_W, rt), :]
            o_ref[pl.ds(base, rt), :] = acc.astype(ACT)
            return c

        lax.fori_loop(0, T // rt, step, 0)

    col = pl.BlockSpec((T, cw), lambda j: (0, j))
    return pl.pallas_call(
        body, name=name, grid=(C // cw,), out_shape=jax.ShapeDtypeStruct((T, C), ACT),
        in_specs=[col, col, pl.BlockSpec((KP, cw), lambda j: (0, j)), pl.BlockSpec((1, cw), lambda j: (0, j))],
        out_specs=col,
        scratch_shapes=[pltpu.VMEM((T + 2 * P, cw), F32)],
        compiler_params=_params(("parallel",)),
    )(u, v, w, b)


def _glu_conv_bwd(u, v, w, dout, *, name):
    T, C = u.shape
    K = 31
    KP = w.shape[0]
    cw, rt = CONV_CW, _tile(T, CONV_RT, GRID_W)
    half = K // 2
    P = half * GRID_W

    def body(u_ref, v_ref, w_ref, d_ref, du_ref, dv_ref, dw_ref, db_ref, pad, dpad, wacc):
        zeros = jnp.zeros((P, cw), F32)
        for p in (pad, dpad):
            p[0:P, :] = zeros
            p[T + P:T + 2 * P, :] = zeros
        wacc[...] = jnp.zeros_like(wacc)

        def fill(i, c):
            base = pl.multiple_of(i * rt, rt)
            uf = u_ref[pl.ds(base, rt), :].astype(F32)
            vf = v_ref[pl.ds(base, rt), :].astype(F32)
            pad[pl.ds(base + P, rt), :] = uf * jax.nn.sigmoid(vf)
            dpad[pl.ds(base + P, rt), :] = d_ref[pl.ds(base, rt), :].astype(F32)
            return c

        lax.fori_loop(0, T // rt, fill, 0)
        wv = w_ref[...]

        def step(i, c):
            base = pl.multiple_of(i * rt, rt)
            d = dpad[pl.ds(base + P, rt), :]
            dg = jnp.zeros((rt, cw), F32)
            for k in range(K):
                wacc[k] += _sum8(d * pad[pl.ds(base + k * GRID_W, rt), :])
                dg = dg + wv[k:k + 1, :] * dpad[pl.ds(base + (K - 1 - k) * GRID_W, rt), :]
            wacc[K] += _sum8(d)
            uf = u_ref[pl.ds(base, rt), :].astype(F32)
            sv = jax.nn.sigmoid(v_ref[pl.ds(base, rt), :].astype(F32))
            du_ref[pl.ds(base, rt), :] = (dg * sv).astype(ACT)
            dv_ref[pl.ds(base, rt), :] = (dg * uf * sv * (1.0 - sv)).astype(ACT)
            return c

        lax.fori_loop(0, T // rt, step, 0)
        rows = [jnp.sum(wacc[k], axis=0, keepdims=True) for k in range(K)]
        rows += [jnp.zeros((1, cw), F32)] * (KP - K)
        dw_ref[...] = jnp.concatenate(rows, axis=0)
        db_ref[...] = jnp.sum(wacc[K], axis=0, keepdims=True)

    col = pl.BlockSpec((T, cw), lambda j: (0, j))
    shp = jax.ShapeDtypeStruct((T, C), ACT)
    return pl.pallas_call(
        body, name=name, grid=(C // cw,),
        out_shape=(shp, shp, jax.ShapeDtypeStruct((KP, C), F32), jax.ShapeDtypeStruct((1, C), F32)),
        in_specs=[col, col, pl.BlockSpec((KP, cw), lambda j: (0, j)), col],
        out_specs=(col, col, pl.BlockSpec((KP, cw), lambda j: (0, j)), pl.BlockSpec((1, cw), lambda j: (0, j))),
        scratch_shapes=[pltpu.VMEM((T + 2 * P, cw), F32), pltpu.VMEM((T + 2 * P, cw), F32),
                        pltpu.VMEM((K + 1, 8, cw), F32)],
        compiler_params=_params(("parallel",)),
    )(u, v, w, dout)


def _dt_fwd(rawT, bias, *, name):
    H2, T = rawT.shape
    tc = _tile(T, 2048, 128)

    def body(r_ref, b_ref, o_ref):
        v = r_ref[...] + b_ref[...]
        o_ref[...] = jnp.maximum(v, 0.0) + jnp.log(1.0 + jnp.exp(-jnp.abs(v)))

    return pl.pallas_call(
        body, name=name, grid=(T // tc,), out_shape=jax.ShapeDtypeStruct((H2, T), F32),
        in_specs=[pl.BlockSpec((H2, tc), lambda i: (0, i)), pl.BlockSpec((H2, 1), lambda i: (0, 0))],
        out_specs=pl.BlockSpec((H2, tc), lambda i: (0, i)),
        compiler_params=_params(("parallel",)),
    )(rawT, bias)


def _dt_bwd(rawT, bias, ddtT, *, name):
    H2, T = rawT.shape
    tc = _tile(T, 2048, 128)

    def body(r_ref, b_ref, d_ref, o_ref, db_ref):
        first = pl.program_id(0) == 0
        dr = d_ref[...] * jax.nn.sigmoid(r_ref[...] + b_ref[...])
        o_ref[...] = dr
        s = jnp.sum(dr, axis=1, keepdims=True)

        @pl.when(first)
        def _():
            db_ref[...] = s

        @pl.when(jnp.logical_not(first))
        def _():
            db_ref[...] += s

    return pl.pallas_call(
        body, name=name, grid=(T // tc,),
        out_shape=(jax.ShapeDtypeStruct((H2, T), F32), jax.ShapeDtypeStruct((H2, 1), F32)),
        in_specs=[pl.BlockSpec((H2, tc), lambda i: (0, i)), pl.BlockSpec((H2, 1), lambda i: (0, 0)),
                  pl.BlockSpec((H2, tc), lambda i: (0, i))],
        out_specs=(pl.BlockSpec((H2, tc), lambda i: (0, i)), pl.BlockSpec((H2, 1), lambda i: (0, 0))),
        compiler_params=_params(("arbitrary",)),
    )(rawT, bias, ddtT)


def _ssd_common(dtT, a, reverse):
    J, Q = dtT.shape
    li = lax.broadcasted_iota(jnp.int32, (Q, Q), 0)
    si = lax.broadcasted_iota(jnp.int32, (Q, Q), 1)
    mask = (si >= li) if reverse else (si <= li)
    Mf = mask.astype(F32)
    daT = dtT * a
    csT = lax.dot_general(daT, Mf, NT, precision=HIGHEST, preferred_element_type=F32)
    last = 0 if reverse else Q - 1
    totT = csT[:, last:last + 1]
    return mask, Mf, csT, totT, last


def _to_cols(rows):
    R, Q = rows.shape
    if R < 128:
        rows = jnp.concatenate([rows, jnp.zeros((128 - R, Q), F32)], axis=0)
    return rows.T


def _ssd_fwd(xbc, dtT, a, h0, *, reverse, name, di):
    T = xbc.shape[0]
    G, JP, N = h0.shape
    J, P, Q = JP // HEAD_DIM, HEAD_DIM, CHUNK
    nc = T // Q
    QW = 256
    HQ = QW // P

    def ci(k):
        return nc - 1 - k if reverse else k

    GB = SSD_GROUPS_PER_STEP

    def body(x_ref, b_ref, c_ref, dt_ref, a_ref, h0_ref, y_ref, hs_ref, hl_ref, h_scr):
        k = pl.program_id(1)

        @pl.when(k == 0)
        def _():
            h_scr[...] = h0_ref[...]

        lh = lax.broadcasted_iota(jnp.int32, (Q, QW), 1) // P

        def scale_heads(vT, rowsT):
            return jnp.concatenate([vT[j * P:(j + 1) * P, :] * rowsT[j:j + 1, :] for j in range(J)], axis=0)

        for gi in range(GB):
            h = h_scr[gi]
            hs_ref[0, gi] = h
            Xb = x_ref[:, gi * JP:(gi + 1) * JP]
            Bm, Cm = b_ref[:, gi * N:(gi + 1) * N], c_ref[:, gi * N:(gi + 1) * N]
            dtT_v = dt_ref[gi * J:(gi + 1) * J, :]
            mask, _, csT, totT, _ = _ssd_common(dtT_v, a_ref[gi * J:(gi + 1) * J, :], reverse)
            cs = _to_cols(csT)
            CB = lax.dot_general(Cm, Bm, NT, preferred_element_type=F32)
            yoT = lax.dot_general(h.astype(ACT), Cm, NT, preferred_element_type=F32)
            yo = scale_heads(yoT, jnp.exp(csT)).T
            for q in range(JP // QW):
                xq = Xb[:, q * QW:(q + 1) * QW]
                acc = yo[:, q * QW:(q + 1) * QW]
                for jj in range(HQ):
                    j = q * HQ + jj
                    seg = cs[:, j:j + 1] - csT[j:j + 1, :]
                    Mj = (CB * jnp.exp(jnp.where(mask, seg, NEG)) * dtT_v[j:j + 1, :]).astype(ACT)
                    acc = acc + jnp.dot(Mj, jnp.where(lh == jj, xq, jnp.zeros_like(xq)),
                                        preferred_element_type=F32)
                y_ref[:, gi * JP + q * QW:gi * JP + (q + 1) * QW] = acc.astype(ACT)
            xwT = scale_heads(Xb.astype(F32).T, dtT_v * jnp.exp(totT - csT)).astype(ACT)
            upd = jnp.dot(xwT, Bm, preferred_element_type=F32)
            for j in range(J):
                rows = slice(j * P, (j + 1) * P)
                h_scr[gi, rows, :] = h[rows, :] * jnp.exp(totT[j:j + 1, :]) + upd[rows, :]

        @pl.when(k == nc - 1)
        def _():
            hl_ref[...] = h_scr[...]

    GN = G * N
    return pl.pallas_call(
        body, name=name, grid=(G // GB, nc),
        out_shape=(jax.ShapeDtypeStruct((T, di), ACT), jax.ShapeDtypeStruct((nc, G, JP, N), F32),
                   jax.ShapeDtypeStruct((G, JP, N), F32)),
        in_specs=[pl.BlockSpec((Q, GB * JP), lambda g, k: (ci(k), g)),
                  pl.BlockSpec((Q, GB * N), lambda g, k: (ci(k), di // (GB * N) + g)),
                  pl.BlockSpec((Q, GB * N), lambda g, k: (ci(k), (di + GN) // (GB * N) + g)),
                  pl.BlockSpec((GB * J, Q), lambda g, k: (g, ci(k))),
                  pl.BlockSpec((GB * J, 1), lambda g, k: (g, 0)),
                  pl.BlockSpec((GB, JP, N), lambda g, k: (g, 0, 0))],
        out_specs=(pl.BlockSpec((Q, GB * JP), lambda g, k: (ci(k), g)),
                   pl.BlockSpec((1, GB, JP, N), lambda g, k: (ci(k), g, 0, 0)),
                   pl.BlockSpec((GB, JP, N), lambda g, k: (g, 0, 0))),
        scratch_shapes=[pltpu.VMEM((GB, JP, N), F32)],
        compiler_params=_params(("arbitrary", "arbitrary")),
    )(xbc, xbc, xbc, dtT, a, h0)


def _ssd_bwd(xbc, dtT, a, dy, hs, dh_last, add, *, reverse, name, di):
    T = xbc.shape[0]
    G, JP, N = dh_last.shape
    J, P, Q = JP // HEAD_DIM, HEAD_DIM, CHUNK
    nc = T // Q
    QW = 256
    HQ = QW // P
    has_add = add is not None

    def ci(k):
        return k if reverse else nc - 1 - k

    GB = SSD_GROUPS_PER_STEP

    def body(*refs):
        for gi in range(GB):
            wide = lambda r, w: r.at[:, pl.ds(gi * w, w)]
            x_ref, b_ref, c_ref, dt_ref, a_ref, dy_ref, hs_ref, dhl_ref = refs[:8]
            views = [wide(x_ref, JP), wide(b_ref, N), wide(c_ref, N), dt_ref.at[pl.ds(gi * J, J)],
                     a_ref.at[pl.ds(gi * J, J)], wide(dy_ref, JP), hs_ref.at[:, pl.ds(gi, 1)],
                     dhl_ref.at[pl.ds(gi, 1)]]
            rest = refs[8:]
            if has_add:
                views += [wide(rest[0], JP), wide(rest[1], N), wide(rest[2], N)]
                rest = rest[3:]
            dx_ref, db_ref, dc_ref, ddt_ref, da_ref, dh0_ref, dh_scr = rest
            views += [wide(dx_ref, JP), wide(db_ref, N), wide(dc_ref, N), ddt_ref.at[pl.ds(gi * J, J)],
                      da_ref.at[pl.ds(gi * J, J)], dh0_ref.at[pl.ds(gi, 1)], dh_scr.at[gi]]
            group_body(*views)

    def group_body(x_ref, b_ref, c_ref, dt_ref, a_ref, dy_ref, hs_ref, dhl_ref, *rest):
        if has_add:
            adx_ref, adb_ref, adc_ref = rest[:3]
            rest = rest[3:]
        dx_ref, db_ref, dc_ref, ddt_ref, da_ref, dh0_ref, dh_scr = rest
        k = pl.program_id(1)

        @pl.when(k == 0)
        def _():
            dh_scr[...] = dhl_ref[0]
            da_ref[...] = jnp.zeros_like(da_ref)

        def scale_heads(vT, rowsT):
            return jnp.concatenate([vT[j * P:(j + 1) * P, :] * rowsT[j:j + 1, :] for j in range(J)], axis=0)

        def head_sums(vT):
            return jnp.sum(vT.reshape(J, P, Q), axis=1)

        dH = dh_scr[...]
        h = hs_ref[0, 0]
        Bm, Cm = b_ref[...], c_ref[...]
        dtT_v = dt_ref[...]
        a_v = a_ref[...]
        mask, Mf, csT, totT, last = _ssd_common(dtT_v, a_v, reverse)
        ecsT = jnp.exp(csT)
        toendT = jnp.exp(totT - csT)
        cs = _to_cols(csT)
        dYb = dy_ref[...]
        XT = x_ref[...].astype(F32).T
        dYT = dYb.astype(F32).T
        xdtT = scale_heads(XT, dtT_v).astype(ACT)
        dYT_b = dYT.astype(ACT)
        dYeT = scale_heads(dYT, ecsT).astype(ACT)
        h_b = h.astype(ACT)
        dH_b = dH.astype(ACT)
        CB = lax.dot_general(Cm, Bm, NT, preferred_element_type=F32)
        dxdt_offT = scale_heads(lax.dot_general(dH_b, Bm, NT, preferred_element_type=F32), toendT)
        dCB = jnp.zeros((Q, Q), F32)
        lh = lax.broadcasted_iota(jnp.int32, (Q, QW), 1) // P
        sh = lax.broadcasted_iota(jnp.int32, (QW, Q), 0) // P
        lane_q = lax.broadcasted_iota(jnp.int32, (Q, Q), 1)
        sub_j = lax.broadcasted_iota(jnp.int32, (J, Q), 0)
        e_rows = jnp.zeros((Q, Q), F32)
        e_cols = jnp.zeros((J, Q), F32)
        diag = []
        for q in range(JP // QW):
            xq = xdtT[q * QW:(q + 1) * QW, :]
            dyq = dYb[:, q * QW:(q + 1) * QW]
            dyTq = dYT_b[q * QW:(q + 1) * QW, :]
            acc = jnp.zeros((QW, Q), F32)
            for jj in range(HQ):
                j = q * HQ + jj
                seg = cs[:, j:j + 1] - csT[j:j + 1, :]
                L = jnp.exp(jnp.where(mask, seg, NEG))
                Mf_j = CB * L
                dyj = jnp.where(lh == jj, dyq, jnp.zeros_like(dyq))
                dyTj = jnp.where(sh == jj, dyTq, jnp.zeros_like(dyTq))
                acc = acc + jnp.dot(dyTj, Mf_j.astype(ACT), preferred_element_type=F32)
                dM = jnp.dot(dyj, xq, preferred_element_type=F32)
                dCB = dCB + dM * L
                E = dM * Mf_j
                e_rows = jnp.where(lane_q == j, jnp.sum(E, axis=1, keepdims=True), e_rows)
                e_cols = jnp.where(sub_j == j, jnp.sum(E, axis=0, keepdims=True), e_cols)
            diag.append(acc)
        dxdtT = dxdt_offT + jnp.concatenate(diag, axis=0)
        dCB_b = dCB.astype(ACT)
        dC = (jnp.dot(dCB_b, Bm, preferred_element_type=F32)
              + lax.dot_general(dYeT, h_b, TN, preferred_element_type=F32))
        xwT = scale_heads(XT, dtT_v * toendT).astype(ACT)
        dB = (lax.dot_general(dCB_b, Cm, TN, preferred_element_type=F32)
              + lax.dot_general(xwT, dH_b, TN, preferred_element_type=F32))
        dHc = jnp.dot(dYeT, Cm, preferred_element_type=F32)
        for j in range(J):
            rows = slice(j * P, (j + 1) * P)
            dh_scr[rows, :] = dH[rows, :] * jnp.exp(totT[j:j + 1, :]) + dHc[rows, :]
        dh0_ref[0] = dh_scr[...]

        yoT = scale_heads(lax.dot_general(h_b, Cm, NT, preferred_element_type=F32), ecsT)
        RT_ = head_sums(dYT * yoT)
        UT_ = head_sums(XT * dxdtT)
        UoT = head_sums(XT * dxdt_offT)
        hsum = jnp.sum(jnp.sum((dH * h).reshape(J, P, N), axis=1), axis=1, keepdims=True)
        dtot = jnp.sum(UoT * dtT_v, axis=1, keepdims=True) + jnp.exp(totT) * hsum
        lane = lax.broadcasted_iota(jnp.int32, (J, Q), 1)
        dcsT = e_rows.T[0:J] - e_cols + RT_ - UoT * dtT_v + jnp.where(lane == last, dtot, 0.0)
        ddaT = jnp.dot(dcsT, Mf, precision=HIGHEST, preferred_element_type=F32)
        ddt_ref[...] = ddaT * a_v + UT_
        da_ref[...] += ddaT * dtT_v
        dX = scale_heads(dxdtT, dtT_v).T
        if has_add:
            dX = dX + adx_ref[...].astype(F32)
            dB = dB + adb_ref[...].astype(F32)
            dC = dC + adc_ref[...].astype(F32)
        dx_ref[...] = dX.astype(ACT)
        db_ref[...] = dB.astype(ACT)
        dc_ref[...] = dC.astype(ACT)

    GN = G * N
    xspec = pl.BlockSpec((Q, GB * JP), lambda g, k: (ci(k), g))
    nspec = pl.BlockSpec((Q, GB * N), lambda g, k: (ci(k), g))
    hspec = pl.BlockSpec((GB, JP, N), lambda g, k: (g, 0, 0))
    in_specs = [xspec,
                pl.BlockSpec((Q, GB * N), lambda g, k: (ci(k), di // (GB * N) + g)),
                pl.BlockSpec((Q, GB * N), lambda g, k: (ci(k), (di + GN) // (GB * N) + g)),
                pl.BlockSpec((GB * J, Q), lambda g, k: (g, ci(k))),
                pl.BlockSpec((GB * J, 1), lambda g, k: (g, 0)),
                xspec,
                pl.BlockSpec((1, GB, JP, N), lambda g, k: (ci(k), g, 0, 0)),
                hspec]
    operands = [xbc, xbc, xbc, dtT, a, dy, hs, dh_last]
    if has_add:
        in_specs += [xspec, nspec, nspec]
        operands += list(add)
    H = G * J
    return pl.pallas_call(
        body, name=name, grid=(G // GB, nc),
        out_shape=(jax.ShapeDtypeStruct((T, di), ACT), jax.ShapeDtypeStruct((T, GN), ACT),
                   jax.ShapeDtypeStruct((T, GN), ACT), jax.ShapeDtypeStruct((H, T), F32),
                   jax.ShapeDtypeStruct((H, Q), F32), jax.ShapeDtypeStruct((G, JP, N), F32)),
        in_specs=in_specs,
        out_specs=(xspec, nspec, nspec,
                   pl.BlockSpec((GB * J, Q), lambda g, k: (g, ci(k))),
                   pl.BlockSpec((GB * J, Q), lambda g, k: (g, 0)),
                   hspec),
        scratch_shapes=[pltpu.VMEM((GB, JP, N), F32)],
        compiler_params=_params(("arbitrary", "arbitrary")),
    )(*operands)


def _mod_fwd(craw, w, b, *, name):
    R, D = craw.shape
    NL = w.shape[1]
    tn = _tile(NL, 512, 128)

    def body(c_ref, w_ref, b_ref, o_ref):
        o_ref[...] = jnp.dot(_silu(c_ref[...]), w_ref[...], preferred_element_type=F32) + b_ref[...]

    return pl.pallas_call(
        body, name=name, grid=(NL // tn,), out_shape=jax.ShapeDtypeStruct((R, NL), F32),
        in_specs=[pl.BlockSpec((R, D), lambda j: (0, 0)), pl.BlockSpec((D, tn), lambda j: (0, j)),
                  pl.BlockSpec((1, tn), lambda j: (0, j))],
        out_specs=pl.BlockSpec((R, tn), lambda j: (0, j)),
        compiler_params=_params(("parallel",)),
    )(craw, w, b)


def _mod_bwd(craw, w, dm, *, name):
    R, D = craw.shape
    NL = w.shape[1]
    tn = _tile(NL, 512, 128)

    def body(c_ref, w_ref, dm_ref, dw_ref, dc_ref):
        first = pl.program_id(0) == 0
        cf = c_ref[...]
        dmv = dm_ref[...]
        dw_ref[...] = lax.dot_general(_silu(cf), dmv, TN, preferred_element_type=F32)
        part = lax.dot_general(dmv, w_ref[...], NT, preferred_element_type=F32) * _dsilu(cf)

        @pl.when(first)
        def _():
            dc_ref[...] = part

        @pl.when(jnp.logical_not(first))
        def _():
            dc_ref[...] += part

    return pl.pallas_call(
        body, name=name, grid=(NL // tn,),
        out_shape=(jax.ShapeDtypeStruct((D, NL), F32), jax.ShapeDtypeStruct((R, D), F32)),
        in_specs=[pl.BlockSpec((R, D), lambda j: (0, 0)), pl.BlockSpec((D, tn), lambda j: (0, j)),
                  pl.BlockSpec((R, tn), lambda j: (0, j))],
        out_specs=(pl.BlockSpec((D, tn), lambda j: (0, j)), pl.BlockSpec((R, D), lambda j: (0, 0))),
        compiler_params=_params(("arbitrary",)),
    )(craw, w, dm)


def _pad_rows(a, rows):
    return jnp.concatenate([a, jnp.zeros((rows - a.shape[0],) + a.shape[1:], a.dtype)], axis=0)


def _cols_whole(g):
    return jnp.transpose(g, (1, 0, 2)).reshape(g.shape[1], N_DEV * g.shape[2])


def _rows_whole(g):
    return g.reshape(N_DEV * g.shape[1], g.shape[2])


def _col_blocks(full):
    K, n8 = full.shape
    return jnp.transpose(full.reshape(K, N_DEV, n8 // N_DEV), (1, 0, 2)).astype(ACT)


def _row_blocks(full):
    K8, n = full.shape
    return full.reshape(N_DEV, K8 // N_DEV, n).astype(ACT)


def kernel(x, c, ctx, c_ctx, w_mod, b_mod, norm_mix, w_in, ssm_conv_w, ssm_conv_b, dt_bias, a_log, d_skip, ssm_norm, cf_conv_w, cf_conv_b, cf_ln_g, cf_ln_b, w_proj_a, w_proj_b, w_out, norm_ffn, w_ffn_gate, w_ffn_up, w_ffn_down, norm_final, loss_target, m_c_ctx, m_w_mod, m_b_mod, m_norm_mix, m_w_in, m_ssm_conv_w, m_ssm_conv_b, m_dt_bias, m_a_log, m_d_skip, m_ssm_norm, m_cf_conv_w, m_cf_conv_b, m_cf_ln_g, m_cf_ln_b, m_w_proj_a, m_w_proj_b, m_w_out, m_norm_ffn, m_w_ffn_gate, m_w_ffn_up, m_w_ffn_down, m_norm_final, v_c_ctx, v_w_mod, v_b_mod, v_norm_mix, v_w_in, v_ssm_conv_w, v_ssm_conv_b, v_dt_bias, v_a_log, v_d_skip, v_ssm_norm, v_cf_conv_w, v_cf_conv_b, v_cf_ln_g, v_cf_ln_b, v_w_proj_a, v_w_proj_b, v_w_out, v_norm_ffn, v_w_ffn_gate, v_w_ffn_up, v_w_ffn_down, v_norm_final):
    args = dict(locals())
    me = 4 * lax.axis_index("x") + 2 * lax.axis_index("y") + lax.axis_index("c")
    T, D = x.shape[1], x.shape[2]
    DI = ssm_norm.shape[1]
    H = DI // HEAD_DIM
    G, J, N = GROUPS, H // GROUPS, STATE
    JP = J * HEAD_DIM
    GN = G * N
    CONV = DI + 2 * GN
    x0 = x[0]
    ctx0 = ctx[0]
    target = loss_target[0]

    got_in = _gather_seq(w_in[0].astype(ACT), name="gather_w_in", collective_id=0)
    k5 = ssm_conv_w.shape[1]
    k31 = cf_conv_w.shape[1]
    cw5 = _exchange(_pad_rows(ssm_conv_w[0], 8), name="gather_conv5", gather=True)
    cw5 = jnp.transpose(cw5, (1, 0, 2)).reshape(8, CONV)
    cw31 = _exchange(_pad_rows(cf_conv_w[0], 32), name="gather_conv31", gather=True)
    cw31 = jnp.transpose(cw31, (1, 0, 2)).reshape(32, D)

    c_all = _exchange(_pad_rows(c, 8), name="gather_c", gather=True)[:, 0, :]
    craw = jnp.concatenate([c_all, c_ctx[None, :], jnp.zeros((7, D), F32)], axis=0)
    NL = w_mod.shape[2]
    b_loc = lax.dynamic_slice(b_mod, (0, me * NL), (1, NL))
    m_loc = _mod_fwd(craw, w_mod[0], b_loc, name="mod_fwd")
    m_all = jnp.transpose(_exchange(m_loc, name="gather_mod", gather=True), (1, 0, 2)).reshape(16, N_DEV * NL)
    m_me = lax.dynamic_slice(m_all, (me, 0), (1, 6 * D))
    sh1, sc1, g1, sh2, sc2, g2 = [m_me[:, i * D:(i + 1) * D] for i in range(6)]
    csh1, csc1 = m_all[8:9, 0:D], m_all[8:9, D:2 * D]

    win = _cols_whole(got_in)
    o_xbc, o_dt, o_glu, o_gates = DI, DI + CONV, DI + CONV + 2 * H, DI + CONV + 2 * H + 2 * D
    w_z, w_xbc, w_dt = win[:, :o_xbc], win[:, o_xbc:o_dt], win[:, o_dt:o_glu]
    w_u, w_v, w_gates = win[:, o_glu:o_glu + D], win[:, o_glu + D:o_gates], win[:, o_gates:]
    def gather_behind(shard, behind, name, cid):
        zero = (behind[(0,) * behind.ndim] * 0).astype(ACT)
        return _gather_seq(shard.astype(ACT) + zero, name=name, collective_id=cid)

    w_pa = _rows_whole(gather_behind(w_proj_a[0], got_in, "gather_w_pa", 1))
    w_pb = _rows_whole(gather_behind(w_proj_b[0], got_in, "gather_w_pb", 2))
    w_o = _rows_whole(gather_behind(w_out[0], got_in, "gather_w_out", 3))

    a_neg = -jnp.exp(a_log[0])
    a_f, a_b = a_neg[0][:, None], a_neg[1][:, None]
    dtb = dt_bias[0].reshape(2 * H, 1)
    dskip_e = jnp.repeat(d_skip[0], HEAD_DIM)[None, :]

    def front(h, tag, full, after=None):
        out = {}
        out["xbc_raw"] = _mm(h, w_xbc, "nn", name="mm_xbc_" + tag, out_dtype=ACT, after=after)
        dt_raw = _mm(h, w_dt, "nn", name="mm_dt_" + tag, out_dtype=F32)
        out["rawT"] = dt_raw.T
        if full:
            out["z"] = _mm(h, w_z, "nn", name="mm_z_" + tag, out_dtype=ACT)
            out["u"] = _mm(h, w_u, "nn", name="mm_u_" + tag, out_dtype=ACT)
            out["v"] = _mm(h, w_v, "nn", name="mm_v_" + tag, out_dtype=ACT)
            out["gates"] = _mm(h, w_gates, "nn", name="mm_gates_" + tag, out_dtype=ACT)
        out["xbc"] = _conv5_silu_fwd(out["xbc_raw"], cw5, ssm_conv_b, name="conv5_fwd_" + tag)
        out["dtT"] = _dt_fwd(out["rawT"], dtb, name="dt_fwd_" + tag)
        return out

    hc = _norm_mod_fwd(ctx0, norm_mix, csh1, csc1, name="norm_mod_ctx")
    fc = front(hc, "ctx", False)
    zero_state = jnp.zeros((G, JP, N), F32)
    _, hs_cf, h_f = _ssd_fwd(fc["xbc"], fc["dtT"][:H], a_f, zero_state, reverse=False, name="ssd_fwd_ctx_f", di=DI)
    _, hs_cb, h_b = _ssd_fwd(fc["xbc"], fc["dtT"][H:], a_b, zero_state, reverse=True, name="ssd_fwd_ctx_b", di=DI)

    hx = _norm_mod_fwd(x0, norm_mix, sh1, sc1, name="norm_mod_x")
    fx = front(hx, "x", True)
    y_f, hs_f, _ = _ssd_fwd(fx["xbc"], fx["dtT"][:H], a_f, h_f, reverse=False, name="ssd_fwd_x_f", di=DI)
    FF = w_ffn_gate.shape[2]
    got_gu = gather_behind(jnp.concatenate([w_ffn_gate[0], w_ffn_up[0]], axis=1), y_f, "gather_w_gate_up", 4)
    w_gu = jnp.concatenate([_cols_whole(got_gu[:, :, :FF]), _cols_whole(got_gu[:, :, FF:])], axis=1)
    w_down = _rows_whole(gather_behind(w_ffn_down[0], y_f, "gather_w_down", 6))
    y_b, hs_b, _ = _ssd_fwd(fx["xbc"], fx["dtT"][H:], a_b, h_b, reverse=True, name="ssd_fwd_x_b", di=DI)
    ya_in = _gate_norm_fwd(y_f, y_b, fx["xbc"], fx["z"], dskip_e, ssm_norm, name="gate_norm_fwd")
    ya = _mm(ya_in, w_pa, "nn", name="mm_proj_a", out_dtype=ACT)
    conv_out = _glu_conv_fwd(fx["u"], fx["v"], cw31, cf_conv_b, name="glu_conv_fwd")
    cf = _ln_silu_fwd(conv_out, cf_ln_g, cf_ln_b, name="ln_silu_fwd")
    yb = _mm(cf, w_pb, "nn", name="mm_proj_b", out_dtype=ACT)
    merged = _merge_fwd(ya, yb, fx["gates"], name="merge_fwd")
    o_mix = _mm(merged, w_o, "nn", name="mm_out", out_dtype=ACT)

    x1, h2 = _resid_norm_mod_fwd(x0, o_mix, g1, norm_ffn, sh2, sc2, name="resid_norm_mod")
    gu = _mm(h2, w_gu, "nn", name="mm_gate_up", out_dtype=ACT)
    act = _swiglu_fwd(gu, name="swiglu_fwd")
    dn = _mm(act, w_down, "nn", name="mm_down", out_dtype=ACT)

    loss_part, dx2, d_dn, g_norm_final, d_g2 = _final_fwd_bwd(x1, dn, g2, norm_final[None, :], target, name="final")
    loss = lax.psum(loss_part[0, 0], AXES)

    d_act = _mm(d_dn, w_down, "nt", name="mm_d_act", out_dtype=ACT)
    gw_down = _mm(act, d_dn, "tn", name="mm_gw_down", out_dtype=F32)
    parts = {}
    parts["w_ffn_down"] = _exchange_seq(_row_blocks(gw_down), name="scatter_w_down", gather=False, collective_id=7)
    d_gu = _swiglu_bwd(gu, d_act, name="swiglu_bwd")
    gw_gu = _mm(h2, d_gu, "tn", name="mm_gw_gate_up", out_dtype=F32)
    DFF = N_DEV * FF
    gu_blocks = jnp.concatenate([_col_blocks(gw_gu[:, :DFF]), _col_blocks(gw_gu[:, DFF:])], axis=2)
    parts_gu = _exchange_seq(gu_blocks, name="scatter_w_gate_up", gather=False, collective_id=8)
    parts["w_ffn_gate"], parts["w_ffn_up"] = parts_gu[:, :, :FF], parts_gu[:, :, FF:]
    d_h2 = _mm(d_gu, w_gu, "nt", name="mm_d_h2", out_dtype=F32)
    dx1, d_sh2, d_sc2, g_norm_ffn, d_o, d_g1 = _norm_mod_bwd(
        x1, norm_ffn, sc2, d_h2, name="norm_mod_bwd_ffn", dres=dx2, o=o_mix, g=g1)

    d_merged = _mm(d_o, w_o, "nt", name="mm_d_merged", out_dtype=ACT)
    gw_out = _mm(merged, d_o, "tn", name="mm_gw_out", out_dtype=F32)
    parts["w_out"] = _exchange_seq(_row_blocks(gw_out), name="scatter_w_out", gather=False, collective_id=10)
    d_ya, d_yb, d_gates = _merge_bwd(d_merged, ya, yb, fx["gates"], name="merge_bwd")
    gw_pa = _mm(ya_in, d_ya, "tn", name="mm_gw_pa", out_dtype=F32)
    parts["w_proj_a"] = _exchange_seq(_row_blocks(gw_pa), name="scatter_w_pa", gather=False, collective_id=11)
    gw_pb = _mm(cf, d_yb, "tn", name="mm_gw_pb", out_dtype=F32)
    parts["w_proj_b"] = _exchange_seq(_row_blocks(gw_pb), name="scatter_w_pb", gather=False, collective_id=12)
    d_ya_in = _mm(d_ya, w_pa, "nt", name="mm_d_ya_in", out_dtype=ACT)
    d_cf = _mm(d_yb, w_pb, "nt", name="mm_d_cf", out_dtype=ACT)
    d_conv, g_ln_g, g_ln_b = _ln_silu_bwd(conv_out, cf_ln_g, cf_ln_b, d_cf, name="ln_silu_bwd")
    d_u, d_v, g_cw31, g_cb31 = _glu_conv_bwd(fx["u"], fx["v"], cw31, d_conv, name="glu_conv_bwd")
    d_y, d_z, dxs_skip, g_ssm_norm, g_dskip_e = _gate_norm_bwd(
        d_ya_in, y_f, y_b, fx["xbc"], fx["z"], dskip_e, ssm_norm, name="gate_norm_bwd")

    zero_bc = jnp.zeros((T, GN), ACT)
    r1 = _ssd_bwd(fx["xbc"], fx["dtT"][:H], a_f, d_y, hs_f, zero_state, (dxs_skip, zero_bc, zero_bc),
                  reverse=False, name="ssd_bwd_x_f", di=DI)
    r2 = _ssd_bwd(fx["xbc"], fx["dtT"][H:], a_b, d_y, hs_b, zero_state, r1[:3],
                  reverse=True, name="ssd_bwd_x_b", di=DI)
    Tc = ctx0.shape[0]
    zero_yc = jnp.zeros((Tc, DI), ACT)
    r3 = _ssd_bwd(fc["xbc"], fc["dtT"][:H], a_f, zero_yc, hs_cf, r1[5], None,
                  reverse=False, name="ssd_bwd_ctx_f", di=DI)
    r4 = _ssd_bwd(fc["xbc"], fc["dtT"][H:], a_b, zero_yc, hs_cb, r2[5], r3[:3],
                  reverse=True, name="ssd_bwd_ctx_b", di=DI)

    def back(f, rf, rb, tag):
        d_xbc = jnp.concatenate([rb[0], rb[1], rb[2]], axis=1)
        d_xbc_raw, g_w5, g_b5 = _conv5_silu_bwd(f["xbc_raw"], cw5, ssm_conv_b, d_xbc, name="conv5_bwd_" + tag)
        ddtT = jnp.concatenate([rf[3], rb[3]], axis=0)
        d_rawT, g_dtb = _dt_bwd(f["rawT"], dtb, ddtT, name="dt_bwd_" + tag)
        g_a = jnp.stack([jnp.sum(rf[4], axis=1), jnp.sum(rb[4], axis=1)])
        return d_xbc_raw, d_rawT.T.astype(ACT), g_w5, g_b5, g_dtb, g_a

    dx_xbc_raw, dx_dt_raw, gx_w5, gx_b5, gx_dtb, gx_a = back(fx, r1, r2, "x")
    dc_xbc_raw, dc_dt_raw, gc_w5, gc_b5, gc_dtb, gc_a = back(fc, r3, r4, "ctx")

    gw_xbc = _mm(hc, dc_xbc_raw, "tn", name="mm_gw_xbc_ctx", out_dtype=F32)
    gw_xbc = _mm(hx, dx_xbc_raw, "tn", name="mm_gw_xbc", out_dtype=F32, add=gw_xbc)
    gw_dt = _mm(hc, dc_dt_raw, "tn", name="mm_gw_dt_ctx", out_dtype=F32)
    gw_dt = _mm(hx, dx_dt_raw, "tn", name="mm_gw_dt", out_dtype=F32, add=gw_dt)
    gw_z = _mm(hx, d_z, "tn", name="mm_gw_z", out_dtype=F32)
    gw_u = _mm(hx, d_u, "tn", name="mm_gw_u", out_dtype=F32)
    gw_v = _mm(hx, d_v, "tn", name="mm_gw_v", out_dtype=F32)
    gw_gates = _mm(hx, d_gates, "tn", name="mm_gw_gates", out_dtype=F32)
    gw_in = jnp.concatenate([gw_z, gw_xbc, gw_dt, gw_u, gw_v, gw_gates], axis=1)
    parts["w_in"] = _exchange_seq(_col_blocks(gw_in), name="scatter_w_in", gather=False, collective_id=13)

    d_hx = _mm(d_z, w_z, "nt", name="mm_d_hx_z", out_dtype=F32)
    d_hx = _mm(dx_xbc_raw, w_xbc, "nt", name="mm_d_hx_xbc", out_dtype=F32, add=d_hx)
    d_hx = _mm(dx_dt_raw, w_dt, "nt", name="mm_d_hx_dt", out_dtype=F32, add=d_hx)
    d_hx = _mm(d_u, w_u, "nt", name="mm_d_hx_u", out_dtype=F32, add=d_hx)
    d_hx = _mm(d_v, w_v, "nt", name="mm_d_hx_v", out_dtype=F32, add=d_hx)
    d_hx = _mm(d_gates, w_gates, "nt", name="mm_d_hx_gates", out_dtype=F32, add=d_hx)
    grad_x, d_sh1, d_sc1, gx_norm_mix = _norm_mod_bwd(x0, norm_mix, sc1, d_hx, name="norm_mod_bwd_x", dres=dx1)
    d_hc = _mm(dc_xbc_raw, w_xbc, "nt", name="mm_d_hc_xbc", out_dtype=F32)
    d_hc = _mm(dc_dt_raw, w_dt, "nt", name="mm_d_hc_dt", out_dtype=F32, add=d_hc)
    _, d_csh1, d_csc1, gc_norm_mix = _norm_mod_bwd(ctx0, norm_mix, csc1, d_hc, name="norm_mod_bwd_ctx")

    zD = jnp.zeros((1, D), F32)
    dm_me = jnp.concatenate([d_sh1, d_sc1, d_g1, d_sh2, d_sc2, d_g2], axis=1)
    dm_ctx = jnp.concatenate([d_csh1, d_csc1, zD, zD, zD, zD], axis=1)
    rows16 = lax.broadcasted_iota(jnp.int32, (16, 1), 0)
    dm_rows = jnp.where(rows16 == me, dm_me, 0.0) + jnp.where(rows16 == 8, dm_ctx, 0.0)
    dm_sum = _sum_slots(_exchange(dm_rows, name="gather_dm", gather=True), name="sum_dm")
    g_b_mod = _colsum(dm_sum, name="colsum_dm")
    dm_loc = lax.dynamic_slice(dm_sum, (0, me * NL), (16, NL))
    g_w_mod, dcraw = _mod_bwd(craw, w_mod[0], dm_loc, name="mod_bwd")

    small = [
        ("c_ctx", dcraw[8]), ("norm_mix", gx_norm_mix + gc_norm_mix),
        ("ssm_conv_w", (gx_w5 + gc_w5)[:k5]), ("ssm_conv_b", gx_b5 + gc_b5),
        ("dt_bias", gx_dtb + gc_dtb), ("a_log", (gx_a + gc_a) * a_neg),
        ("d_skip", jnp.sum(g_dskip_e.reshape(H, HEAD_DIM), axis=1)), ("ssm_norm", g_ssm_norm),
        ("cf_conv_w", g_cw31[:k31]), ("cf_conv_b", g_cb31), ("cf_ln_g", g_ln_g), ("cf_ln_b", g_ln_b),
        ("norm_ffn", g_norm_ffn), ("norm_final", g_norm_final),
    ]
    flat = jnp.concatenate([v.reshape(-1) for _, v in small])
    n_small = flat.shape[0]
    rows_small = -(-n_small // 1024) * 8
    flat = jnp.concatenate([flat, jnp.zeros((rows_small * 128 - n_small,), F32)]).reshape(rows_small, 128)
    summed = _sum_slots(_exchange(flat, name="gather_small", gather=True), name="sum_small").reshape(-1)
    g_small = {}
    pos = 0
    for nm, v in small:
        g_small[nm] = summed[pos:pos + v.size].reshape(v.shape)
        pos += v.size
    g_small["b_mod"] = g_b_mod
    n5, n31 = ssm_conv_w.shape[2], cf_conv_w.shape[2]
    g_small["ssm_conv_w"] = lax.dynamic_slice(g_small["ssm_conv_w"], (0, me * n5), (k5, n5))
    g_small["cf_conv_w"] = lax.dynamic_slice(g_small["cf_conv_w"], (0, me * n31), (k31, n31))

    grads, deltas, new_m, new_v = {}, {}, {}, {}

    def adam2d(nm, parts):
        shape = args[nm].shape
        R, C = shape[-2], shape[-1]
        g, d, m2, v2 = _adamw(parts, args[nm].reshape(R, C), args["m_" + nm].reshape(R, C),
                              args["v_" + nm].reshape(R, C), name="adamw_" + nm)
        grads[nm], deltas[nm], new_m[nm], new_v[nm] = [t.reshape(shape) for t in (g, d, m2, v2)]

    adam2d("w_mod", g_w_mod[None])
    for nm in ("w_ffn_down", "w_ffn_gate", "w_ffn_up", "w_out", "w_proj_a", "w_proj_b", "w_in"):
        adam2d(nm, parts[nm])

    small_names = ["c_ctx", "b_mod", "norm_mix", "ssm_conv_w", "ssm_conv_b", "dt_bias", "a_log", "d_skip", "ssm_norm",
                   "cf_conv_w", "cf_conv_b", "cf_ln_g", "cf_ln_b", "norm_ffn", "norm_final"]

    def pack(vals):
        f = jnp.concatenate([t.reshape(-1) for t in vals])
        rows = -(-f.shape[0] // 1024) * 8
        return jnp.concatenate([f, jnp.zeros((rows * 128 - f.shape[0],), F32)]).reshape(rows, 128)

    pg = pack([g_small[nm] for nm in small_names])
    pw = pack([args[nm] for nm in small_names])
    pm = pack([args["m_" + nm] for nm in small_names])
    pv = pack([args["v_" + nm] for nm in small_names])
    outs = _adamw(pg[None], pw, pm, pv, name="adamw_small")
    pos = 0
    for nm in small_names:
        shape = args[nm].shape
        size = math.prod(shape)
        vals = [t.reshape(-1)[pos:pos + size].reshape(shape) for t in outs]
        grads[nm], deltas[nm], new_m[nm], new_v[nm] = vals
        pos += size

    order = ["c_ctx", "w_mod", "b_mod", "norm_mix", "w_in", "ssm_conv_w", "ssm_conv_b", "dt_bias", "a_log", "d_skip",
             "ssm_norm", "cf_conv_w", "cf_conv_b", "cf_ln_g", "cf_ln_b", "w_proj_a", "w_proj_b", "w_out", "norm_ffn",
             "w_ffn_gate", "w_ffn_up", "w_ffn_down", "norm_final"]
    return (loss, grad_x[None], *[grads[n] for n in order], *[deltas[n] for n in order],
            *[new_m[n] for n in order], *[new_v[n] for n in order])
```

```python
import functools
import math

import jax
import jax.numpy as jnp
from jax import lax
from jax.experimental import pallas as pl
from jax.experimental.pallas import tpu as pltpu
from jax.experimental.pallas import tpu_sc as plsc

F32 = jnp.float32
ACT = jnp.bfloat16
HIGHEST = lax.Precision.HIGHEST
MESH = pl.DeviceIdType.MESH
AXES = ("x", "y", "c")
N_DEV = 8

GRID_W = 64
CHUNK = 128
SSD_GROUPS_PER_STEP = 2
HEAD_DIM = 64
GROUPS = 8
STATE = 128
EPS = 1e-6
ADAM_LR = 0.001
ADAM_B1 = 0.9
ADAM_B2 = 0.999
ADAM_EPS = 1e-08
ADAM_WD = 0.01
ADAM_STEP = 10

V7X_VMEM_LIMIT = 56 * 1024 * 1024
NEG = -1e30

NN = (((1,), (0,)), ((), ()))
NT = (((1,), (1,)), ((), ()))
TN = (((0,), (0,)), ((), ()))


def _tile(n, target, quantum):
    best = None
    t = quantum
    while t <= min(n, target):
        if n % t == 0:
            best = t
        t += quantum
    return n if best is None else best


def _params(sem=None):
    kw = dict(vmem_limit_bytes=V7X_VMEM_LIMIT)
    if sem is not None:
        kw["dimension_semantics"] = sem
    return pltpu.CompilerParams(**kw)


def _silu(v):
    return v * jax.nn.sigmoid(v)


def _dsilu(v):
    s = jax.nn.sigmoid(v)
    return s * (1.0 + v * (1.0 - s))


def _exchange(x, *, name, gather):
    shape = x.shape[-2:]

    def body(x_ref, o_ref, send_sems, recv_sems, loc_sem):
        ix, iy, ic = lax.axis_index("x"), lax.axis_index("y"), lax.axis_index("c")
        me = 4 * ix + 2 * iy + ic

        def src(d):
            return x_ref if gather else x_ref.at[d]

        def remote(k, slot, peer_xyz, src_ref):
            return pltpu.make_async_remote_copy(
                src_ref=src_ref, dst_ref=o_ref.at[slot], send_sem=send_sems.at[k], recv_sem=recv_sems.at[k],
                device_id=peer_xyz, device_id_type=MESH)

        local = pltpu.make_async_copy(src(me), o_ref.at[me], loc_sem)
        local.start()
        sends, peers = [], []
        for k in range(1, N_DEV):
            px = 1 - ix if k & 4 else ix
            py = 1 - iy if k & 2 else iy
            pc = 1 - ic if k & 1 else ic
            peer = 4 * px + 2 * py + pc
            cp = remote(k - 1, me, (px, py, pc), src(peer))
            cp.start()
            sends.append(cp)
            peers.append((peer, (px, py, pc)))
        for k in range(1, N_DEV):
            peer, xyz = peers[k - 1]
            remote(k - 1, peer, xyz, src(peer)).wait_recv()
        for cp in sends:
            cp.wait_send()
        local.wait()

    return pl.pallas_call(
        body, name=name,
        out_shape=jax.ShapeDtypeStruct((N_DEV,) + shape, x.dtype),
        in_specs=[pl.BlockSpec(memory_space=pl.ANY)],
        out_specs=pl.BlockSpec(memory_space=pl.ANY),
        scratch_shapes=[pltpu.SemaphoreType.DMA((N_DEV - 1,)), pltpu.SemaphoreType.DMA((N_DEV - 1,)),
                        pltpu.SemaphoreType.DMA],
    )(x)


HBM_SPEC = pl.BlockSpec(memory_space=pltpu.HBM)
SEM_SPEC = pl.BlockSpec(memory_space=pltpu.SEMAPHORE)
ANY_SPEC = pl.BlockSpec(memory_space=pl.ANY)
DATAFLOW = pltpu.SideEffectType.DATAFLOW_SIDE_EFFECTING


def _peer(k):
    ix, iy, ic = lax.axis_index("x"), lax.axis_index("y"), lax.axis_index("c")
    px = 1 - ix if k & 4 else ix
    py = 1 - iy if k & 2 else iy
    pc = 1 - ic if k & 1 else ic
    return (px, py, pc), 4 * px + 2 * py + pc


def _exchange_start(x, after, *, name, gather):
    shape = x.shape[-2:]

    def body(after_ref, x_ref, land_ref, send_sem, recv_sem, x_thru, land_thru, token, loc_sem):
        _, me = _peer(0)

        def src(d):
            return x_ref if gather else x_ref.at[d]

        local = pltpu.make_async_copy(src(me), land_ref.at[me], loc_sem)
        local.start()
        local.wait()
        for k in range(1, N_DEV):
            xyz, peer = _peer(k)
            pltpu.make_async_remote_copy(
                src_ref=src(peer), dst_ref=land_ref.at[me], send_sem=send_sem, recv_sem=recv_sem,
                device_id=xyz, device_id_type=MESH).start()
        token[...] = jnp.zeros_like(token)

    land = lax.empty((N_DEV,) + shape, x.dtype)
    return pl.pallas_call(
        body, name=name,
        out_shape=(pltpu.SemaphoreType.DMA(()), pltpu.SemaphoreType.DMA(()), pltpu.HBM(x.shape, x.dtype),
                   pltpu.HBM((N_DEV,) + shape, x.dtype), jax.ShapeDtypeStruct((8, 128), F32)),
        in_specs=(ANY_SPEC, HBM_SPEC, HBM_SPEC),
        out_specs=(SEM_SPEC, SEM_SPEC, HBM_SPEC, HBM_SPEC, pl.BlockSpec(memory_space=pltpu.VMEM)),
        input_output_aliases={1: 2, 2: 3},
        scratch_shapes=[pltpu.SemaphoreType.DMA],
        compiler_params=pltpu.CompilerParams(has_side_effects=DATAFLOW),
    )(after, pltpu.with_memory_space_constraint(x, pltpu.HBM), pltpu.with_memory_space_constraint(land, pltpu.HBM))


def _exchange_wait(started, after, *, name):
    send_sem, recv_sem, x_thru, land_thru, _ = started

    def body(x_ref, land_ref, send_sem, recv_sem, after_ref, x_dead, got_ref):
        xyz, _ = _peer(0)
        seven = land_ref.at[pl.ds(0, N_DEV - 1)]
        cp = pltpu.make_async_remote_copy(src_ref=seven, dst_ref=seven, send_sem=send_sem, recv_sem=recv_sem,
                                          device_id=xyz, device_id_type=MESH)
        cp.wait_send()
        cp.wait_recv()

    return pl.pallas_call(
        body, name=name,
        out_shape=(pltpu.HBM(x_thru.shape, x_thru.dtype), pltpu.HBM(land_thru.shape, land_thru.dtype)),
        in_specs=(HBM_SPEC, HBM_SPEC, SEM_SPEC, SEM_SPEC, ANY_SPEC),
        out_specs=(HBM_SPEC, HBM_SPEC),
        input_output_aliases={0: 0, 1: 1},
        compiler_params=pltpu.CompilerParams(has_side_effects=DATAFLOW),
    )(x_thru, land_thru, send_sem, recv_sem, after)[1]


def _exchange_seq(x, *, name, gather, collective_id):
    shape = x.shape[-2:]
    x_ref = jax.new_ref(x, memory_space=pltpu.MemorySpace.HBM)
    out_ref = jax.empty_ref(jax.ShapeDtypeStruct((N_DEV,) + shape, x.dtype), memory_space=pltpu.MemorySpace.HBM)

    @pl.kernel(mesh=plsc.ScalarSubcoreMesh(axis_name="seq", num_cores=1), name=name,
               scratch_types=(pltpu.SemaphoreType.DMA, pltpu.SemaphoreType.DMA, pltpu.SemaphoreType.DMA),
               compiler_params=pltpu.CompilerParams(collective_id=collective_id))
    def launch(send_sem, recv_sem, loc_sem):
        barrier = pltpu.get_barrier_semaphore()
        for k in range(1, N_DEV):
            xyz, _ = _peer(k)
            pl.semaphore_signal(barrier, inc=1, device_id=xyz, device_id_type=MESH)
        pl.semaphore_wait(barrier, N_DEV - 1)
        mine, me = _peer(0)

        def src(d):
            return x_ref if gather else x_ref.at[d]

        local = pltpu.make_async_copy(src(me), out_ref.at[me], loc_sem)
        local.start()
        for k in range(1, N_DEV):
            xyz, peer = _peer(k)
            pltpu.make_async_remote_copy(
                src_ref=src(peer), dst_ref=out_ref.at[me], send_sem=send_sem, recv_sem=recv_sem,
                device_id=xyz, device_id_type=MESH).start()
        seven = out_ref.at[pl.ds(0, N_DEV - 1)]
        pltpu.make_async_remote_copy(src_ref=seven, dst_ref=seven, send_sem=send_sem, recv_sem=recv_sem,
                                     device_id=mine, device_id_type=MESH).wait()
        local.wait()

    launch()
    return out_ref[...]


def _gather_seq(x, *, name, collective_id):
    x_ref = jax.new_ref(x, memory_space=pltpu.MemorySpace.HBM)
    out_ref = jax.empty_ref(jax.ShapeDtypeStruct((N_DEV,) + x.shape, x.dtype), memory_space=pltpu.MemorySpace.HBM)

    @pl.kernel(mesh=plsc.ScalarSubcoreMesh(axis_name="seq", num_cores=1), name=name,
               scratch_types=(pltpu.SemaphoreType.DMA((N_DEV - 1,)), pltpu.SemaphoreType.DMA((N_DEV - 1,)),
                              pltpu.SemaphoreType.DMA),
               compiler_params=pltpu.CompilerParams(collective_id=collective_id))
    def launch(send_sems, recv_sems, loc_sem):
        ix, iy, ic = lax.axis_index("x"), lax.axis_index("y"), lax.axis_index("c")
        me, sibling = (ix, iy, ic), (ix, iy, 1 - ic)
        chips = [(1 - ix, iy), (ix, 1 - iy), (1 - ix, 1 - iy)]
        writers = [sibling] + [(*chip, ic) for chip in chips]
        barrier = pltpu.get_barrier_semaphore()
        for peer in writers:
            pl.semaphore_signal(barrier, inc=1, device_id=peer, device_id_type=MESH)
        pl.semaphore_wait(barrier, len(writers))

        def rows(px, py, pc):
            return out_ref.at[4 * px + 2 * py + pc]

        def copy(k, block, to, src=None):
            return pltpu.make_async_remote_copy(
                src_ref=rows(*block) if src is None else src, dst_ref=rows(*block),
                send_sem=send_sems.at[k], recv_sem=recv_sems.at[k], device_id=to, device_id_type=MESH)

        mine = pltpu.make_async_copy(x_ref, rows(*me), loc_sem)
        mine.start()
        first = [copy(0, me, sibling, src=x_ref)]
        first += [copy(1 + j, me, (*chip, ic), src=x_ref) for j, chip in enumerate(chips)]
        for cp in first:
            cp.start()
        passed = [copy(4 + j, (*chip, ic), sibling) for j, chip in enumerate(chips)]
        for j, chip in enumerate(chips):
            copy(1 + j, (*chip, ic), me).wait_recv()
            passed[j].start()
        copy(0, sibling, me).wait_recv()
        for j, chip in enumerate(chips):
            copy(4 + j, (*chip, 1 - ic), me).wait_recv()
        for cp in first + passed:
            cp.wait_send()
        mine.wait()

    launch()
    return out_ref[...]


def _sum_slots(x, *, name):
    n, R, C = x.shape
    tr = _tile(R, 256, 8)

    def body(x_ref, o_ref):
        acc = x_ref[0].astype(F32)
        for d in range(1, n):
            acc = acc + x_ref[d].astype(F32)
        o_ref[...] = acc

    return pl.pallas_call(
        body, name=name, grid=(R // tr,),
        out_shape=jax.ShapeDtypeStruct((R, C), F32),
        in_specs=[pl.BlockSpec((n, tr, C), lambda i: (0, i, 0))],
        out_specs=pl.BlockSpec((tr, C), lambda i: (i, 0)),
        compiler_params=_params(("parallel",)),
    )(x)


def _colsum(x, *, name):
    R, C = x.shape

    def body(x_ref, o_ref):
        o_ref[...] = jnp.sum(x_ref[...], axis=0, keepdims=True)

    return pl.pallas_call(
        body, name=name, out_shape=jax.ShapeDtypeStruct((1, C), F32),
        in_specs=[pl.BlockSpec((R, C), lambda: (0, 0))], out_specs=pl.BlockSpec((1, C), lambda: (0, 0)),
        compiler_params=_params(),
    )(x)


def _adamw(parts, w, m, v, *, name):
    n, R, C = parts.shape
    tr = _tile(R, 128, 8)
    c1 = 1.0 - ADAM_B1 ** ADAM_STEP
    c2 = 1.0 - ADAM_B2 ** ADAM_STEP

    def body(p_ref, w_ref, m_ref, v_ref, g_ref, d_ref, nm_ref, nv_ref):
        g = p_ref[0].astype(F32)
        for d in range(1, n):
            g = g + p_ref[d].astype(F32)
        mn = ADAM_B1 * m_ref[...] + (1.0 - ADAM_B1) * g
        vn = ADAM_B2 * v_ref[...] + (1.0 - ADAM_B2) * (g * g)
        g_ref[...] = g
        nm_ref[...] = mn
        nv_ref[...] = vn
        d_ref[...] = -ADAM_LR * ((mn / c1) / (jnp.sqrt(vn / c2) + ADAM_EPS) + ADAM_WD * w_ref[...])

    spec = pl.BlockSpec((tr, C), lambda i: (i, 0))
    shp = jax.ShapeDtypeStruct((R, C), F32)
    return pl.pallas_call(
        body, name=name, grid=(R // tr,), out_shape=(shp, shp, shp, shp),
        in_specs=[pl.BlockSpec((n, tr, C), lambda i: (0, i, 0)), spec, spec, spec],
        out_specs=(spec, spec, spec, spec),
        compiler_params=_params(("parallel",)),
    )(parts, w, m, v)


MM_VMEM_BUDGET = 40 * 1024 * 1024
MM_TK_MAX = 2816
MXU_WIDTH = 256


def _divisors(n, quantum, cap):
    return [t for t in range(quantum, min(n, cap) + 1, quantum) if n % t == 0] or [n]


def _mm_tiles(M, N, K, mode, a_bytes, b_bytes, o_bytes, has_add):
    tk = max(_divisors(K, 128, MM_TK_MAX))
    nk = K // tk
    best = None
    for tm in _divisors(M, 128 if mode == "tn" else 8, 1024):
        for tn in _divisors(N, 128, 3072):
            need = 2 * (tm * tk * a_bytes + tk * tn * b_bytes) + 2 * tm * tn * o_bytes + tm * tn * 4
            need += tm * tn * 4 if nk > 1 else 0
            need += 2 * tm * tn * 4 if has_add else 0
            if need > MM_VMEM_BUDGET:
                continue
            score = (tn % MXU_WIDTH == 0 or tn == N, tm * tn, tm)
            if best is None or score > best[0]:
                best = (score, tm, tn)
    assert best is not None, (M, N, K)
    return best[1], best[2], tk


def _mm(a, b, mode, *, name, out_dtype, add=None, after=None):
    if mode == "nn":
        (M, K), (K2, N) = a.shape, b.shape
    elif mode == "nt":
        (M, K), (N, K2) = a.shape, b.shape
    else:
        (K, M), (K2, N) = a.shape, b.shape
    assert K == K2, (name, a.shape, b.shape)
    tm, tn, tk = _mm_tiles(M, N, K, mode, a.dtype.itemsize, b.dtype.itemsize, jnp.dtype(out_dtype).itemsize,
                           add is not None)
    nk = K // tk
    dims = {"nn": NN, "nt": NT, "tn": TN}[mode]

    a_spec = {"nn": pl.BlockSpec((tm, tk), lambda i, j, k: (i, k)),
              "nt": pl.BlockSpec((tm, tk), lambda i, j, k: (i, k)),
              "tn": pl.BlockSpec((tk, tm), lambda i, j, k: (k, i))}[mode]
    b_spec = {"nn": pl.BlockSpec((tk, tn), lambda i, j, k: (k, j)),
              "nt": pl.BlockSpec((tn, tk), lambda i, j, k: (j, k)),
              "tn": pl.BlockSpec((tk, tn), lambda i, j, k: (k, j))}[mode]
    o_spec = pl.BlockSpec((tm, tn), lambda i, j, k: (i, j))

    def body(a_ref, b_ref, *rest):
        rest = list(rest)
        add_ref = rest.pop(0) if add is not None else None
        if after is not None:
            rest.pop(0)
        o_ref = rest.pop(0)
        part = lax.dot_general(a_ref[...].astype(ACT), b_ref[...].astype(ACT), dims, preferred_element_type=F32)

        def finish(r):
            if add is not None:
                r = r + add_ref[...].astype(F32)
            o_ref[...] = r.astype(out_dtype)

        if nk == 1:
            finish(part)
            return
        acc = rest.pop(0)
        k = pl.program_id(2)

        @pl.when(k == 0)
        def _():
            acc[...] = part

        @pl.when(jnp.logical_and(k > 0, k < nk - 1))
        def _():
            acc[...] += part

        @pl.when(k == nk - 1)
        def _():
            finish(acc[...] + part)

    operands = [a, b] + ([] if add is None else [add])
    in_specs = [a_spec, b_spec] + ([] if add is None else [o_spec])
    if after is not None:
        operands.append(after)
        in_specs.append(ANY_SPEC)
    return pl.pallas_call(
        body, name=name, grid=(M // tm, N // tn, nk),
        out_shape=jax.ShapeDtypeStruct((M, N), out_dtype),
        in_specs=in_specs, out_specs=o_spec,
        scratch_shapes=[pltpu.VMEM((tm, tn), F32)] if nk > 1 else [],
        compiler_params=_params(("parallel", "parallel", "arbitrary")),
    )(*operands)


def _row(tr, cols, blk=0):
    return pl.BlockSpec((tr, cols), lambda i: (i, blk))


def _vec(cols):
    return pl.BlockSpec((1, cols), lambda i: (0, 0))


def _rms(xf):
    return lax.rsqrt(jnp.mean(xf * xf, axis=-1, keepdims=True) + EPS)


def _rms_bwd(dxhat, xhat, r):
    return r * (dxhat - xhat * jnp.mean(dxhat * xhat, axis=-1, keepdims=True))


def _acc_rows(ref, val, first):
    s = jnp.sum(val, axis=0, keepdims=True)

    @pl.when(first)
    def _():
        ref[...] = s

    @pl.when(jnp.logical_not(first))
    def _():
        ref[...] += s


def _norm_mod_fwd(x, nw, shift, scale, *, name):
    T, D = x.shape
    tr = _tile(T, 256, 8)

    def body(x_ref, nw_ref, sh_ref, sc_ref, o_ref):
        xf = x_ref[...]
        n = xf * _rms(xf) * nw_ref[...]
        o_ref[...] = (n * (1.0 + sc_ref[...]) + sh_ref[...]).astype(ACT)

    return pl.pallas_call(
        body, name=name, grid=(T // tr,), out_shape=jax.ShapeDtypeStruct((T, D), ACT),
        in_specs=[_row(tr, D), _vec(D), _vec(D), _vec(D)], out_specs=_row(tr, D),
        compiler_params=_params(("parallel",)),
    )(x, nw, shift, scale)


def _resid_norm_mod_fwd(x, o, g, nw, shift, scale, *, name):
    T, D = x.shape
    tr = _tile(T, 256, 8)

    def body(x_ref, o_ref, g_ref, nw_ref, sh_ref, sc_ref, x1_ref, h_ref):
        x1 = x_ref[...] + g_ref[...] * o_ref[...].astype(F32)
        x1_ref[...] = x1
        n = x1 * _rms(x1) * nw_ref[...]
        h_ref[...] = (n * (1.0 + sc_ref[...]) + sh_ref[...]).astype(ACT)

    return pl.pallas_call(
        body, name=name, grid=(T // tr,),
        out_shape=(jax.ShapeDtypeStruct((T, D), F32), jax.ShapeDtypeStruct((T, D), ACT)),
        in_specs=[_row(tr, D), _row(tr, D), _vec(D), _vec(D), _vec(D), _vec(D)],
        out_specs=(_row(tr, D), _row(tr, D)),
        compiler_params=_params(("parallel",)),
    )(x, o, g, nw, shift, scale)


def _final_fwd_bwd(x1, dn, g2, nw, target, *, name):
    T, D = x1.shape
    tr = _tile(T, 256, 8)

    def body(x1_ref, dn_ref, g_ref, nw_ref, t_ref, loss_ref, dx_ref, ddn_ref, dnw_ref, dg_ref):
        first = pl.program_id(0) == 0
        dn_f = dn_ref[...].astype(F32)
        x2 = x1_ref[...] + g_ref[...] * dn_f
        r = _rms(x2)
        xhat = x2 * r
        err = xhat * nw_ref[...] - t_ref[...]
        part = 0.5 * jnp.sum(jnp.mean(err * err, axis=-1, keepdims=True), axis=0, keepdims=True)

        @pl.when(first)
        def _():
            loss_ref[...] = part

        @pl.when(jnp.logical_not(first))
        def _():
            loss_ref[...] += part

        dy = err * (1.0 / D)
        _acc_rows(dnw_ref, dy * xhat, first)
        dx2 = _rms_bwd(dy * nw_ref[...], xhat, r)
        dx_ref[...] = dx2
        ddn_ref[...] = (g_ref[...] * dx2).astype(ACT)
        _acc_rows(dg_ref, dx2 * dn_f, first)

    vec = jax.ShapeDtypeStruct((1, D), F32)
    return pl.pallas_call(
        body, name=name, grid=(T // tr,),
        out_shape=(jax.ShapeDtypeStruct((1, 1), F32), jax.ShapeDtypeStruct((T, D), F32),
                   jax.ShapeDtypeStruct((T, D), ACT), vec, vec),
        in_specs=[_row(tr, D), _row(tr, D), _vec(D), _vec(D), _row(tr, D)],
        out_specs=(pl.BlockSpec((1, 1), lambda i: (0, 0)), _row(tr, D), _row(tr, D), _vec(D), _vec(D)),
        compiler_params=_params(("arbitrary",)),
    )(x1, dn, g2, nw, target)


def _norm_mod_bwd(xin, nw, scale, dh, *, name, dres=None, o=None, g=None):
    T, D = xin.shape
    tr = _tile(T, 256, 8)
    has_res, has_o = dres is not None, o is not None

    def body(*refs):
        refs = list(refs)
        x_ref, nw_ref, sc_ref, dh_ref = refs[:4]
        pos = 4
        dres_ref = o_ref = g_ref = None
        if has_res:
            dres_ref = refs[pos]
            pos += 1
        if has_o:
            o_ref, g_ref = refs[pos], refs[pos + 1]
            pos += 2
        dx_ref, dsh_ref, dsc_ref, dnw_ref = refs[pos:pos + 4]
        pos += 4
        first = pl.program_id(0) == 0
        xf = x_ref[...]
        r = _rms(xf)
        xhat = xf * r
        n = xhat * nw_ref[...]
        dhf = dh_ref[...].astype(F32)
        _acc_rows(dsh_ref, dhf, first)
        _acc_rows(dsc_ref, dhf * n, first)
        dn = dhf * (1.0 + sc_ref[...])
        _acc_rows(dnw_ref, dn * xhat, first)
        dx = _rms_bwd(dn * nw_ref[...], xhat, r)
        if has_res:
            dx = dx + dres_ref[...]
        dx_ref[...] = dx
        if has_o:
            do_ref, dg_ref = refs[pos], refs[pos + 1]
            do_ref[...] = (g_ref[...] * dx).astype(ACT)
            _acc_rows(dg_ref, dx * o_ref[...].astype(F32), first)

    vec = jax.ShapeDtypeStruct((1, D), F32)
    operands = [xin, nw, scale, dh]
    in_specs = [_row(tr, D), _vec(D), _vec(D), _row(tr, D)]
    if has_res:
        operands.append(dres)
        in_specs.append(_row(tr, D))
    if has_o:
        operands += [o, g]
        in_specs += [_row(tr, D), _vec(D)]
    out_shape = [jax.ShapeDtypeStruct((T, D), F32), vec, vec, vec]
    out_specs = [_row(tr, D), _vec(D), _vec(D), _vec(D)]
    if has_o:
        out_shape += [jax.ShapeDtypeStruct((T, D), ACT), vec]
        out_specs += [_row(tr, D), _vec(D)]
    return pl.pallas_call(
        body, name=name, grid=(T // tr,), out_shape=tuple(out_shape),
        in_specs=in_specs, out_specs=tuple(out_specs),
        compiler_params=_params(("arbitrary",)),
    )(*operands)


def _swiglu_fwd(gu, *, name):
    T, F = gu.shape[0], gu.shape[1] // 2
    tr = _tile(T, 256, 8)

    def body(g_ref, u_ref, o_ref):
        o_ref[...] = (_silu(g_ref[...].astype(F32)) * u_ref[...].astype(F32)).astype(ACT)

    return pl.pallas_call(
        body, name=name, grid=(T // tr,), out_shape=jax.ShapeDtypeStruct((T, F), ACT),
        in_specs=[_row(tr, F, 0), _row(tr, F, 1)], out_specs=_row(tr, F),
        compiler_params=_params(("parallel",)),
    )(gu, gu)


def _swiglu_bwd(gu, dact, *, name):
    T, F = gu.shape[0], gu.shape[1] // 2
    tr = _tile(T, 256, 8)

    def body(g_ref, u_ref, d_ref, o_ref):
        gf, uf, df = g_ref[...].astype(F32), u_ref[...].astype(F32), d_ref[...].astype(F32)
        o_ref[:, :F] = (df * uf * _dsilu(gf)).astype(ACT)
        o_ref[:, F:] = (df * _silu(gf)).astype(ACT)

    return pl.pallas_call(
        body, name=name, grid=(T // tr,), out_shape=jax.ShapeDtypeStruct((T, 2 * F), ACT),
        in_specs=[_row(tr, F, 0), _row(tr, F, 1), _row(tr, F)], out_specs=_row(tr, 2 * F),
        compiler_params=_params(("parallel",)),
    )(gu, gu, dact)


def _merge_fwd(ya, yb, gates, *, name):
    T, D = ya.shape
    tr = _tile(T, 256, 8)

    def body(a_ref, b_ref, g_ref, o_ref):
        ga = g_ref[:, :D].astype(F32)
        gb = g_ref[:, D:].astype(F32)
        o_ref[...] = (jax.nn.sigmoid(ga) * a_ref[...].astype(F32)
                      + jax.nn.sigmoid(gb) * b_ref[...].astype(F32)).astype(ACT)

    return pl.pallas_call(
        body, name=name, grid=(T // tr,), out_shape=jax.ShapeDtypeStruct((T, D), ACT),
        in_specs=[_row(tr, D), _row(tr, D), _row(tr, 2 * D)], out_specs=_row(tr, D),
        compiler_params=_params(("parallel",)),
    )(ya, yb, gates)


def _merge_bwd(dmer, ya, yb, gates, *, name):
    T, D = ya.shape
    tr = _tile(T, 256, 8)

    def body(d_ref, a_ref, b_ref, g_ref, da_ref, db_ref, dg_ref):
        d = d_ref[...].astype(F32)
        sa = jax.nn.sigmoid(g_ref[:, :D].astype(F32))
        sb = jax.nn.sigmoid(g_ref[:, D:].astype(F32))
        da_ref[...] = (d * sa).astype(ACT)
        db_ref[...] = (d * sb).astype(ACT)
        dg_ref[:, :D] = (d * a_ref[...].astype(F32) * sa * (1.0 - sa)).astype(ACT)
        dg_ref[:, D:] = (d * b_ref[...].astype(F32) * sb * (1.0 - sb)).astype(ACT)

    shp = jax.ShapeDtypeStruct((T, D), ACT)
    return pl.pallas_call(
        body, name=name, grid=(T // tr,), out_shape=(shp, shp, jax.ShapeDtypeStruct((T, 2 * D), ACT)),
        in_specs=[_row(tr, D), _row(tr, D), _row(tr, D), _row(tr, 2 * D)],
        out_specs=(_row(tr, D), _row(tr, D), _row(tr, 2 * D)),
        compiler_params=_params(("parallel",)),
    )(dmer, ya, yb, gates)


def _gate_norm_fwd(yf, yb, xbc, z, dskip, nw, *, name):
    T, DI = z.shape
    tr = _tile(T, 128, 8)

    def body(yf_ref, yb_ref, xs_ref, z_ref, ds_ref, nw_ref, o_ref):
        y = yf_ref[...].astype(F32) + yb_ref[...].astype(F32) + ds_ref[...] * xs_ref[...].astype(F32)
        gz = y * _silu(z_ref[...].astype(F32))
        o_ref[...] = (gz * _rms(gz) * nw_ref[...]).astype(ACT)

    return pl.pallas_call(
        body, name=name, grid=(T // tr,), out_shape=jax.ShapeDtypeStruct((T, DI), ACT),
        in_specs=[_row(tr, DI), _row(tr, DI), _row(tr, DI), _row(tr, DI), _vec(DI), _vec(DI)],
        out_specs=_row(tr, DI),
        compiler_params=_params(("parallel",)),
    )(yf, yb, xbc, z, dskip, nw)


def _gate_norm_bwd(dout, yf, yb, xbc, z, dskip, nw, *, name):
    T, DI = z.shape
    tr = _tile(T, 128, 8)

    def body(do_ref, yf_ref, yb_ref, xs_ref, z_ref, ds_ref, nw_ref, dy_ref, dz_ref, dxs_ref, dnw_ref, dds_ref):
        first = pl.program_id(0) == 0
        xs = xs_ref[...].astype(F32)
        zf = z_ref[...].astype(F32)
        y = yf_ref[...].astype(F32) + yb_ref[...].astype(F32) + ds_ref[...] * xs
        sz = _silu(zf)
        gz = y * sz
        r = _rms(gz)
        ghat = gz * r
        do = do_ref[...].astype(F32)
        _acc_rows(dnw_ref, do * ghat, first)
        dgz = _rms_bwd(do * nw_ref[...], ghat, r)
        dy = dgz * sz
        dy_ref[...] = dy.astype(ACT)
        dz_ref[...] = (dgz * y * _dsilu(zf)).astype(ACT)
        dxs_ref[...] = (dy * ds_ref[...]).astype(ACT)
        _acc_rows(dds_ref, dy * xs, first)

    shp = jax.ShapeDtypeStruct((T, DI), ACT)
    vec = jax.ShapeDtypeStruct((1, DI), F32)
    return pl.pallas_call(
        body, name=name, grid=(T // tr,), out_shape=(shp, shp, shp, vec, vec),
        in_specs=[_row(tr, DI)] * 5 + [_vec(DI), _vec(DI)],
        out_specs=(_row(tr, DI), _row(tr, DI), _row(tr, DI), _vec(DI), _vec(DI)),
        compiler_params=_params(("arbitrary",)),
    )(dout, yf, yb, xbc, z, dskip, nw)


def _ln_silu_fwd(x, g, b, *, name):
    T, D = x.shape
    tr = _tile(T, 256, 8)

    def body(x_ref, g_ref, b_ref, o_ref):
        xf = x_ref[...].astype(F32)
        xc = xf - jnp.mean(xf, axis=-1, keepdims=True)
        rstd = lax.rsqrt(jnp.mean(xc * xc, axis=-1, keepdims=True) + EPS)
        o_ref[...] = _silu(xc * rstd * g_ref[...] + b_ref[...]).astype(ACT)

    return pl.pallas_call(
        body, name=name, grid=(T // tr,), out_shape=jax.ShapeDtypeStruct((T, D), ACT),
        in_specs=[_row(tr, D), _vec(D), _vec(D)], out_specs=_row(tr, D),
        compiler_params=_params(("parallel",)),
    )(x, g, b)


def _ln_silu_bwd(x, g, b, dcf, *, name):
    T, D = x.shape
    tr = _tile(T, 256, 8)

    def body(x_ref, g_ref, b_ref, d_ref, dx_ref, dg_ref, db_ref):
        first = pl.program_id(0) == 0
        xf = x_ref[...].astype(F32)
        xc = xf - jnp.mean(xf, axis=-1, keepdims=True)
        rstd = lax.rsqrt(jnp.mean(xc * xc, axis=-1, keepdims=True) + EPS)
        xhat = xc * rstd
        dyln = d_ref[...].astype(F32) * _dsilu(xhat * g_ref[...] + b_ref[...])
        _acc_rows(dg_ref, dyln * xhat, first)
        _acc_rows(db_ref, dyln, first)
        dxh = dyln * g_ref[...]
        dx = rstd * (dxh - jnp.mean(dxh, axis=-1, keepdims=True)
                     - xhat * jnp.mean(dxh * xhat, axis=-1, keepdims=True))
        dx_ref[...] = dx.astype(ACT)

    vec = jax.ShapeDtypeStruct((1, D), F32)
    return pl.pallas_call(
        body, name=name, grid=(T // tr,), out_shape=(jax.ShapeDtypeStruct((T, D), ACT), vec, vec),
        in_specs=[_row(tr, D), _vec(D), _vec(D), _row(tr, D)],
        out_specs=(_row(tr, D), _vec(D), _vec(D)),
        compiler_params=_params(("arbitrary",)),
    )(x, g, b, dcf)


CONV_CW = 128
CONV_RT = 256
SEQ_PAD = 8


def _window(ext, off, n):
    if off % 8 == 0:
        return ext[off:off + n]
    return pltpu.roll(ext, ext.shape[0] - off, 0)[:n]


def _sum8(v):
    R, C = v.shape
    return jnp.sum(v.reshape(R // 8, 8, C), axis=0)


def _conv5_silu_fwd(x, w, b, *, name):
    T, C = x.shape
    K = 5
    cw, rt = CONV_CW, _tile(T, CONV_RT, 8)
    half = K // 2

    def body(x_ref, w_ref, b_ref, o_ref, pad):
        zeros = jnp.zeros((SEQ_PAD, cw), F32)
        pad[0:SEQ_PAD, :] = zeros
        pad[T + SEQ_PAD:T + 2 * SEQ_PAD, :] = zeros

        def fill(i, c):
            base = pl.multiple_of(i * rt, rt)
            pad[pl.ds(base + SEQ_PAD, rt), :] = x_ref[pl.ds(base, rt), :].astype(F32)
            return c

        lax.fori_loop(0, T // rt, fill, 0)
        wv = w_ref[...]
        bias = b_ref[...]

        def step(i, c):
            base = pl.multiple_of(i * rt, rt)
            ext = pad[pl.ds(base, rt + 2 * SEQ_PAD), :]
            acc = jnp.zeros((rt, cw), F32) + bias
            for k in range(K):
                acc = acc + wv[k:k + 1, :] * _window(ext, SEQ_PAD + k - half, rt)
            o_ref[pl.ds(base, rt), :] = _silu(acc).astype(ACT)
            return c

        lax.fori_loop(0, T // rt, step, 0)

    return pl.pallas_call(
        body, name=name, grid=(C // cw,), out_shape=jax.ShapeDtypeStruct((T, C), ACT),
        in_specs=[pl.BlockSpec((T, cw), lambda j: (0, j)), pl.BlockSpec((8, cw), lambda j: (0, j)),
                  pl.BlockSpec((1, cw), lambda j: (0, j))],
        out_specs=pl.BlockSpec((T, cw), lambda j: (0, j)),
        scratch_shapes=[pltpu.VMEM((T + 2 * SEQ_PAD, cw), F32)],
        compiler_params=_params(("parallel",)),
    )(x, w, b)


def _conv5_silu_bwd(x, w, b, douts, *, name):
    T, C = x.shape
    K = 5
    cw, rt = CONV_CW, _tile(T, CONV_RT, 8)
    half = K // 2
    tiles = [d.shape[1] // cw for d in douts]
    firsts = [sum(tiles[:i]) for i in range(len(douts))]
    assert sum(tiles) * cw == C

    def body(x_ref, w_ref, b_ref, *rest):
        d_refs = rest[:len(douts)]
        dx_ref, dw_ref, db_ref, pad, dpad, wacc = rest[len(douts):]
        tile = pl.program_id(0)

        def dout_rows(base):
            d = d_refs[-1][pl.ds(base, rt), :]
            for i in range(len(douts) - 2, -1, -1):
                d = jnp.where(tile < firsts[i + 1], d_refs[i][pl.ds(base, rt), :], d)
            return d

        zeros = jnp.zeros((SEQ_PAD, cw), F32)
        for p in (pad, dpad):
            p[0:SEQ_PAD, :] = zeros
            p[T + SEQ_PAD:T + 2 * SEQ_PAD, :] = zeros
        wacc[...] = jnp.zeros_like(wacc)

        def fill(i, c):
            base = pl.multiple_of(i * rt, rt)
            pad[pl.ds(base + SEQ_PAD, rt), :] = x_ref[pl.ds(base, rt), :].astype(F32)
            return c

        lax.fori_loop(0, T // rt, fill, 0)
        wv = w_ref[...]
        bias = b_ref[...]

        def step1(i, c):
            base = pl.multiple_of(i * rt, rt)
            ext = pad[pl.ds(base, rt + 2 * SEQ_PAD), :]
            wins = [_window(ext, SEQ_PAD + k - half, rt) for k in range(K)]
            pre = jnp.zeros((rt, cw), F32) + bias
            for k in range(K):
                pre = pre + wv[k:k + 1, :] * wins[k]
            dpre = dout_rows(base).astype(F32) * _dsilu(pre)
            dpad[pl.ds(base + SEQ_PAD, rt), :] = dpre
            for k in range(K):
                wacc[k] += _sum8(dpre * wins[k])
            wacc[K] += _sum8(dpre)
            return c

        lax.fori_loop(0, T // rt, step1, 0)

        def step2(i, c):
            base = pl.multiple_of(i * rt, rt)
            ext = dpad[pl.ds(base, rt + 2 * SEQ_PAD), :]
            acc = jnp.zeros((rt, cw), F32)
            for k in range(K):
                acc = acc + wv[k:k + 1, :] * _window(ext, SEQ_PAD - (k - half), rt)
            dx_ref[pl.ds(base, rt), :] = acc.astype(ACT)
            return c

        lax.fori_loop(0, T // rt, step2, 0)
        rows = [jnp.sum(wacc[k], axis=0, keepdims=True) for k in range(K)]
        rows += [jnp.zeros((1, cw), F32)] * (8 - K)
        dw_ref[...] = jnp.concatenate(rows, axis=0)
        db_ref[...] = jnp.sum(wacc[K], axis=0, keepdims=True)

    return pl.pallas_call(
        body, name=name, grid=(C // cw,),
        out_shape=(jax.ShapeDtypeStruct((T, C), ACT), jax.ShapeDtypeStruct((8, C), F32),
                   jax.ShapeDtypeStruct((1, C), F32)),
        in_specs=[pl.BlockSpec((T, cw), lambda j: (0, j)), pl.BlockSpec((8, cw), lambda j: (0, j)),
                  pl.BlockSpec((1, cw), lambda j: (0, j))]
        + [pl.BlockSpec((T, cw), functools.partial(lambda j, first, n: (0, jnp.clip(j - first, 0, n - 1)),
                                                   first=firsts[i], n=tiles[i])) for i in range(len(douts))],
        out_specs=(pl.BlockSpec((T, cw), lambda j: (0, j)), pl.BlockSpec((8, cw), lambda j: (0, j)),
                   pl.BlockSpec((1, cw), lambda j: (0, j))),
        scratch_shapes=[pltpu.VMEM((T + 2 * SEQ_PAD, cw), F32), pltpu.VMEM((T + 2 * SEQ_PAD, cw), F32),
                        pltpu.VMEM((K + 1, 8, cw), F32)],
        compiler_params=_params(("parallel",)),
    )(x, w, b, *douts)


def _glu_conv_fwd(u, v, w, b, *, name):
    T, C = u.shape
    K = 31
    KP = w.shape[0]
    cw, rt = CONV_CW, _tile(T, CONV_RT, GRID_W)
    half = K // 2
    P = half * GRID_W

    def body(u_ref, v_ref, w_ref, b_ref, o_ref, pad):
        zeros = jnp.zeros((P, cw), F32)
        pad[0:P, :] = zeros
        pad[T + P:T + 2 * P, :] = zeros

        def fill(i, c):
            base = pl.multiple_of(i * rt, rt)
            uf = u_ref[pl.ds(base, rt), :].astype(F32)
            vf = v_ref[pl.ds(base, rt), :].astype(F32)
            pad[pl.ds(base + P, rt), :] = uf * jax.nn.sigmoid(vf)
            return c

        lax.fori_loop(0, T // rt, fill, 0)
        wv = w_ref[...]
        bias = b_ref[...]

        def step(i, c):
            base = pl.multiple_of(i * rt, rt)
            acc = jnp.zeros((rt, cw), F32) + bias
            for k in range(K):
                acc = acc + wv[k:k + 1, :] * pad[pl.ds(base + k * GRID_W, rt), :]
            o_ref[pl.ds(base, rt), :] = acc.astype(ACT)
            return c

        lax.fori_loop(0, T // rt, step, 0)

    col = pl.BlockSpec((T, cw), lambda j: (0, j))
    return pl.pallas_call(
        body, name=name, grid=(C // cw,), out_shape=jax.ShapeDtypeStruct((T, C), ACT),
        in_specs=[col, col, pl.BlockSpec((KP, cw), lambda j: (0, j)), pl.BlockSpec((1, cw), lambda j: (0, j))],
        out_specs=col,
        scratch_shapes=[pltpu.VMEM((T + 2 * P, cw), F32)],
        compiler_params=_params(("parallel",)),
    )(u, v, w, b)


def _glu_conv_bwd(u, v, w, dout, *, name):
    T, C = u.shape
    K = 31
    KP = w.shape[0]
    cw, rt = CONV_CW, _tile(T, CONV_RT, GRID_W)
    half = K // 2
    P = half * GRID_W

    def body(u_ref, v_ref, w_ref, d_ref, du_ref, dv_ref, dw_ref, db_ref, pad, dpad, wacc):
        zeros = jnp.zeros((P, cw), F32)
        for p in (pad, dpad):
            p[0:P, :] = zeros
            p[T + P:T + 2 * P, :] = zeros
        wacc[...] = jnp.zeros_like(wacc)

        def fill(i, c):
            base = pl.multiple_of(i * rt, rt)
            uf = u_ref[pl.ds(base, rt), :].astype(F32)
            vf = v_ref[pl.ds(base, rt), :].astype(F32)
            pad[pl.ds(base + P, rt), :] = uf * jax.nn.sigmoid(vf)
            dpad[pl.ds(base + P, rt), :] = d_ref[pl.ds(base, rt), :].astype(F32)
            return c

        lax.fori_loop(0, T // rt, fill, 0)
        wv = w_ref[...]

        def step(i, c):
            base = pl.multiple_of(i * rt, rt)
            d = dpad[pl.ds(base + P, rt), :]
            dg = jnp.zeros((rt, cw), F32)
            for k in range(K):
                wacc[k] += _sum8(d * pad[pl.ds(base + k * GRID_W, rt), :])
                dg = dg + wv[k:k + 1, :] * dpad[pl.ds(base + (K - 1 - k) * GRID_W, rt), :]
            wacc[K] += _sum8(d)
            uf = u_ref[pl.ds(base, rt), :].astype(F32)
            sv = jax.nn.sigmoid(v_ref[pl.ds(base, rt), :].astype(F32))
            du_ref[pl.ds(base, rt), :] = (dg * sv).astype(ACT)
            dv_ref[pl.ds(base, rt), :] = (dg * uf * sv * (1.0 - sv)).astype(ACT)
            return c

        lax.fori_loop(0, T // rt, step, 0)
        rows = [jnp.sum(wacc[k], axis=0, keepdims=True) for k in range(K)]
        rows += [jnp.zeros((1, cw), F32)] * (KP - K)
        dw_ref[...] = jnp.concatenate(rows, axis=0)
        db_ref[...] = jnp.sum(wacc[K], axis=0, keepdims=True)

    col = pl.BlockSpec((T, cw), lambda j: (0, j))
    shp = jax.ShapeDtypeStruct((T, C), ACT)
    return pl.pallas_call(
        body, name=name, grid=(C // cw,),
        out_shape=(shp, shp, jax.ShapeDtypeStruct((KP, C), F32), jax.ShapeDtypeStruct((1, C), F32)),
        in_specs=[col, col, pl.BlockSpec((KP, cw), lambda j: (0, j)), col],
        out_specs=(col, col, pl.BlockSpec((KP, cw), lambda j: (0, j)), pl.BlockSpec((1, cw), lambda j: (0, j))),
        scratch_shapes=[pltpu.VMEM((T + 2 * P, cw), F32), pltpu.VMEM((T + 2 * P, cw), F32),
                        pltpu.VMEM((K + 1, 8, cw), F32)],
        compiler_params=_params(("parallel",)),
    )(u, v, w, dout)


def _dt_fwd(rawT, bias, *, name):
    H2, T = rawT.shape
    tc = _tile(T, 2048, 128)

    def body(r_ref, b_ref, o_ref):
        v = r_ref[...] + b_ref[...]
        o_ref[...] = jnp.maximum(v, 0.0) + jnp.log(1.0 + jnp.exp(-jnp.abs(v)))

    return pl.pallas_call(
        body, name=name, grid=(T // tc,), out_shape=jax.ShapeDtypeStruct((H2, T), F32),
        in_specs=[pl.BlockSpec((H2, tc), lambda i: (0, i)), pl.BlockSpec((H2, 1), lambda i: (0, 0))],
        out_specs=pl.BlockSpec((H2, tc), lambda i: (0, i)),
        compiler_params=_params(("parallel",)),
    )(rawT, bias)


def _dt_bwd(rawT, bias, ddtT, *, name):
    H2, T = rawT.shape
    tc = _tile(T, 2048, 128)

    def body(r_ref, b_ref, d_ref, o_ref, db_ref):
        first = pl.program_id(0) == 0
        dr = d_ref[...] * jax.nn.sigmoid(r_ref[...] + b_ref[...])
        o_ref[...] = dr
        s = jnp.sum(dr, axis=1, keepdims=True)

        @pl.when(first)
        def _():
            db_ref[...] = s

        @pl.when(jnp.logical_not(first))
        def _():
            db_ref[...] += s

    return pl.pallas_call(
        body, name=name, grid=(T // tc,),
        out_shape=(jax.ShapeDtypeStruct((H2, T), F32), jax.ShapeDtypeStruct((H2, 1), F32)),
        in_specs=[pl.BlockSpec((H2, tc), lambda i: (0, i)), pl.BlockSpec((H2, 1), lambda i: (0, 0)),
                  pl.BlockSpec((H2, tc), lambda i: (0, i))],
        out_specs=(pl.BlockSpec((H2, tc), lambda i: (0, i)), pl.BlockSpec((H2, 1), lambda i: (0, 0))),
        compiler_params=_params(("arbitrary",)),
    )(rawT, bias, ddtT)


def _ssd_common(dtT, a, reverse):
    J, Q = dtT.shape
    li = lax.broadcasted_iota(jnp.int32, (Q, Q), 0)
    si = lax.broadcasted_iota(jnp.int32, (Q, Q), 1)
    mask = (si >= li) if reverse else (si <= li)
    Mf = mask.astype(F32)
    daT = dtT * a
    csT = lax.dot_general(daT, Mf, NT, precision=HIGHEST, preferred_element_type=F32)
    last = 0 if reverse else Q - 1
    totT = csT[:, last:last + 1]
    return mask, Mf, csT, totT, last


def _to_cols(rows):
    R, Q = rows.shape
    if R < 128:
        rows = jnp.concatenate([rows, jnp.zeros((128 - R, Q), F32)], axis=0)
    return rows.T


def _ssd_fwd(xbc, dtT, a, h0, *, reverse, name, di):
    T = xbc.shape[0]
    G, JP, N = h0.shape
    J, P, Q = JP // HEAD_DIM, HEAD_DIM, CHUNK
    nc = T // Q
    QW = 256
    HQ = QW // P

    def ci(k):
        return nc - 1 - k if reverse else k

    GB = SSD_GROUPS_PER_STEP

    def body(x_ref, b_ref, c_ref, dt_ref, a_ref, h0_ref, y_ref, hs_ref, hl_ref, h_scr):
        k = pl.program_id(1)

        @pl.when(k == 0)
        def _():
            h_scr[...] = h0_ref[...]

        lh = lax.broadcasted_iota(jnp.int32, (Q, QW), 1) // P

        def scale_heads(vT, rowsT):
            return jnp.concatenate([vT[j * P:(j + 1) * P, :] * rowsT[j:j + 1, :] for j in range(J)], axis=0)

        for gi in range(GB):
            h = h_scr[gi]
            hs_ref[0, gi] = h
            Xb = x_ref[:, gi * JP:(gi + 1) * JP]
            Bm, Cm = b_ref[:, gi * N:(gi + 1) * N], c_ref[:, gi * N:(gi + 1) * N]
            dtT_v = dt_ref[gi * J:(gi + 1) * J, :]
            mask, _, csT, totT, _ = _ssd_common(dtT_v, a_ref[gi * J:(gi + 1) * J, :], reverse)
            cs = _to_cols(csT)
            CB = lax.dot_general(Cm, Bm, NT, preferred_element_type=F32)
            yoT = lax.dot_general(h.astype(ACT), Cm, NT, preferred_element_type=F32)
            yo = scale_heads(yoT, jnp.exp(csT)).T
            for q in range(JP // QW):
                xq = Xb[:, q * QW:(q + 1) * QW]
                acc = yo[:, q * QW:(q + 1) * QW]
                for jj in range(HQ):
                    j = q * HQ + jj
                    seg = cs[:, j:j + 1] - csT[j:j + 1, :]
                    Mj = (CB * jnp.exp(jnp.where(mask, seg, NEG)) * dtT_v[j:j + 1, :]).astype(ACT)
                    acc = acc + jnp.dot(Mj, jnp.where(lh == jj, xq, jnp.zeros_like(xq)),
                                        preferred_element_type=F32)
                y_ref[:, gi * JP + q * QW:gi * JP + (q + 1) * QW] = acc.astype(ACT)
            xwT = scale_heads(Xb.astype(F32).T, dtT_v * jnp.exp(totT - csT)).astype(ACT)
            upd = jnp.dot(xwT, Bm, preferred_element_type=F32)
            for j in range(J):
                rows = slice(j * P, (j + 1) * P)
                h_scr[gi, rows, :] = h[rows, :] * jnp.exp(totT[j:j + 1, :]) + upd[rows, :]

        @pl.when(k == nc - 1)
        def _():
            hl_ref[...] = h_scr[...]

    GN = G * N
    return pl.pallas_call(
        body, name=name, grid=(G // GB, nc),
        out_shape=(jax.ShapeDtypeStruct((T, di), ACT), jax.ShapeDtypeStruct((nc, G, JP, N), F32),
                   jax.ShapeDtypeStruct((G, JP, N), F32)),
        in_specs=[pl.BlockSpec((Q, GB * JP), lambda g, k: (ci(k), g)),
                  pl.BlockSpec((Q, GB * N), lambda g, k: (ci(k), di // (GB * N) + g)),
                  pl.BlockSpec((Q, GB * N), lambda g, k: (ci(k), (di + GN) // (GB * N) + g)),
                  pl.BlockSpec((GB * J, Q), lambda g, k: (g, ci(k))),
                  pl.BlockSpec((GB * J, 1), lambda g, k: (g, 0)),
                  pl.BlockSpec((GB, JP, N), lambda g, k: (g, 0, 0))],
        out_specs=(pl.BlockSpec((Q, GB * JP), lambda g, k: (ci(k), g)),
                   pl.BlockSpec((1, GB, JP, N), lambda g, k: (ci(k), g, 0, 0)),
                   pl.BlockSpec((GB, JP, N), lambda g, k: (g, 0, 0))),
        scratch_shapes=[pltpu.VMEM((GB, JP, N), F32)],
        compiler_params=_params(("arbitrary", "arbitrary")),
    )(xbc, xbc, xbc, dtT, a, h0)


def _ssd_bwd(xbc, dtT, a, dy, hs, dh_last, add, *, reverse, name, di):
    T = xbc.shape[0]
    G, JP, N = dh_last.shape
    J, P, Q = JP // HEAD_DIM, HEAD_DIM, CHUNK
    nc = T // Q
    QW = 256
    HQ = QW // P
    has_add = add is not None

    def ci(k):
        return k if reverse else nc - 1 - k

    GB = SSD_GROUPS_PER_STEP

    def body(*refs):
        for gi in range(GB):
            wide = lambda r, w: r.at[:, pl.ds(gi * w, w)]
            x_ref, b_ref, c_ref, dt_ref, a_ref, dy_ref, hs_ref, dhl_ref = refs[:8]
            views = [wide(x_ref, JP), wide(b_ref, N), wide(c_ref, N), dt_ref.at[pl.ds(gi * J, J)],
                     a_ref.at[pl.ds(gi * J, J)], wide(dy_ref, JP), hs_ref.at[:, pl.ds(gi, 1)],
                     dhl_ref.at[pl.ds(gi, 1)]]
            rest = refs[8:]
            if has_add:
                views += [wide(rest[0], JP), wide(rest[1], N), wide(rest[2], N)]
                rest = rest[3:]
            dx_ref, db_ref, dc_ref, ddt_ref, da_ref, dh0_ref, dh_scr = rest
            views += [wide(dx_ref, JP), wide(db_ref, N), wide(dc_ref, N), ddt_ref.at[pl.ds(gi * J, J)],
                      da_ref.at[pl.ds(gi * J, J)], dh0_ref.at[pl.ds(gi, 1)], dh_scr.at[gi]]
            group_body(*views)

    def group_body(x_ref, b_ref, c_ref, dt_ref, a_ref, dy_ref, hs_ref, dhl_ref, *rest):
        if has_add:
            adx_ref, adb_ref, adc_ref = rest[:3]
            rest = rest[3:]
        dx_ref, db_ref, dc_ref, ddt_ref, da_ref, dh0_ref, dh_scr = rest
        k = pl.program_id(1)

        @pl.when(k == 0)
        def _():
            dh_scr[...] = dhl_ref[0]
            da_ref[...] = jnp.zeros_like(da_ref)

        def scale_heads(vT, rowsT):
            return jnp.concatenate([vT[j * P:(j + 1) * P, :] * rowsT[j:j + 1, :] for j in range(J)], axis=0)

        def head_sums(vT):
            return jnp.sum(vT.reshape(J, P, Q), axis=1)

        dH = dh_scr[...]
        h = hs_ref[0, 0]
        Bm, Cm = b_ref[...], c_ref[...]
        dtT_v = dt_ref[...]
        a_v = a_ref[...]
        mask, Mf, csT, totT, last = _ssd_common(dtT_v, a_v, reverse)
        ecsT = jnp.exp(csT)
        toendT = jnp.exp(totT - csT)
        cs = _to_cols(csT)
        dYb = dy_ref[...]
        XT = x_ref[...].astype(F32).T
        dYT = dYb.astype(F32).T
        xdtT = scale_heads(XT, dtT_v).astype(ACT)
        dYT_b = dYT.astype(ACT)
        dYeT = scale_heads(dYT, ecsT).astype(ACT)
        h_b = h.astype(ACT)
        dH_b = dH.astype(ACT)
        CB = lax.dot_general(Cm, Bm, NT, preferred_element_type=F32)
        dxdt_offT = scale_heads(lax.dot_general(dH_b, Bm, NT, preferred_element_type=F32), toendT)
        dCB = jnp.zeros((Q, Q), F32)
        lh = lax.broadcasted_iota(jnp.int32, (Q, QW), 1) // P
        sh = lax.broadcasted_iota(jnp.int32, (QW, Q), 0) // P
        lane_q = lax.broadcasted_iota(jnp.int32, (Q, Q), 1)
        sub_j = lax.broadcasted_iota(jnp.int32, (J, Q), 0)
        e_rows = jnp.zeros((Q, Q), F32)
        e_cols = jnp.zeros((J, Q), F32)
        diag = []
        for q in range(JP // QW):
            xq = xdtT[q * QW:(q + 1) * QW, :]
            dyq = dYb[:, q * QW:(q + 1) * QW]
            dyTq = dYT_b[q * QW:(q + 1) * QW, :]
            acc = jnp.zeros((QW, Q), F32)
            for jj in range(HQ):
                j = q * HQ + jj
                seg = cs[:, j:j + 1] - csT[j:j + 1, :]
                L = jnp.exp(jnp.where(mask, seg, NEG))
                Mf_j = CB * L
                dyj = jnp.where(lh == jj, dyq, jnp.zeros_like(dyq))
                dyTj = jnp.where(sh == jj, dyTq, jnp.zeros_like(dyTq))
                acc = acc + jnp.dot(dyTj, Mf_j.astype(ACT), preferred_element_type=F32)
                dM = jnp.dot(dyj, xq, preferred_element_type=F32)
                dCB = dCB + dM * L
                E = dM * Mf_j
                e_rows = jnp.where(lane_q == j, jnp.sum(E, axis=1, keepdims=True), e_rows)
                e_cols = jnp.where(sub_j == j, jnp.sum(E, axis=0, keepdims=True), e_cols)
            diag.append(acc)
        dxdtT = dxdt_offT + jnp.concatenate(diag, axis=0)
        dCB_b = dCB.astype(ACT)
        dC = (jnp.dot(dCB_b, Bm, preferred_element_type=F32)
              + lax.dot_general(dYeT, h_b, TN, preferred_element_type=F32))
        xwT = scale_heads(XT, dtT_v * toendT).astype(ACT)
        dB = (lax.dot_general(dCB_b, Cm, TN, preferred_element_type=F32)
              + lax.dot_general(xwT, dH_b, TN, preferred_element_type=F32))
        dHc = jnp.dot(dYeT, Cm, preferred_element_type=F32)
        for j in range(J):
            rows = slice(j * P, (j + 1) * P)
            dh_scr[rows, :] = dH[rows, :] * jnp.exp(totT[j:j + 1, :]) + dHc[rows, :]
        dh0_ref[0] = dh_scr[...]

        yoT = scale_heads(lax.dot_general(h_b, Cm, NT, preferred_element_type=F32), ecsT)
        RT_ = head_sums(dYT * yoT)
        UT_ = head_sums(XT * dxdtT)
        UoT = head_sums(XT * dxdt_offT)
        hsum = jnp.sum(jnp.sum((dH * h).reshape(J, P, N), axis=1), axis=1, keepdims=True)
        dtot = jnp.sum(UoT * dtT_v, axis=1, keepdims=True) + jnp.exp(totT) * hsum
        lane = lax.broadcasted_iota(jnp.int32, (J, Q), 1)
        dcsT = e_rows.T[0:J] - e_cols + RT_ - UoT * dtT_v + jnp.where(lane == last, dtot, 0.0)
        ddaT = jnp.dot(dcsT, Mf, precision=HIGHEST, preferred_element_type=F32)
        ddt_ref[...] = ddaT * a_v + UT_
        da_ref[...] += ddaT * dtT_v
        dX = scale_heads(dxdtT, dtT_v).T
        if has_add:
            dX = dX + adx_ref[...].astype(F32)
            dB = dB + adb_ref[...].astype(F32)
            dC = dC + adc_ref[...].astype(F32)
        dx_ref[...] = dX.astype(ACT)
        db_ref[...] = dB.astype(ACT)
        dc_ref[...] = dC.astype(ACT)

    GN = G * N
    xspec = pl.BlockSpec((Q, GB * JP), lambda g, k: (ci(k), g))
    nspec = pl.BlockSpec((Q, GB * N), lambda g, k: (ci(k), g))
    hspec = pl.BlockSpec((GB, JP, N), lambda g, k: (g, 0, 0))
    in_specs = [xspec,
                pl.BlockSpec((Q, GB * N), lambda g, k: (ci(k), di // (GB * N) + g)),
                pl.BlockSpec((Q, GB * N), lambda g, k: (ci(k), (di + GN) // (GB * N) + g)),
                pl.BlockSpec((GB * J, Q), lambda g, k: (g, ci(k))),
                pl.BlockSpec((GB * J, 1), lambda g, k: (g, 0)),
                xspec,
                pl.BlockSpec((1, GB, JP, N), lambda g, k: (ci(k), g, 0, 0)),
                hspec]
    operands = [xbc, xbc, xbc, dtT, a, dy, hs, dh_last]
    if has_add:
        in_specs += [xspec, nspec, nspec]
        operands += list(add)
    H = G * J
    return pl.pallas_call(
        body, name=name, grid=(G // GB, nc),
        out_shape=(jax.ShapeDtypeStruct((T, di), ACT), jax.ShapeDtypeStruct((T, GN), ACT),
                   jax.ShapeDtypeStruct((T, GN), ACT), jax.ShapeDtypeStruct((H, T), F32),
                   jax.ShapeDtypeStruct((H, Q), F32), jax.ShapeDtypeStruct((G, JP, N), F32)),
        in_specs=in_specs,
        out_specs=(xspec, nspec, nspec,
                   pl.BlockSpec((GB * J, Q), lambda g, k: (g, ci(k))),
                   pl.BlockSpec((GB * J, Q), lambda g, k: (g, 0)),
                   hspec),
        scratch_shapes=[pltpu.VMEM((GB, JP, N), F32)],
        compiler_params=_params(("arbitrary", "arbitrary")),
    )(*operands)


def _mod_fwd(craw, w, b, *, name):
    R, D = craw.shape
    NL = w.shape[1]
    tn = _tile(NL, 512, 128)

    def body(c_ref, w_ref, b_ref, o_ref):
        o_ref[...] = jnp.dot(_silu(c_ref[...]), w_ref[...], preferred_element_type=F32) + b_ref[...]

    return pl.pallas_call(
        body, name=name, grid=(NL // tn,), out_shape=jax.ShapeDtypeStruct((R, NL), F32),
        in_specs=[pl.BlockSpec((R, D), lambda j: (0, 0)), pl.BlockSpec((D, tn), lambda j: (0, j)),
                  pl.BlockSpec((1, tn), lambda j: (0, j))],
        out_specs=pl.BlockSpec((R, tn), lambda j: (0, j)),
        compiler_params=_params(("parallel",)),
    )(craw, w, b)


def _mod_bwd(craw, w, dm, *, name):
    R, D = craw.shape
    NL = w.shape[1]
    tn = _tile(NL, 512, 128)

    def body(c_ref, w_ref, dm_ref, dw_ref, dc_ref):
        first = pl.program_id(0) == 0
        cf = c_ref[...]
        dmv = dm_ref[...]
        dw_ref[...] = lax.dot_general(_silu(cf), dmv, TN, preferred_element_type=F32)
        part = lax.dot_general(dmv, w_ref[...], NT, preferred_element_type=F32) * _dsilu(cf)

        @pl.when(first)
        def _():
            dc_ref[...] = part

        @pl.when(jnp.logical_not(first))
        def _():
            dc_ref[...] += part

    return pl.pallas_call(
        body, name=name, grid=(NL // tn,),
        out_shape=(jax.ShapeDtypeStruct((D, NL), F32), jax.ShapeDtypeStruct((R, D), F32)),
        in_specs=[pl.BlockSpec((R, D), lambda j: (0, 0)), pl.BlockSpec((D, tn), lambda j: (0, j)),
                  pl.BlockSpec((R, tn), lambda j: (0, j))],
        out_specs=(pl.BlockSpec((D, tn), lambda j: (0, j)), pl.BlockSpec((R, D), lambda j: (0, 0))),
        compiler_params=_params(("arbitrary",)),
    )(craw, w, dm)


def _pad_rows(a, rows):
    return jnp.concatenate([a, jnp.zeros((rows - a.shape[0],) + a.shape[1:], a.dtype)], axis=0)


def _cols_whole(g):
    return jnp.transpose(g, (1, 0, 2)).reshape(g.shape[1], N_DEV * g.shape[2])


def _rows_whole(g):
    return g.reshape(N_DEV * g.shape[1], g.shape[2])


def _col_blocks(full):
    K, n8 = full.shape
    return jnp.transpose(full.reshape(K, N_DEV, n8 // N_DEV), (1, 0, 2)).astype(ACT)


def _row_blocks(full):
    K8, n = full.shape
    return full.reshape(N_DEV, K8 // N_DEV, n).astype(ACT)


def kernel(x, c, ctx, c_ctx, w_mod, b_mod, norm_mix, w_in, ssm_conv_w, ssm_conv_b, dt_bias, a_log, d_skip, ssm_norm, cf_conv_w, cf_conv_b, cf_ln_g, cf_ln_b, w_proj_a, w_proj_b, w_out, norm_ffn, w_ffn_gate, w_ffn_up, w_ffn_down, norm_final, loss_target, m_c_ctx, m_w_mod, m_b_mod, m_norm_mix, m_w_in, m_ssm_conv_w, m_ssm_conv_b, m_dt_bias, m_a_log, m_d_skip, m_ssm_norm, m_cf_conv_w, m_cf_conv_b, m_cf_ln_g, m_cf_ln_b, m_w_proj_a, m_w_proj_b, m_w_out, m_norm_ffn, m_w_ffn_gate, m_w_ffn_up, m_w_ffn_down, m_norm_final, v_c_ctx, v_w_mod, v_b_mod, v_norm_mix, v_w_in, v_ssm_conv_w, v_ssm_conv_b, v_dt_bias, v_a_log, v_d_skip, v_ssm_norm, v_cf_conv_w, v_cf_conv_b, v_cf_ln_g, v_cf_ln_b, v_w_proj_a, v_w_proj_b, v_w_out, v_norm_ffn, v_w_ffn_gate, v_w_ffn_up, v_w_ffn_down, v_norm_final):
    args = dict(locals())
    me = 4 * lax.axis_index("x") + 2 * lax.axis_index("y") + lax.axis_index("c")
    T, D = x.shape[1], x.shape[2]
    DI = ssm_norm.shape[1]
    H = DI // HEAD_DIM
    G, J, N = GROUPS, H // GROUPS, STATE
    JP = J * HEAD_DIM
    GN = G * N
    CONV = DI + 2 * GN
    x0 = x[0]
    ctx0 = ctx[0]
    target = loss_target[0]

    got_in = _gather_seq(w_in[0].astype(ACT), name="gather_w_in", collective_id=0)
    k5 = ssm_conv_w.shape[1]
    k31 = cf_conv_w.shape[1]
    cw5 = _exchange(_pad_rows(ssm_conv_w[0], 8), name="gather_conv5", gather=True)
    cw5 = jnp.transpose(cw5, (1, 0, 2)).reshape(8, CONV)
    cw31 = _exchange(_pad_rows(cf_conv_w[0], 32), name="gather_conv31", gather=True)
    cw31 = jnp.transpose(cw31, (1, 0, 2)).reshape(32, D)

    c_all = _exchange(_pad_rows(c, 8), name="gather_c", gather=True)[:, 0, :]
    craw = jnp.concatenate([c_all, c_ctx[None, :], jnp.zeros((7, D), F32)], axis=0)
    NL = w_mod.shape[2]
    b_loc = lax.dynamic_slice(b_mod, (0, me * NL), (1, NL))
    m_loc = _mod_fwd(craw, w_mod[0], b_loc, name="mod_fwd")
    m_all = jnp.transpose(_exchange(m_loc, name="gather_mod", gather=True), (1, 0, 2)).reshape(16, N_DEV * NL)
    m_me = lax.dynamic_slice(m_all, (me, 0), (1, 6 * D))
    sh1, sc1, g1, sh2, sc2, g2 = [m_me[:, i * D:(i + 1) * D] for i in range(6)]
    csh1, csc1 = m_all[8:9, 0:D], m_all[8:9, D:2 * D]

    win = _cols_whole(got_in)
    o_xbc, o_dt, o_glu, o_gates = DI, DI + CONV, DI + CONV + 2 * H, DI + CONV + 2 * H + 2 * D
    w_z, w_xbc, w_dt = win[:, :o_xbc], win[:, o_xbc:o_dt], win[:, o_dt:o_glu]
    w_u, w_v, w_gates = win[:, o_glu:o_glu + D], win[:, o_glu + D:o_gates], win[:, o_gates:]
    def gather_behind(shard, behind, name, cid):
        zero = (behind[(0,) * behind.ndim] * 0).astype(ACT)
        return _gather_seq(shard.astype(ACT) + zero, name=name, collective_id=cid)

    ra, rb = w_proj_a.shape[1], w_proj_b.shape[1]
    got_proj = gather_behind(jnp.concatenate([w_proj_a[0], w_proj_b[0], w_out[0]], axis=0), got_in, "gather_w_proj", 1)
    w_pa = _rows_whole(got_proj[:, :ra])
    w_pb = _rows_whole(got_proj[:, ra:ra + rb])
    w_o = _rows_whole(got_proj[:, ra + rb:])

    a_neg = -jnp.exp(a_log[0])
    a_f, a_b = a_neg[0][:, None], a_neg[1][:, None]
    dtb = dt_bias[0].reshape(2 * H, 1)
    dskip_e = jnp.repeat(d_skip[0], HEAD_DIM)[None, :]

    def front(h, tag, full, after=None):
        out = {}
        out["xbc_raw"] = _mm(h, w_xbc, "nn", name="mm_xbc_" + tag, out_dtype=ACT, after=after)
        dt_raw = _mm(h, w_dt, "nn", name="mm_dt_" + tag, out_dtype=F32)
        out["rawT"] = dt_raw.T
        if full:
            out["z"] = _mm(h, w_z, "nn", name="mm_z_" + tag, out_dtype=ACT)
            out["u"] = _mm(h, w_u, "nn", name="mm_u_" + tag, out_dtype=ACT)
            out["v"] = _mm(h, w_v, "nn", name="mm_v_" + tag, out_dtype=ACT)
            out["gates"] = _mm(h, w_gates, "nn", name="mm_gates_" + tag, out_dtype=ACT)
        out["xbc"] = _conv5_silu_fwd(out["xbc_raw"], cw5, ssm_conv_b, name="conv5_fwd_" + tag)
        out["dtT"] = _dt_fwd(out["rawT"], dtb, name="dt_fwd_" + tag)
        return out

    hc = _norm_mod_fwd(ctx0, norm_mix, csh1, csc1, name="norm_mod_ctx")
    fc = front(hc, "ctx", False)
    zero_state = jnp.zeros((G, JP, N), F32)
    _, hs_cf, h_f = _ssd_fwd(fc["xbc"], fc["dtT"][:H], a_f, zero_state, reverse=False, name="ssd_fwd_ctx_f", di=DI)
    _, hs_cb, h_b = _ssd_fwd(fc["xbc"], fc["dtT"][H:], a_b, zero_state, reverse=True, name="ssd_fwd_ctx_b", di=DI)

    hx = _norm_mod_fwd(x0, norm_mix, sh1, sc1, name="norm_mod_x")
    fx = front(hx, "x", True)
    y_f, hs_f, _ = _ssd_fwd(fx["xbc"], fx["dtT"][:H], a_f, h_f, reverse=False, name="ssd_fwd_x_f", di=DI)
    FF = w_ffn_gate.shape[2]
    got_gu = gather_behind(jnp.concatenate([w_ffn_gate[0].T, w_ffn_up[0].T], axis=0), y_f, "gather_w_gate_up", 4)
    w_guT = jnp.concatenate([_rows_whole(got_gu[:, :FF]), _rows_whole(got_gu[:, FF:])], axis=0)
    w_down = _rows_whole(gather_behind(w_ffn_down[0], y_f, "gather_w_down", 6))
    y_b, hs_b, _ = _ssd_fwd(fx["xbc"], fx["dtT"][H:], a_b, h_b, reverse=True, name="ssd_fwd_x_b", di=DI)
    ya_in = _gate_norm_fwd(y_f, y_b, fx["xbc"], fx["z"], dskip_e, ssm_norm, name="gate_norm_fwd")
    ya = _mm(ya_in, w_pa, "nn", name="mm_proj_a", out_dtype=ACT)
    conv_out = _glu_conv_fwd(fx["u"], fx["v"], cw31, cf_conv_b, name="glu_conv_fwd")
    cf = _ln_silu_fwd(conv_out, cf_ln_g, cf_ln_b, name="ln_silu_fwd")
    yb = _mm(cf, w_pb, "nn", name="mm_proj_b", out_dtype=ACT)
    merged = _merge_fwd(ya, yb, fx["gates"], name="merge_fwd")
    o_mix = _mm(merged, w_o, "nn", name="mm_out", out_dtype=ACT)

    x1, h2 = _resid_norm_mod_fwd(x0, o_mix, g1, norm_ffn, sh2, sc2, name="resid_norm_mod")
    gu = _mm(h2, w_guT, "nt", name="mm_gate_up", out_dtype=ACT)
    act = _swiglu_fwd(gu, name="swiglu_fwd")
    dn = _mm(act, w_down, "nn", name="mm_down", out_dtype=ACT)

    loss_part, dx2, d_dn, g_norm_final, d_g2 = _final_fwd_bwd(x1, dn, g2, norm_final[None, :], target, name="final")
    loss = lax.psum(loss_part[0, 0], AXES)

    d_act = _mm(d_dn, w_down, "nt", name="mm_d_act", out_dtype=ACT)
    gw_down = _mm(act, d_dn, "tn", name="mm_gw_down", out_dtype=F32)
    parts = {}
    parts["w_ffn_down"] = _exchange_seq(_row_blocks(gw_down), name="scatter_w_down", gather=False, collective_id=7)
    d_gu = _swiglu_bwd(gu, d_act, name="swiglu_bwd")
    gw_guT = _mm(d_gu, h2, "tn", name="mm_gw_gate_up", out_dtype=F32)
    DFF = N_DEV * FF
    gu_blocks = jnp.concatenate([_row_blocks(gw_guT[:DFF]), _row_blocks(gw_guT[DFF:])], axis=1)
    parts_gu = _exchange_seq(gu_blocks, name="scatter_w_gate_up", gather=False, collective_id=8)
    parts["w_ffn_gate"] = jnp.transpose(parts_gu[:, :FF], (0, 2, 1))
    parts["w_ffn_up"] = jnp.transpose(parts_gu[:, FF:], (0, 2, 1))
    d_h2 = _mm(d_gu, w_guT, "nn", name="mm_d_h2", out_dtype=F32)
    dx1, d_sh2, d_sc2, g_norm_ffn, d_o, d_g1 = _norm_mod_bwd(
        x1, norm_ffn, sc2, d_h2, name="norm_mod_bwd_ffn", dres=dx2, o=o_mix, g=g1)

    d_merged = _mm(d_o, w_o, "nt", name="mm_d_merged", out_dtype=ACT)
    gw_out = _mm(merged, d_o, "tn", name="mm_gw_out", out_dtype=F32)
    d_ya, d_yb, d_gates = _merge_bwd(d_merged, ya, yb, fx["gates"], name="merge_bwd")
    gw_pa = _mm(ya_in, d_ya, "tn", name="mm_gw_pa", out_dtype=F32)
    gw_pb = _mm(cf, d_yb, "tn", name="mm_gw_pb", out_dtype=F32)
    proj_blocks = jnp.concatenate([_row_blocks(gw_pa), _row_blocks(gw_pb), _row_blocks(gw_out)], axis=1)
    parts_proj = _exchange_seq(proj_blocks, name="scatter_w_proj", gather=False, collective_id=10)
    parts["w_proj_a"], parts["w_proj_b"] = parts_proj[:, :ra], parts_proj[:, ra:ra + rb]
    parts["w_out"] = parts_proj[:, ra + rb:]
    d_ya_in = _mm(d_ya, w_pa, "nt", name="mm_d_ya_in", out_dtype=ACT)
    d_cf = _mm(d_yb, w_pb, "nt", name="mm_d_cf", out_dtype=ACT)
    d_conv, g_ln_g, g_ln_b = _ln_silu_bwd(conv_out, cf_ln_g, cf_ln_b, d_cf, name="ln_silu_bwd")
    d_u, d_v, g_cw31, g_cb31 = _glu_conv_bwd(fx["u"], fx["v"], cw31, d_conv, name="glu_conv_bwd")
    d_y, d_z, dxs_skip, g_ssm_norm, g_dskip_e = _gate_norm_bwd(
        d_ya_in, y_f, y_b, fx["xbc"], fx["z"], dskip_e, ssm_norm, name="gate_norm_bwd")

    zero_bc = jnp.zeros((T, GN), ACT)
    r1 = _ssd_bwd(fx["xbc"], fx["dtT"][:H], a_f, d_y, hs_f, zero_state, (dxs_skip, zero_bc, zero_bc),
                  reverse=False, name="ssd_bwd_x_f", di=DI)
    r2 = _ssd_bwd(fx["xbc"], fx["dtT"][H:], a_b, d_y, hs_b, zero_state, r1[:3],
                  reverse=True, name="ssd_bwd_x_b", di=DI)
    Tc = ctx0.shape[0]
    zero_yc = jnp.zeros((Tc, DI), ACT)
    r3 = _ssd_bwd(fc["xbc"], fc["dtT"][:H], a_f, zero_yc, hs_cf, r1[5], None,
                  reverse=False, name="ssd_bwd_ctx_f", di=DI)
    r4 = _ssd_bwd(fc["xbc"], fc["dtT"][H:], a_b, zero_yc, hs_cb, r2[5], r3[:3],
                  reverse=True, name="ssd_bwd_ctx_b", di=DI)

    def back(f, rf, rb, tag):
        d_xbc_raw, g_w5, g_b5 = _conv5_silu_bwd(f["xbc_raw"], cw5, ssm_conv_b, (rb[0], rb[1], rb[2]),
                                                name="conv5_bwd_" + tag)
        ddtT = jnp.concatenate([rf[3], rb[3]], axis=0)
        d_rawT, g_dtb = _dt_bwd(f["rawT"], dtb, ddtT, name="dt_bwd_" + tag)
        g_a = jnp.stack([jnp.sum(rf[4], axis=1), jnp.sum(rb[4], axis=1)])
        return d_xbc_raw, d_rawT.T.astype(ACT), g_w5, g_b5, g_dtb, g_a

    dx_xbc_raw, dx_dt_raw, gx_w5, gx_b5, gx_dtb, gx_a = back(fx, r1, r2, "x")
    dc_xbc_raw, dc_dt_raw, gc_w5, gc_b5, gc_dtb, gc_a = back(fc, r3, r4, "ctx")

    gw_xbc = _mm(hc, dc_xbc_raw, "tn", name="mm_gw_xbc_ctx", out_dtype=F32)
    gw_xbc = _mm(hx, dx_xbc_raw, "tn", name="mm_gw_xbc", out_dtype=F32, add=gw_xbc)
    gw_dt = _mm(hc, dc_dt_raw, "tn", name="mm_gw_dt_ctx", out_dtype=F32)
    gw_dt = _mm(hx, dx_dt_raw, "tn", name="mm_gw_dt", out_dtype=F32, add=gw_dt)
    gw_z = _mm(hx, d_z, "tn", name="mm_gw_z", out_dtype=F32)
    gw_u = _mm(hx, d_u, "tn", name="mm_gw_u", out_dtype=F32)
    gw_v = _mm(hx, d_v, "tn", name="mm_gw_v", out_dtype=F32)
    gw_gates = _mm(hx, d_gates, "tn", name="mm_gw_gates", out_dtype=F32)
    gw_in = jnp.concatenate([gw_z, gw_xbc, gw_dt, gw_u, gw_v, gw_gates], axis=1)
    parts["w_in"] = _exchange_seq(_col_blocks(gw_in), name="scatter_w_in", gather=False, collective_id=13)

    d_hx = _mm(d_z, w_z, "nt", name="mm_d_hx_z", out_dtype=F32)
    d_hx = _mm(dx_xbc_raw, w_xbc, "nt", name="mm_d_hx_xbc", out_dtype=F32, add=d_hx)
    d_hx = _mm(dx_dt_raw, w_dt, "nt", name="mm_d_hx_dt", out_dtype=F32, add=d_hx)
    d_hx = _mm(d_u, w_u, "nt", name="mm_d_hx_u", out_dtype=F32, add=d_hx)
    d_hx = _mm(d_v, w_v, "nt", name="mm_d_hx_v", out_dtype=F32, add=d_hx)
    d_hx = _mm(d_gates, w_gates, "nt", name="mm_d_hx_gates", out_dtype=F32, add=d_hx)
    grad_x, d_sh1, d_sc1, gx_norm_mix = _norm_mod_bwd(x0, norm_mix, sc1, d_hx, name="norm_mod_bwd_x", dres=dx1)
    d_hc = _mm(dc_xbc_raw, w_xbc, "nt", name="mm_d_hc_xbc", out_dtype=F32)
    d_hc = _mm(dc_dt_raw, w_dt, "nt", name="mm_d_hc_dt", out_dtype=F32, add=d_hc)
    _, d_csh1, d_csc1, gc_norm_mix = _norm_mod_bwd(ctx0, norm_mix, csc1, d_hc, name="norm_mod_bwd_ctx")

    zD = jnp.zeros((1, D), F32)
    dm_me = jnp.concatenate([d_sh1, d_sc1, d_g1, d_sh2, d_sc2, d_g2], axis=1)
    dm_ctx = jnp.concatenate([d_csh1, d_csc1, zD, zD, zD, zD], axis=1)
    rows16 = lax.broadcasted_iota(jnp.int32, (16, 1), 0)
    dm_rows = jnp.where(rows16 == me, dm_me, 0.0) + jnp.where(rows16 == 8, dm_ctx, 0.0)
    dm_sum = _sum_slots(_exchange(dm_rows, name="gather_dm", gather=True), name="sum_dm")
    g_b_mod = _colsum(dm_sum, name="colsum_dm")
    dm_loc = lax.dynamic_slice(dm_sum, (0, me * NL), (16, NL))
    g_w_mod, dcraw = _mod_bwd(craw, w_mod[0], dm_loc, name="mod_bwd")

    small = [
        ("c_ctx", dcraw[8]), ("norm_mix", gx_norm_mix + gc_norm_mix),
        ("ssm_conv_w", (gx_w5 + gc_w5)[:k5]), ("ssm_conv_b", gx_b5 + gc_b5),
        ("dt_bias", gx_dtb + gc_dtb), ("a_log", (gx_a + gc_a) * a_neg),
        ("d_skip", jnp.sum(g_dskip_e.reshape(H, HEAD_DIM), axis=1)), ("ssm_norm", g_ssm_norm),
        ("cf_conv_w", g_cw31[:k31]), ("cf_conv_b", g_cb31), ("cf_ln_g", g_ln_g), ("cf_ln_b", g_ln_b),
        ("norm_ffn", g_norm_ffn), ("norm_final", g_norm_final),
    ]
    flat = jnp.concatenate([v.reshape(-1) for _, v in small])
    n_small = flat.shape[0]
    rows_small = -(-n_small // 1024) * 8
    flat = jnp.concatenate([flat, jnp.zeros((rows_small * 128 - n_small,), F32)]).reshape(rows_small, 128)
    summed = _sum_slots(_exchange(flat, name="gather_small", gather=True), name="sum_small").reshape(-1)
    g_small = {}
    pos = 0
    for nm, v in small:
        g_small[nm] = summed[pos:pos + v.size].reshape(v.shape)
        pos += v.size
    g_small["b_mod"] = g_b_mod
    n5, n31 = ssm_conv_w.shape[2], cf_conv_w.shape[2]
    g_small["ssm_conv_w"] = lax.dynamic_slice(g_small["ssm_conv_w"], (0, me * n5), (k5, n5))
    g_small["cf_conv_w"] = lax.dynamic_slice(g_small["cf_conv_w"], (0, me * n31), (k31, n31))

    grads, deltas, new_m, new_v = {}, {}, {}, {}

    def adam2d(nm, parts):
        shape = args[nm].shape
        R, C = shape[-2], shape[-1]
        g, d, m2, v2 = _adamw(parts, args[nm].reshape(R, C), args["m_" + nm].reshape(R, C),
                              args["v_" + nm].reshape(R, C), name="adamw_" + nm)
        grads[nm], deltas[nm], new_m[nm], new_v[nm] = [t.reshape(shape) for t in (g, d, m2, v2)]

    adam2d("w_mod", g_w_mod[None])
    for nm in ("w_ffn_down", "w_ffn_gate", "w_ffn_up", "w_out", "w_proj_a", "w_proj_b", "w_in"):
        adam2d(nm, parts[nm])

    small_names = ["c_ctx", "b_mod", "norm_mix", "ssm_conv_w", "ssm_conv_b", "dt_bias", "a_log", "d_skip", "ssm_norm",
                   "cf_conv_w", "cf_conv_b", "cf_ln_g", "cf_ln_b", "norm_ffn", "norm_final"]

    def pack(vals):
        f = jnp.concatenate([t.reshape(-1) for t in vals])
        rows = -(-f.shape[0] // 1024) * 8
        return jnp.concatenate([f, jnp.zeros((rows * 128 - f.shape[0],), F32)]).reshape(rows, 128)

    pg = pack([g_small[nm] for nm in small_names])
    pw = pack([args[nm] for nm in small_names])
    pm = pack([args["m_" + nm] for nm in small_names])
    pv = pack([args["v_" + nm] for nm in small_names])
    outs = _adamw(pg[None], pw, pm, pv, name="adamw_small")
    pos = 0
    for nm in small_names:
        shape = args[nm].shape
        size = math.prod(shape)
        vals = [t.reshape(-1)[pos:pos + size].reshape(shape) for t in outs]
        grads[nm], deltas[nm], new_m[nm], new_v[nm] = vals
        pos += size

    order = ["c_ctx", "w_mod", "b_mod", "norm_mix", "w_in", "ssm_conv_w", "ssm_conv_b", "dt_bias", "a_log", "d_skip",
             "ssm_norm", "cf_conv_w", "cf_conv_b", "cf_ln_g", "cf_ln_b", "w_proj_a", "w_proj_b", "w_out", "norm_ffn",
             "w_ffn_gate", "w_ffn_up", "w_ffn_down", "norm_final"]
    return (loss, grad_x[None], *[grads[n] for n in order], *[deltas[n] for n in order],
            *[new_m[n] for n in order], *[new_v[n] for n in order])
```

```python
import functools
import math

import jax
import jax.numpy as jnp
from jax import lax
from jax.experimental import pallas as pl
from jax.experimental.pallas import tpu as pltpu
from jax.experimental.pallas import tpu_sc as plsc

F32 = jnp.float32
ACT = jnp.bfloat16
HIGHEST = lax.Precision.HIGHEST
MESH = pl.DeviceIdType.MESH
AXES = ("x", "y", "c")
N_DEV = 8

GRID_W = 64
CHUNK = 128
SSD_GROUPS_PER_STEP = 4
HEAD_DIM = 64
GROUPS = 8
STATE = 128
EPS = 1e-6
ADAM_LR = 0.001
ADAM_B1 = 0.9
ADAM_B2 = 0.999
ADAM_EPS = 1e-08
ADAM_WD = 0.01
ADAM_STEP = 10

V7X_VMEM_LIMIT = 56 * 1024 * 1024
NEG = -1e30

NN = (((1,), (0,)), ((), ()))
NT = (((1,), (1,)), ((), ()))
TN = (((0,), (0,)), ((), ()))


def _tile(n, target, quantum):
    best = None
    t = quantum
    while t <= min(n, target):
        if n % t == 0:
            best = t
        t += quantum
    return n if best is None else best


def _params(sem=None):
    kw = dict(vmem_limit_bytes=V7X_VMEM_LIMIT)
    if sem is not None:
        kw["dimension_semantics"] = sem
    return pltpu.CompilerParams(**kw)


def _silu(v):
    return v * jax.nn.sigmoid(v)


def _dsilu(v):
    s = jax.nn.sigmoid(v)
    return s * (1.0 + v * (1.0 - s))


def _exchange(x, *, name, gather):
    shape = x.shape[-2:]

    def body(x_ref, o_ref, send_sems, recv_sems, loc_sem):
        ix, iy, ic = lax.axis_index("x"), lax.axis_index("y"), lax.axis_index("c")
        me = 4 * ix + 2 * iy + ic

        def src(d):
            return x_ref if gather else x_ref.at[d]

        def remote(k, slot, peer_xyz, src_ref):
            return pltpu.make_async_remote_copy(
                src_ref=src_ref, dst_ref=o_ref.at[slot], send_sem=send_sems.at[k], recv_sem=recv_sems.at[k],
                device_id=peer_xyz, device_id_type=MESH)

        local = pltpu.make_async_copy(src(me), o_ref.at[me], loc_sem)
        local.start()
        sends, peers = [], []
        for k in range(1, N_DEV):
            px = 1 - ix if k & 4 else ix
            py = 1 - iy if k & 2 else iy
            pc = 1 - ic if k & 1 else ic
            peer = 4 * px + 2 * py + pc
            cp = remote(k - 1, me, (px, py, pc), src(peer))
            cp.start()
            sends.append(cp)
            peers.append((peer, (px, py, pc)))
        for k in range(1, N_DEV):
            peer, xyz = peers[k - 1]
            remote(k - 1, peer, xyz, src(peer)).wait_recv()
        for cp in sends:
            cp.wait_send()
        local.wait()

    return pl.pallas_call(
        body, name=name,
        out_shape=jax.ShapeDtypeStruct((N_DEV,) + shape, x.dtype),
        in_specs=[pl.BlockSpec(memory_space=pl.ANY)],
        out_specs=pl.BlockSpec(memory_space=pl.ANY),
        scratch_shapes=[pltpu.SemaphoreType.DMA((N_DEV - 1,)), pltpu.SemaphoreType.DMA((N_DEV - 1,)),
                        pltpu.SemaphoreType.DMA],
    )(x)


HBM_SPEC = pl.BlockSpec(memory_space=pltpu.HBM)
SEM_SPEC = pl.BlockSpec(memory_space=pltpu.SEMAPHORE)
ANY_SPEC = pl.BlockSpec(memory_space=pl.ANY)
DATAFLOW = pltpu.SideEffectType.DATAFLOW_SIDE_EFFECTING


def _peer(k):
    ix, iy, ic = lax.axis_index("x"), lax.axis_index("y"), lax.axis_index("c")
    px = 1 - ix if k & 4 else ix
    py = 1 - iy if k & 2 else iy
    pc = 1 - ic if k & 1 else ic
    return (px, py, pc), 4 * px + 2 * py + pc


def _exchange_start(x, after, *, name, gather):
    shape = x.shape[-2:]

    def body(after_ref, x_ref, land_ref, send_sem, recv_sem, x_thru, land_thru, token, loc_sem):
        _, me = _peer(0)

        def src(d):
            return x_ref if gather else x_ref.at[d]

        local = pltpu.make_async_copy(src(me), land_ref.at[me], loc_sem)
        local.start()
        local.wait()
        for k in range(1, N_DEV):
            xyz, peer = _peer(k)
            pltpu.make_async_remote_copy(
                src_ref=src(peer), dst_ref=land_ref.at[me], send_sem=send_sem, recv_sem=recv_sem,
                device_id=xyz, device_id_type=MESH).start()
        token[...] = jnp.zeros_like(token)

    land = lax.empty((N_DEV,) + shape, x.dtype)
    return pl.pallas_call(
        body, name=name,
        out_shape=(pltpu.SemaphoreType.DMA(()), pltpu.SemaphoreType.DMA(()), pltpu.HBM(x.shape, x.dtype),
                   pltpu.HBM((N_DEV,) + shape, x.dtype), jax.ShapeDtypeStruct((8, 128), F32)),
        in_specs=(ANY_SPEC, HBM_SPEC, HBM_SPEC),
        out_specs=(SEM_SPEC, SEM_SPEC, HBM_SPEC, HBM_SPEC, pl.BlockSpec(memory_space=pltpu.VMEM)),
        input_output_aliases={1: 2, 2: 3},
        scratch_shapes=[pltpu.SemaphoreType.DMA],
        compiler_params=pltpu.CompilerParams(has_side_effects=DATAFLOW),
    )(after, pltpu.with_memory_space_constraint(x, pltpu.HBM), pltpu.with_memory_space_constraint(land, pltpu.HBM))


def _exchange_wait(started, after, *, name):
    send_sem, recv_sem, x_thru, land_thru, _ = started

    def body(x_ref, land_ref, send_sem, recv_sem, after_ref, x_dead, got_ref):
        xyz, _ = _peer(0)
        seven = land_ref.at[pl.ds(0, N_DEV - 1)]
        cp = pltpu.make_async_remote_copy(src_ref=seven, dst_ref=seven, send_sem=send_sem, recv_sem=recv_sem,
                                          device_id=xyz, device_id_type=MESH)
        cp.wait_send()
        cp.wait_recv()

    return pl.pallas_call(
        body, name=name,
        out_shape=(pltpu.HBM(x_thru.shape, x_thru.dtype), pltpu.HBM(land_thru.shape, land_thru.dtype)),
        in_specs=(HBM_SPEC, HBM_SPEC, SEM_SPEC, SEM_SPEC, ANY_SPEC),
        out_specs=(HBM_SPEC, HBM_SPEC),
        input_output_aliases={0: 0, 1: 1},
        compiler_params=pltpu.CompilerParams(has_side_effects=DATAFLOW),
    )(x_thru, land_thru, send_sem, recv_sem, after)[1]


def _exchange_seq(x, *, name, gather, collective_id):
    shape = x.shape[-2:]
    x_ref = jax.new_ref(x, memory_space=pltpu.MemorySpace.HBM)
    out_ref = jax.empty_ref(jax.ShapeDtypeStruct((N_DEV,) + shape, x.dtype), memory_space=pltpu.MemorySpace.HBM)

    @pl.kernel(mesh=plsc.ScalarSubcoreMesh(axis_name="seq", num_cores=1), name=name,
               scratch_types=(pltpu.SemaphoreType.DMA, pltpu.SemaphoreType.DMA, pltpu.SemaphoreType.DMA),
               compiler_params=pltpu.CompilerParams(collective_id=collective_id))
    def launch(send_sem, recv_sem, loc_sem):
        barrier = pltpu.get_barrier_semaphore()
        for k in range(1, N_DEV):
            xyz, _ = _peer(k)
            pl.semaphore_signal(barrier, inc=1, device_id=xyz, device_id_type=MESH)
        pl.semaphore_wait(barrier, N_DEV - 1)
        mine, me = _peer(0)

        def src(d):
            return x_ref if gather else x_ref.at[d]

        local = pltpu.make_async_copy(src(me), out_ref.at[me], loc_sem)
        local.start()
        for k in range(1, N_DEV):
            xyz, peer = _peer(k)
            pltpu.make_async_remote_copy(
                src_ref=src(peer), dst_ref=out_ref.at[me], send_sem=send_sem, recv_sem=recv_sem,
                device_id=xyz, device_id_type=MESH).start()
        seven = out_ref.at[pl.ds(0, N_DEV - 1)]
        pltpu.make_async_remote_copy(src_ref=seven, dst_ref=seven, send_sem=send_sem, recv_sem=recv_sem,
                                     device_id=mine, device_id_type=MESH).wait()
        local.wait()

    launch()
    return out_ref[...]


def _gather_seq(x, *, name, collective_id):
    x_ref = jax.new_ref(x, memory_space=pltpu.MemorySpace.HBM)
    out_ref = jax.empty_ref(jax.ShapeDtypeStruct((N_DEV,) + x.shape, x.dtype), memory_space=pltpu.MemorySpace.HBM)

    @pl.kernel(mesh=plsc.ScalarSubcoreMesh(axis_name="seq", num_cores=1), name=name,
               scratch_types=(pltpu.SemaphoreType.DMA((N_DEV - 1,)), pltpu.SemaphoreType.DMA((N_DEV - 1,)),
                              pltpu.SemaphoreType.DMA),
               compiler_params=pltpu.CompilerParams(collective_id=collective_id))
    def launch(send_sems, recv_sems, loc_sem):
        ix, iy, ic = lax.axis_index("x"), lax.axis_index("y"), lax.axis_index("c")
        me, sibling = (ix, iy, ic), (ix, iy, 1 - ic)
        chips = [(1 - ix, iy), (ix, 1 - iy), (1 - ix, 1 - iy)]
        writers = [sibling] + [(*chip, ic) for chip in chips]
        barrier = pltpu.get_barrier_semaphore()
        for peer in writers:
            pl.semaphore_signal(barrier, inc=1, device_id=peer, device_id_type=MESH)
        pl.semaphore_wait(barrier, len(writers))

        def rows(px, py, pc):
            return out_ref.at[4 * px + 2 * py + pc]

        def copy(k, block, to, src=None):
            return pltpu.make_async_remote_copy(
                src_ref=rows(*block) if src is None else src, dst_ref=rows(*block),
                send_sem=send_sems.at[k], recv_sem=recv_sems.at[k], device_id=to, device_id_type=MESH)

        mine = pltpu.make_async_copy(x_ref, rows(*me), loc_sem)
        mine.start()
        first = [copy(0, me, sibling, src=x_ref)]
        first += [copy(1 + j, me, (*chip, ic), src=x_ref) for j, chip in enumerate(chips)]
        for cp in first:
            cp.start()
        passed = [copy(4 + j, (*chip, ic), sibling) for j, chip in enumerate(chips)]
        for j, chip in enumerate(chips):
            copy(1 + j, (*chip, ic), me).wait_recv()
            passed[j].start()
        copy(0, sibling, me).wait_recv()
        for j, chip in enumerate(chips):
            copy(4 + j, (*chip, 1 - ic), me).wait_recv()
        for cp in first + passed:
            cp.wait_send()
        mine.wait()

    launch()
    return out_ref[...]


def _sum_slots(x, *, name):
    n, R, C = x.shape
    tr = _tile(R, 256, 8)

    def body(x_ref, o_ref):
        acc = x_ref[0].astype(F32)
        for d in range(1, n):
            acc = acc + x_ref[d].astype(F32)
        o_ref[...] = acc

    return pl.pallas_call(
        body, name=name, grid=(R // tr,),
        out_shape=jax.ShapeDtypeStruct((R, C), F32),
        in_specs=[pl.BlockSpec((n, tr, C), lambda i: (0, i, 0))],
        out_specs=pl.BlockSpec((tr, C), lambda i: (i, 0)),
        compiler_params=_params(("parallel",)),
    )(x)


def _colsum(x, *, name):
    R, C = x.shape

    def body(x_ref, o_ref):
        o_ref[...] = jnp.sum(x_ref[...], axis=0, keepdims=True)

    return pl.pallas_call(
        body, name=name, out_shape=jax.ShapeDtypeStruct((1, C), F32),
        in_specs=[pl.BlockSpec((R, C), lambda: (0, 0))], out_specs=pl.BlockSpec((1, C), lambda: (0, 0)),
        compiler_params=_params(),
    )(x)


def _adamw(parts, w, m, v, *, name):
    n, R, C = parts.shape
    tr = _tile(R, 128, 8)
    c1 = 1.0 - ADAM_B1 ** ADAM_STEP
    c2 = 1.0 - ADAM_B2 ** ADAM_STEP

    def body(p_ref, w_ref, m_ref, v_ref, g_ref, d_ref, nm_ref, nv_ref):
        g = p_ref[0].astype(F32)
        for d in range(1, n):
            g = g + p_ref[d].astype(F32)
        mn = ADAM_B1 * m_ref[...] + (1.0 - ADAM_B1) * g
        vn = ADAM_B2 * v_ref[...] + (1.0 - ADAM_B2) * (g * g)
        g_ref[...] = g
        nm_ref[...] = mn
        nv_ref[...] = vn
        d_ref[...] = -ADAM_LR * ((mn / c1) / (jnp.sqrt(vn / c2) + ADAM_EPS) + ADAM_WD * w_ref[...])

    spec = pl.BlockSpec((tr, C), lambda i: (i, 0))
    shp = jax.ShapeDtypeStruct((R, C), F32)
    return pl.pallas_call(
        body, name=name, grid=(R // tr,), out_shape=(shp, shp, shp, shp),
        in_specs=[pl.BlockSpec((n, tr, C), lambda i: (0, i, 0)), spec, spec, spec],
        out_specs=(spec, spec, spec, spec),
        compiler_params=_params(("parallel",)),
    )(parts, w, m, v)


MM_VMEM_BUDGET = 40 * 1024 * 1024
MM_TK_MAX = 2816
MXU_WIDTH = 256


def _divisors(n, quantum, cap):
    return [t for t in range(quantum, min(n, cap) + 1, quantum) if n % t == 0] or [n]


def _mm_tiles(M, N, K, mode, a_bytes, b_bytes, o_bytes, has_add):
    tk = max(_divisors(K, 128, MM_TK_MAX))
    nk = K // tk
    best = None
    for tm in _divisors(M, 128 if mode == "tn" else 8, 1024):
        for tn in _divisors(N, 128, 3072):
            need = 2 * (tm * tk * a_bytes + tk * tn * b_bytes) + 2 * tm * tn * o_bytes + tm * tn * 4
            need += tm * tn * 4 if nk > 1 else 0
            need += 2 * tm * tn * 4 if has_add else 0
            if need > MM_VMEM_BUDGET:
                continue
            score = (tn % MXU_WIDTH == 0 or tn == N, tm * tn, tm)
            if best is None or score > best[0]:
                best = (score, tm, tn)
    assert best is not None, (M, N, K)
    return best[1], best[2], tk


def _mm(a, b, mode, *, name, out_dtype, add=None, after=None):
    if mode == "nn":
        (M, K), (K2, N) = a.shape, b.shape
    elif mode == "nt":
        (M, K), (N, K2) = a.shape, b.shape
    else:
        (K, M), (K2, N) = a.shape, b.shape
    assert K == K2, (name, a.shape, b.shape)
    tm, tn, tk = _mm_tiles(M, N, K, mode, a.dtype.itemsize, b.dtype.itemsize, jnp.dtype(out_dtype).itemsize,
                           add is not None)
    nk = K // tk
    dims = {"nn": NN, "nt": NT, "tn": TN}[mode]

    a_spec = {"nn": pl.BlockSpec((tm, tk), lambda i, j, k: (i, k)),
              "nt": pl.BlockSpec((tm, tk), lambda i, j, k: (i, k)),
              "tn": pl.BlockSpec((tk, tm), lambda i, j, k: (k, i))}[mode]
    b_spec = {"nn": pl.BlockSpec((tk, tn), lambda i, j, k: (k, j)),
              "nt": pl.BlockSpec((tn, tk), lambda i, j, k: (j, k)),
              "tn": pl.BlockSpec((tk, tn), lambda i, j, k: (k, j))}[mode]
    o_spec = pl.BlockSpec((tm, tn), lambda i, j, k: (i, j))

    def body(a_ref, b_ref, *rest):
        rest = list(rest)
        add_ref = rest.pop(0) if add is not None else None
        if after is not None:
            rest.pop(0)
        o_ref = rest.pop(0)
        part = lax.dot_general(a_ref[...].astype(ACT), b_ref[...].astype(ACT), dims, preferred_element_type=F32)

        def finish(r):
            if add is not None:
                r = r + add_ref[...].astype(F32)
            o_ref[...] = r.astype(out_dtype)

        if nk == 1:
            finish(part)
            return
        acc = rest.pop(0)
        k = pl.program_id(2)

        @pl.when(k == 0)
        def _():
            acc[...] = part

        @pl.when(jnp.logical_and(k > 0, k < nk - 1))
        def _():
            acc[...] += part

        @pl.when(k == nk - 1)
        def _():
            finish(acc[...] + part)

    operands = [a, b] + ([] if add is None else [add])
    in_specs = [a_spec, b_spec] + ([] if add is None else [o_spec])
    if after is not None:
        operands.append(after)
        in_specs.append(ANY_SPEC)
    return pl.pallas_call(
        body, name=name, grid=(M // tm, N // tn, nk),
        out_shape=jax.ShapeDtypeStruct((M, N), out_dtype),
        in_specs=in_specs, out_specs=o_spec,
        scratch_shapes=[pltpu.VMEM((tm, tn), F32)] if nk > 1 else [],
        compiler_params=_params(("parallel", "parallel", "arbitrary")),
    )(*operands)


def _row(tr, cols, blk=0):
    return pl.BlockSpec((tr, cols), lambda i: (i, blk))


def _vec(cols):
    return pl.BlockSpec((1, cols), lambda i: (0, 0))


def _rms(xf):
    return lax.rsqrt(jnp.mean(xf * xf, axis=-1, keepdims=True) + EPS)


def _rms_bwd(dxhat, xhat, r):
    return r * (dxhat - xhat * jnp.mean(dxhat * xhat, axis=-1, keepdims=True))


def _acc_rows(ref, val, first):
    s = jnp.sum(val, axis=0, keepdims=True)

    @pl.when(first)
    def _():
        ref[...] = s

    @pl.when(jnp.logical_not(first))
    def _():
        ref[...] += s


def _norm_mod_fwd(x, nw, shift, scale, *, name):
    T, D = x.shape
    tr = _tile(T, 256, 8)

    def body(x_ref, nw_ref, sh_ref, sc_ref, o_ref):
        xf = x_ref[...]
        n = xf * _rms(xf) * nw_ref[...]
        o_ref[...] = (n * (1.0 + sc_ref[...]) + sh_ref[...]).astype(ACT)

    return pl.pallas_call(
        body, name=name, grid=(T // tr,), out_shape=jax.ShapeDtypeStruct((T, D), ACT),
        in_specs=[_row(tr, D), _vec(D), _vec(D), _vec(D)], out_specs=_row(tr, D),
        compiler_params=_params(("parallel",)),
    )(x, nw, shift, scale)


def _resid_norm_mod_fwd(x, o, g, nw, shift, scale, *, name):
    T, D = x.shape
    tr = _tile(T, 256, 8)

    def body(x_ref, o_ref, g_ref, nw_ref, sh_ref, sc_ref, x1_ref, h_ref):
        x1 = x_ref[...] + g_ref[...] * o_ref[...].astype(F32)
        x1_ref[...] = x1
        n = x1 * _rms(x1) * nw_ref[...]
        h_ref[...] = (n * (1.0 + sc_ref[...]) + sh_ref[...]).astype(ACT)

    return pl.pallas_call(
        body, name=name, grid=(T // tr,),
        out_shape=(jax.ShapeDtypeStruct((T, D), F32), jax.ShapeDtypeStruct((T, D), ACT)),
        in_specs=[_row(tr, D), _row(tr, D), _vec(D), _vec(D), _vec(D), _vec(D)],
        out_specs=(_row(tr, D), _row(tr, D)),
        compiler_params=_params(("parallel",)),
    )(x, o, g, nw, shift, scale)


def _final_fwd_bwd(x1, dn, g2, nw, target, *, name):
    T, D = x1.shape
    tr = _tile(T, 256, 8)

    def body(x1_ref, dn_ref, g_ref, nw_ref, t_ref, loss_ref, dx_ref, ddn_ref, dnw_ref, dg_ref):
        first = pl.program_id(0) == 0
        dn_f = dn_ref[...].astype(F32)
        x2 = x1_ref[...] + g_ref[...] * dn_f
        r = _rms(x2)
        xhat = x2 * r
        err = xhat * nw_ref[...] - t_ref[...]
        part = 0.5 * jnp.sum(jnp.mean(err * err, axis=-1, keepdims=True), axis=0, keepdims=True)

        @pl.when(first)
        def _():
            loss_ref[...] = part

        @pl.when(jnp.logical_not(first))
        def _():
            loss_ref[...] += part

        dy = err * (1.0 / D)
        _acc_rows(dnw_ref, dy * xhat, first)
        dx2 = _rms_bwd(dy * nw_ref[...], xhat, r)
        dx_ref[...] = dx2
        ddn_ref[...] = (g_ref[...] * dx2).astype(ACT)
        _acc_rows(dg_ref, dx2 * dn_f, first)

    vec = jax.ShapeDtypeStruct((1, D), F32)
    return pl.pallas_call(
        body, name=name, grid=(T // tr,),
        out_shape=(jax.ShapeDtypeStruct((1, 1), F32), jax.ShapeDtypeStruct((T, D), F32),
                   jax.ShapeDtypeStruct((T, D), ACT), vec, vec),
        in_specs=[_row(tr, D), _row(tr, D), _vec(D), _vec(D), _row(tr, D)],
        out_specs=(pl.BlockSpec((1, 1), lambda i: (0, 0)), _row(tr, D), _row(tr, D), _vec(D), _vec(D)),
        compiler_params=_params(("arbitrary",)),
    )(x1, dn, g2, nw, target)


def _norm_mod_bwd(xin, nw, scale, dh, *, name, dres=None, o=None, g=None):
    T, D = xin.shape
    tr = _tile(T, 256, 8)
    has_res, has_o = dres is not None, o is not None

    def body(*refs):
        refs = list(refs)
        x_ref, nw_ref, sc_ref, dh_ref = refs[:4]
        pos = 4
        dres_ref = o_ref = g_ref = None
        if has_res:
            dres_ref = refs[pos]
            pos += 1
        if has_o:
            o_ref, g_ref = refs[pos], refs[pos + 1]
            pos += 2
        dx_ref, dsh_ref, dsc_ref, dnw_ref = refs[pos:pos + 4]
        pos += 4
        first = pl.program_id(0) == 0
        xf = x_ref[...]
        r = _rms(xf)
        xhat = xf * r
        n = xhat * nw_ref[...]
        dhf = dh_ref[...].astype(F32)
        _acc_rows(dsh_ref, dhf, first)
        _acc_rows(dsc_ref, dhf * n, first)
        dn = dhf * (1.0 + sc_ref[...])
        _acc_rows(dnw_ref, dn * xhat, first)
        dx = _rms_bwd(dn * nw_ref[...], xhat, r)
        if has_res:
            dx = dx + dres_ref[...]
        dx_ref[...] = dx
        if has_o:
            do_ref, dg_ref = refs[pos], refs[pos + 1]
            do_ref[...] = (g_ref[...] * dx).astype(ACT)
            _acc_rows(dg_ref, dx * o_ref[...].astype(F32), first)

    vec = jax.ShapeDtypeStruct((1, D), F32)
    operands = [xin, nw, scale, dh]
    in_specs = [_row(tr, D), _vec(D), _vec(D), _row(tr, D)]
    if has_res:
        operands.append(dres)
        in_specs.append(_row(tr, D))
    if has_o:
        operands += [o, g]
        in_specs += [_row(tr, D), _vec(D)]
    out_shape = [jax.ShapeDtypeStruct((T, D), F32), vec, vec, vec]
    out_specs = [_row(tr, D), _vec(D), _vec(D), _vec(D)]
    if has_o:
        out_shape += [jax.ShapeDtypeStruct((T, D), ACT), vec]
        out_specs += [_row(tr, D), _vec(D)]
    return pl.pallas_call(
        body, name=name, grid=(T // tr,), out_shape=tuple(out_shape),
        in_specs=in_specs, out_specs=tuple(out_specs),
        compiler_params=_params(("arbitrary",)),
    )(*operands)


def _swiglu_fwd(gu, *, name):
    T, F = gu.shape[0], gu.shape[1] // 2
    tr = _tile(T, 256, 8)

    def body(g_ref, u_ref, o_ref):
        o_ref[...] = (_silu(g_ref[...].astype(F32)) * u_ref[...].astype(F32)).astype(ACT)

    return pl.pallas_call(
        body, name=name, grid=(T // tr,), out_shape=jax.ShapeDtypeStruct((T, F), ACT),
        in_specs=[_row(tr, F, 0), _row(tr, F, 1)], out_specs=_row(tr, F),
        compiler_params=_params(("parallel",)),
    )(gu, gu)


def _swiglu_bwd(gu, dact, *, name):
    T, F = gu.shape[0], gu.shape[1] // 2
    tr = _tile(T, 256, 8)

    def body(g_ref, u_ref, d_ref, o_ref):
        gf, uf, df = g_ref[...].astype(F32), u_ref[...].astype(F32), d_ref[...].astype(F32)
        o_ref[:, :F] = (df * uf * _dsilu(gf)).astype(ACT)
        o_ref[:, F:] = (df * _silu(gf)).astype(ACT)

    return pl.pallas_call(
        body, name=name, grid=(T // tr,), out_shape=jax.ShapeDtypeStruct((T, 2 * F), ACT),
        in_specs=[_row(tr, F, 0), _row(tr, F, 1), _row(tr, F)], out_specs=_row(tr, 2 * F),
        compiler_params=_params(("parallel",)),
    )(gu, gu, dact)


def _merge_fwd(ya, yb, gates, *, name):
    T, D = ya.shape
    tr = _tile(T, 256, 8)

    def body(a_ref, b_ref, g_ref, o_ref):
        ga = g_ref[:, :D].astype(F32)
        gb = g_ref[:, D:].astype(F32)
        o_ref[...] = (jax.nn.sigmoid(ga) * a_ref[...].astype(F32)
                      + jax.nn.sigmoid(gb) * b_ref[...].astype(F32)).astype(ACT)

    return pl.pallas_call(
        body, name=name, grid=(T // tr,), out_shape=jax.ShapeDtypeStruct((T, D), ACT),
        in_specs=[_row(tr, D), _row(tr, D), _row(tr, 2 * D)], out_specs=_row(tr, D),
        compiler_params=_params(("parallel",)),
    )(ya, yb, gates)


def _merge_bwd(dmer, ya, yb, gates, *, name):
    T, D = ya.shape
    tr = _tile(T, 256, 8)

    def body(d_ref, a_ref, b_ref, g_ref, da_ref, db_ref, dg_ref):
        d = d_ref[...].astype(F32)
        sa = jax.nn.sigmoid(g_ref[:, :D].astype(F32))
        sb = jax.nn.sigmoid(g_ref[:, D:].astype(F32))
        da_ref[...] = (d * sa).astype(ACT)
        db_ref[...] = (d * sb).astype(ACT)
        dg_ref[:, :D] = (d * a_ref[...].astype(F32) * sa * (1.0 - sa)).astype(ACT)
        dg_ref[:, D:] = (d * b_ref[...].astype(F32) * sb * (1.0 - sb)).astype(ACT)

    shp = jax.ShapeDtypeStruct((T, D), ACT)
    return pl.pallas_call(
        body, name=name, grid=(T // tr,), out_shape=(shp, shp, jax.ShapeDtypeStruct((T, 2 * D), ACT)),
        in_specs=[_row(tr, D), _row(tr, D), _row(tr, D), _row(tr, 2 * D)],
        out_specs=(_row(tr, D), _row(tr, D), _row(tr, 2 * D)),
        compiler_params=_params(("parallel",)),
    )(dmer, ya, yb, gates)


def _gate_norm_fwd(yf, yb, xbc, z, dskip, nw, *, name):
    T, DI = z.shape
    tr = _tile(T, 128, 8)

    def body(yf_ref, yb_ref, xs_ref, z_ref, ds_ref, nw_ref, o_ref):
        y = yf_ref[...].astype(F32) + yb_ref[...].astype(F32) + ds_ref[...] * xs_ref[...].astype(F32)
        gz = y * _silu(z_ref[...].astype(F32))
        o_ref[...] = (gz * _rms(gz) * nw_ref[...]).astype(ACT)

    return pl.pallas_call(
        body, name=name, grid=(T // tr,), out_shape=jax.ShapeDtypeStruct((T, DI), ACT),
        in_specs=[_row(tr, DI), _row(tr, DI), _row(tr, DI), _row(tr, DI), _vec(DI), _vec(DI)],
        out_specs=_row(tr, DI),
        compiler_params=_params(("parallel",)),
    )(yf, yb, xbc, z, dskip, nw)


def _gate_norm_bwd(dout, yf, yb, xbc, z, dskip, nw, *, name):
    T, DI = z.shape
    tr = _tile(T, 128, 8)

    def body(do_ref, yf_ref, yb_ref, xs_ref, z_ref, ds_ref, nw_ref, dy_ref, dz_ref, dxs_ref, dnw_ref, dds_ref):
        first = pl.program_id(0) == 0
        xs = xs_ref[...].astype(F32)
        zf = z_ref[...].astype(F32)
        y = yf_ref[...].astype(F32) + yb_ref[...].astype(F32) + ds_ref[...] * xs
        sz = _silu(zf)
        gz = y * sz
        r = _rms(gz)
        ghat = gz * r
        do = do_ref[...].astype(F32)
        _acc_rows(dnw_ref, do * ghat, first)
        dgz = _rms_bwd(do * nw_ref[...], ghat, r)
        dy = dgz * sz
        dy_ref[...] = dy.astype(ACT)
        dz_ref[...] = (dgz * y * _dsilu(zf)).astype(ACT)
        dxs_ref[...] = (dy * ds_ref[...]).astype(ACT)
        _acc_rows(dds_ref, dy * xs, first)

    shp = jax.ShapeDtypeStruct((T, DI), ACT)
    vec = jax.ShapeDtypeStruct((1, DI), F32)
    return pl.pallas_call(
        body, name=name, grid=(T // tr,), out_shape=(shp, shp, shp, vec, vec),
        in_specs=[_row(tr, DI)] * 5 + [_vec(DI), _vec(DI)],
        out_specs=(_row(tr, DI), _row(tr, DI), _row(tr, DI), _vec(DI), _vec(DI)),
        compiler_params=_params(("arbitrary",)),
    )(dout, yf, yb, xbc, z, dskip, nw)


def _ln_silu_fwd(x, g, b, *, name):
    T, D = x.shape
    tr = _tile(T, 256, 8)

    def body(x_ref, g_ref, b_ref, o_ref):
        xf = x_ref[...].astype(F32)
        xc = xf - jnp.mean(xf, axis=-1, keepdims=True)
        rstd = lax.rsqrt(jnp.mean(xc * xc, axis=-1, keepdims=True) + EPS)
        o_ref[...] = _silu(xc * rstd * g_ref[...] + b_ref[...]).astype(ACT)

    return pl.pallas_call(
        body, name=name, grid=(T // tr,), out_shape=jax.ShapeDtypeStruct((T, D), ACT),
        in_specs=[_row(tr, D), _vec(D), _vec(D)], out_specs=_row(tr, D),
        compiler_params=_params(("parallel",)),
    )(x, g, b)


def _ln_silu_bwd(x, g, b, dcf, *, name):
    T, D = x.shape
    tr = _tile(T, 256, 8)

    def body(x_ref, g_ref, b_ref, d_ref, dx_ref, dg_ref, db_ref):
        first = pl.program_id(0) == 0
        xf = x_ref[...].astype(F32)
        xc = xf - jnp.mean(xf, axis=-1, keepdims=True)
        rstd = lax.rsqrt(jnp.mean(xc * xc, axis=-1, keepdims=True) + EPS)
        xhat = xc * rstd
        dyln = d_ref[...].astype(F32) * _dsilu(xhat * g_ref[...] + b_ref[...])
        _acc_rows(dg_ref, dyln * xhat, first)
        _acc_rows(db_ref, dyln, first)
        dxh = dyln * g_ref[...]
        dx = rstd * (dxh - jnp.mean(dxh, axis=-1, keepdims=True)
                     - xhat * jnp.mean(dxh * xhat, axis=-1, keepdims=True))
        dx_ref[...] = dx.astype(ACT)

    vec = jax.ShapeDtypeStruct((1, D), F32)
    return pl.pallas_call(
        body, name=name, grid=(T // tr,), out_shape=(jax.ShapeDtypeStruct((T, D), ACT), vec, vec),
        in_specs=[_row(tr, D), _vec(D), _vec(D), _row(tr, D)],
        out_specs=(_row(tr, D), _vec(D), _vec(D)),
        compiler_params=_params(("arbitrary",)),
    )(x, g, b, dcf)


CONV_CW = 128
CONV_RT = 256
SEQ_PAD = 8


def _window(ext, off, n):
    if off % 8 == 0:
        return ext[off:off + n]
    return pltpu.roll(ext, ext.shape[0] - off, 0)[:n]


def _sum8(v):
    R, C = v.shape
    return jnp.sum(v.reshape(R // 8, 8, C), axis=0)


def _conv5_silu_fwd(x, w, b, *, name):
    T, C = x.shape
    K = 5
    cw, rt = CONV_CW, _tile(T, CONV_RT, 8)
    half = K // 2

    def body(x_ref, w_ref, b_ref, o_ref, pad):
        zeros = jnp.zeros((SEQ_PAD, cw), F32)
        pad[0:SEQ_PAD, :] = zeros
        pad[T + SEQ_PAD:T + 2 * SEQ_PAD, :] = zeros

        def fill(i, c):
            base = pl.multiple_of(i * rt, rt)
            pad[pl.ds(base + SEQ_PAD, rt), :] = x_ref[pl.ds(base, rt), :].astype(F32)
            return c

        lax.fori_loop(0, T // rt, fill, 0)
        wv = w_ref[...]
        bias = b_ref[...]

        def step(i, c):
            base = pl.multiple_of(i * rt, rt)
            ext = pad[pl.ds(base, rt + 2 * SEQ_PAD), :]
            acc = jnp.zeros((rt, cw), F32) + bias
            for k in range(K):
                acc = acc + wv[k:k + 1, :] * _window(ext, SEQ_PAD + k - half, rt)
            o_ref[pl.ds(base, rt), :] = _silu(acc).astype(ACT)
            return c

        lax.fori_loop(0, T // rt, step, 0)

    return pl.pallas_call(
        body, name=name, grid=(C // cw,), out_shape=jax.ShapeDtypeStruct((T, C), ACT),
        in_specs=[pl.BlockSpec((T, cw), lambda j: (0, j)), pl.BlockSpec((8, cw), lambda j: (0, j)),
                  pl.BlockSpec((1, cw), lambda j: (0, j))],
        out_specs=pl.BlockSpec((T, cw), lambda j: (0, j)),
        scratch_shapes=[pltpu.VMEM((T + 2 * SEQ_PAD, cw), F32)],
        compiler_params=_params(("parallel",)),
    )(x, w, b)


def _conv5_silu_bwd(x, w, b, douts, *, name):
    T, C = x.shape
    K = 5
    cw, rt = CONV_CW, _tile(T, CONV_RT, 8)
    half = K // 2
    tiles = [d.shape[1] // cw for d in douts]
    firsts = [sum(tiles[:i]) for i in range(len(douts))]
    assert sum(tiles) * cw == C

    def body(x_ref, w_ref, b_ref, *rest):
        d_refs = rest[:len(douts)]
        dx_ref, dw_ref, db_ref, pad, dpad, wacc = rest[len(douts):]
        tile = pl.program_id(0)

        def dout_rows(base):
            d = d_refs[-1][pl.ds(base, rt), :]
            for i in range(len(douts) - 2, -1, -1):
                d = jnp.where(tile < firsts[i + 1], d_refs[i][pl.ds(base, rt), :], d)
            return d

        zeros = jnp.zeros((SEQ_PAD, cw), F32)
        for p in (pad, dpad):
            p[0:SEQ_PAD, :] = zeros
            p[T + SEQ_PAD:T + 2 * SEQ_PAD, :] = zeros
        wacc[...] = jnp.zeros_like(wacc)

        def fill(i, c):
            base = pl.multiple_of(i * rt, rt)
            pad[pl.ds(base + SEQ_PAD, rt), :] = x_ref[pl.ds(base, rt), :].astype(F32)
            return c

        lax.fori_loop(0, T // rt, fill, 0)
        wv = w_ref[...]
        bias = b_ref[...]

        def step1(i, c):
            base = pl.multiple_of(i * rt, rt)
            ext = pad[pl.ds(base, rt + 2 * SEQ_PAD), :]
            wins = [_window(ext, SEQ_PAD + k - half, rt) for k in range(K)]
            pre = jnp.zeros((rt, cw), F32) + bias
            for k in range(K):
                pre = pre + wv[k:k + 1, :] * wins[k]
            dpre = dout_rows(base).astype(F32) * _dsilu(pre)
            dpad[pl.ds(base + SEQ_PAD, rt), :] = dpre
            for k in range(K):
                wacc[k] += _sum8(dpre * wins[k])
            wacc[K] += _sum8(dpre)
            return c

        lax.fori_loop(0, T // rt, step1, 0)

        def step2(i, c):
            base = pl.multiple_of(i * rt, rt)
            ext = dpad[pl.ds(base, rt + 2 * SEQ_PAD), :]
            acc = jnp.zeros((rt, cw), F32)
            for k in range(K):
                acc = acc + wv[k:k + 1, :] * _window(ext, SEQ_PAD - (k - half), rt)
            dx_ref[pl.ds(base, rt), :] = acc.astype(ACT)
            return c

        lax.fori_loop(0, T // rt, step2, 0)
        rows = [jnp.sum(wacc[k], axis=0, keepdims=True) for k in range(K)]
        rows += [jnp.zeros((1, cw), F32)] * (8 - K)
        dw_ref[...] = jnp.concatenate(rows, axis=0)
        db_ref[...] = jnp.sum(wacc[K], axis=0, keepdims=True)

    return pl.pallas_call(
        body, name=name, grid=(C // cw,),
        out_shape=(jax.ShapeDtypeStruct((T, C), ACT), jax.ShapeDtypeStruct((8, C), F32),
                   jax.ShapeDtypeStruct((1, C), F32)),
        in_specs=[pl.BlockSpec((T, cw), lambda j: (0, j)), pl.BlockSpec((8, cw), lambda j: (0, j)),
                  pl.BlockSpec((1, cw), lambda j: (0, j))]
        + [pl.BlockSpec((T, cw), functools.partial(lambda j, first, n: (0, jnp.clip(j - first, 0, n - 1)),
                                                   first=firsts[i], n=tiles[i])) for i in range(len(douts))],
        out_specs=(pl.BlockSpec((T, cw), lambda j: (0, j)), pl.BlockSpec((8, cw), lambda j: (0, j)),
                   pl.BlockSpec((1, cw), lambda j: (0, j))),
        scratch_shapes=[pltpu.VMEM((T + 2 * SEQ_PAD, cw), F32), pltpu.VMEM((T + 2 * SEQ_PAD, cw), F32),
                        pltpu.VMEM((K + 1, 8, cw), F32)],
        compiler_params=_params(("parallel",)),
    )(x, w, b, *douts)


def _glu_conv_fwd(u, v, w, b, *, name):
    T, C = u.shape
    K = 31
    KP = w.shape[0]
    cw, rt = CONV_CW, _tile(T, CONV_RT, GRID_W)
    half = K // 2
    P = half * GRID_W

    def body(u_ref, v_ref, w_ref, b_ref, o_ref, pad):
        zeros = jnp.zeros((P, cw), F32)
        pad[0:P, :] = zeros
        pad[T + P:T + 2 * P, :] = zeros

        def fill(i, c):
            base = pl.multiple_of(i * rt, rt)
            uf = u_ref[pl.ds(base, rt), :].astype(F32)
            vf = v_ref[pl.ds(base, rt), :].astype(F32)
            pad[pl.ds(base + P, rt), :] = uf * jax.nn.sigmoid(vf)
            return c

        lax.fori_loop(0, T // rt, fill, 0)
        wv = w_ref[...]
        bias = b_ref[...]

        def step(i, c):
            base = pl.multiple_of(i * rt, rt)
            acc = jnp.zeros((rt, cw), F32) + bias
            for k in range(K):
                acc = acc + wv[k:k + 1, :] * pad[pl.ds(base + k * GRID_W, rt), :]
            o_ref[pl.ds(base, rt), :] = acc.astype(ACT)
            return c

        lax.fori_loop(0, T // rt, step, 0)

    col = pl.BlockSpec((T, cw), lambda j: (0, j))
    return pl.pallas_call(
        body, name=name, grid=(C // cw,), out_shape=jax.ShapeDtypeStruct((T, C), ACT),
        in_specs=[col, col, pl.BlockSpec((KP, cw), lambda j: (0, j)), pl.BlockSpec((1, cw), lambda j: (0, j))],
        out_specs=col,
        scratch_shapes=[pltpu.VMEM((T + 2 * P, cw), F32)],
        compiler_params=_params(("parallel",)),
    )(u, v, w, b)


def _glu_conv_bwd(u, v, w, dout, *, name):
    T, C = u.shape
    K = 31
    KP = w.shape[0]
    cw, rt = CONV_CW, _tile(T, CONV_RT, GRID_W)
    half = K // 2
    P = half * GRID_W

    def body(u_ref, v_ref, w_ref, d_ref, du_ref, dv_ref, dw_ref, db_ref, pad, dpad, wacc):
        zeros = jnp.zeros((P, cw), F32)
        for p in (pad, dpad):
            p[0:P, :] = zeros
            p[T + P:T + 2 * P, :] = zeros
        wacc[...] = jnp.zeros_like(wacc)

        def fill(i, c):
            base = pl.multiple_of(i * rt, rt)
            uf = u_ref[pl.ds(base, rt), :].astype(F32)
            vf = v_ref[pl.ds(base, rt), :].astype(F32)
            pad[pl.ds(base + P, rt), :] = uf * jax.nn.sigmoid(vf)
            dpad[pl.ds(base + P, rt), :] = d_ref[pl.ds(base, rt), :].astype(F32)
            return c

        lax.fori_loop(0, T // rt, fill, 0)
        wv = w_ref[...]

        def step(i, c):
            base = pl.multiple_of(i * rt, rt)
            d = dpad[pl.ds(base + P, rt), :]
            dg = jnp.zeros((rt, cw), F32)
            for k in range(K):
                wacc[k] += _sum8(d * pad[pl.ds(base + k * GRID_W, rt), :])
                dg = dg + wv[k:k + 1, :] * dpad[pl.ds(base + (K - 1 - k) * GRID_W, rt), :]
            wacc[K] += _sum8(d)
            uf = u_ref[pl.ds(base, rt), :].astype(F32)
            sv = jax.nn.sigmoid(v_ref[pl.ds(base, rt), :].astype(F32))
            du_ref[pl.ds(base, rt), :] = (dg * sv).astype(ACT)
            dv_ref[pl.ds(base, rt), :] = (dg * uf * sv * (1.0 - sv)).astype(ACT)
            return c

        lax.fori_loop(0, T // rt, step, 0)
        rows = [jnp.sum(wacc[k], axis=0, keepdims=True) for k in range(K)]
        rows += [jnp.zeros((1, cw), F32)] * (KP - K)
        dw_ref[...] = jnp.concatenate(rows, axis=0)
        db_ref[...] = jnp.sum(wacc[K], axis=0, keepdims=True)

    col = pl.BlockSpec((T, cw), lambda j: (0, j))
    shp = jax.ShapeDtypeStruct((T, C), ACT)
    return pl.pallas_call(
        body, name=name, grid=(C // cw,),
        out_shape=(shp, shp, jax.ShapeDtypeStruct((KP, C), F32), jax.ShapeDtypeStruct((1, C), F32)),
        in_specs=[col, col, pl.BlockSpec((KP, cw), lambda j: (0, j)), col],
        out_specs=(col, col, pl.BlockSpec((KP, cw), lambda j: (0, j)), pl.BlockSpec((1, cw), lambda j: (0, j))),
        scratch_shapes=[pltpu.VMEM((T + 2 * P, cw), F32), pltpu.VMEM((T + 2 * P, cw), F32),
                        pltpu.VMEM((K + 1, 8, cw), F32)],
        compiler_params=_params(("parallel",)),
    )(u, v, w, dout)


def _dt_fwd(rawT, bias, *, name):
    H2, T = rawT.shape
    tc = _tile(T, 2048, 128)

    def body(r_ref, b_ref, o_ref):
        v = r_ref[...] + b_ref[...]
        o_ref[...] = jnp.maximum(v, 0.0) + jnp.log(1.0 + jnp.exp(-jnp.abs(v)))

    return pl.pallas_call(
        body, name=name, grid=(T // tc,), out_shape=jax.ShapeDtypeStruct((H2, T), F32),
        in_specs=[pl.BlockSpec((H2, tc), lambda i: (0, i)), pl.BlockSpec((H2, 1), lambda i: (0, 0))],
        out_specs=pl.BlockSpec((H2, tc), lambda i: (0, i)),
        compiler_params=_params(("parallel",)),
    )(rawT, bias)


def _dt_bwd(rawT, bias, ddtT, *, name):
    H2, T = rawT.shape
    tc = _tile(T, 2048, 128)

    def body(r_ref, b_ref, d_ref, o_ref, db_ref):
        first = pl.program_id(0) == 0
        dr = d_ref[...] * jax.nn.sigmoid(r_ref[...] + b_ref[...])
        o_ref[...] = dr
        s = jnp.sum(dr, axis=1, keepdims=True)

        @pl.when(first)
        def _():
            db_ref[...] = s

        @pl.when(jnp.logical_not(first))
        def _():
            db_ref[...] += s

    return pl.pallas_call(
        body, name=name, grid=(T // tc,),
        out_shape=(jax.ShapeDtypeStruct((H2, T), F32), jax.ShapeDtypeStruct((H2, 1), F32)),
        in_specs=[pl.BlockSpec((H2, tc), lambda i: (0, i)), pl.BlockSpec((H2, 1), lambda i: (0, 0)),
                  pl.BlockSpec((H2, tc), lambda i: (0, i))],
        out_specs=(pl.BlockSpec((H2, tc), lambda i: (0, i)), pl.BlockSpec((H2, 1), lambda i: (0, 0))),
        compiler_params=_params(("arbitrary",)),
    )(rawT, bias, ddtT)


def _ssd_common(dtT, a, reverse):
    J, Q = dtT.shape
    li = lax.broadcasted_iota(jnp.int32, (Q, Q), 0)
    si = lax.broadcasted_iota(jnp.int32, (Q, Q), 1)
    mask = (si >= li) if reverse else (si <= li)
    Mf = mask.astype(F32)
    daT = dtT * a
    csT = lax.dot_general(daT, Mf, NT, precision=HIGHEST, preferred_element_type=F32)
    last = 0 if reverse else Q - 1
    totT = csT[:, last:last + 1]
    return mask, Mf, csT, totT, last


def _to_cols(rows):
    R, Q = rows.shape
    if R < 128:
        rows = jnp.concatenate([rows, jnp.zeros((128 - R, Q), F32)], axis=0)
    return rows.T


def _ssd_fwd(xbc, dtT, a, h0, *, reverse, name, di):
    T = xbc.shape[0]
    G, JP, N = h0.shape
    J, P, Q = JP // HEAD_DIM, HEAD_DIM, CHUNK
    nc = T // Q
    QW = 256
    HQ = QW // P

    def ci(k):
        return nc - 1 - k if reverse else k

    GB = SSD_GROUPS_PER_STEP

    def body(x_ref, b_ref, c_ref, dt_ref, a_ref, h0_ref, y_ref, hs_ref, hl_ref, h_scr):
        k = pl.program_id(1)

        @pl.when(k == 0)
        def _():
            h_scr[...] = h0_ref[...]

        lh = lax.broadcasted_iota(jnp.int32, (Q, QW), 1) // P

        def scale_heads(vT, rowsT):
            return jnp.concatenate([vT[j * P:(j + 1) * P, :] * rowsT[j:j + 1, :] for j in range(J)], axis=0)

        for gi in range(GB):
            h = h_scr[gi]
            hs_ref[0, gi] = h
            Xb = x_ref[:, gi * JP:(gi + 1) * JP]
            Bm, Cm = b_ref[:, gi * N:(gi + 1) * N], c_ref[:, gi * N:(gi + 1) * N]
            dtT_v = dt_ref[gi * J:(gi + 1) * J, :]
            mask, _, csT, totT, _ = _ssd_common(dtT_v, a_ref[gi * J:(gi + 1) * J, :], reverse)
            cs = _to_cols(csT)
            CB = lax.dot_general(Cm, Bm, NT, preferred_element_type=F32)
            yoT = lax.dot_general(h.astype(ACT), Cm, NT, preferred_element_type=F32)
            yo = scale_heads(yoT, jnp.exp(csT)).T
            for q in range(JP // QW):
                xq = Xb[:, q * QW:(q + 1) * QW]
                acc = yo[:, q * QW:(q + 1) * QW]
                for jj in range(HQ):
                    j = q * HQ + jj
                    seg = cs[:, j:j + 1] - csT[j:j + 1, :]
                    Mj = (CB * jnp.exp(jnp.where(mask, seg, NEG)) * dtT_v[j:j + 1, :]).astype(ACT)
                    acc = acc + jnp.dot(Mj, jnp.where(lh == jj, xq, jnp.zeros_like(xq)),
                                        preferred_element_type=F32)
                y_ref[:, gi * JP + q * QW:gi * JP + (q + 1) * QW] = acc.astype(ACT)
            xwT = scale_heads(Xb.astype(F32).T, dtT_v * jnp.exp(totT - csT)).astype(ACT)
            upd = jnp.dot(xwT, Bm, preferred_element_type=F32)
            for j in range(J):
                rows = slice(j * P, (j + 1) * P)
                h_scr[gi, rows, :] = h[rows, :] * jnp.exp(totT[j:j + 1, :]) + upd[rows, :]

        @pl.when(k == nc - 1)
        def _():
            hl_ref[...] = h_scr[...]

    GN = G * N
    return pl.pallas_call(
        body, name=name, grid=(G // GB, nc),
        out_shape=(jax.ShapeDtypeStruct((T, di), ACT), jax.ShapeDtypeStruct((nc, G, JP, N), F32),
                   jax.ShapeDtypeStruct((G, JP, N), F32)),
        in_specs=[pl.BlockSpec((Q, GB * JP), lambda g, k: (ci(k), g)),
                  pl.BlockSpec((Q, GB * N), lambda g, k: (ci(k), di // (GB * N) + g)),
                  pl.BlockSpec((Q, GB * N), lambda g, k: (ci(k), (di + GN) // (GB * N) + g)),
                  pl.BlockSpec((GB * J, Q), lambda g, k: (g, ci(k))),
                  pl.BlockSpec((GB * J, 1), lambda g, k: (g, 0)),
                  pl.BlockSpec((GB, JP, N), lambda g, k: (g, 0, 0))],
        out_specs=(pl.BlockSpec((Q, GB * JP), lambda g, k: (ci(k), g)),
                   pl.BlockSpec((1, GB, JP, N), lambda g, k: (ci(k), g, 0, 0)),
                   pl.BlockSpec((GB, JP, N), lambda g, k: (g, 0, 0))),
        scratch_shapes=[pltpu.VMEM((GB, JP, N), F32)],
        compiler_params=_params(("arbitrary", "arbitrary")),
    )(xbc, xbc, xbc, dtT, a, h0)


def _ssd_bwd(xbc, dtT, a, dy, hs, dh_last, add, *, reverse, name, di):
    T = xbc.shape[0]
    G, JP, N = dh_last.shape
    J, P, Q = JP // HEAD_DIM, HEAD_DIM, CHUNK
    nc = T // Q
    QW = 256
    HQ = QW // P
    has_add = add is not None

    def ci(k):
        return k if reverse else nc - 1 - k

    GB = SSD_GROUPS_PER_STEP

    def body(*refs):
        for gi in range(GB):
            wide = lambda r, w: r.at[:, pl.ds(gi * w, w)]
            x_ref, b_ref, c_ref, dt_ref, a_ref, dy_ref, hs_ref, dhl_ref = refs[:8]
            views = [wide(x_ref, JP), wide(b_ref, N), wide(c_ref, N), dt_ref.at[pl.ds(gi * J, J)],
                     a_ref.at[pl.ds(gi * J, J)], wide(dy_ref, JP), hs_ref.at[:, pl.ds(gi, 1)],
                     dhl_ref.at[pl.ds(gi, 1)]]
            rest = refs[8:]
            if has_add:
                views += [wide(rest[0], JP), wide(rest[1], N), wide(rest[2], N)]
                rest = rest[3:]
            dx_ref, db_ref, dc_ref, ddt_ref, da_ref, dh0_ref, dh_scr = rest
            views += [wide(dx_ref, JP), wide(db_ref, N), wide(dc_ref, N), ddt_ref.at[pl.ds(gi * J, J)],
                      da_ref.at[pl.ds(gi * J, J)], dh0_ref.at[pl.ds(gi, 1)], dh_scr.at[gi]]
            group_body(*views)

    def group_body(x_ref, b_ref, c_ref, dt_ref, a_ref, dy_ref, hs_ref, dhl_ref, *rest):
        if has_add:
            adx_ref, adb_ref, adc_ref = rest[:3]
            rest = rest[3:]
        dx_ref, db_ref, dc_ref, ddt_ref, da_ref, dh0_ref, dh_scr = rest
        k = pl.program_id(1)

        @pl.when(k == 0)
        def _():
            dh_scr[...] = dhl_ref[0]
            da_ref[...] = jnp.zeros_like(da_ref)

        def scale_heads(vT, rowsT):
            return jnp.concatenate([vT[j * P:(j + 1) * P, :] * rowsT[j:j + 1, :] for j in range(J)], axis=0)

        def head_sums(vT):
            return jnp.sum(vT.reshape(J, P, Q), axis=1)

        dH = dh_scr[...]
        h = hs_ref[0, 0]
        Bm, Cm = b_ref[...], c_ref[...]
        dtT_v = dt_ref[...]
        a_v = a_ref[...]
        mask, Mf, csT, totT, last = _ssd_common(dtT_v, a_v, reverse)
        ecsT = jnp.exp(csT)
        toendT = jnp.exp(totT - csT)
        cs = _to_cols(csT)
        dYb = dy_ref[...]
        XT = x_ref[...].astype(F32).T
        dYT = dYb.astype(F32).T
        xdtT = scale_heads(XT, dtT_v).astype(ACT)
        dYT_b = dYT.astype(ACT)
        dYeT = scale_heads(dYT, ecsT).astype(ACT)
        h_b = h.astype(ACT)
        dH_b = dH.astype(ACT)
        CB = lax.dot_general(Cm, Bm, NT, preferred_element_type=F32)
        dxdt_offT = scale_heads(lax.dot_general(dH_b, Bm, NT, preferred_element_type=F32), toendT)
        dCB = jnp.zeros((Q, Q), F32)
        lh = lax.broadcasted_iota(jnp.int32, (Q, QW), 1) // P
        sh = lax.broadcasted_iota(jnp.int32, (QW, Q), 0) // P
        lane_q = lax.broadcasted_iota(jnp.int32, (Q, Q), 1)
        sub_j = lax.broadcasted_iota(jnp.int32, (J, Q), 0)
        e_rows = jnp.zeros((Q, Q), F32)
        e_cols = jnp.zeros((J, Q), F32)
        diag = []
        for q in range(JP // QW):
            xq = xdtT[q * QW:(q + 1) * QW, :]
            dyq = dYb[:, q * QW:(q + 1) * QW]
            dyTq = dYT_b[q * QW:(q + 1) * QW, :]
            acc = jnp.zeros((QW, Q), F32)
            for jj in range(HQ):
                j = q * HQ + jj
                seg = cs[:, j:j + 1] - csT[j:j + 1, :]
                L = jnp.exp(jnp.where(mask, seg, NEG))
                Mf_j = CB * L
                dyj = jnp.where(lh == jj, dyq, jnp.zeros_like(dyq))
                dyTj = jnp.where(sh == jj, dyTq, jnp.zeros_like(dyTq))
                acc = acc + jnp.dot(dyTj, Mf_j.astype(ACT), preferred_element_type=F32)
                dM = jnp.dot(dyj, xq, preferred_element_type=F32)
                dCB = dCB + dM * L
                E = dM * Mf_j
                e_rows = jnp.where(lane_q == j, jnp.sum(E, axis=1, keepdims=True), e_rows)
                e_cols = jnp.where(sub_j == j, jnp.sum(E, axis=0, keepdims=True), e_cols)
            diag.append(acc)
        dxdtT = dxdt_offT + jnp.concatenate(diag, axis=0)
        dCB_b = dCB.astype(ACT)
        dC = (jnp.dot(dCB_b, Bm, preferred_element_type=F32)
              + lax.dot_general(dYeT, h_b, TN, preferred_element_type=F32))
        xwT = scale_heads(XT, dtT_v * toendT).astype(ACT)
        dB = (lax.dot_general(dCB_b, Cm, TN, preferred_element_type=F32)
              + lax.dot_general(xwT, dH_b, TN, preferred_element_type=F32))
        dHc = jnp.dot(dYeT, Cm, preferred_element_type=F32)
        for j in range(J):
            rows = slice(j * P, (j + 1) * P)
            dh_scr[rows, :] = dH[rows, :] * jnp.exp(totT[j:j + 1, :]) + dHc[rows, :]
        dh0_ref[0] = dh_scr[...]

        yoT = scale_heads(lax.dot_general(h_b, Cm, NT, preferred_element_type=F32), ecsT)
        RT_ = head_sums(dYT * yoT)
        UT_ = head_sums(XT * dxdtT)
        UoT = head_sums(XT * dxdt_offT)
        hsum = jnp.sum(jnp.sum((dH * h).reshape(J, P, N), axis=1), axis=1, keepdims=True)
        dtot = jnp.sum(UoT * dtT_v, axis=1, keepdims=True) + jnp.exp(totT) * hsum
        lane = lax.broadcasted_iota(jnp.int32, (J, Q), 1)
        dcsT = e_rows.T[0:J] - e_cols + RT_ - UoT * dtT_v + jnp.where(lane == last, dtot, 0.0)
        ddaT = jnp.dot(dcsT, Mf, precision=HIGHEST, preferred_element_type=F32)
        ddt_ref[...] = ddaT * a_v + UT_
        da_ref[...] += ddaT * dtT_v
        dX = scale_heads(dxdtT, dtT_v).T
        if has_add:
            dX = dX + adx_ref[...].astype(F32)
            dB = dB + adb_ref[...].astype(F32)
            dC = dC + adc_ref[...].astype(F32)
        dx_ref[...] = dX.astype(ACT)
        db_ref[...] = dB.astype(ACT)
        dc_ref[...] = dC.astype(ACT)

    GN = G * N
    xspec = pl.BlockSpec((Q, GB * JP), lambda g, k: (ci(k), g))
    nspec = pl.BlockSpec((Q, GB * N), lambda g, k: (ci(k), g))
    hspec = pl.BlockSpec((GB, JP, N), lambda g, k: (g, 0, 0))
    in_specs = [xspec,
                pl.BlockSpec((Q, GB * N), lambda g, k: (ci(k), di // (GB * N) + g)),
                pl.BlockSpec((Q, GB * N), lambda g, k: (ci(k), (di + GN) // (GB * N) + g)),
                pl.BlockSpec((GB * J, Q), lambda g, k: (g, ci(k))),
                pl.BlockSpec((GB * J, 1), lambda g, k: (g, 0)),
                xspec,
                pl.BlockSpec((1, GB, JP, N), lambda g, k: (ci(k), g, 0, 0)),
                hspec]
    operands = [xbc, xbc, xbc, dtT, a, dy, hs, dh_last]
    if has_add:
        in_specs += [xspec, nspec, nspec]
        operands += list(add)
    H = G * J
    return pl.pallas_call(
        body, name=name, grid=(G // GB, nc),
        out_shape=(jax.ShapeDtypeStruct((T, di), ACT), jax.ShapeDtypeStruct((T, GN), ACT),
                   jax.ShapeDtypeStruct((T, GN), ACT), jax.ShapeDtypeStruct((H, T), F32),
                   jax.ShapeDtypeStruct((H, Q), F32), jax.ShapeDtypeStruct((G, JP, N), F32)),
        in_specs=in_specs,
        out_specs=(xspec, nspec, nspec,
                   pl.BlockSpec((GB * J, Q), lambda g, k: (g, ci(k))),
                   pl.BlockSpec((GB * J, Q), lambda g, k: (g, 0)),
                   hspec),
        scratch_shapes=[pltpu.VMEM((GB, JP, N), F32)],
        compiler_params=_params(("arbitrary", "arbitrary")),
    )(*operands)


def _mod_fwd(craw, w, b, *, name):
    R, D = craw.shape
    NL = w.shape[1]
    tn = _tile(NL, 512, 128)

    def body(c_ref, w_ref, b_ref, o_ref):
        o_ref[...] = jnp.dot(_silu(c_ref[...]), w_ref[...], preferred_element_type=F32) + b_ref[...]

    return pl.pallas_call(
        body, name=name, grid=(NL // tn,), out_shape=jax.ShapeDtypeStruct((R, NL), F32),
        in_specs=[pl.BlockSpec((R, D), lambda j: (0, 0)), pl.BlockSpec((D, tn), lambda j: (0, j)),
                  pl.BlockSpec((1, tn), lambda j: (0, j))],
        out_specs=pl.BlockSpec((R, tn), lambda j: (0, j)),
        compiler_params=_params(("parallel",)),
    )(craw, w, b)


def _mod_bwd(craw, w, dm, *, name):
    R, D = craw.shape
    NL = w.shape[1]
    tn = _tile(NL, 512, 128)

    def body(c_ref, w_ref, dm_ref, dw_ref, dc_ref):
        first = pl.program_id(0) == 0
        cf = c_ref[...]
        dmv = dm_ref[...]
        dw_ref[...] = lax.dot_general(_silu(cf), dmv, TN, preferred_element_type=F32)
        part = lax.dot_general(dmv, w_ref[...], NT, preferred_element_type=F32) * _dsilu(cf)

        @pl.when(first)
        def _():
            dc_ref[...] = part

        @pl.when(jnp.logical_not(first))
        def _():
            dc_ref[...] += part

    return pl.pallas_call(
        body, name=name, grid=(NL // tn,),
        out_shape=(jax.ShapeDtypeStruct((D, NL), F32), jax.ShapeDtypeStruct((R, D), F32)),
        in_specs=[pl.BlockSpec((R, D), lambda j: (0, 0)), pl.BlockSpec((D, tn), lambda j: (0, j)),
                  pl.BlockSpec((R, tn), lambda j: (0, j))],
        out_specs=(pl.BlockSpec((D, tn), lambda j: (0, j)), pl.BlockSpec((R, D), lambda j: (0, 0))),
        compiler_params=_params(("arbitrary",)),
    )(craw, w, dm)


def _pad_rows(a, rows):
    return jnp.concatenate([a, jnp.zeros((rows - a.shape[0],) + a.shape[1:], a.dtype)], axis=0)


def _cols_whole(g):
    return jnp.transpose(g, (1, 0, 2)).reshape(g.shape[1], N_DEV * g.shape[2])


def _rows_whole(g):
    return g.reshape(N_DEV * g.shape[1], g.shape[2])


def _col_blocks(full):
    K, n8 = full.shape
    return jnp.transpose(full.reshape(K, N_DEV, n8 // N_DEV), (1, 0, 2)).astype(ACT)


def _row_blocks(full):
    K8, n = full.shape
    return full.reshape(N_DEV, K8 // N_DEV, n).astype(ACT)


def kernel(x, c, ctx, c_ctx, w_mod, b_mod, norm_mix, w_in, ssm_conv_w, ssm_conv_b, dt_bias, a_log, d_skip, ssm_norm, cf_conv_w, cf_conv_b, cf_ln_g, cf_ln_b, w_proj_a, w_proj_b, w_out, norm_ffn, w_ffn_gate, w_ffn_up, w_ffn_down, norm_final, loss_target, m_c_ctx, m_w_mod, m_b_mod, m_norm_mix, m_w_in, m_ssm_conv_w, m_ssm_conv_b, m_dt_bias, m_a_log, m_d_skip, m_ssm_norm, m_cf_conv_w, m_cf_conv_b, m_cf_ln_g, m_cf_ln_b, m_w_proj_a, m_w_proj_b, m_w_out, m_norm_ffn, m_w_ffn_gate, m_w_ffn_up, m_w_ffn_down, m_norm_final, v_c_ctx, v_w_mod, v_b_mod, v_norm_mix, v_w_in, v_ssm_conv_w, v_ssm_conv_b, v_dt_bias, v_a_log, v_d_skip, v_ssm_norm, v_cf_conv_w, v_cf_conv_b, v_cf_ln_g, v_cf_ln_b, v_w_proj_a, v_w_proj_b, v_w_out, v_norm_ffn, v_w_ffn_gate, v_w_ffn_up, v_w_ffn_down, v_norm_final):
    args = dict(locals())
    me = 4 * lax.axis_index("x") + 2 * lax.axis_index("y") + lax.axis_index("c")
    T, D = x.shape[1], x.shape[2]
    DI = ssm_norm.shape[1]
    H = DI // HEAD_DIM
    G, J, N = GROUPS, H // GROUPS, STATE
    JP = J * HEAD_DIM
    GN = G * N
    CONV = DI + 2 * GN
    x0 = x[0]
    ctx0 = ctx[0]
    target = loss_target[0]

    got_in = _gather_seq(w_in[0].astype(ACT), name="gather_w_in", collective_id=0)
    k5 = ssm_conv_w.shape[1]
    k31 = cf_conv_w.shape[1]
    n5, n31 = 8 * ssm_conv_w.shape[2], 32 * cf_conv_w.shape[2]
    early = jnp.concatenate([c[0], _pad_rows(ssm_conv_w[0], 8).reshape(-1), _pad_rows(cf_conv_w[0], 32).reshape(-1)])
    rows_early = -(-early.shape[0] // 1024) * 8
    early = jnp.concatenate([early, jnp.zeros((rows_early * 128 - early.shape[0],), F32)]).reshape(rows_early, 128)
    early = _exchange(early, name="gather_early", gather=True).reshape(N_DEV, rows_early * 128)
    cw5 = jnp.transpose(early[:, D:D + n5].reshape(N_DEV, 8, n5 // 8), (1, 0, 2)).reshape(8, CONV)
    cw31 = jnp.transpose(early[:, D + n5:D + n5 + n31].reshape(N_DEV, 32, n31 // 32), (1, 0, 2)).reshape(32, D)

    c_all = early[:, :D]
    craw = jnp.concatenate([c_all, c_ctx[None, :], jnp.zeros((7, D), F32)], axis=0)
    NL = w_mod.shape[2]
    b_loc = lax.dynamic_slice(b_mod, (0, me * NL), (1, NL))
    m_loc = _mod_fwd(craw, w_mod[0], b_loc, name="mod_fwd")
    m_all = jnp.transpose(_exchange(m_loc, name="gather_mod", gather=True), (1, 0, 2)).reshape(16, N_DEV * NL)
    m_me = lax.dynamic_slice(m_all, (me, 0), (1, 6 * D))
    sh1, sc1, g1, sh2, sc2, g2 = [m_me[:, i * D:(i + 1) * D] for i in range(6)]
    csh1, csc1 = m_all[8:9, 0:D], m_all[8:9, D:2 * D]

    win = _cols_whole(got_in)
    o_xbc, o_dt, o_glu, o_gates = DI, DI + CONV, DI + CONV + 2 * H, DI + CONV + 2 * H + 2 * D
    w_z, w_xbc, w_dt = win[:, :o_xbc], win[:, o_xbc:o_dt], win[:, o_dt:o_glu]
    w_u, w_v, w_gates = win[:, o_glu:o_glu + D], win[:, o_glu + D:o_gates], win[:, o_gates:]
    def gather_behind(shard, behind, name, cid):
        zero = (behind[(0,) * behind.ndim] * 0).astype(ACT)
        return _gather_seq(shard.astype(ACT) + zero, name=name, collective_id=cid)

    ra, rb = w_proj_a.shape[1], w_proj_b.shape[1]
    got_proj = gather_behind(jnp.concatenate([w_proj_a[0], w_proj_b[0], w_out[0]], axis=0), got_in, "gather_w_proj", 1)
    w_pa = _rows_whole(got_proj[:, :ra])
    w_pb = _rows_whole(got_proj[:, ra:ra + rb])
    w_o = _rows_whole(got_proj[:, ra + rb:])

    a_neg = -jnp.exp(a_log[0])
    a_f, a_b = a_neg[0][:, None], a_neg[1][:, None]
    dtb = dt_bias[0].reshape(2 * H, 1)
    dskip_e = jnp.repeat(d_skip[0], HEAD_DIM)[None, :]

    def front(h, tag, full, after=None):
        out = {}
        out["xbc_raw"] = _mm(h, w_xbc, "nn", name="mm_xbc_" + tag, out_dtype=ACT, after=after)
        dt_raw = _mm(h, w_dt, "nn", name="mm_dt_" + tag, out_dtype=F32)
        out["rawT"] = dt_raw.T
        if full:
            out["z"] = _mm(h, w_z, "nn", name="mm_z_" + tag, out_dtype=ACT)
            out["u"] = _mm(h, w_u, "nn", name="mm_u_" + tag, out_dtype=ACT)
            out["v"] = _mm(h, w_v, "nn", name="mm_v_" + tag, out_dtype=ACT)
            out["gates"] = _mm(h, w_gates, "nn", name="mm_gates_" + tag, out_dtype=ACT)
        out["xbc"] = _conv5_silu_fwd(out["xbc_raw"], cw5, ssm_conv_b, name="conv5_fwd_" + tag)
        out["dtT"] = _dt_fwd(out["rawT"], dtb, name="dt_fwd_" + tag)
        return out

    hc = _norm_mod_fwd(ctx0, norm_mix, csh1, csc1, name="norm_mod_ctx")
    fc = front(hc, "ctx", False)
    zero_state = jnp.zeros((G, JP, N), F32)
    _, hs_cf, h_f = _ssd_fwd(fc["xbc"], fc["dtT"][:H], a_f, zero_state, reverse=False, name="ssd_fwd_ctx_f", di=DI)
    _, hs_cb, h_b = _ssd_fwd(fc["xbc"], fc["dtT"][H:], a_b, zero_state, reverse=True, name="ssd_fwd_ctx_b", di=DI)

    hx = _norm_mod_fwd(x0, norm_mix, sh1, sc1, name="norm_mod_x")
    fx = front(hx, "x", True)
    y_f, hs_f, _ = _ssd_fwd(fx["xbc"], fx["dtT"][:H], a_f, h_f, reverse=False, name="ssd_fwd_x_f", di=DI)
    FF = w_ffn_gate.shape[2]
    got_gu = gather_behind(jnp.concatenate([w_ffn_gate[0].T, w_ffn_up[0].T], axis=0), y_f, "gather_w_gate_up", 4)
    w_guT = jnp.concatenate([_rows_whole(got_gu[:, :FF]), _rows_whole(got_gu[:, FF:])], axis=0)
    w_down = _rows_whole(gather_behind(w_ffn_down[0], y_f, "gather_w_down", 6))
    y_b, hs_b, _ = _ssd_fwd(fx["xbc"], fx["dtT"][H:], a_b, h_b, reverse=True, name="ssd_fwd_x_b", di=DI)
    ya_in = _gate_norm_fwd(y_f, y_b, fx["xbc"], fx["z"], dskip_e, ssm_norm, name="gate_norm_fwd")
    ya = _mm(ya_in, w_pa, "nn", name="mm_proj_a", out_dtype=ACT)
    conv_out = _glu_conv_fwd(fx["u"], fx["v"], cw31, cf_conv_b, name="glu_conv_fwd")
    cf = _ln_silu_fwd(conv_out, cf_ln_g, cf_ln_b, name="ln_silu_fwd")
    yb = _mm(cf, w_pb, "nn", name="mm_proj_b", out_dtype=ACT)
    merged = _merge_fwd(ya, yb, fx["gates"], name="merge_fwd")
    o_mix = _mm(merged, w_o, "nn", name="mm_out", out_dtype=ACT)

    x1, h2 = _resid_norm_mod_fwd(x0, o_mix, g1, norm_ffn, sh2, sc2, name="resid_norm_mod")
    gu = _mm(h2, w_guT, "nt", name="mm_gate_up", out_dtype=ACT)
    act = _swiglu_fwd(gu, name="swiglu_fwd")
    dn = _mm(act, w_down, "nn", name="mm_down", out_dtype=ACT)

    loss_part, dx2, d_dn, g_norm_final, d_g2 = _final_fwd_bwd(x1, dn, g2, norm_final[None, :], target, name="final")
    loss = lax.psum(loss_part[0, 0], AXES)

    d_act = _mm(d_dn, w_down, "nt", name="mm_d_act", out_dtype=ACT)
    gw_down = _mm(act, d_dn, "tn", name="mm_gw_down", out_dtype=F32)
    parts = {}
    parts["w_ffn_down"] = _exchange_seq(_row_blocks(gw_down), name="scatter_w_down", gather=False, collective_id=7)
    d_gu = _swiglu_bwd(gu, d_act, name="swiglu_bwd")
    gw_guT = _mm(d_gu, h2, "tn", name="mm_gw_gate_up", out_dtype=F32)
    DFF = N_DEV * FF
    gu_blocks = jnp.concatenate([_row_blocks(gw_guT[:DFF]), _row_blocks(gw_guT[DFF:])], axis=1)
    parts_gu = _exchange_seq(gu_blocks, name="scatter_w_gate_up", gather=False, collective_id=8)
    parts["w_ffn_gate"] = jnp.transpose(parts_gu[:, :FF], (0, 2, 1))
    parts["w_ffn_up"] = jnp.transpose(parts_gu[:, FF:], (0, 2, 1))
    d_h2 = _mm(d_gu, w_guT, "nn", name="mm_d_h2", out_dtype=F32)
    dx1, d_sh2, d_sc2, g_norm_ffn, d_o, d_g1 = _norm_mod_bwd(
        x1, norm_ffn, sc2, d_h2, name="norm_mod_bwd_ffn", dres=dx2, o=o_mix, g=g1)

    d_merged = _mm(d_o, w_o, "nt", name="mm_d_merged", out_dtype=ACT)
    gw_out = _mm(merged, d_o, "tn", name="mm_gw_out", out_dtype=F32)
    d_ya, d_yb, d_gates = _merge_bwd(d_merged, ya, yb, fx["gates"], name="merge_bwd")
    gw_pa = _mm(ya_in, d_ya, "tn", name="mm_gw_pa", out_dtype=F32)
    gw_pb = _mm(cf, d_yb, "tn", name="mm_gw_pb", out_dtype=F32)
    proj_blocks = jnp.concatenate([_row_blocks(gw_pa), _row_blocks(gw_pb), _row_blocks(gw_out)], axis=1)
    parts_proj = _exchange_seq(proj_blocks, name="scatter_w_proj", gather=False, collective_id=10)
    parts["w_proj_a"], parts["w_proj_b"] = parts_proj[:, :ra], parts_proj[:, ra:ra + rb]
    parts["w_out"] = parts_proj[:, ra + rb:]
    d_ya_in = _mm(d_ya, w_pa, "nt", name="mm_d_ya_in", out_dtype=ACT)
    d_cf = _mm(d_yb, w_pb, "nt", name="mm_d_cf", out_dtype=ACT)
    d_conv, g_ln_g, g_ln_b = _ln_silu_bwd(conv_out, cf_ln_g, cf_ln_b, d_cf, name="ln_silu_bwd")
    d_u, d_v, g_cw31, g_cb31 = _glu_conv_bwd(fx["u"], fx["v"], cw31, d_conv, name="glu_conv_bwd")
    d_y, d_z, dxs_skip, g_ssm_norm, g_dskip_e = _gate_norm_bwd(
        d_ya_in, y_f, y_b, fx["xbc"], fx["z"], dskip_e, ssm_norm, name="gate_norm_bwd")

    zero_bc = jnp.zeros((T, GN), ACT)
    r1 = _ssd_bwd(fx["xbc"], fx["dtT"][:H], a_f, d_y, hs_f, zero_state, (dxs_skip, zero_bc, zero_bc),
                  reverse=False, name="ssd_bwd_x_f", di=DI)
    r2 = _ssd_bwd(fx["xbc"], fx["dtT"][H:], a_b, d_y, hs_b, zero_state, r1[:3],
                  reverse=True, name="ssd_bwd_x_b", di=DI)
    Tc = ctx0.shape[0]
    zero_yc = jnp.zeros((Tc, DI), ACT)
    r3 = _ssd_bwd(fc["xbc"], fc["dtT"][:H], a_f, zero_yc, hs_cf, r1[5], None,
                  reverse=False, name="ssd_bwd_ctx_f", di=DI)
    r4 = _ssd_bwd(fc["xbc"], fc["dtT"][H:], a_b, zero_yc, hs_cb, r2[5], r3[:3],
                  reverse=True, name="ssd_bwd_ctx_b", di=DI)

    def back(f, rf, rb, tag):
        d_xbc_raw, g_w5, g_b5 = _conv5_silu_bwd(f["xbc_raw"], cw5, ssm_conv_b, (rb[0], rb[1], rb[2]),
                                                name="conv5_bwd_" + tag)
        ddtT = jnp.concatenate([rf[3], rb[3]], axis=0)
        d_rawT, g_dtb = _dt_bwd(f["rawT"], dtb, ddtT, name="dt_bwd_" + tag)
        g_a = jnp.stack([jnp.sum(rf[4], axis=1), jnp.sum(rb[4], axis=1)])
        return d_xbc_raw, d_rawT.T.astype(ACT), g_w5, g_b5, g_dtb, g_a

    dx_xbc_raw, dx_dt_raw, gx_w5, gx_b5, gx_dtb, gx_a = back(fx, r1, r2, "x")
    dc_xbc_raw, dc_dt_raw, gc_w5, gc_b5, gc_dtb, gc_a = back(fc, r3, r4, "ctx")

    gw_xbc = _mm(hc, dc_xbc_raw, "tn", name="mm_gw_xbc_ctx", out_dtype=F32)
    gw_xbc = _mm(hx, dx_xbc_raw, "tn", name="mm_gw_xbc", out_dtype=F32, add=gw_xbc)
    gw_dt = _mm(hc, dc_dt_raw, "tn", name="mm_gw_dt_ctx", out_dtype=F32)
    gw_dt = _mm(hx, dx_dt_raw, "tn", name="mm_gw_dt", out_dtype=F32, add=gw_dt)
    gw_z = _mm(hx, d_z, "tn", name="mm_gw_z", out_dtype=F32)
    gw_u = _mm(hx, d_u, "tn", name="mm_gw_u", out_dtype=F32)
    gw_v = _mm(hx, d_v, "tn", name="mm_gw_v", out_dtype=F32)
    gw_gates = _mm(hx, d_gates, "tn", name="mm_gw_gates", out_dtype=F32)
    gw_in = jnp.concatenate([gw_z, gw_xbc, gw_dt, gw_u, gw_v, gw_gates], axis=1)
    parts["w_in"] = _exchange_seq(_col_blocks(gw_in), name="scatter_w_in", gather=False, collective_id=13)

    d_hx = _mm(d_z, w_z, "nt", name="mm_d_hx_z", out_dtype=F32)
    d_hx = _mm(dx_xbc_raw, w_xbc, "nt", name="mm_d_hx_xbc", out_dtype=F32, add=d_hx)
    d_hx = _mm(dx_dt_raw, w_dt, "nt", name="mm_d_hx_dt", out_dtype=F32, add=d_hx)
    d_hx = _mm(d_u, w_u, "nt", name="mm_d_hx_u", out_dtype=F32, add=d_hx)
    d_hx = _mm(d_v, w_v, "nt", name="mm_d_hx_v", out_dtype=F32, add=d_hx)
    d_hx = _mm(d_gates, w_gates, "nt", name="mm_d_hx_gates", out_dtype=F32, add=d_hx)
    grad_x, d_sh1, d_sc1, gx_norm_mix = _norm_mod_bwd(x0, norm_mix, sc1, d_hx, name="norm_mod_bwd_x", dres=dx1)
    d_hc = _mm(dc_xbc_raw, w_xbc, "nt", name="mm_d_hc_xbc", out_dtype=F32)
    d_hc = _mm(dc_dt_raw, w_dt, "nt", name="mm_d_hc_dt", out_dtype=F32, add=d_hc)
    _, d_csh1, d_csc1, gc_norm_mix = _norm_mod_bwd(ctx0, norm_mix, csc1, d_hc, name="norm_mod_bwd_ctx")

    zD = jnp.zeros((1, D), F32)
    dm_me = jnp.concatenate([d_sh1, d_sc1, d_g1, d_sh2, d_sc2, d_g2], axis=1)
    dm_ctx = jnp.concatenate([d_csh1, d_csc1, zD, zD, zD, zD], axis=1)
    rows16 = lax.broadcasted_iota(jnp.int32, (16, 1), 0)
    dm_rows = jnp.where(rows16 == me, dm_me, 0.0) + jnp.where(rows16 == 8, dm_ctx, 0.0)
    dm_sum = _sum_slots(_exchange(dm_rows, name="gather_dm", gather=True), name="sum_dm")
    g_b_mod = _colsum(dm_sum, name="colsum_dm")
    dm_loc = lax.dynamic_slice(dm_sum, (0, me * NL), (16, NL))
    g_w_mod, dcraw = _mod_bwd(craw, w_mod[0], dm_loc, name="mod_bwd")

    small = [
        ("c_ctx", dcraw[8]), ("norm_mix", gx_norm_mix + gc_norm_mix),
        ("ssm_conv_w", (gx_w5 + gc_w5)[:k5]), ("ssm_conv_b", gx_b5 + gc_b5),
        ("dt_bias", gx_dtb + gc_dtb), ("a_log", (gx_a + gc_a) * a_neg),
        ("d_skip", jnp.sum(g_dskip_e.reshape(H, HEAD_DIM), axis=1)), ("ssm_norm", g_ssm_norm),
        ("cf_conv_w", g_cw31[:k31]), ("cf_conv_b", g_cb31), ("cf_ln_g", g_ln_g), ("cf_ln_b", g_ln_b),
        ("norm_ffn", g_norm_ffn), ("norm_final", g_norm_final),
    ]
    flat = jnp.concatenate([v.reshape(-1) for _, v in small])
    n_small = flat.shape[0]
    rows_small = -(-n_small // 1024) * 8
    flat = jnp.concatenate([flat, jnp.zeros((rows_small * 128 - n_small,), F32)]).reshape(rows_small, 128)
    summed = _sum_slots(_exchange(flat, name="gather_small", gather=True), name="sum_small").reshape(-1)
    g_small = {}
    pos = 0
    for nm, v in small:
        g_small[nm] = summed[pos:pos + v.size].reshape(v.shape)
        pos += v.size
    g_small["b_mod"] = g_b_mod
    n5, n31 = ssm_conv_w.shape[2], cf_conv_w.shape[2]
    g_small["ssm_conv_w"] = lax.dynamic_slice(g_small["ssm_conv_w"], (0, me * n5), (k5, n5))
    g_small["cf_conv_w"] = lax.dynamic_slice(g_small["cf_conv_w"], (0, me * n31), (k31, n31))

    grads, deltas, new_m, new_v = {}, {}, {}, {}

    def adam2d(nm, parts):
        shape = args[nm].shape
        R, C = shape[-2], shape[-1]
        g, d, m2, v2 = _adamw(parts, args[nm].reshape(R, C), args["m_" + nm].reshape(R, C),
                              args["v_" + nm].reshape(R, C), name="adamw_" + nm)
        grads[nm], deltas[nm], new_m[nm], new_v[nm] = [t.reshape(shape) for t in (g, d, m2, v2)]

    adam2d("w_mod", g_w_mod[None])
    for nm in ("w_ffn_down", "w_ffn_gate", "w_ffn_up", "w_out", "w_proj_a", "w_proj_b", "w_in"):
        adam2d(nm, parts[nm])

    small_names = ["c_ctx", "b_mod", "norm_mix", "ssm_conv_w", "ssm_conv_b", "dt_bias", "a_log", "d_skip", "ssm_norm",
                   "cf_conv_w", "cf_conv_b", "cf_ln_g", "cf_ln_b", "norm_ffn", "norm_final"]

    def pack(vals):
        f = jnp.concatenate([t.reshape(-1) for t in vals])
        rows = -(-f.shape[0] // 1024) * 8
        return jnp.concatenate([f, jnp.zeros((rows * 128 - f.shape[0],), F32)]).reshape(rows, 128)

    pg = pack([g_small[nm] for nm in small_names])
    pw = pack([args[nm] for nm in small_names])
    pm = pack([args["m_" + nm] for nm in small_names])
    pv = pack([args["v_" + nm] for nm in small_names])
    outs = _adamw(pg[None], pw, pm, pv, name="adamw_small")
    pos = 0
    for nm in small_names:
        shape = args[nm].shape
        size = math.prod(shape)
        vals = [t.reshape(-1)[pos:pos + size].reshape(shape) for t in outs]
        grads[nm], deltas[nm], new_m[nm], new_v[nm] = vals
        pos += size

    order = ["c_ctx", "w_mod", "b_mod", "norm_mix", "w_in", "ssm_conv_w", "ssm_conv_b", "dt_bias", "a_log", "d_skip",
             "ssm_norm", "cf_conv_w", "cf_conv_b", "cf_ln_g", "cf_ln_b", "w_proj_a", "w_proj_b", "w_out", "norm_ffn",
             "w_ffn_gate", "w_ffn_up", "w_ffn_down", "norm_final"]
    return (loss, grad_x[None], *[grads[n] for n in order], *[deltas[n] for n in order],
            *[new_m[n] for n in order], *[new_v[n] for n in order])
```

```python
import functools
import math

import jax
import jax.numpy as jnp
from jax import lax
from jax.experimental import pallas as pl
from jax.experimental.pallas import tpu as pltpu
from jax.experimental.pallas import tpu_sc as plsc

F32 = jnp.float32
ACT = jnp.bfloat16
HIGHEST = lax.Precision.HIGHEST
MESH = pl.DeviceIdType.MESH
AXES = ("x", "y", "c")
N_DEV = 8

GRID_W = 64
CHUNK = 128
SSD_GROUPS_PER_STEP = 8
HEAD_DIM = 64
GROUPS = 8
STATE = 128
EPS = 1e-6
ADAM_LR = 0.001
ADAM_B1 = 0.9
ADAM_B2 = 0.999
ADAM_EPS = 1e-08
ADAM_WD = 0.01
ADAM_STEP = 10

V7X_VMEM_LIMIT = 56 * 1024 * 1024
NEG = -1e30

NN = (((1,), (0,)), ((), ()))
NT = (((1,), (1,)), ((), ()))
TN = (((0,), (0,)), ((), ()))


def _tile(n, target, quantum):
    best = None
    t = quantum
    while t <= min(n, target):
        if n % t == 0:
            best = t
        t += quantum
    return n if best is None else best


def _params(sem=None):
    kw = dict(vmem_limit_bytes=V7X_VMEM_LIMIT)
    if sem is not None:
        kw["dimension_semantics"] = sem
    return pltpu.CompilerParams(**kw)


def _silu(v):
    return v * jax.nn.sigmoid(v)


def _dsilu(v):
    s = jax.nn.sigmoid(v)
    return s * (1.0 + v * (1.0 - s))


def _exchange(x, *, name, gather):
    shape = x.shape[-2:]

    def body(x_ref, o_ref, send_sems, recv_sems, loc_sem):
        ix, iy, ic = lax.axis_index("x"), lax.axis_index("y"), lax.axis_index("c")
        me = 4 * ix + 2 * iy + ic

        def src(d):
            return x_ref if gather else x_ref.at[d]

        def remote(k, slot, peer_xyz, src_ref):
            return pltpu.make_async_remote_copy(
                src_ref=src_ref, dst_ref=o_ref.at[slot], send_sem=send_sems.at[k], recv_sem=recv_sems.at[k],
                device_id=peer_xyz, device_id_type=MESH)

        local = pltpu.make_async_copy(src(me), o_ref.at[me], loc_sem)
        local.start()
        sends, peers = [], []
        for k in range(1, N_DEV):
            px = 1 - ix if k & 4 else ix
            py = 1 - iy if k & 2 else iy
            pc = 1 - ic if k & 1 else ic
            peer = 4 * px + 2 * py + pc
            cp = remote(k - 1, me, (px, py, pc), src(peer))
            cp.start()
            sends.append(cp)
            peers.append((peer, (px, py, pc)))
        for k in range(1, N_DEV):
            peer, xyz = peers[k - 1]
            remote(k - 1, peer, xyz, src(peer)).wait_recv()
        for cp in sends:
            cp.wait_send()
        local.wait()

    return pl.pallas_call(
        body, name=name,
        out_shape=jax.ShapeDtypeStruct((N_DEV,) + shape, x.dtype),
        in_specs=[pl.BlockSpec(memory_space=pl.ANY)],
        out_specs=pl.BlockSpec(memory_space=pl.ANY),
        scratch_shapes=[pltpu.SemaphoreType.DMA((N_DEV - 1,)), pltpu.SemaphoreType.DMA((N_DEV - 1,)),
                        pltpu.SemaphoreType.DMA],
    )(x)


HBM_SPEC = pl.BlockSpec(memory_space=pltpu.HBM)
SEM_SPEC = pl.BlockSpec(memory_space=pltpu.SEMAPHORE)
ANY_SPEC = pl.BlockSpec(memory_space=pl.ANY)
DATAFLOW = pltpu.SideEffectType.DATAFLOW_SIDE_EFFECTING


def _peer(k):
    ix, iy, ic = lax.axis_index("x"), lax.axis_index("y"), lax.axis_index("c")
    px = 1 - ix if k & 4 else ix
    py = 1 - iy if k & 2 else iy
    pc = 1 - ic if k & 1 else ic
    return (px, py, pc), 4 * px + 2 * py + pc


def _exchange_start(x, after, *, name, gather):
    shape = x.shape[-2:]

    def body(after_ref, x_ref, land_ref, send_sem, recv_sem, x_thru, land_thru, token, loc_sem):
        _, me = _peer(0)

        def src(d):
            return x_ref if gather else x_ref.at[d]

        local = pltpu.make_async_copy(src(me), land_ref.at[me], loc_sem)
        local.start()
        local.wait()
        for k in range(1, N_DEV):
            xyz, peer = _peer(k)
            pltpu.make_async_remote_copy(
                src_ref=src(peer), dst_ref=land_ref.at[me], send_sem=send_sem, recv_sem=recv_sem,
                device_id=xyz, device_id_type=MESH).start()
        token[...] = jnp.zeros_like(token)

    land = lax.empty((N_DEV,) + shape, x.dtype)
    return pl.pallas_call(
        body, name=name,
        out_shape=(pltpu.SemaphoreType.DMA(()), pltpu.SemaphoreType.DMA(()), pltpu.HBM(x.shape, x.dtype),
                   pltpu.HBM((N_DEV,) + shape, x.dtype), jax.ShapeDtypeStruct((8, 128), F32)),
        in_specs=(ANY_SPEC, HBM_SPEC, HBM_SPEC),
        out_specs=(SEM_SPEC, SEM_SPEC, HBM_SPEC, HBM_SPEC, pl.BlockSpec(memory_space=pltpu.VMEM)),
        input_output_aliases={1: 2, 2: 3},
        scratch_shapes=[pltpu.SemaphoreType.DMA],
        compiler_params=pltpu.CompilerParams(has_side_effects=DATAFLOW),
    )(after, pltpu.with_memory_space_constraint(x, pltpu.HBM), pltpu.with_memory_space_constraint(land, pltpu.HBM))


def _exchange_wait(started, after, *, name):
    send_sem, recv_sem, x_thru, land_thru, _ = started

    def body(x_ref, land_ref, send_sem, recv_sem, after_ref, x_dead, got_ref):
        xyz, _ = _peer(0)
        seven = land_ref.at[pl.ds(0, N_DEV - 1)]
        cp = pltpu.make_async_remote_copy(src_ref=seven, dst_ref=seven, send_sem=send_sem, recv_sem=recv_sem,
                                          device_id=xyz, device_id_type=MESH)
        cp.wait_send()
        cp.wait_recv()

    return pl.pallas_call(
        body, name=name,
        out_shape=(pltpu.HBM(x_thru.shape, x_thru.dtype), pltpu.HBM(land_thru.shape, land_thru.dtype)),
        in_specs=(HBM_SPEC, HBM_SPEC, SEM_SPEC, SEM_SPEC, ANY_SPEC),
        out_specs=(HBM_SPEC, HBM_SPEC),
        input_output_aliases={0: 0, 1: 1},
        compiler_params=pltpu.CompilerParams(has_side_effects=DATAFLOW),
    )(x_thru, land_thru, send_sem, recv_sem, after)[1]


def _exchange_seq(x, *, name, gather, collective_id):
    shape = x.shape[-2:]
    x_ref = jax.new_ref(x, memory_space=pltpu.MemorySpace.HBM)
    out_ref = jax.empty_ref(jax.ShapeDtypeStruct((N_DEV,) + shape, x.dtype), memory_space=pltpu.MemorySpace.HBM)

    @pl.kernel(mesh=plsc.ScalarSubcoreMesh(axis_name="seq", num_cores=1), name=name,
               scratch_types=(pltpu.SemaphoreType.DMA, pltpu.SemaphoreType.DMA, pltpu.SemaphoreType.DMA),
               compiler_params=pltpu.CompilerParams(collective_id=collective_id))
    def launch(send_sem, recv_sem, loc_sem):
        barrier = pltpu.get_barrier_semaphore()
        for k in range(1, N_DEV):
            xyz, _ = _peer(k)
            pl.semaphore_signal(barrier, inc=1, device_id=xyz, device_id_type=MESH)
        pl.semaphore_wait(barrier, N_DEV - 1)
        mine, me = _peer(0)

        def src(d):
            return x_ref if gather else x_ref.at[d]

        local = pltpu.make_async_copy(src(me), out_ref.at[me], loc_sem)
        local.start()
        for k in range(1, N_DEV):
            xyz, peer = _peer(k)
            pltpu.make_async_remote_copy(
                src_ref=src(peer), dst_ref=out_ref.at[me], send_sem=send_sem, recv_sem=recv_sem,
                device_id=xyz, device_id_type=MESH).start()
        seven = out_ref.at[pl.ds(0, N_DEV - 1)]
        pltpu.make_async_remote_copy(src_ref=seven, dst_ref=seven, send_sem=send_sem, recv_sem=recv_sem,
                                     device_id=mine, device_id_type=MESH).wait()
        local.wait()

    launch()
    return out_ref[...]


def _gather_seq(x, *, name, collective_id):
    x_ref = jax.new_ref(x, memory_space=pltpu.MemorySpace.HBM)
    out_ref = jax.empty_ref(jax.ShapeDtypeStruct((N_DEV,) + x.shape, x.dtype), memory_space=pltpu.MemorySpace.HBM)

    @pl.kernel(mesh=plsc.ScalarSubcoreMesh(axis_name="seq", num_cores=1), name=name,
               scratch_types=(pltpu.SemaphoreType.DMA((N_DEV - 1,)), pltpu.SemaphoreType.DMA((N_DEV - 1,)),
                              pltpu.SemaphoreType.DMA),
               compiler_params=pltpu.CompilerParams(collective_id=collective_id))
    def launch(send_sems, recv_sems, loc_sem):
        ix, iy, ic = lax.axis_index("x"), lax.axis_index("y"), lax.axis_index("c")
        me, sibling = (ix, iy, ic), (ix, iy, 1 - ic)
        chips = [(1 - ix, iy), (ix, 1 - iy), (1 - ix, 1 - iy)]
        writers = [sibling] + [(*chip, ic) for chip in chips]
        barrier = pltpu.get_barrier_semaphore()
        for peer in writers:
            pl.semaphore_signal(barrier, inc=1, device_id=peer, device_id_type=MESH)
        pl.semaphore_wait(barrier, len(writers))

        def rows(px, py, pc):
            return out_ref.at[4 * px + 2 * py + pc]

        def copy(k, block, to, src=None):
            return pltpu.make_async_remote_copy(
                src_ref=rows(*block) if src is None else src, dst_ref=rows(*block),
                send_sem=send_sems.at[k], recv_sem=recv_sems.at[k], device_id=to, device_id_type=MESH)

        mine = pltpu.make_async_copy(x_ref, rows(*me), loc_sem)
        mine.start()
        first = [copy(0, me, sibling, src=x_ref)]
        first += [copy(1 + j, me, (*chip, ic), src=x_ref) for j, chip in enumerate(chips)]
        for cp in first:
            cp.start()
        passed = [copy(4 + j, (*chip, ic), sibling) for j, chip in enumerate(chips)]
        for j, chip in enumerate(chips):
            copy(1 + j, (*chip, ic), me).wait_recv()
            passed[j].start()
        copy(0, sibling, me).wait_recv()
        for j, chip in enumerate(chips):
            copy(4 + j, (*chip, 1 - ic), me).wait_recv()
        for cp in first + passed:
            cp.wait_send()
        mine.wait()

    launch()
    return out_ref[...]


def _sum_slots(x, *, name):
    n, R, C = x.shape
    tr = _tile(R, 256, 8)

    def body(x_ref, o_ref):
        acc = x_ref[0].astype(F32)
        for d in range(1, n):
            acc = acc + x_ref[d].astype(F32)
        o_ref[...] = acc

    return pl.pallas_call(
        body, name=name, grid=(R // tr,),
        out_shape=jax.ShapeDtypeStruct((R, C), F32),
        in_specs=[pl.BlockSpec((n, tr, C), lambda i: (0, i, 0))],
        out_specs=pl.BlockSpec((tr, C), lambda i: (i, 0)),
        compiler_params=_params(("parallel",)),
    )(x)


def _colsum(x, *, name):
    R, C = x.shape

    def body(x_ref, o_ref):
        o_ref[...] = jnp.sum(x_ref[...], axis=0, keepdims=True)

    return pl.pallas_call(
        body, name=name, out_shape=jax.ShapeDtypeStruct((1, C), F32),
        in_specs=[pl.BlockSpec((R, C), lambda: (0, 0))], out_specs=pl.BlockSpec((1, C), lambda: (0, 0)),
        compiler_params=_params(),
    )(x)


def _adamw(parts, w, m, v, *, name):
    n, R, C = parts.shape
    tr = _tile(R, 128, 8)
    c1 = 1.0 - ADAM_B1 ** ADAM_STEP
    c2 = 1.0 - ADAM_B2 ** ADAM_STEP

    def body(p_ref, w_ref, m_ref, v_ref, g_ref, d_ref, nm_ref, nv_ref):
        g = p_ref[0].astype(F32)
        for d in range(1, n):
            g = g + p_ref[d].astype(F32)
        mn = ADAM_B1 * m_ref[...] + (1.0 - ADAM_B1) * g
        vn = ADAM_B2 * v_ref[...] + (1.0 - ADAM_B2) * (g * g)
        g_ref[...] = g
        nm_ref[...] = mn
        nv_ref[...] = vn
        d_ref[...] = -ADAM_LR * ((mn / c1) / (jnp.sqrt(vn / c2) + ADAM_EPS) + ADAM_WD * w_ref[...])

    spec = pl.BlockSpec((tr, C), lambda i: (i, 0))
    shp = jax.ShapeDtypeStruct((R, C), F32)
    return pl.pallas_call(
        body, name=name, grid=(R // tr,), out_shape=(shp, shp, shp, shp),
        in_specs=[pl.BlockSpec((n, tr, C), lambda i: (0, i, 0)), spec, spec, spec],
        out_specs=(spec, spec, spec, spec),
        compiler_params=_params(("parallel",)),
    )(parts, w, m, v)


MM_VMEM_BUDGET = 40 * 1024 * 1024
MM_TK_MAX = 2816
MXU_WIDTH = 256


def _divisors(n, quantum, cap):
    return [t for t in range(quantum, min(n, cap) + 1, quantum) if n % t == 0] or [n]


def _mm_tiles(M, N, K, mode, a_bytes, b_bytes, o_bytes, has_add):
    tk = max(_divisors(K, 128, MM_TK_MAX))
    nk = K // tk
    best = None
    for tm in _divisors(M, 128 if mode == "tn" else 8, 1024):
        for tn in _divisors(N, 128, 3072):
            need = 2 * (tm * tk * a_bytes + tk * tn * b_bytes) + 2 * tm * tn * o_bytes + tm * tn * 4
            need += tm * tn * 4 if nk > 1 else 0
            need += 2 * tm * tn * 4 if has_add else 0
            if need > MM_VMEM_BUDGET:
                continue
            score = (tn % MXU_WIDTH == 0 or tn == N, tm * tn, tm)
            if best is None or score > best[0]:
                best = (score, tm, tn)
    assert best is not None, (M, N, K)
    return best[1], best[2], tk


def _mm(a, b, mode, *, name, out_dtype, add=None, after=None):
    if mode == "nn":
        (M, K), (K2, N) = a.shape, b.shape
    elif mode == "nt":
        (M, K), (N, K2) = a.shape, b.shape
    else:
        (K, M), (K2, N) = a.shape, b.shape
    assert K == K2, (name, a.shape, b.shape)
    tm, tn, tk = _mm_tiles(M, N, K, mode, a.dtype.itemsize, b.dtype.itemsize, jnp.dtype(out_dtype).itemsize,
                           add is not None)
    nk = K // tk
    dims = {"nn": NN, "nt": NT, "tn": TN}[mode]

    a_spec = {"nn": pl.BlockSpec((tm, tk), lambda i, j, k: (i, k)),
              "nt": pl.BlockSpec((tm, tk), lambda i, j, k: (i, k)),
              "tn": pl.BlockSpec((tk, tm), lambda i, j, k: (k, i))}[mode]
    b_spec = {"nn": pl.BlockSpec((tk, tn), lambda i, j, k: (k, j)),
              "nt": pl.BlockSpec((tn, tk), lambda i, j, k: (j, k)),
              "tn": pl.BlockSpec((tk, tn), lambda i, j, k: (k, j))}[mode]
    o_spec = pl.BlockSpec((tm, tn), lambda i, j, k: (i, j))

    def body(a_ref, b_ref, *rest):
        rest = list(rest)
        add_ref = rest.pop(0) if add is not None else None
        if after is not None:
            rest.pop(0)
        o_ref = rest.pop(0)
        part = lax.dot_general(a_ref[...].astype(ACT), b_ref[...].astype(ACT), dims, preferred_element_type=F32)

        def finish(r):
            if add is not None:
                r = r + add_ref[...].astype(F32)
            o_ref[...] = r.astype(out_dtype)

        if nk == 1:
            finish(part)
            return
        acc = rest.pop(0)
        k = pl.program_id(2)

        @pl.when(k == 0)
        def _():
            acc[...] = part

        @pl.when(jnp.logical_and(k > 0, k < nk - 1))
        def _():
            acc[...] += part

        @pl.when(k == nk - 1)
        def _():
            finish(acc[...] + part)

    operands = [a, b] + ([] if add is None else [add])
    in_specs = [a_spec, b_spec] + ([] if add is None else [o_spec])
    if after is not None:
        operands.append(after)
        in_specs.append(ANY_SPEC)
    return pl.pallas_call(
        body, name=name, grid=(M // tm, N // tn, nk),
        out_shape=jax.ShapeDtypeStruct((M, N), out_dtype),
        in_specs=in_specs, out_specs=o_spec,
        scratch_shapes=[pltpu.VMEM((tm, tn), F32)] if nk > 1 else [],
        compiler_params=_params(("parallel", "parallel", "arbitrary")),
    )(*operands)


def _row(tr, cols, blk=0):
    return pl.BlockSpec((tr, cols), lambda i: (i, blk))


def _vec(cols):
    return pl.BlockSpec((1, cols), lambda i: (0, 0))


def _rms(xf):
    return lax.rsqrt(jnp.mean(xf * xf, axis=-1, keepdims=True) + EPS)


def _rms_bwd(dxhat, xhat, r):
    return r * (dxhat - xhat * jnp.mean(dxhat * xhat, axis=-1, keepdims=True))


def _acc_rows(ref, val, first):
    s = jnp.sum(val, axis=0, keepdims=True)

    @pl.when(first)
    def _():
        ref[...] = s

    @pl.when(jnp.logical_not(first))
    def _():
        ref[...] += s


def _norm_mod_fwd(x, nw, shift, scale, *, name):
    T, D = x.shape
    tr = _tile(T, 256, 8)

    def body(x_ref, nw_ref, sh_ref, sc_ref, o_ref):
        xf = x_ref[...]
        n = xf * _rms(xf) * nw_ref[...]
        o_ref[...] = (n * (1.0 + sc_ref[...]) + sh_ref[...]).astype(ACT)

    return pl.pallas_call(
        body, name=name, grid=(T // tr,), out_shape=jax.ShapeDtypeStruct((T, D), ACT),
        in_specs=[_row(tr, D), _vec(D), _vec(D), _vec(D)], out_specs=_row(tr, D),
        compiler_params=_params(("parallel",)),
    )(x, nw, shift, scale)


def _resid_norm_mod_fwd(x, o, g, nw, shift, scale, *, name):
    T, D = x.shape
    tr = _tile(T, 256, 8)

    def body(x_ref, o_ref, g_ref, nw_ref, sh_ref, sc_ref, x1_ref, h_ref):
        x1 = x_ref[...] + g_ref[...] * o_ref[...].astype(F32)
        x1_ref[...] = x1
        n = x1 * _rms(x1) * nw_ref[...]
        h_ref[...] = (n * (1.0 + sc_ref[...]) + sh_ref[...]).astype(ACT)

    return pl.pallas_call(
        body, name=name, grid=(T // tr,),
        out_shape=(jax.ShapeDtypeStruct((T, D), F32), jax.ShapeDtypeStruct((T, D), ACT)),
        in_specs=[_row(tr, D), _row(tr, D), _vec(D), _vec(D), _vec(D), _vec(D)],
        out_specs=(_row(tr, D), _row(tr, D)),
        compiler_params=_params(("parallel",)),
    )(x, o, g, nw, shift, scale)


def _final_fwd_bwd(x1, dn, g2, nw, target, *, name):
    T, D = x1.shape
    tr = _tile(T, 256, 8)

    def body(x1_ref, dn_ref, g_ref, nw_ref, t_ref, loss_ref, dx_ref, ddn_ref, dnw_ref, dg_ref):
        first = pl.program_id(0) == 0
        dn_f = dn_ref[...].astype(F32)
        x2 = x1_ref[...] + g_ref[...] * dn_f
        r = _rms(x2)
        xhat = x2 * r
        err = xhat * nw_ref[...] - t_ref[...]
        part = 0.5 * jnp.sum(jnp.mean(err * err, axis=-1, keepdims=True), axis=0, keepdims=True)

        @pl.when(first)
        def _():
            loss_ref[...] = part

        @pl.when(jnp.logical_not(first))
        def _():
            loss_ref[...] += part

        dy = err * (1.0 / D)
        _acc_rows(dnw_ref, dy * xhat, first)
        dx2 = _rms_bwd(dy * nw_ref[...], xhat, r)
        dx_ref[...] = dx2
        ddn_ref[...] = (g_ref[...] * dx2).astype(ACT)
        _acc_rows(dg_ref, dx2 * dn_f, first)

    vec = jax.ShapeDtypeStruct((1, D), F32)
    return pl.pallas_call(
        body, name=name, grid=(T // tr,),
        out_shape=(jax.ShapeDtypeStruct((1, 1), F32), jax.ShapeDtypeStruct((T, D), F32),
                   jax.ShapeDtypeStruct((T, D), ACT), vec, vec),
        in_specs=[_row(tr, D), _row(tr, D), _vec(D), _vec(D), _row(tr, D)],
        out_specs=(pl.BlockSpec((1, 1), lambda i: (0, 0)), _row(tr, D), _row(tr, D), _vec(D), _vec(D)),
        compiler_params=_params(("arbitrary",)),
    )(x1, dn, g2, nw, target)


def _norm_mod_bwd(xin, nw, scale, dh, *, name, dres=None, o=None, g=None):
    T, D = xin.shape
    tr = _tile(T, 256, 8)
    has_res, has_o = dres is not None, o is not None

    def body(*refs):
        refs = list(refs)
        x_ref, nw_ref, sc_ref, dh_ref = refs[:4]
        pos = 4
        dres_ref = o_ref = g_ref = None
        if has_res:
            dres_ref = refs[pos]
            pos += 1
        if has_o:
            o_ref, g_ref = refs[pos], refs[pos + 1]
            pos += 2
        dx_ref, dsh_ref, dsc_ref, dnw_ref = refs[pos:pos + 4]
        pos += 4
        first = pl.program_id(0) == 0
        xf = x_ref[...]
        r = _rms(xf)
        xhat = xf * r
        n = xhat * nw_ref[...]
        dhf = dh_ref[...].astype(F32)
        _acc_rows(dsh_ref, dhf, first)
        _acc_rows(dsc_ref, dhf * n, first)
        dn = dhf * (1.0 + sc_ref[...])
        _acc_rows(dnw_ref, dn * xhat, first)
        dx = _rms_bwd(dn * nw_ref[...], xhat, r)
        if has_res:
            dx = dx + dres_ref[...]
        dx_ref[...] = dx
        if has_o:
            do_ref, dg_ref = refs[pos], refs[pos + 1]
            do_ref[...] = (g_ref[...] * dx).astype(ACT)
            _acc_rows(dg_ref, dx * o_ref[...].astype(F32), first)

    vec = jax.ShapeDtypeStruct((1, D), F32)
    operands = [xin, nw, scale, dh]
    in_specs = [_row(tr, D), _vec(D), _vec(D), _row(tr, D)]
    if has_res:
        operands.append(dres)
        in_specs.append(_row(tr, D))
    if has_o:
        operands += [o, g]
        in_specs += [_row(tr, D), _vec(D)]
    out_shape = [jax.ShapeDtypeStruct((T, D), F32), vec, vec, vec]
    out_specs = [_row(tr, D), _vec(D), _vec(D), _vec(D)]
    if has_o:
        out_shape += [jax.ShapeDtypeStruct((T, D), ACT), vec]
        out_specs += [_row(tr, D), _vec(D)]
    return pl.pallas_call(
        body, name=name, grid=(T // tr,), out_shape=tuple(out_shape),
        in_specs=in_specs, out_specs=tuple(out_specs),
        compiler_params=_params(("arbitrary",)),
    )(*operands)


def _swiglu_fwd(gu, *, name):
    T, F = gu.shape[0], gu.shape[1] // 2
    tr = _tile(T, 256, 8)

    def body(g_ref, u_ref, o_ref):
        o_ref[...] = (_silu(g_ref[...].astype(F32)) * u_ref[...].astype(F32)).astype(ACT)

    return pl.pallas_call(
        body, name=name, grid=(T // tr,), out_shape=jax.ShapeDtypeStruct((T, F), ACT),
        in_specs=[_row(tr, F, 0), _row(tr, F, 1)], out_specs=_row(tr, F),
        compiler_params=_params(("parallel",)),
    )(gu, gu)


def _swiglu_bwd(gu, dact, *, name):
    T, F = gu.shape[0], gu.shape[1] // 2
    tr = _tile(T, 256, 8)

    def body(g_ref, u_ref, d_ref, o_ref):
        gf, uf, df = g_ref[...].astype(F32), u_ref[...].astype(F32), d_ref[...].astype(F32)
        o_ref[:, :F] = (df * uf * _dsilu(gf)).astype(ACT)
        o_ref[:, F:] = (df * _silu(gf)).astype(ACT)

    return pl.pallas_call(
        body, name=name, grid=(T // tr,), out_shape=jax.ShapeDtypeStruct((T, 2 * F), ACT),
        in_specs=[_row(tr, F, 0), _row(tr, F, 1), _row(tr, F)], out_specs=_row(tr, 2 * F),
        compiler_params=_params(("parallel",)),
    )(gu, gu, dact)


def _merge_fwd(ya, yb, gates, *, name):
    T, D = ya.shape
    tr = _tile(T, 256, 8)

    def body(a_ref, b_ref, g_ref, o_ref):
        ga = g_ref[:, :D].astype(F32)
        gb = g_ref[:, D:].astype(F32)
        o_ref[...] = (jax.nn.sigmoid(ga) * a_ref[...].astype(F32)
                      + jax.nn.sigmoid(gb) * b_ref[...].astype(F32)).astype(ACT)

    return pl.pallas_call(
        body, name=name, grid=(T // tr,), out_shape=jax.ShapeDtypeStruct((T, D), ACT),
        in_specs=[_row(tr, D), _row(tr, D), _row(tr, 2 * D)], out_specs=_row(tr, D),
        compiler_params=_params(("parallel",)),
    )(ya, yb, gates)


def _merge_bwd(dmer, ya, yb, gates, *, name):
    T, D = ya.shape
    tr = _tile(T, 256, 8)

    def body(d_ref, a_ref, b_ref, g_ref, da_ref, db_ref, dg_ref):
        d = d_ref[...].astype(F32)
        sa = jax.nn.sigmoid(g_ref[:, :D].astype(F32))
        sb = jax.nn.sigmoid(g_ref[:, D:].astype(F32))
        da_ref[...] = (d * sa).astype(ACT)
        db_ref[...] = (d * sb).astype(ACT)
        dg_ref[:, :D] = (d * a_ref[...].astype(F32) * sa * (1.0 - sa)).astype(ACT)
        dg_ref[:, D:] = (d * b_ref[...].astype(F32) * sb * (1.0 - sb)).astype(ACT)

    shp = jax.ShapeDtypeStruct((T, D), ACT)
    return pl.pallas_call(
        body, name=name, grid=(T // tr,), out_shape=(shp, shp, jax.ShapeDtypeStruct((T, 2 * D), ACT)),
        in_specs=[_row(tr, D), _row(tr, D), _row(tr, D), _row(tr, 2 * D)],
        out_specs=(_row(tr, D), _row(tr, D), _row(tr, 2 * D)),
        compiler_params=_params(("parallel",)),
    )(dmer, ya, yb, gates)


def _gate_norm_fwd(yf, yb, xbc, z, dskip, nw, *, name):
    T, DI = z.shape
    tr = _tile(T, 128, 8)

    def body(yf_ref, yb_ref, xs_ref, z_ref, ds_ref, nw_ref, o_ref):
        y = yf_ref[...].astype(F32) + yb_ref[...].astype(F32) + ds_ref[...] * xs_ref[...].astype(F32)
        gz = y * _silu(z_ref[...].astype(F32))
        o_ref[...] = (gz * _rms(gz) * nw_ref[...]).astype(ACT)

    return pl.pallas_call(
        body, name=name, grid=(T // tr,), out_shape=jax.ShapeDtypeStruct((T, DI), ACT),
        in_specs=[_row(tr, DI), _row(tr, DI), _row(tr, DI), _row(tr, DI), _vec(DI), _vec(DI)],
        out_specs=_row(tr, DI),
        compiler_params=_params(("parallel",)),
    )(yf, yb, xbc, z, dskip, nw)


def _gate_norm_bwd(dout, yf, yb, xbc, z, dskip, nw, *, name):
    T, DI = z.shape
    tr = _tile(T, 128, 8)

    def body(do_ref, yf_ref, yb_ref, xs_ref, z_ref, ds_ref, nw_ref, dy_ref, dz_ref, dxs_ref, dnw_ref, dds_ref):
        first = pl.program_id(0) == 0
        xs = xs_ref[...].astype(F32)
        zf = z_ref[...].astype(F32)
        y = yf_ref[...].astype(F32) + yb_ref[...].astype(F32) + ds_ref[...] * xs
        sz = _silu(zf)
        gz = y * sz
        r = _rms(gz)
        ghat = gz * r
        do = do_ref[...].astype(F32)
        _acc_rows(dnw_ref, do * ghat, first)
        dgz = _rms_bwd(do * nw_ref[...], ghat, r)
        dy = dgz * sz
        dy_ref[...] = dy.astype(ACT)
        dz_ref[...] = (dgz * y * _dsilu(zf)).astype(ACT)
        dxs_ref[...] = (dy * ds_ref[...]).astype(ACT)
        _acc_rows(dds_ref, dy * xs, first)

    shp = jax.ShapeDtypeStruct((T, DI), ACT)
    vec = jax.ShapeDtypeStruct((1, DI), F32)
    return pl.pallas_call(
        body, name=name, grid=(T // tr,), out_shape=(shp, shp, shp, vec, vec),
        in_specs=[_row(tr, DI)] * 5 + [_vec(DI), _vec(DI)],
        out_specs=(_row(tr, DI), _row(tr, DI), _row(tr, DI), _vec(DI), _vec(DI)),
        compiler_params=_params(("arbitrary",)),
    )(dout, yf, yb, xbc, z, dskip, nw)


def _ln_silu_fwd(x, g, b, *, name):
    T, D = x.shape
    tr = _tile(T, 256, 8)

    def body(x_ref, g_ref, b_ref, o_ref):
        xf = x_ref[...].astype(F32)
        xc = xf - jnp.mean(xf, axis=-1, keepdims=True)
        rstd = lax.rsqrt(jnp.mean(xc * xc, axis=-1, keepdims=True) + EPS)
        o_ref[...] = _silu(xc * rstd * g_ref[...] + b_ref[...]).astype(ACT)

    return pl.pallas_call(
        body, name=name, grid=(T // tr,), out_shape=jax.ShapeDtypeStruct((T, D), ACT),
        in_specs=[_row(tr, D), _vec(D), _vec(D)], out_specs=_row(tr, D),
        compiler_params=_params(("parallel",)),
    )(x, g, b)


def _ln_silu_bwd(x, g, b, dcf, *, name):
    T, D = x.shape
    tr = _tile(T, 256, 8)

    def body(x_ref, g_ref, b_ref, d_ref, dx_ref, dg_ref, db_ref):
        first = pl.program_id(0) == 0
        xf = x_ref[...].astype(F32)
        xc = xf - jnp.mean(xf, axis=-1, keepdims=True)
        rstd = lax.rsqrt(jnp.mean(xc * xc, axis=-1, keepdims=True) + EPS)
        xhat = xc * rstd
        dyln = d_ref[...].astype(F32) * _dsilu(xhat * g_ref[...] + b_ref[...])
        _acc_rows(dg_ref, dyln * xhat, first)
        _acc_rows(db_ref, dyln, first)
        dxh = dyln * g_ref[...]
        dx = rstd * (dxh - jnp.mean(dxh, axis=-1, keepdims=True)
                     - xhat * jnp.mean(dxh * xhat, axis=-1, keepdims=True))
        dx_ref[...] = dx.astype(ACT)

    vec = jax.ShapeDtypeStruct((1, D), F32)
    return pl.pallas_call(
        body, name=name, grid=(T // tr,), out_shape=(jax.ShapeDtypeStruct((T, D), ACT), vec, vec),
        in_specs=[_row(tr, D), _vec(D), _vec(D), _row(tr, D)],
        out_specs=(_row(tr, D), _vec(D), _vec(D)),
        compiler_params=_params(("arbitrary",)),
    )(x, g, b, dcf)


CONV_CW = 128
CONV_RT = 256
SEQ_PAD = 8


def _window(ext, off, n):
    if off % 8 == 0:
        return ext[off:off + n]
    return pltpu.roll(ext, ext.shape[0] - off, 0)[:n]


def _sum8(v):
    R, C = v.shape
    return jnp.sum(v.reshape(R // 8, 8, C), axis=0)


def _conv5_silu_fwd(x, w, b, *, name):
    T, C = x.shape
    K = 5
    cw, rt = CONV_CW, _tile(T, CONV_RT, 8)
    half = K // 2

    def body(x_ref, w_ref, b_ref, o_ref, pad):
        zeros = jnp.zeros((SEQ_PAD, cw), F32)
        pad[0:SEQ_PAD, :] = zeros
        pad[T + SEQ_PAD:T + 2 * SEQ_PAD, :] = zeros

        def fill(i, c):
            base = pl.multiple_of(i * rt, rt)
            pad[pl.ds(base + SEQ_PAD, rt), :] = x_ref[pl.ds(base, rt), :].astype(F32)
            return c

        lax.fori_loop(0, T // rt, fill, 0)
        wv = w_ref[...]
        bias = b_ref[...]

        def step(i, c):
            base = pl.multiple_of(i * rt, rt)
            ext = pad[pl.ds(base, rt + 2 * SEQ_PAD), :]
            acc = jnp.zeros((rt, cw), F32) + bias
            for k in range(K):
                acc = acc + wv[k:k + 1, :] * _window(ext, SEQ_PAD + k - half, rt)
            o_ref[pl.ds(base, rt), :] = _silu(acc).astype(ACT)
            return c

        lax.fori_loop(0, T // rt, step, 0)

    return pl.pallas_call(
        body, name=name, grid=(C // cw,), out_shape=jax.ShapeDtypeStruct((T, C), ACT),
        in_specs=[pl.BlockSpec((T, cw), lambda j: (0, j)), pl.BlockSpec((8, cw), lambda j: (0, j)),
                  pl.BlockSpec((1, cw), lambda j: (0, j))],
        out_specs=pl.BlockSpec((T, cw), lambda j: (0, j)),
        scratch_shapes=[pltpu.VMEM((T + 2 * SEQ_PAD, cw), F32)],
        compiler_params=_params(("parallel",)),
    )(x, w, b)


def _conv5_silu_bwd(x, w, b, douts, *, name):
    T, C = x.shape
    K = 5
    cw, rt = CONV_CW, _tile(T, CONV_RT, 8)
    half = K // 2
    tiles = [d.shape[1] // cw for d in douts]
    firsts = [sum(tiles[:i]) for i in range(len(douts))]
    assert sum(tiles) * cw == C

    def body(x_ref, w_ref, b_ref, *rest):
        d_refs = rest[:len(douts)]
        dx_ref, dw_ref, db_ref, pad, dpad, wacc = rest[len(douts):]
        tile = pl.program_id(0)

        def dout_rows(base):
            d = d_refs[-1][pl.ds(base, rt), :]
            for i in range(len(douts) - 2, -1, -1):
                d = jnp.where(tile < firsts[i + 1], d_refs[i][pl.ds(base, rt), :], d)
            return d

        zeros = jnp.zeros((SEQ_PAD, cw), F32)
        for p in (pad, dpad):
            p[0:SEQ_PAD, :] = zeros
            p[T + SEQ_PAD:T + 2 * SEQ_PAD, :] = zeros
        wacc[...] = jnp.zeros_like(wacc)

        def fill(i, c):
            base = pl.multiple_of(i * rt, rt)
            pad[pl.ds(base + SEQ_PAD, rt), :] = x_ref[pl.ds(base, rt), :].astype(F32)
            return c

        lax.fori_loop(0, T // rt, fill, 0)
        wv = w_ref[...]
        bias = b_ref[...]

        def step1(i, c):
            base = pl.multiple_of(i * rt, rt)
            ext = pad[pl.ds(base, rt + 2 * SEQ_PAD), :]
            wins = [_window(ext, SEQ_PAD + k - half, rt) for k in range(K)]
            pre = jnp.zeros((rt, cw), F32) + bias
            for k in range(K):
                pre = pre + wv[k:k + 1, :] * wins[k]
            dpre = dout_rows(base).astype(F32) * _dsilu(pre)
            dpad[pl.ds(base + SEQ_PAD, rt), :] = dpre
            for k in range(K):
                wacc[k] += _sum8(dpre * wins[k])
            wacc[K] += _sum8(dpre)
            return c

        lax.fori_loop(0, T // rt, step1, 0)

        def step2(i, c):
            base = pl.multiple_of(i * rt, rt)
            ext = dpad[pl.ds(base, rt + 2 * SEQ_PAD), :]
            acc = jnp.zeros((rt, cw), F32)
            for k in range(K):
                acc = acc + wv[k:k + 1, :] * _window(ext, SEQ_PAD - (k - half), rt)
            dx_ref[pl.ds(base, rt), :] = acc.astype(ACT)
            return c

        lax.fori_loop(0, T // rt, step2, 0)
        rows = [jnp.sum(wacc[k], axis=0, keepdims=True) for k in range(K)]
        rows += [jnp.zeros((1, cw), F32)] * (8 - K)
        dw_ref[...] = jnp.concatenate(rows, axis=0)
        db_ref[...] = jnp.sum(wacc[K], axis=0, keepdims=True)

    return pl.pallas_call(
        body, name=name, grid=(C // cw,),
        out_shape=(jax.ShapeDtypeStruct((T, C), ACT), jax.ShapeDtypeStruct((8, C), F32),
                   jax.ShapeDtypeStruct((1, C), F32)),
        in_specs=[pl.BlockSpec((T, cw), lambda j: (0, j)), pl.BlockSpec((8, cw), lambda j: (0, j)),
                  pl.BlockSpec((1, cw), lambda j: (0, j))]
        + [pl.BlockSpec((T, cw), functools.partial(lambda j, first, n: (0, jnp.clip(j - first, 0, n - 1)),
                                                   first=firsts[i], n=tiles[i])) for i in range(len(douts))],
        out_specs=(pl.BlockSpec((T, cw), lambda j: (0, j)), pl.BlockSpec((8, cw), lambda j: (0, j)),
                   pl.BlockSpec((1, cw), lambda j: (0, j))),
        scratch_shapes=[pltpu.VMEM((T + 2 * SEQ_PAD, cw), F32), pltpu.VMEM((T + 2 * SEQ_PAD, cw), F32),
                        pltpu.VMEM((K + 1, 8, cw), F32)],
        compiler_params=_params(("parallel",)),
    )(x, w, b, *douts)


def _glu_conv_fwd(u, v, w, b, *, name):
    T, C = u.shape
    K = 31
    KP = w.shape[0]
    cw, rt = CONV_CW, _tile(T, CONV_RT, GRID_W)
    half = K // 2
    P = half * GRID_W

    def body(u_ref, v_ref, w_ref, b_ref, o_ref, pad):
        zeros = jnp.zeros((P, cw), F32)
        pad[0:P, :] = zeros
        pad[T + P:T + 2 * P, :] = zeros

        def fill(i, c):
            base = pl.multiple_of(i * rt, rt)
            uf = u_ref[pl.ds(base, rt), :].astype(F32)
            vf = v_ref[pl.ds(base, rt), :].astype(F32)
            pad[pl.ds(base + P, rt), :] = uf * jax.nn.sigmoid(vf)
            return c

        lax.fori_loop(0, T // rt, fill, 0)
        wv = w_ref[...]
        bias = b_ref[...]

        def step(i, c):
            base = pl.multiple_of(i * rt, rt)
            acc = jnp.zeros((rt, cw), F32) + bias
            for k in range(K):
                acc = acc + wv[k:k + 1, :] * pad[pl.ds(base + k * GRID_W, rt), :]
            o_ref[pl.ds(base, rt), :] = acc.astype(ACT)
            return c

        lax.fori_loop(0, T // rt, step, 0)

    col = pl.BlockSpec((T, cw), lambda j: (0, j))
    return pl.pallas_call(
        body, name=name, grid=(C // cw,), out_shape=jax.ShapeDtypeStruct((T, C), ACT),
        in_specs=[col, col, pl.BlockSpec((KP, cw), lambda j: (0, j)), pl.BlockSpec((1, cw), lambda j: (0, j))],
        out_specs=col,
        scratch_shapes=[pltpu.VMEM((T + 2 * P, cw), F32)],
        compiler_params=_params(("parallel",)),
    )(u, v, w, b)


def _glu_conv_bwd(u, v, w, dout, *, name):
    T, C = u.shape
    K = 31
    KP = w.shape[0]
    cw, rt = CONV_CW, _tile(T, CONV_RT, GRID_W)
    half = K // 2
    P = half * GRID_W

    def body(u_ref, v_ref, w_ref, d_ref, du_ref, dv_ref, dw_ref, db_ref, pad, dpad, wacc):
        zeros = jnp.zeros((P, cw), F32)
        for p in (pad, dpad):
            p[0:P, :] = zeros
            p[T + P:T + 2 * P, :] = zeros
        wacc[...] = jnp.zeros_like(wacc)

        def fill(i, c):
            base = pl.multiple_of(i * rt, rt)
            uf = u_ref[pl.ds(base, rt), :].astype(F32)
            vf = v_ref[pl.ds(base, rt), :].astype(F32)
            pad[pl.ds(base + P, rt), :] = uf * jax.nn.sigmoid(vf)
            dpad[pl.ds(base + P, rt), :] = d_ref[pl.ds(base, rt), :].astype(F32)
            return c

        lax.fori_loop(0, T // rt, fill, 0)
        wv = w_ref[...]

        def step(i, c):
            base = pl.multiple_of(i * rt, rt)
            d = dpad[pl.ds(base + P, rt), :]
            dg = jnp.zeros((rt, cw), F32)
            for k in range(K):
                wacc[k] += _sum8(d * pad[pl.ds(base + k * GRID_W, rt), :])
                dg = dg + wv[k:k + 1, :] * dpad[pl.ds(base + (K - 1 - k) * GRID_W, rt), :]
            wacc[K] += _sum8(d)
            uf = u_ref[pl.ds(base, rt), :].astype(F32)
            sv = jax.nn.sigmoid(v_ref[pl.ds(base, rt), :].astype(F32))
            du_ref[pl.ds(base, rt), :] = (dg * sv).astype(ACT)
            dv_ref[pl.ds(base, rt), :] = (dg * uf * sv * (1.0 - sv)).astype(ACT)
            return c

        lax.fori_loop(0, T // rt, step, 0)
        rows = [jnp.sum(wacc[k], axis=0, keepdims=True) for k in range(K)]
        rows += [jnp.zeros((1, cw), F32)] * (KP - K)
        dw_ref[...] = jnp.concatenate(rows, axis=0)
        db_ref[...] = jnp.sum(wacc[K], axis=0, keepdims=True)

    col = pl.BlockSpec((T, cw), lambda j: (0, j))
    shp = jax.ShapeDtypeStruct((T, C), ACT)
    return pl.pallas_call(
        body, name=name, grid=(C // cw,),
        out_shape=(shp, shp, jax.ShapeDtypeStruct((KP, C), F32), jax.ShapeDtypeStruct((1, C), F32)),
        in_specs=[col, col, pl.BlockSpec((KP, cw), lambda j: (0, j)), col],
        out_specs=(col, col, pl.BlockSpec((KP, cw), lambda j: (0, j)), pl.BlockSpec((1, cw), lambda j: (0, j))),
        scratch_shapes=[pltpu.VMEM((T + 2 * P, cw), F32), pltpu.VMEM((T + 2 * P, cw), F32),
                        pltpu.VMEM((K + 1, 8, cw), F32)],
        compiler_params=_params(("parallel",)),
    )(u, v, w, dout)


def _dt_fwd(rawT, bias, *, name):
    H2, T = rawT.shape
    tc = _tile(T, 2048, 128)

    def body(r_ref, b_ref, o_ref):
        v = r_ref[...] + b_ref[...]
        o_ref[...] = jnp.maximum(v, 0.0) + jnp.log(1.0 + jnp.exp(-jnp.abs(v)))

    return pl.pallas_call(
        body, name=name, grid=(T // tc,), out_shape=jax.ShapeDtypeStruct((H2, T), F32),
        in_specs=[pl.BlockSpec((H2, tc), lambda i: (0, i)), pl.BlockSpec((H2, 1), lambda i: (0, 0))],
        out_specs=pl.BlockSpec((H2, tc), lambda i: (0, i)),
        compiler_params=_params(("parallel",)),
    )(rawT, bias)


def _dt_bwd(rawT, bias, ddtT, *, name):
    H2, T = rawT.shape
    tc = _tile(T, 2048, 128)

    def body(r_ref, b_ref, d_ref, o_ref, db_ref):
        first = pl.program_id(0) == 0
        dr = d_ref[...] * jax.nn.sigmoid(r_ref[...] + b_ref[...])
        o_ref[...] = dr
        s = jnp.sum(dr, axis=1, keepdims=True)

        @pl.when(first)
        def _():
            db_ref[...] = s

        @pl.when(jnp.logical_not(first))
        def _():
            db_ref[...] += s

    return pl.pallas_call(
        body, name=name, grid=(T // tc,),
        out_shape=(jax.ShapeDtypeStruct((H2, T), F32), jax.ShapeDtypeStruct((H2, 1), F32)),
        in_specs=[pl.BlockSpec((H2, tc), lambda i: (0, i)), pl.BlockSpec((H2, 1), lambda i: (0, 0)),
                  pl.BlockSpec((H2, tc), lambda i: (0, i))],
        out_specs=(pl.BlockSpec((H2, tc), lambda i: (0, i)), pl.BlockSpec((H2, 1), lambda i: (0, 0))),
        compiler_params=_params(("arbitrary",)),
    )(rawT, bias, ddtT)


def _ssd_common(dtT, a, reverse):
    J, Q = dtT.shape
    li = lax.broadcasted_iota(jnp.int32, (Q, Q), 0)
    si = lax.broadcasted_iota(jnp.int32, (Q, Q), 1)
    mask = (si >= li) if reverse else (si <= li)
    Mf = mask.astype(F32)
    daT = dtT * a
    csT = lax.dot_general(daT, Mf, NT, precision=HIGHEST, preferred_element_type=F32)
    last = 0 if reverse else Q - 1
    totT = csT[:, last:last + 1]
    return mask, Mf, csT, totT, last


def _to_cols(rows):
    R, Q = rows.shape
    if R < 128:
        rows = jnp.concatenate([rows, jnp.zeros((128 - R, Q), F32)], axis=0)
    return rows.T


def _ssd_fwd(xbc, dtT, a, h0, *, reverse, name, di):
    T = xbc.shape[0]
    G, JP, N = h0.shape
    J, P, Q = JP // HEAD_DIM, HEAD_DIM, CHUNK
    nc = T // Q
    QW = 256
    HQ = QW // P

    def ci(k):
        return nc - 1 - k if reverse else k

    GB = SSD_GROUPS_PER_STEP

    def body(x_ref, b_ref, c_ref, dt_ref, a_ref, h0_ref, y_ref, hs_ref, hl_ref, h_scr):
        k = pl.program_id(1)

        @pl.when(k == 0)
        def _():
            h_scr[...] = h0_ref[...]

        lh = lax.broadcasted_iota(jnp.int32, (Q, QW), 1) // P

        def scale_heads(vT, rowsT):
            return jnp.concatenate([vT[j * P:(j + 1) * P, :] * rowsT[j:j + 1, :] for j in range(J)], axis=0)

        for gi in range(GB):
            h = h_scr[gi]
            hs_ref[0, gi] = h
            Xb = x_ref[:, gi * JP:(gi + 1) * JP]
            Bm, Cm = b_ref[:, gi * N:(gi + 1) * N], c_ref[:, gi * N:(gi + 1) * N]
            dtT_v = dt_ref[gi * J:(gi + 1) * J, :]
            mask, _, csT, totT, _ = _ssd_common(dtT_v, a_ref[gi * J:(gi + 1) * J, :], reverse)
            cs = _to_cols(csT)
            CB = lax.dot_general(Cm, Bm, NT, preferred_element_type=F32)
            yoT = lax.dot_general(h.astype(ACT), Cm, NT, preferred_element_type=F32)
            yo = scale_heads(yoT, jnp.exp(csT)).T
            for q in range(JP // QW):
                xq = Xb[:, q * QW:(q + 1) * QW]
                acc = yo[:, q * QW:(q + 1) * QW]
                for jj in range(HQ):
                    j = q * HQ + jj
                    seg = cs[:, j:j + 1] - csT[j:j + 1, :]
                    Mj = (CB * jnp.exp(jnp.where(mask, seg, NEG)) * dtT_v[j:j + 1, :]).astype(ACT)
                    acc = acc + jnp.dot(Mj, jnp.where(lh == jj, xq, jnp.zeros_like(xq)),
                                        preferred_element_type=F32)
                y_ref[:, gi * JP + q * QW:gi * JP + (q + 1) * QW] = acc.astype(ACT)
            xwT = scale_heads(Xb.astype(F32).T, dtT_v * jnp.exp(totT - csT)).astype(ACT)
            upd = jnp.dot(xwT, Bm, preferred_element_type=F32)
            for j in range(J):
                rows = slice(j * P, (j + 1) * P)
                h_scr[gi, rows, :] = h[rows, :] * jnp.exp(totT[j:j + 1, :]) + upd[rows, :]

        @pl.when(k == nc - 1)
        def _():
            hl_ref[...] = h_scr[...]

    GN = G * N
    return pl.pallas_call(
        body, name=name, grid=(G // GB, nc),
        out_shape=(jax.ShapeDtypeStruct((T, di), ACT), jax.ShapeDtypeStruct((nc, G, JP, N), F32),
                   jax.ShapeDtypeStruct((G, JP, N), F32)),
        in_specs=[pl.BlockSpec((Q, GB * JP), lambda g, k: (ci(k), g)),
                  pl.BlockSpec((Q, GB * N), lambda g, k: (ci(k), di // (GB * N) + g)),
                  pl.BlockSpec((Q, GB * N), lambda g, k: (ci(k), (di + GN) // (GB * N) + g)),
                  pl.BlockSpec((GB * J, Q), lambda g, k: (g, ci(k))),
                  pl.BlockSpec((GB * J, 1), lambda g, k: (g, 0)),
                  pl.BlockSpec((GB, JP, N), lambda g, k: (g, 0, 0))],
        out_specs=(pl.BlockSpec((Q, GB * JP), lambda g, k: (ci(k), g)),
                   pl.BlockSpec((1, GB, JP, N), lambda g, k: (ci(k), g, 0, 0)),
                   pl.BlockSpec((GB, JP, N), lambda g, k: (g, 0, 0))),
        scratch_shapes=[pltpu.VMEM((GB, JP, N), F32)],
        compiler_params=_params(("arbitrary", "arbitrary")),
    )(xbc, xbc, xbc, dtT, a, h0)


def _ssd_bwd(xbc, dtT, a, dy, hs, dh_last, add, *, reverse, name, di):
    T = xbc.shape[0]
    G, JP, N = dh_last.shape
    J, P, Q = JP // HEAD_DIM, HEAD_DIM, CHUNK
    nc = T // Q
    QW = 256
    HQ = QW // P
    has_add = add is not None

    def ci(k):
        return k if reverse else nc - 1 - k

    GB = SSD_GROUPS_PER_STEP

    def body(*refs):
        for gi in range(GB):
            wide = lambda r, w: r.at[:, pl.ds(gi * w, w)]
            x_ref, b_ref, c_ref, dt_ref, a_ref, dy_ref, hs_ref, dhl_ref = refs[:8]
            views = [wide(x_ref, JP), wide(b_ref, N), wide(c_ref, N), dt_ref.at[pl.ds(gi * J, J)],
                     a_ref.at[pl.ds(gi * J, J)], wide(dy_ref, JP), hs_ref.at[:, pl.ds(gi, 1)],
                     dhl_ref.at[pl.ds(gi, 1)]]
            rest = refs[8:]
            if has_add:
                views += [wide(rest[0], JP), wide(rest[1], N), wide(rest[2], N)]
                rest = rest[3:]
            dx_ref, db_ref, dc_ref, ddt_ref, da_ref, dh0_ref, dh_scr = rest
            views += [wide(dx_ref, JP), wide(db_ref, N), wide(dc_ref, N), ddt_ref.at[pl.ds(gi * J, J)],
                      da_ref.at[pl.ds(gi * J, J)], dh0_ref.at[pl.ds(gi, 1)], dh_scr.at[gi]]
            group_body(*views)

    def group_body(x_ref, b_ref, c_ref, dt_ref, a_ref, dy_ref, hs_ref, dhl_ref, *rest):
        if has_add:
            adx_ref, adb_ref, adc_ref = rest[:3]
            rest = rest[3:]
        dx_ref, db_ref, dc_ref, ddt_ref, da_ref, dh0_ref, dh_scr = rest
        k = pl.program_id(1)

        @pl.when(k == 0)
        def _():
            dh_scr[...] = dhl_ref[0]
            da_ref[...] = jnp.zeros_like(da_ref)

        def scale_heads(vT, rowsT):
            return jnp.concatenate([vT[j * P:(j + 1) * P, :] * rowsT[j:j + 1, :] for j in range(J)], axis=0)

        def head_sums(vT):
            return jnp.sum(vT.reshape(J, P, Q), axis=1)

        dH = dh_scr[...]
        h = hs_ref[0, 0]
        Bm, Cm = b_ref[...], c_ref[...]
        dtT_v = dt_ref[...]
        a_v = a_ref[...]
        mask, Mf, csT, totT, last = _ssd_common(dtT_v, a_v, reverse)
        ecsT = jnp.exp(csT)
        toendT = jnp.exp(totT - csT)
        cs = _to_cols(csT)
        dYb = dy_ref[...]
        XT = x_ref[...].astype(F32).T
        dYT = dYb.astype(F32).T
        xdtT = scale_heads(XT, dtT_v).astype(ACT)
        dYT_b = dYT.astype(ACT)
        dYeT = scale_heads(dYT, ecsT).astype(ACT)
        h_b = h.astype(ACT)
        dH_b = dH.astype(ACT)
        CB = lax.dot_general(Cm, Bm, NT, preferred_element_type=F32)
        dxdt_offT = scale_heads(lax.dot_general(dH_b, Bm, NT, preferred_element_type=F32), toendT)
        dCB = jnp.zeros((Q, Q), F32)
        lh = lax.broadcasted_iota(jnp.int32, (Q, QW), 1) // P
        sh = lax.broadcasted_iota(jnp.int32, (QW, Q), 0) // P
        lane_q = lax.broadcasted_iota(jnp.int32, (Q, Q), 1)
        sub_j = lax.broadcasted_iota(jnp.int32, (J, Q), 0)
        e_rows = jnp.zeros((Q, Q), F32)
        e_cols = jnp.zeros((J, Q), F32)
        diag = []
        for q in range(JP // QW):
            xq = xdtT[q * QW:(q + 1) * QW, :]
            dyq = dYb[:, q * QW:(q + 1) * QW]
            dyTq = dYT_b[q * QW:(q + 1) * QW, :]
            acc = jnp.zeros((QW, Q), F32)
            for jj in range(HQ):
                j = q * HQ + jj
                seg = cs[:, j:j + 1] - csT[j:j + 1, :]
                L = jnp.exp(jnp.where(mask, seg, NEG))
                Mf_j = CB * L
                dyj = jnp.where(lh == jj, dyq, jnp.zeros_like(dyq))
                dyTj = jnp.where(sh == jj, dyTq, jnp.zeros_like(dyTq))
                acc = acc + jnp.dot(dyTj, Mf_j.astype(ACT), preferred_element_type=F32)
                dM = jnp.dot(dyj, xq, preferred_element_type=F32)
                dCB = dCB + dM * L
                E = dM * Mf_j
                e_rows = jnp.where(lane_q == j, jnp.sum(E, axis=1, keepdims=True), e_rows)
                e_cols = jnp.where(sub_j == j, jnp.sum(E, axis=0, keepdims=True), e_cols)
            diag.append(acc)
        dxdtT = dxdt_offT + jnp.concatenate(diag, axis=0)
        dCB_b = dCB.astype(ACT)
        dC = (jnp.dot(dCB_b, Bm, preferred_element_type=F32)
              + lax.dot_general(dYeT, h_b, TN, preferred_element_type=F32))
        xwT = scale_heads(XT, dtT_v * toendT).astype(ACT)
        dB = (lax.dot_general(dCB_b, Cm, TN, preferred_element_type=F32)
              + lax.dot_general(xwT, dH_b, TN, preferred_element_type=F32))
        dHc = jnp.dot(dYeT, Cm, preferred_element_type=F32)
        for j in range(J):
            rows = slice(j * P, (j + 1) * P)
            dh_scr[rows, :] = dH[rows, :] * jnp.exp(totT[j:j + 1, :]) + dHc[rows, :]
        dh0_ref[0] = dh_scr[...]

        yoT = scale_heads(lax.dot_general(h_b, Cm, NT, preferred_element_type=F32), ecsT)
        RT_ = head_sums(dYT * yoT)
        UT_ = head_sums(XT * dxdtT)
        UoT = head_sums(XT * dxdt_offT)
        hsum = jnp.sum(jnp.sum((dH * h).reshape(J, P, N), axis=1), axis=1, keepdims=True)
        dtot = jnp.sum(UoT * dtT_v, axis=1, keepdims=True) + jnp.exp(totT) * hsum
        lane = lax.broadcasted_iota(jnp.int32, (J, Q), 1)
        dcsT = e_rows.T[0:J] - e_cols + RT_ - UoT * dtT_v + jnp.where(lane == last, dtot, 0.0)
        ddaT = jnp.dot(dcsT, Mf, precision=HIGHEST, preferred_element_type=F32)
        ddt_ref[...] = ddaT * a_v + UT_
        da_ref[...] += ddaT * dtT_v
        dX = scale_heads(dxdtT, dtT_v).T
        if has_add:
            dX = dX + adx_ref[...].astype(F32)
            dB = dB + adb_ref[...].astype(F32)
            dC = dC + adc_ref[...].astype(F32)
        dx_ref[...] = dX.astype(ACT)
        db_ref[...] = dB.astype(ACT)
        dc_ref[...] = dC.astype(ACT)

    GN = G * N
    xspec = pl.BlockSpec((Q, GB * JP), lambda g, k: (ci(k), g))
    nspec = pl.BlockSpec((Q, GB * N), lambda g, k: (ci(k), g))
    hspec = pl.BlockSpec((GB, JP, N), lambda g, k: (g, 0, 0))
    in_specs = [xspec,
                pl.BlockSpec((Q, GB * N), lambda g, k: (ci(k), di // (GB * N) + g)),
                pl.BlockSpec((Q, GB * N), lambda g, k: (ci(k), (di + GN) // (GB * N) + g)),
                pl.BlockSpec((GB * J, Q), lambda g, k: (g, ci(k))),
                pl.BlockSpec((GB * J, 1), lambda g, k: (g, 0)),
                xspec,
                pl.BlockSpec((1, GB, JP, N), lambda g, k: (ci(k), g, 0, 0)),
                hspec]
    operands = [xbc, xbc, xbc, dtT, a, dy, hs, dh_last]
    if has_add:
        in_specs += [xspec, nspec, nspec]
        operands += list(add)
    H = G * J
    return pl.pallas_call(
        body, name=name, grid=(G // GB, nc),
        out_shape=(jax.ShapeDtypeStruct((T, di), ACT), jax.ShapeDtypeStruct((T, GN), ACT),
                   jax.ShapeDtypeStruct((T, GN), ACT), jax.ShapeDtypeStruct((H, T), F32),
                   jax.ShapeDtypeStruct((H, Q), F32), jax.ShapeDtypeStruct((G, JP, N), F32)),
        in_specs=in_specs,
        out_specs=(xspec, nspec, nspec,
                   pl.BlockSpec((GB * J, Q), lambda g, k: (g, ci(k))),
                   pl.BlockSpec((GB * J, Q), lambda g, k: (g, 0)),
                   hspec),
        scratch_shapes=[pltpu.VMEM((GB, JP, N), F32)],
        compiler_params=_params(("arbitrary", "arbitrary")),
    )(*operands)


def _mod_fwd(craw, w, b, *, name):
    R, D = craw.shape
    NL = w.shape[1]
    tn = _tile(NL, 512, 128)

    def body(c_ref, w_ref, b_ref, o_ref):
        o_ref[...] = jnp.dot(_silu(c_ref[...]), w_ref[...], preferred_element_type=F32) + b_ref[...]

    return pl.pallas_call(
        body, name=name, grid=(NL // tn,), out_shape=jax.ShapeDtypeStruct((R, NL), F32),
        in_specs=[pl.BlockSpec((R, D), lambda j: (0, 0)), pl.BlockSpec((D, tn), lambda j: (0, j)),
                  pl.BlockSpec((1, tn), lambda j: (0, j))],
        out_specs=pl.BlockSpec((R, tn), lambda j: (0, j)),
        compiler_params=_params(("parallel",)),
    )(craw, w, b)


def _mod_bwd(craw, w, dm, *, name):
    R, D = craw.shape
    NL = w.shape[1]
    tn = _tile(NL, 512, 128)

    def body(c_ref, w_ref, dm_ref, dw_ref, dc_ref):
        first = pl.program_id(0) == 0
        cf = c_ref[...]
        dmv = dm_ref[...]
        dw_ref[...] = lax.dot_general(_silu(cf), dmv, TN, preferred_element_type=F32)
        part = lax.dot_general(dmv, w_ref[...], NT, preferred_element_type=F32) * _dsilu(cf)

        @pl.when(first)
        def _():
            dc_ref[...] = part

        @pl.when(jnp.logical_not(first))
        def _():
            dc_ref[...] += part

    return pl.pallas_call(
        body, name=name, grid=(NL // tn,),
        out_shape=(jax.ShapeDtypeStruct((D, NL), F32), jax.ShapeDtypeStruct((R, D), F32)),
        in_specs=[pl.BlockSpec((R, D), lambda j: (0, 0)), pl.BlockSpec((D, tn), lambda j: (0, j)),
                  pl.BlockSpec((R, tn), lambda j: (0, j))],
        out_specs=(pl.BlockSpec((D, tn), lambda j: (0, j)), pl.BlockSpec((R, D), lambda j: (0, 0))),
        compiler_params=_params(("arbitrary",)),
    )(craw, w, dm)


def _pad_rows(a, rows):
    return jnp.concatenate([a, jnp.zeros((rows - a.shape[0],) + a.shape[1:], a.dtype)], axis=0)


def _cols_whole(g):
    return jnp.transpose(g, (1, 0, 2)).reshape(g.shape[1], N_DEV * g.shape[2])


def _rows_whole(g):
    return g.reshape(N_DEV * g.shape[1], g.shape[2])


def _col_blocks(full):
    K, n8 = full.shape
    return jnp.transpose(full.reshape(K, N_DEV, n8 // N_DEV), (1, 0, 2)).astype(ACT)


def _row_blocks(full):
    K8, n = full.shape
    return full.reshape(N_DEV, K8 // N_DEV, n).astype(ACT)


def kernel(x, c, ctx, c_ctx, w_mod, b_mod, norm_mix, w_in, ssm_conv_w, ssm_conv_b, dt_bias, a_log, d_skip, ssm_norm, cf_conv_w, cf_conv_b, cf_ln_g, cf_ln_b, w_proj_a, w_proj_b, w_out, norm_ffn, w_ffn_gate, w_ffn_up, w_ffn_down, norm_final, loss_target, m_c_ctx, m_w_mod, m_b_mod, m_norm_mix, m_w_in, m_ssm_conv_w, m_ssm_conv_b, m_dt_bias, m_a_log, m_d_skip, m_ssm_norm, m_cf_conv_w, m_cf_conv_b, m_cf_ln_g, m_cf_ln_b, m_w_proj_a, m_w_proj_b, m_w_out, m_norm_ffn, m_w_ffn_gate, m_w_ffn_up, m_w_ffn_down, m_norm_final, v_c_ctx, v_w_mod, v_b_mod, v_norm_mix, v_w_in, v_ssm_conv_w, v_ssm_conv_b, v_dt_bias, v_a_log, v_d_skip, v_ssm_norm, v_cf_conv_w, v_cf_conv_b, v_cf_ln_g, v_cf_ln_b, v_w_proj_a, v_w_proj_b, v_w_out, v_norm_ffn, v_w_ffn_gate, v_w_ffn_up, v_w_ffn_down, v_norm_final):
    args = dict(locals())
    me = 4 * lax.axis_index("x") + 2 * lax.axis_index("y") + lax.axis_index("c")
    T, D = x.shape[1], x.shape[2]
    DI = ssm_norm.shape[1]
    H = DI // HEAD_DIM
    G, J, N = GROUPS, H // GROUPS, STATE
    JP = J * HEAD_DIM
    GN = G * N
    CONV = DI + 2 * GN
    x0 = x[0]
    ctx0 = ctx[0]
    target = loss_target[0]

    got_in = _gather_seq(w_in[0].astype(ACT), name="gather_w_in", collective_id=0)
    k5 = ssm_conv_w.shape[1]
    k31 = cf_conv_w.shape[1]
    n5, n31 = 8 * ssm_conv_w.shape[2], 32 * cf_conv_w.shape[2]
    early = jnp.concatenate([c[0], _pad_rows(ssm_conv_w[0], 8).reshape(-1), _pad_rows(cf_conv_w[0], 32).reshape(-1)])
    rows_early = -(-early.shape[0] // 1024) * 8
    early = jnp.concatenate([early, jnp.zeros((rows_early * 128 - early.shape[0],), F32)]).reshape(rows_early, 128)
    early = _exchange(early, name="gather_early", gather=True).reshape(N_DEV, rows_early * 128)
    cw5 = jnp.transpose(early[:, D:D + n5].reshape(N_DEV, 8, n5 // 8), (1, 0, 2)).reshape(8, CONV)
    cw31 = jnp.transpose(early[:, D + n5:D + n5 + n31].reshape(N_DEV, 32, n31 // 32), (1, 0, 2)).reshape(32, D)

    c_all = early[:, :D]
    craw = jnp.concatenate([c_all, c_ctx[None, :], jnp.zeros((7, D), F32)], axis=0)
    NL = w_mod.shape[2]
    b_loc = lax.dynamic_slice(b_mod, (0, me * NL), (1, NL))
    m_loc = _mod_fwd(craw, w_mod[0], b_loc, name="mod_fwd")
    m_all = jnp.transpose(_exchange(m_loc, name="gather_mod", gather=True), (1, 0, 2)).reshape(16, N_DEV * NL)
    m_me = lax.dynamic_slice(m_all, (me, 0), (1, 6 * D))
    sh1, sc1, g1, sh2, sc2, g2 = [m_me[:, i * D:(i + 1) * D] for i in range(6)]
    csh1, csc1 = m_all[8:9, 0:D], m_all[8:9, D:2 * D]

    win = _cols_whole(got_in)
    o_xbc, o_dt, o_glu, o_gates = DI, DI + CONV, DI + CONV + 2 * H, DI + CONV + 2 * H + 2 * D
    w_z, w_xbc, w_dt = win[:, :o_xbc], win[:, o_xbc:o_dt], win[:, o_dt:o_glu]
    w_u, w_v, w_gates = win[:, o_glu:o_glu + D], win[:, o_glu + D:o_gates], win[:, o_gates:]
    def gather_behind(shard, behind, name, cid):
        zero = (behind[(0,) * behind.ndim] * 0).astype(ACT)
        return _gather_seq(shard.astype(ACT) + zero, name=name, collective_id=cid)

    ra, rb = w_proj_a.shape[1], w_proj_b.shape[1]
    got_proj = gather_behind(jnp.concatenate([w_proj_a[0], w_proj_b[0], w_out[0]], axis=0), got_in, "gather_w_proj", 1)
    w_pa = _rows_whole(got_proj[:, :ra])
    w_pb = _rows_whole(got_proj[:, ra:ra + rb])
    w_o = _rows_whole(got_proj[:, ra + rb:])

    a_neg = -jnp.exp(a_log[0])
    a_f, a_b = a_neg[0][:, None], a_neg[1][:, None]
    dtb = dt_bias[0].reshape(2 * H, 1)
    dskip_e = jnp.repeat(d_skip[0], HEAD_DIM)[None, :]

    def front(h, tag, full, after=None):
        out = {}
        out["xbc_raw"] = _mm(h, w_xbc, "nn", name="mm_xbc_" + tag, out_dtype=ACT, after=after)
        dt_raw = _mm(h, w_dt, "nn", name="mm_dt_" + tag, out_dtype=F32)
        out["rawT"] = dt_raw.T
        if full:
            out["z"] = _mm(h, w_z, "nn", name="mm_z_" + tag, out_dtype=ACT)
            out["u"] = _mm(h, w_u, "nn", name="mm_u_" + tag, out_dtype=ACT)
            out["v"] = _mm(h, w_v, "nn", name="mm_v_" + tag, out_dtype=ACT)
            out["gates"] = _mm(h, w_gates, "nn", name="mm_gates_" + tag, out_dtype=ACT)
        out["xbc"] = _conv5_silu_fwd(out["xbc_raw"], cw5, ssm_conv_b, name="conv5_fwd_" + tag)
        out["dtT"] = _dt_fwd(out["rawT"], dtb, name="dt_fwd_" + tag)
        return out

    hc = _norm_mod_fwd(ctx0, norm_mix, csh1, csc1, name="norm_mod_ctx")
    fc = front(hc, "ctx", False)
    zero_state = jnp.zeros((G, JP, N), F32)
    _, hs_cf, h_f = _ssd_fwd(fc["xbc"], fc["dtT"][:H], a_f, zero_state, reverse=False, name="ssd_fwd_ctx_f", di=DI)
    _, hs_cb, h_b = _ssd_fwd(fc["xbc"], fc["dtT"][H:], a_b, zero_state, reverse=True, name="ssd_fwd_ctx_b", di=DI)

    hx = _norm_mod_fwd(x0, norm_mix, sh1, sc1, name="norm_mod_x")
    fx = front(hx, "x", True)
    y_f, hs_f, _ = _ssd_fwd(fx["xbc"], fx["dtT"][:H], a_f, h_f, reverse=False, name="ssd_fwd_x_f", di=DI)
    FF = w_ffn_gate.shape[2]
    got_gu = gather_behind(jnp.concatenate([w_ffn_gate[0].T, w_ffn_up[0].T], axis=0), y_f, "gather_w_gate_up", 4)
    w_guT = jnp.concatenate([_rows_whole(got_gu[:, :FF]), _rows_whole(got_gu[:, FF:])], axis=0)
    w_down = _rows_whole(gather_behind(w_ffn_down[0], y_f, "gather_w_down", 6))
    y_b, hs_b, _ = _ssd_fwd(fx["xbc"], fx["dtT"][H:], a_b, h_b, reverse=True, name="ssd_fwd_x_b", di=DI)
    ya_in = _gate_norm_fwd(y_f, y_b, fx["xbc"], fx["z"], dskip_e, ssm_norm, name="gate_norm_fwd")
    ya = _mm(ya_in, w_pa, "nn", name="mm_proj_a", out_dtype=ACT)
    conv_out = _glu_conv_fwd(fx["u"], fx["v"], cw31, cf_conv_b, name="glu_conv_fwd")
    cf = _ln_silu_fwd(conv_out, cf_ln_g, cf_ln_b, name="ln_silu_fwd")
    yb = _mm(cf, w_pb, "nn", name="mm_proj_b", out_dtype=ACT)
    merged = _merge_fwd(ya, yb, fx["gates"], name="merge_fwd")
    o_mix = _mm(merged, w_o, "nn", name="mm_out", out_dtype=ACT)

    x1, h2 = _resid_norm_mod_fwd(x0, o_mix, g1, norm_ffn, sh2, sc2, name="resid_norm_mod")
    gu = _mm(h2, w_guT, "nt", name="mm_gate_up", out_dtype=ACT)
    act = _swiglu_fwd(gu, name="swiglu_fwd")
    dn = _mm(act, w_down, "nn", name="mm_down", out_dtype=ACT)

    loss_part, dx2, d_dn, g_norm_final, d_g2 = _final_fwd_bwd(x1, dn, g2, norm_final[None, :], target, name="final")
    loss = lax.psum(loss_part[0, 0], AXES)

    d_act = _mm(d_dn, w_down, "nt", name="mm_d_act", out_dtype=ACT)
    gw_down = _mm(act, d_dn, "tn", name="mm_gw_down", out_dtype=F32)
    parts = {}
    parts["w_ffn_down"] = _exchange_seq(_row_blocks(gw_down), name="scatter_w_down", gather=False, collective_id=7)
    d_gu = _swiglu_bwd(gu, d_act, name="swiglu_bwd")
    gw_guT = _mm(d_gu, h2, "tn", name="mm_gw_gate_up", out_dtype=F32)
    DFF = N_DEV * FF
    gu_blocks = jnp.concatenate([_row_blocks(gw_guT[:DFF]), _row_blocks(gw_guT[DFF:])], axis=1)
    parts_gu = _exchange_seq(gu_blocks, name="scatter_w_gate_up", gather=False, collective_id=8)
    parts["w_ffn_gate"] = jnp.transpose(parts_gu[:, :FF], (0, 2, 1))
    parts["w_ffn_up"] = jnp.transpose(parts_gu[:, FF:], (0, 2, 1))
    d_h2 = _mm(d_gu, w_guT, "nn", name="mm_d_h2", out_dtype=F32)
    dx1, d_sh2, d_sc2, g_norm_ffn, d_o, d_g1 = _norm_mod_bwd(
        x1, norm_ffn, sc2, d_h2, name="norm_mod_bwd_ffn", dres=dx2, o=o_mix, g=g1)

    d_merged = _mm(d_o, w_o, "nt", name="mm_d_merged", out_dtype=ACT)
    gw_out = _mm(merged, d_o, "tn", name="mm_gw_out", out_dtype=F32)
    d_ya, d_yb, d_gates = _merge_bwd(d_merged, ya, yb, fx["gates"], name="merge_bwd")
    gw_pa = _mm(ya_in, d_ya, "tn", name="mm_gw_pa", out_dtype=F32)
    gw_pb = _mm(cf, d_yb, "tn", name="mm_gw_pb", out_dtype=F32)
    proj_blocks = jnp.concatenate([_row_blocks(gw_pa), _row_blocks(gw_pb), _row_blocks(gw_out)], axis=1)
    parts_proj = _exchange_seq(proj_blocks, name="scatter_w_proj", gather=False, collective_id=10)
    parts["w_proj_a"], parts["w_proj_b"] = parts_proj[:, :ra], parts_proj[:, ra:ra + rb]
    parts["w_out"] = parts_proj[:, ra + rb:]
    d_ya_in = _mm(d_ya, w_pa, "nt", name="mm_d_ya_in", out_dtype=ACT)
    d_cf = _mm(d_yb, w_pb, "nt", name="mm_d_cf", out_dtype=ACT)
    d_conv, g_ln_g, g_ln_b = _ln_silu_bwd(conv_out, cf_ln_g, cf_ln_b, d_cf, name="ln_silu_bwd")
    d_u, d_v, g_cw31, g_cb31 = _glu_conv_bwd(fx["u"], fx["v"], cw31, d_conv, name="glu_conv_bwd")
    d_y, d_z, dxs_skip, g_ssm_norm, g_dskip_e = _gate_norm_bwd(
        d_ya_in, y_f, y_b, fx["xbc"], fx["z"], dskip_e, ssm_norm, name="gate_norm_bwd")

    zero_bc = jnp.zeros((T, GN), ACT)
    r1 = _ssd_bwd(fx["xbc"], fx["dtT"][:H], a_f, d_y, hs_f, zero_state, (dxs_skip, zero_bc, zero_bc),
                  reverse=False, name="ssd_bwd_x_f", di=DI)
    r2 = _ssd_bwd(fx["xbc"], fx["dtT"][H:], a_b, d_y, hs_b, zero_state, r1[:3],
                  reverse=True, name="ssd_bwd_x_b", di=DI)
    Tc = ctx0.shape[0]
    zero_yc = jnp.zeros((Tc, DI), ACT)
    r3 = _ssd_bwd(fc["xbc"], fc["dtT"][:H], a_f, zero_yc, hs_cf, r1[5], None,
                  reverse=False, name="ssd_bwd_ctx_f", di=DI)
    r4 = _ssd_bwd(fc["xbc"], fc["dtT"][H:], a_b, zero_yc, hs_cb, r2[5], r3[:3],
                  reverse=True, name="ssd_bwd_ctx_b", di=DI)

    def back(f, rf, rb, tag):
        d_xbc_raw, g_w5, g_b5 = _conv5_silu_bwd(f["xbc_raw"], cw5, ssm_conv_b, (rb[0], rb[1], rb[2]),
                                                name="conv5_bwd_" + tag)
        ddtT = jnp.concatenate([rf[3], rb[3]], axis=0)
        d_rawT, g_dtb = _dt_bwd(f["rawT"], dtb, ddtT, name="dt_bwd_" + tag)
        g_a = jnp.stack([jnp.sum(rf[4], axis=1), jnp.sum(rb[4], axis=1)])
        return d_xbc_raw, d_rawT.T.astype(ACT), g_w5, g_b5, g_dtb, g_a

    dx_xbc_raw, dx_dt_raw, gx_w5, gx_b5, gx_dtb, gx_a = back(fx, r1, r2, "x")
    dc_xbc_raw, dc_dt_raw, gc_w5, gc_b5, gc_dtb, gc_a = back(fc, r3, r4, "ctx")

    gw_xbc = _mm(hc, dc_xbc_raw, "tn", name="mm_gw_xbc_ctx", out_dtype=F32)
    gw_xbc = _mm(hx, dx_xbc_raw, "tn", name="mm_gw_xbc", out_dtype=F32, add=gw_xbc)
    gw_dt = _mm(hc, dc_dt_raw, "tn", name="mm_gw_dt_ctx", out_dtype=F32)
    gw_dt = _mm(hx, dx_dt_raw, "tn", name="mm_gw_dt", out_dtype=F32, add=gw_dt)
    gw_z = _mm(hx, d_z, "tn", name="mm_gw_z", out_dtype=F32)
    gw_u = _mm(hx, d_u, "tn", name="mm_gw_u", out_dtype=F32)
    gw_v = _mm(hx, d_v, "tn", name="mm_gw_v", out_dtype=F32)
    gw_gates = _mm(hx, d_gates, "tn", name="mm_gw_gates", out_dtype=F32)
    gw_in = jnp.concatenate([gw_z, gw_xbc, gw_dt, gw_u, gw_v, gw_gates], axis=1)
    parts["w_in"] = _exchange_seq(_col_blocks(gw_in), name="scatter_w_in", gather=False, collective_id=13)

    d_hx = _mm(d_z, w_z, "nt", name="mm_d_hx_z", out_dtype=F32)
    d_hx = _mm(dx_xbc_raw, w_xbc, "nt", name="mm_d_hx_xbc", out_dtype=F32, add=d_hx)
    d_hx = _mm(dx_dt_raw, w_dt, "nt", name="mm_d_hx_dt", out_dtype=F32, add=d_hx)
    d_hx = _mm(d_u, w_u, "nt", name="mm_d_hx_u", out_dtype=F32, add=d_hx)
    d_hx = _mm(d_v, w_v, "nt", name="mm_d_hx_v", out_dtype=F32, add=d_hx)
    d_hx = _mm(d_gates, w_gates, "nt", name="mm_d_hx_gates", out_dtype=F32, add=d_hx)
    grad_x, d_sh1, d_sc1, gx_norm_mix = _norm_mod_bwd(x0, norm_mix, sc1, d_hx, name="norm_mod_bwd_x", dres=dx1)
    d_hc = _mm(dc_xbc_raw, w_xbc, "nt", name="mm_d_hc_xbc", out_dtype=F32)
    d_hc = _mm(dc_dt_raw, w_dt, "nt", name="mm_d_hc_dt", out_dtype=F32, add=d_hc)
    _, d_csh1, d_csc1, gc_norm_mix = _norm_mod_bwd(ctx0, norm_mix, csc1, d_hc, name="norm_mod_bwd_ctx")

    zD = jnp.zeros((1, D), F32)
    dm_me = jnp.concatenate([d_sh1, d_sc1, d_g1, d_sh2, d_sc2, d_g2], axis=1)
    dm_ctx = jnp.concatenate([d_csh1, d_csc1, zD, zD, zD, zD], axis=1)
    rows16 = lax.broadcasted_iota(jnp.int32, (16, 1), 0)
    dm_rows = jnp.where(rows16 == me, dm_me, 0.0) + jnp.where(rows16 == 8, dm_ctx, 0.0)
    dm_sum = _sum_slots(_exchange(dm_rows, name="gather_dm", gather=True), name="sum_dm")
    g_b_mod = _colsum(dm_sum, name="colsum_dm")
    dm_loc = lax.dynamic_slice(dm_sum, (0, me * NL), (16, NL))
    g_w_mod, dcraw = _mod_bwd(craw, w_mod[0], dm_loc, name="mod_bwd")

    small = [
        ("c_ctx", dcraw[8]), ("norm_mix", gx_norm_mix + gc_norm_mix),
        ("ssm_conv_w", (gx_w5 + gc_w5)[:k5]), ("ssm_conv_b", gx_b5 + gc_b5),
        ("dt_bias", gx_dtb + gc_dtb), ("a_log", (gx_a + gc_a) * a_neg),
        ("d_skip", jnp.sum(g_dskip_e.reshape(H, HEAD_DIM), axis=1)), ("ssm_norm", g_ssm_norm),
        ("cf_conv_w", g_cw31[:k31]), ("cf_conv_b", g_cb31), ("cf_ln_g", g_ln_g), ("cf_ln_b", g_ln_b),
        ("norm_ffn", g_norm_ffn), ("norm_final", g_norm_final),
    ]
    flat = jnp.concatenate([v.reshape(-1) for _, v in small])
    n_small = flat.shape[0]
    rows_small = -(-n_small // 1024) * 8
    flat = jnp.concatenate([flat, jnp.zeros((rows_small * 128 - n_small,), F32)]).reshape(rows_small, 128)
    summed = _sum_slots(_exchange(flat, name="gather_small", gather=True), name="sum_small").reshape(-1)
    g_small = {}
    pos = 0
    for nm, v in small:
        g_small[nm] = summed[pos:pos + v.size].reshape(v.shape)
        pos += v.size
    g_small["b_mod"] = g_b_mod
    n5, n31 = ssm_conv_w.shape[2], cf_conv_w.shape[2]
    g_small["ssm_conv_w"] = lax.dynamic_slice(g_small["ssm_conv_w"], (0, me * n5), (k5, n5))
    g_small["cf_conv_w"] = lax.dynamic_slice(g_small["cf_conv_w"], (0, me * n31), (k31, n31))

    grads, deltas, new_m, new_v = {}, {}, {}, {}

    def adam2d(nm, parts):
        shape = args[nm].shape
        R, C = shape[-2], shape[-1]
        g, d, m2, v2 = _adamw(parts, args[nm].reshape(R, C), args["m_" + nm].reshape(R, C),
                              args["v_" + nm].reshape(R, C), name="adamw_" + nm)
        grads[nm], deltas[nm], new_m[nm], new_v[nm] = [t.reshape(shape) for t in (g, d, m2, v2)]

    adam2d("w_mod", g_w_mod[None])
    for nm in ("w_ffn_down", "w_ffn_gate", "w_ffn_up", "w_out", "w_proj_a", "w_proj_b", "w_in"):
        adam2d(nm, parts[nm])

    small_names = ["c_ctx", "b_mod", "norm_mix", "ssm_conv_w", "ssm_conv_b", "dt_bias", "a_log", "d_skip", "ssm_norm",
                   "cf_conv_w", "cf_conv_b", "cf_ln_g", "cf_ln_b", "norm_ffn", "norm_final"]

    def pack(vals):
        f = jnp.concatenate([t.reshape(-1) for t in vals])
        rows = -(-f.shape[0] // 1024) * 8
        return jnp.concatenate([f, jnp.zeros((rows * 128 - f.shape[0],), F32)]).reshape(rows, 128)

    pg = pack([g_small[nm] for nm in small_names])
    pw = pack([args[nm] for nm in small_names])
    pm = pack([args["m_" + nm] for nm in small_names])
    pv = pack([args["v_" + nm] for nm in small_names])
    outs = _adamw(pg[None], pw, pm, pv, name="adamw_small")
    pos = 0
    for nm in small_names:
        shape = args[nm].shape
        size = math.prod(shape)
        vals = [t.reshape(-1)[pos:pos + size].reshape(shape) for t in outs]
        grads[nm], deltas[nm], new_m[nm], new_v[nm] = vals
        pos += size

    order = ["c_ctx", "w_mod", "b_mod", "norm_mix", "w_in", "ssm_conv_w", "ssm_conv_b", "dt_bias", "a_log", "d_skip",
             "ssm_norm", "cf_conv_w", "cf_conv_b", "cf_ln_g", "cf_ln_b", "w_proj_a", "w_proj_b", "w_out", "norm_ffn",
             "w_ffn_gate", "w_ffn_up", "w_ffn_down", "norm_final"]
    return (loss, grad_x[None], *[grads[n] for n in order], *[deltas[n] for n in order],
            *[new_m[n] for n in order], *[new_v[n] for n in order])
```
